```python
import math
import jax, jax.numpy as jnp
from jax import lax
import numpy as np

D_MODEL = 2048
BATCH = 8
SEQ = 2048
DEPTH = 4

CHUNK = 64
Q_BLOCK = 128
HEAD_DIM = 128
D_MIX = D_MODEL
D_SB = D_MIX // 2
D_CH = D_MIX - D_SB
N_HEADS_SB = D_SB // HEAD_DIM
N_HEADS_CH = D_CH // HEAD_DIM
LEFT_CHUNKS = 8
BAND = LEFT_CHUNKS + 1
REL_CLIP = 256
N_REL = REL_CLIP + CHUNK
D_IN = 4 * D_SB + 4 * D_CH
NORM_EPS = 1e-6
NEG_BIG = -1e30

kernel_name = "hybrid_stickbreak_chunkband_trunk"


def rms_norm(x, g):
    xf = x.astype(jnp.float32)
    y = xf * lax.rsqrt(jnp.mean(xf * xf, axis=-1, keepdims=True) + NORM_EPS)
    return (y * g.astype(jnp.float32)).astype(x.dtype)


def split_heads(t, n_heads):
    b, s, _ = t.shape
    return t.reshape(b, s, n_heads, HEAD_DIM).transpose(0, 2, 1, 3)


def merge_heads(t):
    b, h, s, d = t.shape
    return t.transpose(0, 2, 1, 3).reshape(b, s, h * d)


def stick_breaking_attention(q, k, v):
    seq = q.shape[2]
    scale = q.shape[-1] ** -0.5
    outs = []
    for blk in range(seq // Q_BLOCK):
        t0 = blk * Q_BLOCK
        t1 = t0 + Q_BLOCK
        qb = q[:, :, t0:t1]
        kb = k[:, :, :t1]
        vb = v[:, :, :t1]
        z = jnp.einsum('bhtd,bhsd->bhts', qb, kb).astype(jnp.float32) * scale
        t_idx = jnp.arange(t0, t1)[:, None]
        s_idx = jnp.arange(t1)[None, :]
        causal = s_idx < t_idx
        log_stay = jnp.where(causal, jax.nn.log_sigmoid(-z), 0.0)
        after = lax.cumsum(log_stay, axis=3, reverse=True) - log_stay
        w = jnp.where(causal, jnp.exp(jax.nn.log_sigmoid(z) + after), 0.0)
        outs.append(jnp.einsum('bhts,bhsd->bhtd', w.astype(v.dtype), vb))
    return jnp.concatenate(outs, axis=2)


def chunk_band_attention(q, k, v, q_gain, k_gain, rel_table):
    b, h, seq, d = q.shape
    nc = seq // CHUNK
    q = rms_norm(q, q_gain)
    k = rms_norm(k, k_gain)
    qc = q.reshape(b, h, nc, CHUNK, d)
    pad = ((0, 0), (0, 0), (LEFT_CHUNKS * CHUNK, 0), (0, 0))
    kc = jnp.pad(k, pad).reshape(b, h, nc + LEFT_CHUNKS, CHUNK, d)
    vc = jnp.pad(v, pad).reshape(b, h, nc + LEFT_CHUNKS, CHUNK, d)
    band_idx = jnp.arange(nc)[:, None] + jnp.arange(BAND)[None, :]
    kband = kc[:, :, band_idx].reshape(b, h, nc, BAND * CHUNK, d)
    vband = vc[:, :, band_idx].reshape(b, h, nc, BAND * CHUNK, d)
    scores = jnp.einsum('bhcid,bhcpd->bhcip', qc, kband).astype(jnp.float32) * (d ** -0.5)
    i_pos = np.arange(CHUNK)[:, None]
    p_pos = np.arange(BAND * CHUNK)[None, :]
    dist = LEFT_CHUNKS * CHUNK + i_pos - p_pos
    rel_idx = np.clip(dist, -(CHUNK - 1), REL_CLIP) + (CHUNK - 1)
    bias = rel_table.astype(jnp.float32)[:, rel_idx]
    scores = scores + bias[None, :, None]
    valid = jnp.repeat(band_idx >= LEFT_CHUNKS, CHUNK, axis=1)
    scores = jnp.where(valid[None, None, :, None, :], scores, NEG_BIG)
    probs = jax.nn.softmax(scores, axis=-1)
    out = jnp.einsum('bhcip,bhcpd->bhcid', probs.astype(v.dtype), vband)
    return out.reshape(b, h, seq, d)


def _fwd_setup_inputs(seed: int = 0) -> dict:
    key = jax.random.key(seed)
    ks = jax.random.split(key, 8)
    x = jax.random.normal(ks[0], (BATCH, SEQ, D_MODEL), jnp.float32)
    norm_g = 1.0 + 0.02 * jax.random.normal(ks[1], (DEPTH, D_MODEL), jnp.float32)
    w_in = jax.random.normal(ks[2], (DEPTH, D_MODEL, D_IN), jnp.float32) * D_MODEL ** -0.5
    q_norm_g = 1.0 + 0.02 * jax.random.normal(ks[3], (DEPTH, HEAD_DIM), jnp.float32)
    k_norm_g = 1.0 + 0.02 * jax.random.normal(ks[4], (DEPTH, HEAD_DIM), jnp.float32)
    rel_bias = 0.1 * jax.random.normal(ks[5], (DEPTH, N_HEADS_CH, N_REL), jnp.float32)
    w_out = jax.random.normal(ks[6], (DEPTH, D_MIX, D_MODEL), jnp.float32) * D_MIX ** -0.5
    return {"x": x, "norm_g": norm_g, "w_in": w_in, "q_norm_g": q_norm_g,
            "k_norm_g": k_norm_g, "rel_bias": rel_bias, "w_out": w_out}


def _fwd_reference(x, norm_g, w_in, q_norm_g, k_norm_g, rel_bias, w_out):
    splits = np.cumsum([D_SB, D_SB, D_SB, D_SB, D_CH, D_CH, D_CH])
    for layer in range(DEPTH):
        h = rms_norm(x, norm_g[layer])
        proj = jnp.einsum('bsd,de->bse', h, w_in[layer])
        qa, ka, va, ga, qb, kb, vb, gb = jnp.split(proj, splits, axis=-1)
        ya = stick_breaking_attention(split_heads(qa, N_HEADS_SB),
                                      split_heads(ka, N_HEADS_SB),
                                      split_heads(va, N_HEADS_SB))
        yb = chunk_band_attention(split_heads(qb, N_HEADS_CH),
                                  split_heads(kb, N_HEADS_CH),
                                  split_heads(vb, N_HEADS_CH),
                                  q_norm_g[layer], k_norm_g[layer], rel_bias[layer])
        mixed = jnp.concatenate([merge_heads(ya) * jax.nn.silu(ga),
                                 merge_heads(yb) * jax.nn.silu(gb)], axis=-1)
        x = x + jnp.einsum('bse,ed->bsd', mixed, w_out[layer])
    return x


import jax as _jax
import jax.numpy as _jnp

TWIN_FORMAT = 'train_step'
FWD_PARAMS = ['x', 'norm_g', 'w_in', 'q_norm_g', 'k_norm_g', 'rel_bias', 'w_out']
TWIN_WEIGHTS = ['norm_g', 'w_in', 'q_norm_g', 'k_norm_g', 'rel_bias', 'w_out']
TWIN_DIFF_INPUT = 'x'
TWIN_INPUTS = ['x', 'norm_g', 'w_in', 'q_norm_g', 'k_norm_g', 'rel_bias', 'w_out', 'loss_target', 'm_norm_g', 'm_w_in', 'm_q_norm_g', 'm_k_norm_g', 'm_rel_bias', 'm_w_out', 'v_norm_g', 'v_w_in', 'v_q_norm_g', 'v_k_norm_g', 'v_rel_bias', 'v_w_out']
TWIN_OUTPUTS = ['loss', 'grad_x', 'grad_norm_g', 'grad_w_in', 'grad_q_norm_g', 'grad_k_norm_g', 'grad_rel_bias', 'grad_w_out', 'delta_norm_g', 'delta_w_in', 'delta_q_norm_g', 'delta_k_norm_g', 'delta_rel_bias', 'delta_w_out', 'new_m_norm_g', 'new_m_w_in', 'new_m_q_norm_g', 'new_m_k_norm_g', 'new_m_rel_bias', 'new_m_w_out', 'new_v_norm_g', 'new_v_w_in', 'new_v_q_norm_g', 'new_v_k_norm_g', 'new_v_rel_bias', 'new_v_w_out']
TWIN_LEAF_KINDS = {'loss': 'loss', 'grad_x': 'grad_x', 'grad_norm_g': 'grad_w', 'grad_w_in': 'grad_w', 'grad_q_norm_g': 'grad_w', 'grad_k_norm_g': 'grad_w', 'grad_rel_bias': 'grad_w', 'grad_w_out': 'grad_w', 'delta_norm_g': 'delta_w', 'delta_w_in': 'delta_w', 'delta_q_norm_g': 'delta_w', 'delta_k_norm_g': 'delta_w', 'delta_rel_bias': 'delta_w', 'delta_w_out': 'delta_w', 'new_m_norm_g': 'new_m', 'new_m_w_in': 'new_m', 'new_m_q_norm_g': 'new_m', 'new_m_k_norm_g': 'new_m', 'new_m_rel_bias': 'new_m', 'new_m_w_out': 'new_m', 'new_v_norm_g': 'new_v', 'new_v_w_in': 'new_v', 'new_v_q_norm_g': 'new_v', 'new_v_k_norm_g': 'new_v', 'new_v_rel_bias': 'new_v', 'new_v_w_out': 'new_v'}


def _forward(args):
    return _fwd_reference(*[args[k] for k in FWD_PARAMS])


def _output_shape():
    out = _jax.eval_shape(lambda: _forward(_fwd_setup_inputs(0)))
    return out.shape, out.dtype

N_MICROBATCH = 1
ADAM_LR = 0.001
ADAM_B1 = 0.9
ADAM_B2 = 0.999
ADAM_EPS = 1e-08
ADAM_WD = 0.01
ADAM_STEP = 10
PER_EXAMPLE_BATCH_AXIS = {'x': 0, 'loss_target': 0}
SHARED_INPUTS = []
_WEIGHT_DTYPES = {'norm_g': _jnp.float32, 'w_in': _jnp.float32, 'q_norm_g': _jnp.float32, 'k_norm_g': _jnp.float32, 'rel_bias': _jnp.float32, 'w_out': _jnp.float32}
MOMENT_SCALE = {'norm_g': 1.460306e+00, 'w_in': 3.283970e-02, 'q_norm_g': 1.000078e-01, 'k_norm_g': 9.995956e-02, 'rel_bias': 6.278455e-03, 'w_out': 3.420715e-02}


def _to_microbatches(a, axis):
    t = _jnp.moveaxis(a, axis, 0)
    t = t.reshape((N_MICROBATCH, t.shape[0] // N_MICROBATCH) + t.shape[1:])
    return _jnp.moveaxis(t, 1, axis + 1)


def setup_inputs(seed: int = 0) -> dict:
    inp = _fwd_setup_inputs(seed)
    key = _jax.random.fold_in(_jax.random.key(seed), 7919)
    shape, _ = _output_shape()
    out = dict(inp)
    out["loss_target"] = _jax.random.normal(_jax.random.fold_in(key, 0), shape, _jnp.float32)
    for i, name in enumerate(TWIN_WEIGHTS):
        w = inp[name].astype(_jnp.float32)
        if MOMENT_SCALE is None:
            s = _jnp.sqrt(_jnp.mean(_jnp.square(w)) + 1e-30)
        else:
            s = MOMENT_SCALE[name]
        km, kv = _jax.random.split(_jax.random.fold_in(key, i + 1))
        out[name] = w
        out["m_" + name] = s * _jax.random.normal(km, w.shape, _jnp.float32)
        out["v_" + name] = (s * s) * _jax.random.uniform(kv, w.shape, _jnp.float32, 0.5, 1.5)
    if N_MICROBATCH > 1:
        for name, axis in PER_EXAMPLE_BATCH_AXIS.items():
            out[name] = _to_microbatches(out[name], axis)
    return {'x': out['x'], 'norm_g': out['norm_g'], 'w_in': out['w_in'], 'q_norm_g': out['q_norm_g'], 'k_norm_g': out['k_norm_g'], 'rel_bias': out['rel_bias'], 'w_out': out['w_out'], 'loss_target': out['loss_target'], 'm_norm_g': out['m_norm_g'], 'm_w_in': out['m_w_in'], 'm_q_norm_g': out['m_q_norm_g'], 'm_k_norm_g': out['m_k_norm_g'], 'm_rel_bias': out['m_rel_bias'], 'm_w_out': out['m_w_out'], 'v_norm_g': out['v_norm_g'], 'v_w_in': out['v_w_in'], 'v_q_norm_g': out['v_q_norm_g'], 'v_k_norm_g': out['v_k_norm_g'], 'v_rel_bias': out['v_rel_bias'], 'v_w_out': out['v_w_out']}


def _loss(weights, diff, rest, loss_target):
    with _jax.named_scope("forward"):
        args = {**rest, TWIN_DIFF_INPUT: diff, **{k: w.astype(_WEIGHT_DTYPES[k]) for k, w in weights.items()}}
        y = _forward(args)
    with _jax.named_scope("loss_head"):
        err = _jnp.square(y.astype(_jnp.float32) - loss_target)
        return 0.5 * _jnp.sum(_jnp.mean(err, axis=-1)) if err.ndim else 0.5 * err


def _adamw(w, g, m, v):
    m = ADAM_B1 * m + (1.0 - ADAM_B1) * g
    v = ADAM_B2 * v + (1.0 - ADAM_B2) * _jnp.square(g)
    m_hat = m / (1.0 - ADAM_B1 ** ADAM_STEP)
    v_hat = v / (1.0 - ADAM_B2 ** ADAM_STEP)
    delta = -ADAM_LR * (m_hat / (_jnp.sqrt(v_hat) + ADAM_EPS) + ADAM_WD * w)
    return delta, m, v


def reference(x, norm_g, w_in, q_norm_g, k_norm_g, rel_bias, w_out, loss_target, m_norm_g, m_w_in, m_q_norm_g, m_k_norm_g, m_rel_bias, m_w_out, v_norm_g, v_w_in, v_q_norm_g, v_k_norm_g, v_rel_bias, v_w_out):
    given = dict(x=x, norm_g=norm_g, w_in=w_in, q_norm_g=q_norm_g, k_norm_g=k_norm_g, rel_bias=rel_bias, w_out=w_out, loss_target=loss_target, m_norm_g=m_norm_g, m_w_in=m_w_in, m_q_norm_g=m_q_norm_g, m_k_norm_g=m_k_norm_g, m_rel_bias=m_rel_bias, m_w_out=m_w_out, v_norm_g=v_norm_g, v_w_in=v_w_in, v_q_norm_g=v_q_norm_g, v_k_norm_g=v_k_norm_g, v_rel_bias=v_rel_bias, v_w_out=v_w_out)
    weights = {n: given[n] for n in TWIN_WEIGHTS}
    shared = {n: given[n] for n in SHARED_INPUTS}
    per_example = {n: given[n] for n in ['x']}
    grad_fn = _jax.value_and_grad(_loss, argnums=(0, 1))

    def one_microbatch(ex, loss_target):
        ex = dict(ex)
        diff = ex.pop(TWIN_DIFF_INPUT)
        return grad_fn(weights, diff, {**shared, **ex}, loss_target)

    if N_MICROBATCH == 1:
        loss, (grad_w, grad_x) = one_microbatch(per_example, given["loss_target"])
    else:
        def body(carry, xs):
            loss_sum, grad_sum = carry
            l_k, (gw_k, gx_k) = one_microbatch(xs[0], xs[1])
            with _jax.named_scope("update"):
                return (loss_sum + l_k, _jax.tree.map(_jnp.add, grad_sum, gw_k)), gx_k

        init = (_jnp.zeros((), _jnp.float32), _jax.tree.map(_jnp.zeros_like, weights))
        (loss, grad_w), grad_x = _jax.lax.scan(body, init, (per_example, given["loss_target"]))
    with _jax.named_scope("update"):
        delta_w, new_m, new_v = {}, {}, {}
        for n in TWIN_WEIGHTS:
            delta_w[n], new_m[n], new_v[n] = _adamw(weights[n], grad_w[n], given["m_" + n], given["v_" + n])
    return (loss, grad_x, *[grad_w[n] for n in TWIN_WEIGHTS], *[delta_w[n] for n in TWIN_WEIGHTS],
            *[new_m[n] for n in TWIN_WEIGHTS], *[new_v[n] for n in TWIN_WEIGHTS])
```

```python
import functools

import jax
import jax.numpy as jnp
from jax import lax
from jax.experimental import pallas as pl
from jax.experimental.pallas import tpu as pltpu

F32 = jnp.float32
BF16 = jnp.bfloat16
MESH_ID = pl.DeviceIdType.MESH
AXES = ("x", "y", "c")

N_DEV = 8
HEAD_DIM = 128
CHUNK = 64
LEFT_CHUNKS = 8
BAND_W = (LEFT_CHUNKS + 1) * CHUNK
PAD_K = LEFT_CHUNKS * CHUNK
REL_CLIP = 256
N_REL = REL_CLIP + CHUNK
NORM_EPS = 1e-6
NEG_BIG = -1e30
SB_TILE = 128
LANES = 128

ADAM_LR = 0.001
ADAM_B1 = 0.9
ADAM_B2 = 0.999
ADAM_EPS = 1e-08
ADAM_WD = 0.01
ADAM_STEP = 10

VMEM_LIMIT_BYTES = 56 * 1024 * 1024

NT = (((1,), (1,)), ((), ()))
TN = (((0,), (0,)), ((), ()))


def _params(*sem, **kw):
    return pltpu.CompilerParams(dimension_semantics=sem or None, vmem_limit_bytes=VMEM_LIMIT_BYTES, **kw)


def _any():
    return pl.BlockSpec(memory_space=pl.ANY)


def _vmem():
    return pl.BlockSpec(memory_space=pltpu.VMEM)


def _tile(n, want):
    return want if n % want == 0 else n


def _dot(a, b, dims=None):
    if dims is None:
        return jnp.dot(a, b, preferred_element_type=F32)
    return lax.dot_general(a, b, dims, preferred_element_type=F32)


def _split_dot(a, b, parts, dims=None):
    acc = None
    rest = a
    for _ in range(parts):
        piece = rest.astype(BF16)
        rest = rest - piece.astype(F32)
        term = _dot(piece, b, dims)
        acc = term if acc is None else acc + term
    return acc


def _log_sigmoid(z):
    return jnp.minimum(z, 0.0) - jnp.log(1.0 + jnp.exp(-jnp.abs(z)))


def _silu_and_grad(g):
    sig = jax.nn.sigmoid(g)
    return g * sig, sig * (1.0 + g * (1.0 - sig))


def _my_place():
    return lax.axis_index("x"), lax.axis_index("y"), lax.axis_index("c")


def _flat(place):
    return 4 * place[0] + 2 * place[1] + place[2]


def _flip(place, k):
    return tuple(1 - p if (k >> s) & 1 else p for p, s in zip(place, (2, 1, 0)))


def _cast_bf16(a, name):
    rows, cols = a.shape
    tr = _tile(rows, 1024)

    def body(a_ref, o_ref):
        o_ref[...] = a_ref[...].astype(BF16)

    return pl.pallas_call(
        body, name=name, grid=(rows // tr,),
        in_specs=[pl.BlockSpec((tr, cols), lambda i: (i, 0))],
        out_specs=pl.BlockSpec((tr, cols), lambda i: (i, 0)),
        out_shape=jax.ShapeDtypeStruct((rows, cols), BF16),
        compiler_params=_params("parallel"),
    )(a)


def _gather_layer(win_b, wout_b, layer):
    _, d, e = win_b.shape
    _, r, _ = wout_b.shape

    def body(win_ref, wout_ref, oin_ref, oout_ref, send_sems, recv_sems, local_sems):
        me = _my_place()
        x, y, c = me
        sibling = (x, y, 1 - c)
        chips = [(1 - x, y), (x, 1 - y), (1 - x, 1 - y)]
        shards = (win_ref.at[layer], wout_ref.at[layer])
        outs = (oin_ref, oout_ref)

        def copy(t, k, owner, to, from_shard=False):
            dst = outs[t].at[_flat(owner)]
            return pltpu.make_async_remote_copy(
                src_ref=shards[t] if from_shard else dst, dst_ref=dst,
                send_sem=send_sems.at[t, k], recv_sem=recv_sems.at[t, k],
                device_id=to, device_id_type=MESH_ID)

        mine = [pltpu.make_async_copy(shards[t], outs[t].at[_flat(me)], local_sems.at[t]) for t in range(2)]
        for cp in mine:
            cp.start()
        first = []
        for t in range(2):
            first.append(copy(t, 0, me, sibling, from_shard=True))
            first += [copy(t, 1 + j, me, (*chip, c), from_shard=True) for j, chip in enumerate(chips)]
        for cp in first:
            cp.start()
        passed = []
        for j, chip in enumerate(chips):
            for t in range(2):
                copy(t, 1 + j, (*chip, c), me).wait_recv()
                fwd = copy(t, 4 + j, (*chip, c), sibling)
                fwd.start()
                passed.append(fwd)
        for t in range(2):
            copy(t, 0, sibling, me).wait_recv()
            for j, chip in enumerate(chips):
                copy(t, 4 + j, (*chip, 1 - c), me).wait_recv()
        for cp in first + passed:
            cp.wait_send()
        for cp in mine:
            cp.wait()

    return pl.pallas_call(
        body, name=f"gather_weights_l{layer}",
        in_specs=[_any(), _any()], out_specs=(_any(), _any()),
        out_shape=(jax.ShapeDtypeStruct((N_DEV, d, e), BF16), jax.ShapeDtypeStruct((N_DEV, r, d), BF16)),
        scratch_shapes=[pltpu.SemaphoreType.DMA((2, 7)), pltpu.SemaphoreType.DMA((2, 7)),
                        pltpu.SemaphoreType.DMA((2,))],
        compiler_params=_params(has_side_effects=True),
    )(win_b, wout_b)


def _exchange_layer(gwin, gwout, layer):
    def body(gin_ref, gout_ref, rin_ref, rout_ref, send_sems, recv_sems, local_sems):
        me = _my_place()
        srcs = (gin_ref, gout_ref)
        dsts = (rin_ref, rout_ref)
        mine = [pltpu.make_async_copy(srcs[t].at[_flat(me)], dsts[t].at[_flat(me)], local_sems.at[t])
                for t in range(2)]
        for cp in mine:
            cp.start()
        copies = []
        for k in range(1, N_DEV):
            peer = _flip(me, k)
            for t in range(2):
                copies.append(pltpu.make_async_remote_copy(
                    src_ref=srcs[t].at[_flat(peer)], dst_ref=dsts[t].at[_flat(me)],
                    send_sem=send_sems.at[t, k - 1], recv_sem=recv_sems.at[t, k - 1],
                    device_id=peer, device_id_type=MESH_ID))
        for cp in copies:
            cp.start()
        for cp in copies:
            cp.wait()
        for cp in mine:
            cp.wait()

    return pl.pallas_call(
        body, name=f"exchange_grads_l{layer}",
        in_specs=[_any(), _any()], out_specs=(_any(), _any()),
        out_shape=(jax.ShapeDtypeStruct(gwin.shape, BF16), jax.ShapeDtypeStruct(gwout.shape, BF16)),
        scratch_shapes=[pltpu.SemaphoreType.DMA((2, 7)), pltpu.SemaphoreType.DMA((2, 7)),
                        pltpu.SemaphoreType.DMA((2,))],
        compiler_params=_params(has_side_effects=True),
    )(gwin, gwout)


def _gather_small(v, name):
    def body(v_ref, o_ref, send_sems, recv_sems):
        me = _my_place()
        o_ref[_flat(me)] = v_ref[...]
        copies = []
        for k in range(1, N_DEV):
            copies.append(pltpu.make_async_remote_copy(
                src_ref=v_ref, dst_ref=o_ref.at[_flat(me)],
                send_sem=send_sems.at[k - 1], recv_sem=recv_sems.at[k - 1],
                device_id=_flip(me, k), device_id_type=MESH_ID))
        for cp in copies:
            cp.start()
        for cp in copies:
            cp.wait()

    return pl.pallas_call(
        body, name=name,
        in_specs=[_vmem()], out_specs=_vmem(),
        out_shape=jax.ShapeDtypeStruct((N_DEV,) + v.shape, v.dtype),
        scratch_shapes=[pltpu.SemaphoreType.DMA((7,)), pltpu.SemaphoreType.DMA((7,))],
        compiler_params=_params(has_side_effects=True),
    )(v)


def _rel_onehot(i):
    r_io = lax.broadcasted_iota(jnp.int32, (N_REL, BAND_W), 0)
    p_io = lax.broadcasted_iota(jnp.int32, (N_REL, BAND_W), 1)
    idx = jnp.clip(PAD_K + i - p_io, -(CHUNK - 1), REL_CLIP) + (CHUNK - 1)
    return (r_io == idx).astype(BF16)


def _bias_expand(rel):
    lh = rel.shape[0]

    def body(rel_ref, o_ref):
        o_ref[...] = _split_dot(rel_ref[...], _rel_onehot(pl.program_id(0)), 3)

    return pl.pallas_call(
        body, name="bias_expand", grid=(CHUNK,),
        in_specs=[pl.BlockSpec((lh, N_REL), lambda i: (0, 0))],
        out_specs=pl.BlockSpec((None, lh, BAND_W), lambda i: (i, 0, 0)),
        out_shape=jax.ShapeDtypeStruct((CHUNK, lh, BAND_W), F32),
        compiler_params=_params("parallel"),
    )(rel)


def _bias_grad(dbias):
    lh = dbias.shape[1]

    def body(db_ref, o_ref):
        i = pl.program_id(0)

        @pl.when(i == 0)
        def _():
            o_ref[...] = jnp.zeros_like(o_ref)

        o_ref[...] += _split_dot(db_ref[...], _rel_onehot(i), 3, NT)

    return pl.pallas_call(
        body, name="bias_grad", grid=(CHUNK,),
        in_specs=[pl.BlockSpec((None, lh, BAND_W), lambda i: (i, 0, 0))],
        out_specs=pl.BlockSpec((lh, N_REL), lambda i: (0, 0)),
        out_shape=jax.ShapeDtypeStruct((lh, N_REL), F32),
        compiler_params=_params("arbitrary"),
    )(dbias)


def _norm_proj(x, g, w_all, layer):
    s, d = x.shape
    e = w_all.shape[2]
    tm, tn = _tile(s, 512), _tile(e, 1024)
    nb = e // tn

    def body(x_ref, g_ref, w_ref, proj_ref, h_ref):
        @pl.when(pl.program_id(1) == 0)
        def _():
            xv = x_ref[...]
            r = lax.rsqrt(jnp.mean(xv * xv, axis=-1, keepdims=True) + NORM_EPS)
            h_ref[...] = ((xv * r) * g_ref[...]).astype(BF16)

        proj_ref[...] = _dot(h_ref[...], w_ref[...])

    return pl.pallas_call(
        body, name=f"norm_proj_l{layer}", grid=(s // tm, N_DEV * nb),
        in_specs=[pl.BlockSpec((tm, d), lambda m, n: (m, 0)),
                  pl.BlockSpec((1, d), lambda m, n: (0, 0)),
                  pl.BlockSpec((None, d, tn), lambda m, n: (n // nb, 0, n % nb))],
        out_specs=(pl.BlockSpec((None, tm, tn), lambda m, n: (n // nb, m, n % nb)),
                   pl.BlockSpec((tm, d), lambda m, n: (m, 0))),
        out_shape=(jax.ShapeDtypeStruct((N_DEV, s, e), F32), jax.ShapeDtypeStruct((s, d), BF16)),
        compiler_params=_params("parallel", "arbitrary"),
    )(x, g, w_all)


def _sb_forward(proj, layer):
    _, s, e = proj.shape
    heads = e // HEAD_DIM
    n_blk = s // SB_TILE
    scale = HEAD_DIM ** -0.5

    def body(p_ref, y_ref, mix_ref, tot_ref, kb_ref, vb_ref):
        kb_ref[...] = p_ref[1].astype(BF16)
        vb_ref[...] = p_ref[2].astype(BF16)
        row = lax.broadcasted_iota(jnp.int32, (SB_TILE, SB_TILE), 0)
        col = lax.broadcasted_iota(jnp.int32, (SB_TILE, SB_TILE), 1)
        causal = col < row
        later = (row > col).astype(BF16)

        def q_block(qi, _):
            t0 = pl.multiple_of(qi * SB_TILE, SB_TILE)
            qb = p_ref[0, pl.ds(t0, SB_TILE), :].astype(BF16)

            def tile(s0, carry, acc, diag):
                z = _dot(qb, kb_ref[pl.ds(s0, SB_TILE), :], NT) * scale
                ls = _log_sigmoid(z)
                stay = ls - z
                if diag:
                    stay = jnp.where(causal, stay, 0.0)
                w = jnp.exp(ls + carry + _split_dot(stay, later, 2))
                if diag:
                    w = jnp.where(causal, w, 0.0)
                acc = acc + _dot(w.astype(BF16), vb_ref[pl.ds(s0, SB_TILE), :])
                return carry + jnp.sum(stay, axis=1, keepdims=True), acc

            carry, acc = tile(t0, jnp.zeros((SB_TILE, 1), F32), jnp.zeros((SB_TILE, HEAD_DIM), F32), True)

            def k_block(j, state):
                s0 = pl.multiple_of((qi - 1 - j) * SB_TILE, SB_TILE)
                return tile(s0, state[0], state[1], False)

            carry, acc = lax.fori_loop(0, qi, k_block, (carry, acc))
            y_ref[pl.ds(t0, SB_TILE), :] = acc
            silu, _ = _silu_and_grad(p_ref[3, pl.ds(t0, SB_TILE), :])
            mix_ref[pl.ds(t0, SB_TILE), :] = (acc * silu).astype(BF16)
            tot_ref[pl.ds(t0, SB_TILE), :] = carry
            return 0

        lax.fori_loop(0, n_blk, q_block, 0)

    return pl.pallas_call(
        body, name=f"sb_forward_l{layer}", grid=(heads,),
        in_specs=[pl.BlockSpec((4, s, HEAD_DIM), lambda h: (0, 0, h))],
        out_specs=(pl.BlockSpec((s, HEAD_DIM), lambda h: (0, h)),
                   pl.BlockSpec((s, HEAD_DIM), lambda h: (0, h)),
                   pl.BlockSpec((None, s, 1), lambda h: (h, 0, 0))),
        out_shape=(jax.ShapeDtypeStruct((s, 2 * e), F32), jax.ShapeDtypeStruct((s, 2 * e), BF16),
                   jax.ShapeDtypeStruct((heads, s, 1), F32)),
        scratch_shapes=[pltpu.VMEM((s, HEAD_DIM), BF16), pltpu.VMEM((s, HEAD_DIM), BF16)],
        compiler_params=_params("parallel"),
    )(proj)


def _qk_norm(t, gain):
    r = lax.rsqrt(jnp.mean(t * t, axis=-1, keepdims=True) + NORM_EPS)
    return t * r, r, (t * r) * gain


def _chunk_scores(qc, kw, bias, t0, scale):
    sc = _dot(qc, kw, NT) * scale + bias
    col = lax.broadcasted_iota(jnp.int32, (CHUNK, BAND_W), 1)
    sc = jnp.where(col + t0 >= PAD_K, sc, NEG_BIG)
    ex = jnp.exp(sc - jnp.max(sc, axis=-1, keepdims=True))
    return ex / jnp.sum(ex, axis=-1, keepdims=True)


def _chunk_forward(proj, bias, q_gain, k_gain, y, mixed, layer):
    _, s, e = proj.shape
    heads = e // HEAD_DIM
    scale = HEAD_DIM ** -0.5

    def body(p_ref, b_ref, qg_ref, kg_ref, y_in, mix_in, y_ref, mix_ref, qn_ref, kp_ref, vp_ref):
        del y_in, mix_in
        qn_ref[...] = _qk_norm(p_ref[0], qg_ref[...])[2].astype(BF16)
        kp_ref[pl.ds(0, PAD_K), :] = jnp.zeros((PAD_K, HEAD_DIM), BF16)
        vp_ref[pl.ds(0, PAD_K), :] = jnp.zeros((PAD_K, HEAD_DIM), BF16)
        kp_ref[pl.ds(PAD_K, s), :] = _qk_norm(p_ref[1], kg_ref[...])[2].astype(BF16)
        vp_ref[pl.ds(PAD_K, s), :] = p_ref[2].astype(BF16)

        def chunk(ci, _):
            t0 = pl.multiple_of(ci * CHUNK, CHUNK)
            probs = _chunk_scores(qn_ref[pl.ds(t0, CHUNK), :], kp_ref[pl.ds(t0, BAND_W), :], b_ref[...], t0, scale)
            out = _dot(probs.astype(BF16), vp_ref[pl.ds(t0, BAND_W), :])
            y_ref[pl.ds(t0, CHUNK), :] = out
            silu, _ = _silu_and_grad(p_ref[3, pl.ds(t0, CHUNK), :])
            mix_ref[pl.ds(t0, CHUNK), :] = (out * silu).astype(BF16)
            return 0

        lax.fori_loop(0, s // CHUNK, chunk, 0)

    return pl.pallas_call(
        body, name=f"chunk_forward_l{layer}", grid=(heads,),
        in_specs=[pl.BlockSpec((4, s, HEAD_DIM), lambda h: (1, 0, h)),
                  pl.BlockSpec((None, CHUNK, BAND_W), lambda h: (layer * heads + h, 0, 0)),
                  pl.BlockSpec((1, HEAD_DIM), lambda h: (0, 0)),
                  pl.BlockSpec((1, HEAD_DIM), lambda h: (0, 0)),
                  _any(), _any()],
        out_specs=(pl.BlockSpec((s, HEAD_DIM), lambda h: (0, heads + h)),
                   pl.BlockSpec((s, HEAD_DIM), lambda h: (0, heads + h))),
        out_shape=(jax.ShapeDtypeStruct(y.shape, F32), jax.ShapeDtypeStruct(mixed.shape, BF16)),
        input_output_aliases={4: 0, 5: 1},
        scratch_shapes=[pltpu.VMEM((s, HEAD_DIM), BF16), pltpu.VMEM((s + PAD_K, HEAD_DIM), BF16),
                        pltpu.VMEM((s + PAD_K, HEAD_DIM), BF16)],
        compiler_params=_params("parallel"),
    )(proj, bias, q_gain, k_gain, y, mixed)


def _out_proj(mixed, w, x, layer):
    s, d = x.shape
    tm, tn = _tile(s, 512), _tile(d, 1024)

    def body(a_ref, w_ref, x_ref, o_ref):
        o_ref[...] = x_ref[...] + _dot(a_ref[...], w_ref[...])

    return pl.pallas_call(
        body, name=f"out_proj_l{layer}", grid=(s // tm, d // tn),
        in_specs=[pl.BlockSpec((tm, d), lambda m, n: (m, 0)),
                  pl.BlockSpec((d, tn), lambda m, n: (0, n)),
                  pl.BlockSpec((tm, tn), lambda m, n: (m, n))],
        out_specs=pl.BlockSpec((tm, tn), lambda m, n: (m, n)),
        out_shape=jax.ShapeDtypeStruct((s, d), F32),
        compiler_params=_params("parallel", "parallel"),
    )(mixed, w, x)


def _loss_head(y, target):
    s, d = y.shape
    tm = _tile(s, 256)

    def body(y_ref, t_ref, dy_ref, part_ref):
        diff = y_ref[...] - t_ref[...]
        dy_ref[...] = diff * (1.0 / d)
        sq = (diff * diff).reshape(tm // 8, 8, d).sum(axis=0)
        acc = sq[:, 0:LANES]
        for j in range(1, d // LANES):
            acc = acc + sq[:, j * LANES:(j + 1) * LANES]
        part_ref[...] = acc * (0.5 / d)

    return pl.pallas_call(
        body, name="loss_head", grid=(s // tm,),
        in_specs=[pl.BlockSpec((tm, d), lambda i: (i, 0)), pl.BlockSpec((tm, d), lambda i: (i, 0))],
        out_specs=(pl.BlockSpec((tm, d), lambda i: (i, 0)), pl.BlockSpec((None, 8, LANES), lambda i: (i, 0, 0))),
        out_shape=(jax.ShapeDtypeStruct((s, d), F32), jax.ShapeDtypeStruct((s // tm, 8, LANES), F32)),
        compiler_params=_params("parallel"),
    )(y, target)


def _out_proj_bwd_input(dx, w, layer):
    s, d = dx.shape
    tm, tn = _tile(s, 512), _tile(d, 1024)

    def body(dx_ref, w_ref, o_ref):
        o_ref[...] = _dot(dx_ref[...].astype(BF16), w_ref[...], NT)

    return pl.pallas_call(
        body, name=f"out_proj_dx_l{layer}", grid=(s // tm, d // tn),
        in_specs=[pl.BlockSpec((tm, d), lambda m, n: (m, 0)), pl.BlockSpec((tn, d), lambda m, n: (n, 0))],
        out_specs=pl.BlockSpec((tm, tn), lambda m, n: (m, n)),
        out_shape=jax.ShapeDtypeStruct((s, d), F32),
        compiler_params=_params("parallel", "parallel"),
    )(dx, w)


def _out_proj_bwd_weight(mixed, dx, layer):
    s, d = dx.shape
    te, tn = _tile(d, 512), _tile(d, 1024)

    def body(a_ref, dx_ref, o_ref):
        o_ref[...] = _dot(a_ref[...], dx_ref[...].astype(BF16), TN).astype(BF16)

    return pl.pallas_call(
        body, name=f"out_proj_dw_l{layer}", grid=(d // te, d // tn),
        in_specs=[pl.BlockSpec((s, te), lambda i, n: (0, i)), pl.BlockSpec((s, tn), lambda i, n: (0, n))],
        out_specs=pl.BlockSpec((te, tn), lambda i, n: (i, n)),
        out_shape=jax.ShapeDtypeStruct((d, d), BF16),
        compiler_params=_params("parallel", "parallel"),
    )(mixed, dx)


def _sb_backward(proj, y, dmixed, tot, layer):
    _, s, e = proj.shape
    heads = e // HEAD_DIM
    n_blk = s // SB_TILE
    scale = HEAD_DIM ** -0.5

    def body(p_ref, y_ref, dm_ref, tot_ref, o_ref, kb_ref, vb_ref, do_ref, dk_ref, dv_ref):
        kb_ref[...] = p_ref[1].astype(BF16)
        vb_ref[...] = p_ref[2].astype(BF16)
        silu, dsilu = _silu_and_grad(p_ref[3])
        dm = dm_ref[...]
        do_ref[...] = (dm * silu).astype(BF16)
        o_ref[3] = (dm * y_ref[...] * dsilu).astype(BF16)
        dk_ref[...] = jnp.zeros_like(dk_ref)
        dv_ref[...] = jnp.zeros_like(dv_ref)
        row = lax.broadcasted_iota(jnp.int32, (SB_TILE, SB_TILE), 0)
        col = lax.broadcasted_iota(jnp.int32, (SB_TILE, SB_TILE), 1)
        causal = col < row
        upto = (row <= col).astype(BF16)
        before = (row < col).astype(BF16)

        def q_block(qi, _):
            t0 = pl.multiple_of(qi * SB_TILE, SB_TILE)
            qb = p_ref[0, pl.ds(t0, SB_TILE), :].astype(BF16)
            dob = do_ref[pl.ds(t0, SB_TILE), :]
            total = tot_ref[pl.ds(t0, SB_TILE), :]

            def tile(s0, stay_sum, dlw_sum, dq, diag):
                kt = kb_ref[pl.ds(s0, SB_TILE), :]
                vt = vb_ref[pl.ds(s0, SB_TILE), :]
                z = _dot(qb, kt, NT) * scale
                ls = _log_sigmoid(z)
                stay = ls - z
                if diag:
                    stay = jnp.where(causal, stay, 0.0)
                after = total - (stay_sum + _split_dot(stay, upto, 3))
                w = jnp.exp(ls + after)
                if diag:
                    w = jnp.where(causal, w, 0.0)
                dlw = _dot(dob, vt, NT) * w
                prior = dlw_sum + _split_dot(dlw, before, 2)
                sig = jnp.exp(ls)
                dz = (dlw * (1.0 - sig) - sig * prior) * scale
                if diag:
                    dz = jnp.where(causal, dz, 0.0)
                dzb = dz.astype(BF16)
                dq = dq + _dot(dzb, kt)
                dk_ref[pl.ds(s0, SB_TILE), :] += _dot(dzb, qb, TN)
                dv_ref[pl.ds(s0, SB_TILE), :] += _dot(w.astype(BF16), dob, TN)
                return (stay_sum + jnp.sum(stay, axis=1, keepdims=True),
                        dlw_sum + jnp.sum(dlw, axis=1, keepdims=True), dq)

            def k_block(j, state):
                return tile(pl.multiple_of(j * SB_TILE, SB_TILE), *state, False)

            zero = jnp.zeros((SB_TILE, 1), F32)
            state = lax.fori_loop(0, qi, k_block, (zero, zero, jnp.zeros((SB_TILE, HEAD_DIM), F32)))
            _, _, dq = tile(t0, *state, True)
            o_ref[0, pl.ds(t0, SB_TILE), :] = dq.astype(BF16)
            return 0

        lax.fori_loop(0, n_blk, q_block, 0)
        o_ref[1] = dk_ref[...].astype(BF16)
        o_ref[2] = dv_ref[...].astype(BF16)

    return pl.pallas_call(
        body, name=f"sb_backward_l{layer}", grid=(heads,),
        in_specs=[pl.BlockSpec((4, s, HEAD_DIM), lambda h: (0, 0, h)),
                  pl.BlockSpec((s, HEAD_DIM), lambda h: (0, h)),
                  pl.BlockSpec((s, HEAD_DIM), lambda h: (0, h)),
                  pl.BlockSpec((None, s, 1), lambda h: (h, 0, 0))],
        out_specs=pl.BlockSpec((4, s, HEAD_DIM), lambda h: (0, 0, h)),
        out_shape=jax.ShapeDtypeStruct((N_DEV, s, e), BF16),
        scratch_shapes=[pltpu.VMEM((s, HEAD_DIM), BF16), pltpu.VMEM((s, HEAD_DIM), BF16),
                        pltpu.VMEM((s, HEAD_DIM), BF16), pltpu.VMEM((s, HEAD_DIM), F32),
                        pltpu.VMEM((s, HEAD_DIM), F32)],
        compiler_params=_params("parallel"),
    )(proj, y, dmixed, tot)


def _norm_bwd(dn, xh, r, gain):
    dxh = dn * gain
    return r * (dxh - xh * jnp.mean(dxh * xh, axis=-1, keepdims=True)), dn * xh


def _chunk_backward(proj, bias, q_gain, k_gain, y, dmixed, dproj, layer):
    _, s, e = proj.shape
    heads = e // HEAD_DIM
    scale = HEAD_DIM ** -0.5

    def body(p_ref, b_ref, qg_ref, kg_ref, y_ref, dm_ref, dp_in, o_ref, db_ref, dqg_ref, dkg_ref,
             qn_ref, kp_ref, vp_ref, do_ref, dqn_ref, dkn_ref, dvp_ref):
        del dp_in
        qh, rq, qn = _qk_norm(p_ref[0], qg_ref[...])
        kh, rk, kn = _qk_norm(p_ref[1], kg_ref[...])
        qn_ref[...] = qn.astype(BF16)
        kp_ref[pl.ds(0, PAD_K), :] = jnp.zeros((PAD_K, HEAD_DIM), BF16)
        vp_ref[pl.ds(0, PAD_K), :] = jnp.zeros((PAD_K, HEAD_DIM), BF16)
        kp_ref[pl.ds(PAD_K, s), :] = kn.astype(BF16)
        vp_ref[pl.ds(PAD_K, s), :] = p_ref[2].astype(BF16)
        silu, dsilu = _silu_and_grad(p_ref[3])
        dm = dm_ref[...]
        do_ref[...] = (dm * silu).astype(BF16)
        o_ref[3] = (dm * y_ref[...] * dsilu).astype(BF16)
        dkn_ref[...] = jnp.zeros_like(dkn_ref)
        dvp_ref[...] = jnp.zeros_like(dvp_ref)
        db_ref[...] = jnp.zeros_like(db_ref)

        def chunk(ci, _):
            t0 = pl.multiple_of(ci * CHUNK, CHUNK)
            qc = qn_ref[pl.ds(t0, CHUNK), :]
            kw = kp_ref[pl.ds(t0, BAND_W), :]
            vw = vp_ref[pl.ds(t0, BAND_W), :]
            dob = do_ref[pl.ds(t0, CHUNK), :]
            probs = _chunk_scores(qc, kw, b_ref[...], t0, scale)
            dprobs = _dot(dob, vw, NT)
            dsc = probs * (dprobs - jnp.sum(probs * dprobs, axis=-1, keepdims=True))
            db_ref[...] += dsc
            dsb = (dsc * scale).astype(BF16)
            dqn_ref[pl.ds(t0, CHUNK), :] = _dot(dsb, kw)
            dkn_ref[pl.ds(t0, BAND_W), :] += _dot(dsb, qc, TN)
            dvp_ref[pl.ds(t0, BAND_W), :] += _dot(probs.astype(BF16), dob, TN)
            return 0

        lax.fori_loop(0, s // CHUNK, chunk, 0)
        dq, dqg_rows = _norm_bwd(dqn_ref[...], qh, rq, qg_ref[...])
        dk, dkg_rows = _norm_bwd(dkn_ref[pl.ds(PAD_K, s), :], kh, rk, kg_ref[...])
        o_ref[0] = dq.astype(BF16)
        o_ref[1] = dk.astype(BF16)
        o_ref[2] = dvp_ref[pl.ds(PAD_K, s), :].astype(BF16)

        @pl.when(pl.program_id(0) == 0)
        def _():
            dqg_ref[...] = jnp.zeros_like(dqg_ref)
            dkg_ref[...] = jnp.zeros_like(dkg_ref)

        dqg_ref[...] += jnp.sum(dqg_rows, axis=0, keepdims=True)
        dkg_ref[...] += jnp.sum(dkg_rows, axis=0, keepdims=True)

    return pl.pallas_call(
        body, name=f"chunk_backward_l{layer}", grid=(heads,),
        in_specs=[pl.BlockSpec((4, s, HEAD_DIM), lambda h: (1, 0, h)),
                  pl.BlockSpec((None, CHUNK, BAND_W), lambda h: (layer * heads + h, 0, 0)),
                  pl.BlockSpec((1, HEAD_DIM), lambda h: (0, 0)),
                  pl.BlockSpec((1, HEAD_DIM), lambda h: (0, 0)),
                  pl.BlockSpec((s, HEAD_DIM), lambda h: (0, heads + h)),
                  pl.BlockSpec((s, HEAD_DIM), lambda h: (0, heads + h)),
                  _any()],
        out_specs=(pl.BlockSpec((4, s, HEAD_DIM), lambda h: (1, 0, h)),
                   pl.BlockSpec((None, CHUNK, BAND_W), lambda h: (h, 0, 0)),
                   pl.BlockSpec((1, HEAD_DIM), lambda h: (0, 0)),
                   pl.BlockSpec((1, HEAD_DIM), lambda h: (0, 0))),
        out_shape=(jax.ShapeDtypeStruct(dproj.shape, BF16),
                   jax.ShapeDtypeStruct((heads, CHUNK, BAND_W), F32),
                   jax.ShapeDtypeStruct((1, HEAD_DIM), F32), jax.ShapeDtypeStruct((1, HEAD_DIM), F32)),
        input_output_aliases={6: 0},
        scratch_shapes=[pltpu.VMEM((s, HEAD_DIM), BF16), pltpu.VMEM((s + PAD_K, HEAD_DIM), BF16),
                        pltpu.VMEM((s + PAD_K, HEAD_DIM), BF16), pltpu.VMEM((s, HEAD_DIM), BF16),
                        pltpu.VMEM((s, HEAD_DIM), F32), pltpu.VMEM((s + PAD_K, HEAD_DIM), F32),
                        pltpu.VMEM((s + PAD_K, HEAD_DIM), F32)],
        compiler_params=_params("arbitrary"),
    )(proj, bias, q_gain, k_gain, y, dmixed, dproj)


def _proj_bwd_input(dproj, w_all, x, g, dx, layer):
    s, d = x.shape
    e = w_all.shape[2]
    tm = _tile(s, 512)

    def body(dp_ref, w_ref, x_ref, g_ref, dx_ref, o_ref, dg_ref, acc_ref):
        j = pl.program_id(1)

        @pl.when(j == 0)
        def _():
            acc_ref[...] = jnp.zeros_like(acc_ref)

        acc_ref[...] += _dot(dp_ref[...], w_ref[...], NT)

        @pl.when(jnp.logical_and(j == N_DEV - 1, pl.program_id(0) == 0))
        def _():
            dg_ref[...] = jnp.zeros_like(dg_ref)

        @pl.when(j == N_DEV - 1)
        def _():
            xv = x_ref[...]
            r = lax.rsqrt(jnp.mean(xv * xv, axis=-1, keepdims=True) + NORM_EPS)
            dxn, dg_rows = _norm_bwd(acc_ref[...], xv * r, r, g_ref[...])
            o_ref[...] = dx_ref[...] + dxn
            dg_ref[...] += jnp.sum(dg_rows, axis=0, keepdims=True)

    return pl.pallas_call(
        body, name=f"proj_dx_l{layer}", grid=(s // tm, N_DEV),
        in_specs=[pl.BlockSpec((None, tm, e), lambda m, j: (j, m, 0)),
                  pl.BlockSpec((None, d, e), lambda m, j: (j, 0, 0)),
                  pl.BlockSpec((tm, d), lambda m, j: (m, 0)),
                  pl.BlockSpec((1, d), lambda m, j: (0, 0)),
                  pl.BlockSpec((tm, d), lambda m, j: (m, 0))],
        out_specs=(pl.BlockSpec((tm, d), lambda m, j: (m, 0)), pl.BlockSpec((1, d), lambda m, j: (0, 0))),
        out_shape=(jax.ShapeDtypeStruct((s, d), F32), jax.ShapeDtypeStruct((1, d), F32)),
        scratch_shapes=[pltpu.VMEM((tm, d), F32)],
        compiler_params=_params("arbitrary", "arbitrary"),
    )(dproj, w_all, x, g, dx)


def _proj_bwd_weight(h, dproj, layer):
    s, d = h.shape
    e = dproj.shape[2]
    td, tn = _tile(d, 1024), _tile(e, 1024)
    nb = e // tn

    def body(h_ref, dp_ref, o_ref):
        o_ref[...] = _dot(h_ref[...], dp_ref[...], TN).astype(BF16)

    return pl.pallas_call(
        body, name=f"proj_dw_l{layer}", grid=(d // td, N_DEV * nb),
        in_specs=[pl.BlockSpec((s, td), lambda i, n: (0, i)),
                  pl.BlockSpec((None, s, tn), lambda i, n: (n // nb, 0, n % nb))],
        out_specs=pl.BlockSpec((None, td, tn), lambda i, n: (n // nb, i, n % nb)),
        out_shape=jax.ShapeDtypeStruct((N_DEV, d, e), BF16),
        compiler_params=_params("parallel", "parallel"),
    )(h, dproj)


def _adamw_math(w, g, m, v):
    m = ADAM_B1 * m + (1.0 - ADAM_B1) * g
    v = ADAM_B2 * v + (1.0 - ADAM_B2) * (g * g)
    m_hat = m / (1.0 - ADAM_B1 ** ADAM_STEP)
    v_hat = v / (1.0 - ADAM_B2 ** ADAM_STEP)
    return -ADAM_LR * (m_hat / (jnp.sqrt(v_hat) + ADAM_EPS) + ADAM_WD * w), m, v


def _adamw_layer(parts, w, m, v, prev, layer, name):
    n_layers, rows, cols = w.shape
    tr = _tile(rows, max(8, (256 * 1024) // cols))

    def body(p_ref, w_ref, m_ref, v_ref, *rest):
        g_ref, d_ref, nm_ref, nv_ref = rest[-4:]
        g = p_ref[0].astype(F32)
        for j in range(1, N_DEV):
            g = g + p_ref[j].astype(F32)
        g_ref[...] = g
        d_ref[...], nm_ref[...], nv_ref[...] = _adamw_math(w_ref[...], g, m_ref[...], v_ref[...])

    blk = pl.BlockSpec((None, tr, cols), lambda i: (layer, i, 0))
    out_shape = tuple(jax.ShapeDtypeStruct(w.shape, F32) for _ in range(4))
    in_specs = [pl.BlockSpec((N_DEV, tr, cols), lambda i: (0, i, 0)), blk, blk, blk]
    args = [parts, w, m, v]
    aliases = {}
    if prev is not None:
        in_specs += [_any()] * 4
        args += list(prev)
        aliases = {4 + k: k for k in range(4)}
    return pl.pallas_call(
        body, name=f"{name}_l{layer}", grid=(rows // tr,),
        in_specs=in_specs, out_specs=(blk, blk, blk, blk), out_shape=out_shape,
        input_output_aliases=aliases,
        compiler_params=_params("parallel"),
    )(*args)


def _sum_slots(parts):
    def body(p_ref, o_ref):
        g = p_ref[0]
        for j in range(1, N_DEV):
            g = g + p_ref[j]
        o_ref[...] = g

    return pl.pallas_call(
        body, name="sum_small_grads",
        in_specs=[_vmem()], out_specs=_vmem(),
        out_shape=jax.ShapeDtypeStruct(parts.shape[1:], F32),
        compiler_params=_params(),
    )(parts)


def _adamw_small(w, g, m, v):
    def body(w_ref, g_ref, m_ref, v_ref, d_ref, nm_ref, nv_ref):
        d_ref[...], nm_ref[...], nv_ref[...] = _adamw_math(w_ref[...], g_ref[...], m_ref[...], v_ref[...])

    return pl.pallas_call(
        body, name="adamw_small",
        in_specs=[_vmem()] * 4, out_specs=(_vmem(),) * 3,
        out_shape=tuple(jax.ShapeDtypeStruct(w.shape, F32) for _ in range(3)),
        compiler_params=_params(),
    )(w, g, m, v)


def _pack_rows(arrays):
    rows = []
    for a in arrays:
        flat = a.reshape(-1)
        pad = (-flat.shape[0]) % (8 * LANES)
        rows.append(jnp.pad(flat, (0, pad)).reshape(-1, LANES))
    return jnp.concatenate(rows, axis=0)


def _unpack_rows(packed, like):
    out, r0 = [], 0
    for a in like:
        n = a.size
        nr = -(-n // (8 * LANES)) * 8
        out.append(packed[r0:r0 + nr].reshape(-1)[:n].reshape(a.shape))
        r0 += nr
    return out


def kernel(x, norm_g, w_in, q_norm_g, k_norm_g, rel_bias, w_out, loss_target, m_norm_g, m_w_in, m_q_norm_g, m_k_norm_g, m_rel_bias, m_w_out, v_norm_g, v_w_in, v_q_norm_g, v_k_norm_g, v_rel_bias, v_w_out):
    depth, d, e = w_in.shape
    r_out = w_out.shape[1]
    heads = e // HEAD_DIM
    rel_w = rel_bias.shape[2]
    x0 = x[0]
    target = loss_target[0]
    s = x0.shape[0]

    win_b = _cast_bf16(w_in.reshape(depth * d, e), "cast_w_in").reshape(depth, d, e)
    wout_b = _cast_bf16(w_out.reshape(depth * r_out, d), "cast_w_out").reshape(depth, r_out, d)
    rel_all = _gather_small(rel_bias, "gather_rel_bias")
    rel_full = jnp.transpose(rel_all, (1, 2, 0, 3)).reshape(depth * heads, N_DEV * rel_w)
    bias = jnp.transpose(_bias_expand(rel_full), (1, 0, 2))

    xs, hs, projs, ys, mixes, tots, weights = [], [], [], [], [], [], []
    xl = x0
    for l in range(depth):
        win_all, wout_all = _gather_layer(win_b, wout_b, l)
        wout_full = wout_all.reshape(d, d)
        proj, h = _norm_proj(xl, norm_g[l:l + 1], win_all, l)
        y, mixed, tot = _sb_forward(proj, l)
        y, mixed = _chunk_forward(proj, bias, q_norm_g[l:l + 1], k_norm_g[l:l + 1], y, mixed, l)
        xs.append(xl), hs.append(h), projs.append(proj), ys.append(y), mixes.append(mixed), tots.append(tot)
        weights.append((win_all, wout_full))
        xl = _out_proj(mixed, wout_full, xl, l)

    dx, loss_parts = _loss_head(xl, target)
    loss = lax.psum(jnp.sum(loss_parts), AXES)

    dbias, dng, dqg, dkg = [None] * depth, [None] * depth, [None] * depth, [None] * depth
    res_in, res_out = None, None
    for l in reversed(range(depth)):
        win_all, wout_full = weights[l]
        dmixed = _out_proj_bwd_input(dx, wout_full, l)
        gwout = _out_proj_bwd_weight(mixes[l], dx, l).reshape(N_DEV, r_out, d)
        dproj = _sb_backward(projs[l], ys[l], dmixed, tots[l], l)
        dproj, dbias[l], dqg[l], dkg[l] = _chunk_backward(
            projs[l], bias, q_norm_g[l:l + 1], k_norm_g[l:l + 1], ys[l], dmixed, dproj, l)
        dx, dng[l] = _proj_bwd_input(dproj, win_all, xs[l], norm_g[l:l + 1], dx, l)
        gwin = _proj_bwd_weight(hs[l], dproj, l)
        rin, rout = _exchange_layer(gwin, gwout, l)
        res_in = _adamw_layer(rin, w_in, m_w_in, v_w_in, res_in, l, "adamw_w_in")
        res_out = _adamw_layer(rout, w_out, m_w_out, v_w_out, res_out, l, "adamw_w_out")

    drel = _bias_grad(jnp.transpose(jnp.concatenate(dbias, axis=0), (1, 0, 2)))
    small_like = [norm_g, q_norm_g, k_norm_g, drel]
    mine = _pack_rows([jnp.concatenate(dng, axis=0), jnp.concatenate(dqg, axis=0),
                       jnp.concatenate(dkg, axis=0), drel])
    g_norm, g_qn, g_kn, g_rel_full = _unpack_rows(_sum_slots(_gather_small(mine, "gather_small_grads")), small_like)
    my_block = _flat(_my_place())
    g_rel = lax.dynamic_slice_in_dim(g_rel_full.reshape(depth, heads, N_REL), my_block * rel_w, rel_w, axis=2)
    small_w = [norm_g, q_norm_g, k_norm_g, rel_bias]
    small = _adamw_small(_pack_rows(small_w), _pack_rows([g_norm, g_qn, g_kn, g_rel]),
                         _pack_rows([m_norm_g, m_q_norm_g, m_k_norm_g, m_rel_bias]),
                         _pack_rows([v_norm_g, v_q_norm_g, v_k_norm_g, v_rel_bias]))
    d_small, nm_small, nv_small = (_unpack_rows(p, small_w) for p in small)

    g_win, d_win, nm_win, nv_win = res_in
    g_wout, d_wout, nm_wout, nv_wout = res_out
    grads = (g_norm, g_win, g_qn, g_kn, g_rel, g_wout)

    def order(sm, big_in, big_out):
        return (sm[0], big_in, sm[1], sm[2], sm[3], big_out)

    return (loss, dx[None], *grads, *order(d_small, d_win, d_wout),
            *order(nm_small, nm_win, nm_wout), *order(nv_small, nv_win, nv_wout))
```

```python
import functools

import jax
import jax.numpy as jnp
from jax import lax
from jax.experimental import pallas as pl
from jax.experimental.pallas import tpu as pltpu

F32 = jnp.float32
BF16 = jnp.bfloat16
MESH_ID = pl.DeviceIdType.MESH
AXES = ("x", "y", "c")

N_DEV = 8
HEAD_DIM = 128
CHUNK = 64
LEFT_CHUNKS = 8
BAND_W = (LEFT_CHUNKS + 1) * CHUNK
PAD_K = LEFT_CHUNKS * CHUNK
REL_CLIP = 256
N_REL = REL_CLIP + CHUNK
NORM_EPS = 1e-6
NEG_BIG = -1e30
SB_TILE = 128
SB_Q = 512
SB_K = 128
SB_HEADS = 2
LANES = 128

ADAM_LR = 0.001
ADAM_B1 = 0.9
ADAM_B2 = 0.999
ADAM_EPS = 1e-08
ADAM_WD = 0.01
ADAM_STEP = 10

VMEM_LIMIT_BYTES = 56 * 1024 * 1024

NT = (((1,), (1,)), ((), ()))
TN = (((0,), (0,)), ((), ()))


def _params(*sem, **kw):
    return pltpu.CompilerParams(dimension_semantics=sem or None, vmem_limit_bytes=VMEM_LIMIT_BYTES, **kw)


def _any():
    return pl.BlockSpec(memory_space=pl.ANY)


def _vmem():
    return pl.BlockSpec(memory_space=pltpu.VMEM)


def _tile(n, want):
    return want if n % want == 0 else n


def _dot(a, b, dims=None):
    if dims is None:
        return jnp.dot(a, b, preferred_element_type=F32)
    return lax.dot_general(a, b, dims, preferred_element_type=F32)


def _split_dot(a, b, parts, dims=None):
    acc = None
    rest = a
    for _ in range(parts):
        piece = rest.astype(BF16)
        rest = rest - piece.astype(F32)
        term = _dot(piece, b, dims)
        acc = term if acc is None else acc + term
    return acc


def _log_sigmoid(z):
    return jnp.minimum(z, 0.0) - jnp.log(1.0 + jnp.exp(-jnp.abs(z)))


def _silu_and_grad(g):
    sig = jax.nn.sigmoid(g)
    return g * sig, sig * (1.0 + g * (1.0 - sig))


def _my_place():
    return lax.axis_index("x"), lax.axis_index("y"), lax.axis_index("c")


def _flat(place):
    return 4 * place[0] + 2 * place[1] + place[2]


def _flip(place, k):
    return tuple(1 - p if (k >> s) & 1 else p for p, s in zip(place, (2, 1, 0)))


def _cast_bf16(a, name):
    rows, cols = a.shape
    tr = _tile(rows, 1024)

    def body(a_ref, o_ref):
        o_ref[...] = a_ref[...].astype(BF16)

    return pl.pallas_call(
        body, name=name, grid=(rows // tr,),
        in_specs=[pl.BlockSpec((tr, cols), lambda i: (i, 0))],
        out_specs=pl.BlockSpec((tr, cols), lambda i: (i, 0)),
        out_shape=jax.ShapeDtypeStruct((rows, cols), BF16),
        compiler_params=_params("parallel"),
    )(a)


def _gather_layer(win_b, wout_b, layer):
    _, d, e = win_b.shape
    _, r, _ = wout_b.shape

    def body(win_ref, wout_ref, oin_ref, oout_ref, send_sems, recv_sems, local_sems):
        me = _my_place()
        x, y, c = me
        sibling = (x, y, 1 - c)
        chips = [(1 - x, y), (x, 1 - y), (1 - x, 1 - y)]
        shards = (win_ref.at[layer], wout_ref.at[layer])
        outs = (oin_ref, oout_ref)

        def copy(t, k, owner, to, from_shard=False):
            dst = outs[t].at[_flat(owner)]
            return pltpu.make_async_remote_copy(
                src_ref=shards[t] if from_shard else dst, dst_ref=dst,
                send_sem=send_sems.at[t, k], recv_sem=recv_sems.at[t, k],
                device_id=to, device_id_type=MESH_ID)

        mine = [pltpu.make_async_copy(shards[t], outs[t].at[_flat(me)], local_sems.at[t]) for t in range(2)]
        for cp in mine:
            cp.start()
        first = []
        for t in range(2):
            first.append(copy(t, 0, me, sibling, from_shard=True))
            first += [copy(t, 1 + j, me, (*chip, c), from_shard=True) for j, chip in enumerate(chips)]
        for cp in first:
            cp.start()
        passed = []
        for j, chip in enumerate(chips):
            for t in range(2):
                copy(t, 1 + j, (*chip, c), me).wait_recv()
                fwd = copy(t, 4 + j, (*chip, c), sibling)
                fwd.start()
                passed.append(fwd)
        for t in range(2):
            copy(t, 0, sibling, me).wait_recv()
            for j, chip in enumerate(chips):
                copy(t, 4 + j, (*chip, 1 - c), me).wait_recv()
        for cp in first + passed:
            cp.wait_send()
        for cp in mine:
            cp.wait()

    return pl.pallas_call(
        body, name=f"gather_weights_l{layer}",
        in_specs=[_any(), _any()], out_specs=(_any(), _any()),
        out_shape=(jax.ShapeDtypeStruct((N_DEV, d, e), BF16), jax.ShapeDtypeStruct((N_DEV, r, d), BF16)),
        scratch_shapes=[pltpu.SemaphoreType.DMA((2, 7)), pltpu.SemaphoreType.DMA((2, 7)),
                        pltpu.SemaphoreType.DMA((2,))],
        compiler_params=_params(has_side_effects=True),
    )(win_b, wout_b)


def _exchange_layer(gwin, gwout, layer):
    def body(gin_ref, gout_ref, rin_ref, rout_ref, send_sems, recv_sems, local_sems):
        me = _my_place()
        srcs = (gin_ref, gout_ref)
        dsts = (rin_ref, rout_ref)
        mine = [pltpu.make_async_copy(srcs[t].at[_flat(me)], dsts[t].at[_flat(me)], local_sems.at[t])
                for t in range(2)]
        for cp in mine:
            cp.start()
        copies = []
        for k in range(1, N_DEV):
            peer = _flip(me, k)
            for t in range(2):
                copies.append(pltpu.make_async_remote_copy(
                    src_ref=srcs[t].at[_flat(peer)], dst_ref=dsts[t].at[_flat(me)],
                    send_sem=send_sems.at[t, k - 1], recv_sem=recv_sems.at[t, k - 1],
                    device_id=peer, device_id_type=MESH_ID))
        for cp in copies:
            cp.start()
        for cp in copies:
            cp.wait()
        for cp in mine:
            cp.wait()

    return pl.pallas_call(
        body, name=f"exchange_grads_l{layer}",
        in_specs=[_any(), _any()], out_specs=(_any(), _any()),
        out_shape=(jax.ShapeDtypeStruct(gwin.shape, BF16), jax.ShapeDtypeStruct(gwout.shape, BF16)),
        scratch_shapes=[pltpu.SemaphoreType.DMA((2, 7)), pltpu.SemaphoreType.DMA((2, 7)),
                        pltpu.SemaphoreType.DMA((2,))],
        compiler_params=_params(has_side_effects=True),
    )(gwin, gwout)


def _gather_small(v, name):
    def body(v_ref, o_ref, send_sems, recv_sems):
        me = _my_place()
        o_ref[_flat(me)] = v_ref[...]
        copies = []
        for k in range(1, N_DEV):
            copies.append(pltpu.make_async_remote_copy(
                src_ref=v_ref, dst_ref=o_ref.at[_flat(me)],
                send_sem=send_sems.at[k - 1], recv_sem=recv_sems.at[k - 1],
                device_id=_flip(me, k), device_id_type=MESH_ID))
        for cp in copies:
            cp.start()
        for cp in copies:
            cp.wait()

    return pl.pallas_call(
        body, name=name,
        in_specs=[_vmem()], out_specs=_vmem(),
        out_shape=jax.ShapeDtypeStruct((N_DEV,) + v.shape, v.dtype),
        scratch_shapes=[pltpu.SemaphoreType.DMA((7,)), pltpu.SemaphoreType.DMA((7,))],
        compiler_params=_params(has_side_effects=True),
    )(v)


def _rel_onehot(i):
    r_io = lax.broadcasted_iota(jnp.int32, (N_REL, BAND_W), 0)
    p_io = lax.broadcasted_iota(jnp.int32, (N_REL, BAND_W), 1)
    idx = jnp.clip(PAD_K + i - p_io, -(CHUNK - 1), REL_CLIP) + (CHUNK - 1)
    return (r_io == idx).astype(BF16)


def _bias_expand(rel):
    lh = rel.shape[0]

    def body(rel_ref, o_ref):
        o_ref[...] = _split_dot(rel_ref[...], _rel_onehot(pl.program_id(0)), 3)

    return pl.pallas_call(
        body, name="bias_expand", grid=(CHUNK,),
        in_specs=[pl.BlockSpec((lh, N_REL), lambda i: (0, 0))],
        out_specs=pl.BlockSpec((None, lh, BAND_W), lambda i: (i, 0, 0)),
        out_shape=jax.ShapeDtypeStruct((CHUNK, lh, BAND_W), F32),
        compiler_params=_params("parallel"),
    )(rel)


def _bias_grad(dbias):
    lh = dbias.shape[1]

    def body(db_ref, o_ref):
        i = pl.program_id(0)

        @pl.when(i == 0)
        def _():
            o_ref[...] = jnp.zeros_like(o_ref)

        o_ref[...] += _split_dot(db_ref[...], _rel_onehot(i), 3, NT)

    return pl.pallas_call(
        body, name="bias_grad", grid=(CHUNK,),
        in_specs=[pl.BlockSpec((None, lh, BAND_W), lambda i: (i, 0, 0))],
        out_specs=pl.BlockSpec((lh, N_REL), lambda i: (0, 0)),
        out_shape=jax.ShapeDtypeStruct((lh, N_REL), F32),
        compiler_params=_params("arbitrary"),
    )(dbias)


def _norm_proj(x, g, w_all, layer):
    s, d = x.shape
    e = w_all.shape[2]
    tm, tn = _tile(s, 512), _tile(e, 1024)
    nb = e // tn

    def body(x_ref, g_ref, w_ref, proj_ref, h_ref):
        @pl.when(pl.program_id(1) == 0)
        def _():
            xv = x_ref[...]
            r = lax.rsqrt(jnp.mean(xv * xv, axis=-1, keepdims=True) + NORM_EPS)
            h_ref[...] = ((xv * r) * g_ref[...]).astype(BF16)

        proj_ref[...] = _dot(h_ref[...], w_ref[...])

    return pl.pallas_call(
        body, name=f"norm_proj_l{layer}", grid=(s // tm, N_DEV * nb),
        in_specs=[pl.BlockSpec((tm, d), lambda m, n: (m, 0)),
                  pl.BlockSpec((1, d), lambda m, n: (0, 0)),
                  pl.BlockSpec((None, d, tn), lambda m, n: (n // nb, 0, n % nb))],
        out_specs=(pl.BlockSpec((None, tm, tn), lambda m, n: (n // nb, m, n % nb)),
                   pl.BlockSpec((tm, d), lambda m, n: (m, 0))),
        out_shape=(jax.ShapeDtypeStruct((N_DEV, s, e), F32), jax.ShapeDtypeStruct((s, d), BF16)),
        compiler_params=_params("parallel", "arbitrary"),
    )(x, g, w_all)


def _sb_forward(proj, layer):
    _, s, e = proj.shape
    hp = _tile(e // HEAD_DIM, SB_HEADS)
    width = hp * HEAD_DIM
    tq = _tile(s, SB_Q)
    diag_tiles = tq // SB_K
    scale = HEAD_DIM ** -0.5

    def body(p_ref, y_ref, mix_ref, tot_ref, kb_ref, vb_ref):
        kb_ref[...] = p_ref[1].astype(BF16)
        vb_ref[...] = p_ref[2].astype(BF16)
        row = lax.broadcasted_iota(jnp.int32, (tq, SB_K), 0)
        col = lax.broadcasted_iota(jnp.int32, (tq, SB_K), 1)
        kj = lax.broadcasted_iota(jnp.int32, (SB_K, SB_K), 0)
        ks = lax.broadcasted_iota(jnp.int32, (SB_K, SB_K), 1)
        later = (kj > ks).astype(BF16)

        def q_block(qi, _):
            t0 = pl.multiple_of(qi * tq, tq)
            qb = [p_ref[0, pl.ds(t0, tq), h * HEAD_DIM:(h + 1) * HEAD_DIM].astype(BF16) for h in range(hp)]

            def tile(s0, state, causal):
                out = []
                for h in range(hp):
                    carry, acc = state[h]
                    lanes = slice(h * HEAD_DIM, (h + 1) * HEAD_DIM)
                    z = _dot(qb[h], kb_ref[pl.ds(s0, SB_K), lanes], NT) * scale
                    ls = _log_sigmoid(z)
                    stay = ls - z
                    if causal is not None:
                        stay = jnp.where(causal, stay, 0.0)
                    w = jnp.exp(ls + carry + _split_dot(stay, later, 2))
                    if causal is not None:
                        w = jnp.where(causal, w, 0.0)
                    acc = acc + _dot(w.astype(BF16), vb_ref[pl.ds(s0, SB_K), lanes])
                    out.append((carry + jnp.sum(stay, axis=1, keepdims=True), acc))
                return tuple(out)

            state = tuple((jnp.zeros((tq, 1), F32), jnp.zeros((tq, HEAD_DIM), F32)) for _ in range(hp))
            for dt in reversed(range(diag_tiles)):
                state = tile(t0 + dt * SB_K, state, col + dt * SB_K < row)

            def k_block(j, st):
                return tile(pl.multiple_of((diag_tiles * qi - 1 - j) * SB_K, SB_K), st, None)

            state = lax.fori_loop(0, diag_tiles * qi, k_block, state)
            silu, _ = _silu_and_grad(p_ref[3, pl.ds(t0, tq), :])
            for h in range(hp):
                lanes = slice(h * HEAD_DIM, (h + 1) * HEAD_DIM)
                y_ref[pl.ds(t0, tq), lanes] = state[h][1]
                mix_ref[pl.ds(t0, tq), lanes] = (state[h][1] * silu[:, lanes]).astype(BF16)
                tot_ref[h, pl.ds(t0, tq), :] = state[h][0]
            return 0

        lax.fori_loop(0, s // tq, q_block, 0)

    return pl.pallas_call(
        body, name=f"sb_forward_l{layer}", grid=(e // width,),
        in_specs=[pl.BlockSpec((4, s, width), lambda h: (0, 0, h))],
        out_specs=(pl.BlockSpec((s, width), lambda h: (0, h)),
                   pl.BlockSpec((s, width), lambda h: (0, h)),
                   pl.BlockSpec((hp, s, 1), lambda h: (h, 0, 0))),
        out_shape=(jax.ShapeDtypeStruct((s, 2 * e), F32), jax.ShapeDtypeStruct((s, 2 * e), BF16),
                   jax.ShapeDtypeStruct((e // HEAD_DIM, s, 1), F32)),
        scratch_shapes=[pltpu.VMEM((s, width), BF16), pltpu.VMEM((s, width), BF16)],
        compiler_params=_params("parallel"),
    )(proj)


def _qk_norm(t, gain):
    r = lax.rsqrt(jnp.mean(t * t, axis=-1, keepdims=True) + NORM_EPS)
    return t * r, r, (t * r) * gain


def _chunk_scores(qc, kw, bias, t0, scale):
    sc = _dot(qc, kw, NT) * scale + bias
    col = lax.broadcasted_iota(jnp.int32, (CHUNK, BAND_W), 1)
    sc = jnp.where(col + t0 >= PAD_K, sc, NEG_BIG)
    ex = jnp.exp(sc - jnp.max(sc, axis=-1, keepdims=True))
    return ex / jnp.sum(ex, axis=-1, keepdims=True)


def _chunk_forward(proj, bias, q_gain, k_gain, y, mixed, layer):
    _, s, e = proj.shape
    heads = e // HEAD_DIM
    scale = HEAD_DIM ** -0.5

    def body(p_ref, b_ref, qg_ref, kg_ref, y_in, mix_in, y_ref, mix_ref, qn_ref, kp_ref, vp_ref):
        del y_in, mix_in
        qn_ref[...] = _qk_norm(p_ref[0], qg_ref[...])[2].astype(BF16)
        kp_ref[pl.ds(0, PAD_K), :] = jnp.zeros((PAD_K, HEAD_DIM), BF16)
        vp_ref[pl.ds(0, PAD_K), :] = jnp.zeros((PAD_K, HEAD_DIM), BF16)
        kp_ref[pl.ds(PAD_K, s), :] = _qk_norm(p_ref[1], kg_ref[...])[2].astype(BF16)
        vp_ref[pl.ds(PAD_K, s), :] = p_ref[2].astype(BF16)

        def chunk(ci, _):
            t0 = pl.multiple_of(ci * CHUNK, CHUNK)
            probs = _chunk_scores(qn_ref[pl.ds(t0, CHUNK), :], kp_ref[pl.ds(t0, BAND_W), :], b_ref[...], t0, scale)
            out = _dot(probs.astype(BF16), vp_ref[pl.ds(t0, BAND_W), :])
            y_ref[pl.ds(t0, CHUNK), :] = out
            silu, _ = _silu_and_grad(p_ref[3, pl.ds(t0, CHUNK), :])
            mix_ref[pl.ds(t0, CHUNK), :] = (out * silu).astype(BF16)
            return 0

        lax.fori_loop(0, s // CHUNK, chunk, 0)

    return pl.pallas_call(
        body, name=f"chunk_forward_l{layer}", grid=(heads,),
        in_specs=[pl.BlockSpec((4, s, HEAD_DIM), lambda h: (1, 0, h)),
                  pl.BlockSpec((None, CHUNK, BAND_W), lambda h: (layer * heads + h, 0, 0)),
                  pl.BlockSpec((1, HEAD_DIM), lambda h: (0, 0)),
                  pl.BlockSpec((1, HEAD_DIM), lambda h: (0, 0)),
                  _any(), _any()],
        out_specs=(pl.BlockSpec((s, HEAD_DIM), lambda h: (0, heads + h)),
                   pl.BlockSpec((s, HEAD_DIM), lambda h: (0, heads + h))),
        out_shape=(jax.ShapeDtypeStruct(y.shape, F32), jax.ShapeDtypeStruct(mixed.shape, BF16)),
        input_output_aliases={4: 0, 5: 1},
        scratch_shapes=[pltpu.VMEM((s, HEAD_DIM), BF16), pltpu.VMEM((s + PAD_K, HEAD_DIM), BF16),
                        pltpu.VMEM((s + PAD_K, HEAD_DIM), BF16)],
        compiler_params=_params("parallel"),
    )(proj, bias, q_gain, k_gain, y, mixed)


def _out_proj(mixed, w, x, layer):
    s, d = x.shape
    tm, tn = _tile(s, 512), _tile(d, 1024)

    def body(a_ref, w_ref, x_ref, o_ref):
        o_ref[...] = x_ref[...] + _dot(a_ref[...], w_ref[...])

    return pl.pallas_call(
        body, name=f"out_proj_l{layer}", grid=(s // tm, d // tn),
        in_specs=[pl.BlockSpec((tm, d), lambda m, n: (m, 0)),
                  pl.BlockSpec((d, tn), lambda m, n: (0, n)),
                  pl.BlockSpec((tm, tn), lambda m, n: (m, n))],
        out_specs=pl.BlockSpec((tm, tn), lambda m, n: (m, n)),
        out_shape=jax.ShapeDtypeStruct((s, d), F32),
        compiler_params=_params("parallel", "parallel"),
    )(mixed, w, x)


def _loss_head(y, target):
    s, d = y.shape
    tm = _tile(s, 256)

    def body(y_ref, t_ref, dy_ref, part_ref):
        diff = y_ref[...] - t_ref[...]
        dy_ref[...] = diff * (1.0 / d)
        sq = (diff * diff).reshape(tm // 8, 8, d).sum(axis=0)
        acc = sq[:, 0:LANES]
        for j in range(1, d // LANES):
            acc = acc + sq[:, j * LANES:(j + 1) * LANES]
        part_ref[...] = acc * (0.5 / d)

    return pl.pallas_call(
        body, name="loss_head", grid=(s // tm,),
        in_specs=[pl.BlockSpec((tm, d), lambda i: (i, 0)), pl.BlockSpec((tm, d), lambda i: (i, 0))],
        out_specs=(pl.BlockSpec((tm, d), lambda i: (i, 0)), pl.BlockSpec((None, 8, LANES), lambda i: (i, 0, 0))),
        out_shape=(jax.ShapeDtypeStruct((s, d), F32), jax.ShapeDtypeStruct((s // tm, 8, LANES), F32)),
        compiler_params=_params("parallel"),
    )(y, target)


def _out_proj_bwd_input(dx, w, layer):
    s, d = dx.shape
    tm, tn = _tile(s, 512), _tile(d, 1024)

    def body(dx_ref, w_ref, o_ref):
        o_ref[...] = _dot(dx_ref[...].astype(BF16), w_ref[...], NT)

    return pl.pallas_call(
        body, name=f"out_proj_dx_l{layer}", grid=(s // tm, d // tn),
        in_specs=[pl.BlockSpec((tm, d), lambda m, n: (m, 0)), pl.BlockSpec((tn, d), lambda m, n: (n, 0))],
        out_specs=pl.BlockSpec((tm, tn), lambda m, n: (m, n)),
        out_shape=jax.ShapeDtypeStruct((s, d), F32),
        compiler_params=_params("parallel", "parallel"),
    )(dx, w)


def _out_proj_bwd_weight(mixed, dx, layer):
    s, d = dx.shape
    te, tn = _tile(d, 512), _tile(d, 1024)

    def body(a_ref, dx_ref, o_ref):
        o_ref[...] = _dot(a_ref[...], dx_ref[...].astype(BF16), TN).astype(BF16)

    return pl.pallas_call(
        body, name=f"out_proj_dw_l{layer}", grid=(d // te, d // tn),
        in_specs=[pl.BlockSpec((s, te), lambda i, n: (0, i)), pl.BlockSpec((s, tn), lambda i, n: (0, n))],
        out_specs=pl.BlockSpec((te, tn), lambda i, n: (i, n)),
        out_shape=jax.ShapeDtypeStruct((d, d), BF16),
        compiler_params=_params("parallel", "parallel"),
    )(mixed, dx)


def _sb_backward(proj, y, dmixed, tot, layer):
    _, s, e = proj.shape
    hp = _tile(e // HEAD_DIM, SB_HEADS)
    width = hp * HEAD_DIM
    tq = _tile(s, SB_Q)
    diag_tiles = tq // SB_K
    scale = HEAD_DIM ** -0.5

    def body(p_ref, y_ref, dm_ref, tot_ref, o_ref, kb_ref, vb_ref, do_ref, dk_ref, dv_ref):
        kb_ref[...] = p_ref[1].astype(BF16)
        vb_ref[...] = p_ref[2].astype(BF16)
        silu, dsilu = _silu_and_grad(p_ref[3])
        dm = dm_ref[...]
        do_ref[...] = (dm * silu).astype(BF16)
        o_ref[3] = (dm * y_ref[...] * dsilu).astype(BF16)
        dk_ref[...] = jnp.zeros_like(dk_ref)
        dv_ref[...] = jnp.zeros_like(dv_ref)
        row = lax.broadcasted_iota(jnp.int32, (tq, SB_K), 0)
        col = lax.broadcasted_iota(jnp.int32, (tq, SB_K), 1)
        kj = lax.broadcasted_iota(jnp.int32, (SB_K, SB_K), 0)
        ks = lax.broadcasted_iota(jnp.int32, (SB_K, SB_K), 1)
        upto = (kj <= ks).astype(BF16)
        before = (kj < ks).astype(BF16)

        def q_block(qi, _):
            t0 = pl.multiple_of(qi * tq, tq)
            heads = [slice(h * HEAD_DIM, (h + 1) * HEAD_DIM) for h in range(hp)]
            qb = [p_ref[0, pl.ds(t0, tq), lanes].astype(BF16) for lanes in heads]
            dob = [do_ref[pl.ds(t0, tq), lanes] for lanes in heads]
            total = [tot_ref[h, pl.ds(t0, tq), :] for h in range(hp)]

            def tile(s0, state, causal):
                out = []
                for h, lanes in enumerate(heads):
                    stay_sum, dlw_sum, dq = state[h]
                    kt = kb_ref[pl.ds(s0, SB_K), lanes]
                    vt = vb_ref[pl.ds(s0, SB_K), lanes]
                    z = _dot(qb[h], kt, NT) * scale
                    ls = _log_sigmoid(z)
                    stay = ls - z
                    if causal is not None:
                        stay = jnp.where(causal, stay, 0.0)
                    after = total[h] - (stay_sum + _split_dot(stay, upto, 3))
                    w = jnp.exp(ls + after)
                    if causal is not None:
                        w = jnp.where(causal, w, 0.0)
                    dlw = _dot(dob[h], vt, NT) * w
                    prior = dlw_sum + _split_dot(dlw, before, 2)
                    sig = jnp.exp(ls)
                    dz = (dlw * (1.0 - sig) - sig * prior) * scale
                    if causal is not None:
                        dz = jnp.where(causal, dz, 0.0)
                    dzb = dz.astype(BF16)
                    dq = dq + _dot(dzb, kt)
                    dk_ref[pl.ds(s0, SB_K), lanes] += _dot(dzb, qb[h], TN)
                    dv_ref[pl.ds(s0, SB_K), lanes] += _dot(w.astype(BF16), dob[h], TN)
                    out.append((stay_sum + jnp.sum(stay, axis=1, keepdims=True),
                                dlw_sum + jnp.sum(dlw, axis=1, keepdims=True), dq))
                return tuple(out)

            def k_block(j, st):
                return tile(pl.multiple_of(j * SB_K, SB_K), st, None)

            zero = jnp.zeros((tq, 1), F32)
            state = tuple((zero, zero, jnp.zeros((tq, HEAD_DIM), F32)) for _ in range(hp))
            state = lax.fori_loop(0, diag_tiles * qi, k_block, state)
            for dt in range(diag_tiles):
                state = tile(t0 + dt * SB_K, state, col + dt * SB_K < row)
            for h, lanes in enumerate(heads):
                o_ref[0, pl.ds(t0, tq), lanes] = state[h][2].astype(BF16)
            return 0

        lax.fori_loop(0, s // tq, q_block, 0)
        o_ref[1] = dk_ref[...].astype(BF16)
        o_ref[2] = dv_ref[...].astype(BF16)

    return pl.pallas_call(
        body, name=f"sb_backward_l{layer}", grid=(e // width,),
        in_specs=[pl.BlockSpec((4, s, width), lambda h: (0, 0, h)),
                  pl.BlockSpec((s, width), lambda h: (0, h)),
                  pl.BlockSpec((s, width), lambda h: (0, h)),
                  pl.BlockSpec((hp, s, 1), lambda h: (h, 0, 0))],
        out_specs=pl.BlockSpec((4, s, width), lambda h: (0, 0, h)),
        out_shape=jax.ShapeDtypeStruct((N_DEV, s, e), BF16),
        scratch_shapes=[pltpu.VMEM((s, width), BF16), pltpu.VMEM((s, width), BF16),
                        pltpu.VMEM((s, width), BF16), pltpu.VMEM((s, width), F32),
                        pltpu.VMEM((s, width), F32)],
        compiler_params=_params("parallel"),
    )(proj, y, dmixed, tot)


def _norm_bwd(dn, xh, r, gain):
    dxh = dn * gain
    return r * (dxh - xh * jnp.mean(dxh * xh, axis=-1, keepdims=True)), dn * xh


def _chunk_backward(proj, bias, q_gain, k_gain, y, dmixed, dproj, layer):
    _, s, e = proj.shape
    heads = e // HEAD_DIM
    scale = HEAD_DIM ** -0.5

    def body(p_ref, b_ref, qg_ref, kg_ref, y_ref, dm_ref, dp_in, o_ref, db_ref, dqg_ref, dkg_ref,
             qn_ref, kp_ref, vp_ref, do_ref, dqn_ref, dkn_ref, dvp_ref):
        del dp_in
        qh, rq, qn = _qk_norm(p_ref[0], qg_ref[...])
        kh, rk, kn = _qk_norm(p_ref[1], kg_ref[...])
        qn_ref[...] = qn.astype(BF16)
        kp_ref[pl.ds(0, PAD_K), :] = jnp.zeros((PAD_K, HEAD_DIM), BF16)
        vp_ref[pl.ds(0, PAD_K), :] = jnp.zeros((PAD_K, HEAD_DIM), BF16)
        kp_ref[pl.ds(PAD_K, s), :] = kn.astype(BF16)
        vp_ref[pl.ds(PAD_K, s), :] = p_ref[2].astype(BF16)
        silu, dsilu = _silu_and_grad(p_ref[3])
        dm = dm_ref[...]
        do_ref[...] = (dm * silu).astype(BF16)
        o_ref[3] = (dm * y_ref[...] * dsilu).astype(BF16)
        dkn_ref[...] = jnp.zeros_like(dkn_ref)
        dvp_ref[...] = jnp.zeros_like(dvp_ref)
        db_ref[...] = jnp.zeros_like(db_ref)

        def chunk(ci, _):
            t0 = pl.multiple_of(ci * CHUNK, CHUNK)
            qc = qn_ref[pl.ds(t0, CHUNK), :]
            kw = kp_ref[pl.ds(t0, BAND_W), :]
            vw = vp_ref[pl.ds(t0, BAND_W), :]
            dob = do_ref[pl.ds(t0, CHUNK), :]
            probs = _chunk_scores(qc, kw, b_ref[...], t0, scale)
            dprobs = _dot(dob, vw, NT)
            dsc = probs * (dprobs - jnp.sum(probs * dprobs, axis=-1, keepdims=True))
            db_ref[...] += dsc
            dsb = (dsc * scale).astype(BF16)
            dqn_ref[pl.ds(t0, CHUNK), :] = _dot(dsb, kw)
            dkn_ref[pl.ds(t0, BAND_W), :] += _dot(dsb, qc, TN)
            dvp_ref[pl.ds(t0, BAND_W), :] += _dot(probs.astype(BF16), dob, TN)
            return 0

        lax.fori_loop(0, s // CHUNK, chunk, 0)
        dq, dqg_rows = _norm_bwd(dqn_ref[...], qh, rq, qg_ref[...])
        dk, dkg_rows = _norm_bwd(dkn_ref[pl.ds(PAD_K, s), :], kh, rk, kg_ref[...])
        o_ref[0] = dq.astype(BF16)
        o_ref[1] = dk.astype(BF16)
        o_ref[2] = dvp_ref[pl.ds(PAD_K, s), :].astype(BF16)

        @pl.when(pl.program_id(0) == 0)
        def _():
            dqg_ref[...] = jnp.zeros_like(dqg_ref)
            dkg_ref[...] = jnp.zeros_like(dkg_ref)

        dqg_ref[...] += jnp.sum(dqg_rows, axis=0, keepdims=True)
        dkg_ref[...] += jnp.sum(dkg_rows, axis=0, keepdims=True)

    return pl.pallas_call(
        body, name=f"chunk_backward_l{layer}", grid=(heads,),
        in_specs=[pl.BlockSpec((4, s, HEAD_DIM), lambda h: (1, 0, h)),
                  pl.BlockSpec((None, CHUNK, BAND_W), lambda h: (layer * heads + h, 0, 0)),
                  pl.BlockSpec((1, HEAD_DIM), lambda h: (0, 0)),
                  pl.BlockSpec((1, HEAD_DIM), lambda h: (0, 0)),
                  pl.BlockSpec((s, HEAD_DIM), lambda h: (0, heads + h)),
                  pl.BlockSpec((s, HEAD_DIM), lambda h: (0, heads + h)),
                  _any()],
        out_specs=(pl.BlockSpec((4, s, HEAD_DIM), lambda h: (1, 0, h)),
                   pl.BlockSpec((None, CHUNK, BAND_W), lambda h: (h, 0, 0)),
                   pl.BlockSpec((1, HEAD_DIM), lambda h: (0, 0)),
                   pl.BlockSpec((1, HEAD_DIM), lambda h: (0, 0))),
        out_shape=(jax.ShapeDtypeStruct(dproj.shape, BF16),
                   jax.ShapeDtypeStruct((heads, CHUNK, BAND_W), F32),
                   jax.ShapeDtypeStruct((1, HEAD_DIM), F32), jax.ShapeDtypeStruct((1, HEAD_DIM), F32)),
        input_output_aliases={6: 0},
        scratch_shapes=[pltpu.VMEM((s, HEAD_DIM), BF16), pltpu.VMEM((s + PAD_K, HEAD_DIM), BF16),
                        pltpu.VMEM((s + PAD_K, HEAD_DIM), BF16), pltpu.VMEM((s, HEAD_DIM), BF16),
                        pltpu.VMEM((s, HEAD_DIM), F32), pltpu.VMEM((s + PAD_K, HEAD_DIM), F32),
                        pltpu.VMEM((s + PAD_K, HEAD_DIM), F32)],
        compiler_params=_params("arbitrary"),
    )(proj, bias, q_gain, k_gain, y, dmixed, dproj)


def _proj_bwd_input(dproj, w_all, x, g, dx, layer):
    s, d = x.shape
    e = w_all.shape[2]
    tm = _tile(s, 512)

    def body(dp_ref, w_ref, x_ref, g_ref, dx_ref, o_ref, dg_ref, acc_ref):
        j = pl.program_id(1)

        @pl.when(j == 0)
        def _():
            acc_ref[...] = jnp.zeros_like(acc_ref)

        acc_ref[...] += _dot(dp_ref[...], w_ref[...], NT)

        @pl.when(jnp.logical_and(j == N_DEV - 1, pl.program_id(0) == 0))
        def _():
            dg_ref[...] = jnp.zeros_like(dg_ref)

        @pl.when(j == N_DEV - 1)
        def _():
            xv = x_ref[...]
            r = lax.rsqrt(jnp.mean(xv * xv, axis=-1, keepdims=True) + NORM_EPS)
            dxn, dg_rows = _norm_bwd(acc_ref[...], xv * r, r, g_ref[...])
            o_ref[...] = dx_ref[...] + dxn
            dg_ref[...] += jnp.sum(dg_rows, axis=0, keepdims=True)

    return pl.pallas_call(
        body, name=f"proj_dx_l{layer}", grid=(s // tm, N_DEV),
        in_specs=[pl.BlockSpec((None, tm, e), lambda m, j: (j, m, 0)),
                  pl.BlockSpec((None, d, e), lambda m, j: (j, 0, 0)),
                  pl.BlockSpec((tm, d), lambda m, j: (m, 0)),
                  pl.BlockSpec((1, d), lambda m, j: (0, 0)),
                  pl.BlockSpec((tm, d), lambda m, j: (m, 0))],
        out_specs=(pl.BlockSpec((tm, d), lambda m, j: (m, 0)), pl.BlockSpec((1, d), lambda m, j: (0, 0))),
        out_shape=(jax.ShapeDtypeStruct((s, d), F32), jax.ShapeDtypeStruct((1, d), F32)),
        scratch_shapes=[pltpu.VMEM((tm, d), F32)],
        compiler_params=_params("arbitrary", "arbitrary"),
    )(dproj, w_all, x, g, dx)


def _proj_bwd_weight(h, dproj, layer):
    s, d = h.shape
    e = dproj.shape[2]
    td, tn = _tile(d, 1024), _tile(e, 1024)
    nb = e // tn

    def body(h_ref, dp_ref, o_ref):
        o_ref[...] = _dot(h_ref[...], dp_ref[...], TN).astype(BF16)

    return pl.pallas_call(
        body, name=f"proj_dw_l{layer}", grid=(d // td, N_DEV * nb),
        in_specs=[pl.BlockSpec((s, td), lambda i, n: (0, i)),
                  pl.BlockSpec((None, s, tn), lambda i, n: (n // nb, 0, n % nb))],
        out_specs=pl.BlockSpec((None, td, tn), lambda i, n: (n // nb, i, n % nb)),
        out_shape=jax.ShapeDtypeStruct((N_DEV, d, e), BF16),
        compiler_params=_params("parallel", "parallel"),
    )(h, dproj)


def _adamw_math(w, g, m, v):
    m = ADAM_B1 * m + (1.0 - ADAM_B1) * g
    v = ADAM_B2 * v + (1.0 - ADAM_B2) * (g * g)
    m_hat = m / (1.0 - ADAM_B1 ** ADAM_STEP)
    v_hat = v / (1.0 - ADAM_B2 ** ADAM_STEP)
    return -ADAM_LR * (m_hat / (jnp.sqrt(v_hat) + ADAM_EPS) + ADAM_WD * w), m, v


def _adamw_layer(parts, w, m, v, prev, layer, name):
    n_layers, rows, cols = w.shape
    tr = _tile(rows, max(8, (256 * 1024) // cols))

    def body(p_ref, w_ref, m_ref, v_ref, *rest):
        g_ref, d_ref, nm_ref, nv_ref = rest[-4:]
        g = p_ref[0].astype(F32)
        for j in range(1, N_DEV):
            g = g + p_ref[j].astype(F32)
        g_ref[...] = g
        d_ref[...], nm_ref[...], nv_ref[...] = _adamw_math(w_ref[...], g, m_ref[...], v_ref[...])

    blk = pl.BlockSpec((None, tr, cols), lambda i: (layer, i, 0))
    out_shape = tuple(jax.ShapeDtypeStruct(w.shape, F32) for _ in range(4))
    in_specs = [pl.BlockSpec((N_DEV, tr, cols), lambda i: (0, i, 0)), blk, blk, blk]
    args = [parts, w, m, v]
    aliases = {}
    if prev is not None:
        in_specs += [_any()] * 4
        args += list(prev)
        aliases = {4 + k: k for k in range(4)}
    return pl.pallas_call(
        body, name=f"{name}_l{layer}", grid=(rows // tr,),
        in_specs=in_specs, out_specs=(blk, blk, blk, blk), out_shape=out_shape,
        input_output_aliases=aliases,
        compiler_params=_params("parallel"),
    )(*args)


def _sum_slots(parts):
    def body(p_ref, o_ref):
        g = p_ref[0]
        for j in range(1, N_DEV):
            g = g + p_ref[j]
        o_ref[...] = g

    return pl.pallas_call(
        body, name="sum_small_grads",
        in_specs=[_vmem()], out_specs=_vmem(),
        out_shape=jax.ShapeDtypeStruct(parts.shape[1:], F32),
        compiler_params=_params(),
    )(parts)


def _adamw_small(w, g, m, v):
    def body(w_ref, g_ref, m_ref, v_ref, d_ref, nm_ref, nv_ref):
        d_ref[...], nm_ref[...], nv_ref[...] = _adamw_math(w_ref[...], g_ref[...], m_ref[...], v_ref[...])

    return pl.pallas_call(
        body, name="adamw_small",
        in_specs=[_vmem()] * 4, out_specs=(_vmem(),) * 3,
        out_shape=tuple(jax.ShapeDtypeStruct(w.shape, F32) for _ in range(3)),
        compiler_params=_params(),
    )(w, g, m, v)


def _pack_rows(arrays):
    rows = []
    for a in arrays:
        flat = a.reshape(-1)
        pad = (-flat.shape[0]) % (8 * LANES)
        rows.append(jnp.pad(flat, (0, pad)).reshape(-1, LANES))
    return jnp.concatenate(rows, axis=0)


def _unpack_rows(packed, like):
    out, r0 = [], 0
    for a in like:
        n = a.size
        nr = -(-n // (8 * LANES)) * 8
        out.append(packed[r0:r0 + nr].reshape(-1)[:n].reshape(a.shape))
        r0 += nr
    return out


def kernel(x, norm_g, w_in, q_norm_g, k_norm_g, rel_bias, w_out, loss_target, m_norm_g, m_w_in, m_q_norm_g, m_k_norm_g, m_rel_bias, m_w_out, v_norm_g, v_w_in, v_q_norm_g, v_k_norm_g, v_rel_bias, v_w_out):
    depth, d, e = w_in.shape
    r_out = w_out.shape[1]
    heads = e // HEAD_DIM
    rel_w = rel_bias.shape[2]
    x0 = x[0]
    target = loss_target[0]
    s = x0.shape[0]

    win_b = _cast_bf16(w_in.reshape(depth * d, e), "cast_w_in").reshape(depth, d, e)
    wout_b = _cast_bf16(w_out.reshape(depth * r_out, d), "cast_w_out").reshape(depth, r_out, d)
    rel_all = _gather_small(rel_bias, "gather_rel_bias")
    rel_full = jnp.transpose(rel_all, (1, 2, 0, 3)).reshape(depth * heads, N_DEV * rel_w)
    bias = jnp.transpose(_bias_expand(rel_full), (1, 0, 2))

    xs, hs, projs, ys, mixes, tots, weights = [], [], [], [], [], [], []
    xl = x0
    for l in range(depth):
        win_all, wout_all = _gather_layer(win_b, wout_b, l)
        wout_full = wout_all.reshape(d, d)
        proj, h = _norm_proj(xl, norm_g[l:l + 1], win_all, l)
        y, mixed, tot = _sb_forward(proj, l)
        y, mixed = _chunk_forward(proj, bias, q_norm_g[l:l + 1], k_norm_g[l:l + 1], y, mixed, l)
        xs.append(xl), hs.append(h), projs.append(proj), ys.append(y), mixes.append(mixed), tots.append(tot)
        weights.append((win_all, wout_full))
        xl = _out_proj(mixed, wout_full, xl, l)

    dx, loss_parts = _loss_head(xl, target)
    loss = lax.psum(jnp.sum(loss_parts), AXES)

    dbias, dng, dqg, dkg = [None] * depth, [None] * depth, [None] * depth, [None] * depth
    res_in, res_out = None, None
    for l in reversed(range(depth)):
        win_all, wout_full = weights[l]
        dmixed = _out_proj_bwd_input(dx, wout_full, l)
        gwout = _out_proj_bwd_weight(mixes[l], dx, l).reshape(N_DEV, r_out, d)
        dproj = _sb_backward(projs[l], ys[l], dmixed, tots[l], l)
        dproj, dbias[l], dqg[l], dkg[l] = _chunk_backward(
            projs[l], bias, q_norm_g[l:l + 1], k_norm_g[l:l + 1], ys[l], dmixed, dproj, l)
        dx, dng[l] = _proj_bwd_input(dproj, win_all, xs[l], norm_g[l:l + 1], dx, l)
        gwin = _proj_bwd_weight(hs[l], dproj, l)
        rin, rout = _exchange_layer(gwin, gwout, l)
        res_in = _adamw_layer(rin, w_in, m_w_in, v_w_in, res_in, l, "adamw_w_in")
        res_out = _adamw_layer(rout, w_out, m_w_out, v_w_out, res_out, l, "adamw_w_out")

    drel = _bias_grad(jnp.transpose(jnp.concatenate(dbias, axis=0), (1, 0, 2)))
    small_like = [norm_g, q_norm_g, k_norm_g, drel]
    mine = _pack_rows([jnp.concatenate(dng, axis=0), jnp.concatenate(dqg, axis=0),
                       jnp.concatenate(dkg, axis=0), drel])
    g_norm, g_qn, g_kn, g_rel_full = _unpack_rows(_sum_slots(_gather_small(mine, "gather_small_grads")), small_like)
    my_block = _flat(_my_place())
    g_rel = lax.dynamic_slice_in_dim(g_rel_full.reshape(depth, heads, N_REL), my_block * rel_w, rel_w, axis=2)
    small_w = [norm_g, q_norm_g, k_norm_g, rel_bias]
    small = _adamw_small(_pack_rows(small_w), _pack_rows([g_norm, g_qn, g_kn, g_rel]),
                         _pack_rows([m_norm_g, m_q_norm_g, m_k_norm_g, m_rel_bias]),
                         _pack_rows([v_norm_g, v_q_norm_g, v_k_norm_g, v_rel_bias]))
    d_small, nm_small, nv_small = (_unpack_rows(p, small_w) for p in small)

    g_win, d_win, nm_win, nv_win = res_in
    g_wout, d_wout, nm_wout, nv_wout = res_out
    grads = (g_norm, g_win, g_qn, g_kn, g_rel, g_wout)

    def order(sm, big_in, big_out):
        return (sm[0], big_in, sm[1], sm[2], sm[3], big_out)

    return (loss, dx[None], *grads, *order(d_small, d_win, d_wout),
            *order(nm_small, nm_win, nm_wout), *order(nv_small, nv_win, nv_wout))
```

```python
import functools

import jax
import jax.numpy as jnp
from jax import lax
from jax.experimental import pallas as pl
from jax.experimental.pallas import tpu as pltpu

F32 = jnp.float32
BF16 = jnp.bfloat16
MESH_ID = pl.DeviceIdType.MESH
AXES = ("x", "y", "c")

N_DEV = 8
HEAD_DIM = 128
CHUNK = 64
LEFT_CHUNKS = 8
BAND_W = (LEFT_CHUNKS + 1) * CHUNK
PAD_K = LEFT_CHUNKS * CHUNK
REL_CLIP = 256
N_REL = REL_CLIP + CHUNK
NORM_EPS = 1e-6
NEG_BIG = -1e30
SB_TILE = 128
SB_Q = 512
SB_K = 128
SB_HEADS = 2
LANES = 128

ADAM_LR = 0.001
ADAM_B1 = 0.9
ADAM_B2 = 0.999
ADAM_EPS = 1e-08
ADAM_WD = 0.01
ADAM_STEP = 10

VMEM_LIMIT_BYTES = 56 * 1024 * 1024

NT = (((1,), (1,)), ((), ()))
TN = (((0,), (0,)), ((), ()))


def _params(*sem, **kw):
    return pltpu.CompilerParams(dimension_semantics=sem or None, vmem_limit_bytes=VMEM_LIMIT_BYTES, **kw)


def _any():
    return pl.BlockSpec(memory_space=pl.ANY)


def _vmem():
    return pl.BlockSpec(memory_space=pltpu.VMEM)


def _tile(n, want):
    return want if n % want == 0 else n


def _dot(a, b, dims=None):
    if dims is None:
        return jnp.dot(a, b, preferred_element_type=F32)
    return lax.dot_general(a, b, dims, preferred_element_type=F32)


def _split_dot(a, b, parts, dims=None):
    acc = None
    rest = a
    for _ in range(parts):
        piece = rest.astype(BF16)
        rest = rest - piece.astype(F32)
        term = _dot(piece, b, dims)
        acc = term if acc is None else acc + term
    return acc


def _log_sigmoid(z):
    return jnp.minimum(z, 0.0) - jnp.log(1.0 + jnp.exp(-jnp.abs(z)))


def _silu_and_grad(g):
    sig = jax.nn.sigmoid(g)
    return g * sig, sig * (1.0 + g * (1.0 - sig))


def _my_place():
    return lax.axis_index("x"), lax.axis_index("y"), lax.axis_index("c")


def _flat(place):
    return 4 * place[0] + 2 * place[1] + place[2]


def _flip(place, k):
    return tuple(1 - p if (k >> s) & 1 else p for p, s in zip(place, (2, 1, 0)))


def _cast_layer(w, layer, name):
    _, rows, cols = w.shape
    tr = _tile(rows, 1024)

    def body(a_ref, o_ref):
        o_ref[...] = a_ref[...].astype(BF16)

    return pl.pallas_call(
        body, name=f"{name}_l{layer}", grid=(rows // tr,),
        in_specs=[pl.BlockSpec((None, tr, cols), lambda i: (layer, i, 0))],
        out_specs=pl.BlockSpec((tr, cols), lambda i: (i, 0)),
        out_shape=jax.ShapeDtypeStruct((rows, cols), BF16),
        compiler_params=_params("parallel"),
    )(w)


HBM_SPEC = pl.BlockSpec(memory_space=pltpu.HBM)
SEM_SPEC = pl.BlockSpec(memory_space=pltpu.SEMAPHORE)
N_PEERS = N_DEV - 1


def _place_own(srcs, slot_of_src, dep, name):
    n = len(srcs)

    def body(*refs):
        ins, outs, sems = refs[:n], refs[-n - 1:-1], refs[-1]
        me = _flat(_my_place())
        copies = [pltpu.make_async_copy(ins[t].at[me] if slot_of_src else ins[t], outs[t].at[me], sems.at[t])
                  for t in range(n)]
        for cp in copies:
            cp.start()
        for cp in copies:
            cp.wait()

    shapes = [a.shape[1:] if slot_of_src else a.shape for a in srcs]
    args = list(srcs) + ([] if dep is None else [dep])
    return pl.pallas_call(
        body, name=name,
        in_specs=[_any()] * len(args), out_specs=tuple(_any() for _ in range(n)),
        out_shape=tuple(jax.ShapeDtypeStruct((N_DEV,) + sh, a.dtype) for sh, a in zip(shapes, srcs)),
        scratch_shapes=[pltpu.SemaphoreType.DMA((n,))],
        compiler_params=_params(has_side_effects=True),
    )(*args)


def _peer_copies(srcs, lands, send_sems, recv_sems, slot_of_src):
    me = _my_place()
    copies = []
    for k in range(1, N_DEV):
        peer = _flip(me, k)
        for t in range(len(srcs)):
            copies.append(pltpu.make_async_remote_copy(
                src_ref=srcs[t].at[_flat(peer)] if slot_of_src else srcs[t], dst_ref=lands[t].at[_flat(me)],
                send_sem=send_sems.at[t * N_PEERS + k - 1], recv_sem=recv_sems.at[t * N_PEERS + k - 1],
                device_id=peer, device_id_type=MESH_ID))
    return copies


def _copies_start(srcs, lands, slot_of_src, name):
    n = len(srcs)

    def body(*refs):
        for cp in _peer_copies(refs[:n], refs[n:2 * n], refs[2 * n], refs[2 * n + 1], slot_of_src):
            cp.start()
        refs[-1][...] = jnp.zeros_like(refs[-1])

    thru = [pltpu.HBM(a.shape, a.dtype) for a in list(srcs) + list(lands)]
    out = pl.pallas_call(
        body, name=name,
        in_specs=[HBM_SPEC] * (2 * n),
        out_specs=(SEM_SPEC, SEM_SPEC, *([HBM_SPEC] * (2 * n)), _vmem()),
        out_shape=(pltpu.SemaphoreType.DMA((n * N_PEERS,)), pltpu.SemaphoreType.DMA((n * N_PEERS,)), *thru,
                   jax.ShapeDtypeStruct((8, LANES), F32)),
        input_output_aliases={i: 2 + i for i in range(2 * n)},
        compiler_params=pltpu.CompilerParams(has_side_effects=pltpu.SideEffectType.DATAFLOW_SIDE_EFFECTING),
    )(*[pltpu.with_memory_space_constraint(a, pltpu.HBM) for a in list(srcs) + list(lands)])
    return out[0], out[1], out[2:2 + n], out[2 + n:2 + 2 * n], out[-1]


def _copies_wait(pending, after, slot_of_src, name):
    send_sems, recv_sems, srcs, lands, _ = pending
    n = len(srcs)

    def body(*refs):
        for cp in _peer_copies(refs[:n], refs[n:2 * n], refs[2 * n], refs[2 * n + 1], slot_of_src):
            cp.wait_send()
            cp.wait_recv()

    thru = [pltpu.HBM(a.shape, a.dtype) for a in list(srcs) + list(lands)]
    out = pl.pallas_call(
        body, name=name,
        in_specs=[HBM_SPEC] * (2 * n) + [SEM_SPEC, SEM_SPEC, _any()],
        out_specs=tuple([HBM_SPEC] * (2 * n)),
        out_shape=tuple(thru),
        input_output_aliases={i: i for i in range(2 * n)},
        compiler_params=pltpu.CompilerParams(has_side_effects=pltpu.SideEffectType.DATAFLOW_SIDE_EFFECTING),
    )(*srcs, *lands, send_sems, recv_sems, after)
    return out[n:]


def _gather_small(v, name):
    def body(v_ref, o_ref, send_sems, recv_sems):
        me = _my_place()
        o_ref[_flat(me)] = v_ref[...]
        copies = []
        for k in range(1, N_DEV):
            copies.append(pltpu.make_async_remote_copy(
                src_ref=v_ref, dst_ref=o_ref.at[_flat(me)],
                send_sem=send_sems.at[k - 1], recv_sem=recv_sems.at[k - 1],
                device_id=_flip(me, k), device_id_type=MESH_ID))
        for cp in copies:
            cp.start()
        for cp in copies:
            cp.wait()

    return pl.pallas_call(
        body, name=name,
        in_specs=[_vmem()], out_specs=_vmem(),
        out_shape=jax.ShapeDtypeStruct((N_DEV,) + v.shape, v.dtype),
        scratch_shapes=[pltpu.SemaphoreType.DMA((7,)), pltpu.SemaphoreType.DMA((7,))],
        compiler_params=_params(has_side_effects=True),
    )(v)


def _rel_onehot(i):
    r_io = lax.broadcasted_iota(jnp.int32, (N_REL, BAND_W), 0)
    p_io = lax.broadcasted_iota(jnp.int32, (N_REL, BAND_W), 1)
    idx = jnp.clip(PAD_K + i - p_io, -(CHUNK - 1), REL_CLIP) + (CHUNK - 1)
    return (r_io == idx).astype(BF16)


def _bias_expand(rel):
    lh = rel.shape[0]

    def body(rel_ref, o_ref):
        o_ref[...] = _split_dot(rel_ref[...], _rel_onehot(pl.program_id(0)), 3)

    return pl.pallas_call(
        body, name="bias_expand", grid=(CHUNK,),
        in_specs=[pl.BlockSpec((lh, N_REL), lambda i: (0, 0))],
        out_specs=pl.BlockSpec((None, lh, BAND_W), lambda i: (i, 0, 0)),
        out_shape=jax.ShapeDtypeStruct((CHUNK, lh, BAND_W), F32),
        compiler_params=_params("parallel"),
    )(rel)


def _bias_grad(dbias):
    lh = dbias.shape[1]

    def body(db_ref, o_ref):
        i = pl.program_id(0)

        @pl.when(i == 0)
        def _():
            o_ref[...] = jnp.zeros_like(o_ref)

        o_ref[...] += _split_dot(db_ref[...], _rel_onehot(i), 3, NT)

    return pl.pallas_call(
        body, name="bias_grad", grid=(CHUNK,),
        in_specs=[pl.BlockSpec((None, lh, BAND_W), lambda i: (i, 0, 0))],
        out_specs=pl.BlockSpec((lh, N_REL), lambda i: (0, 0)),
        out_shape=jax.ShapeDtypeStruct((lh, N_REL), F32),
        compiler_params=_params("arbitrary"),
    )(dbias)


def _norm_proj(x, g, w_all, tie, layer):
    s, d = x.shape
    e = w_all.shape[2]
    tm, tn = _tile(s, 512), _tile(e, 1024)
    nb = e // tn
    ties = [] if tie is None else [tie]

    def body(x_ref, g_ref, w_ref, *rest):
        proj_ref, h_ref = rest[-2:]

        @pl.when(pl.program_id(1) == 0)
        def _():
            xv = x_ref[...]
            r = lax.rsqrt(jnp.mean(xv * xv, axis=-1, keepdims=True) + NORM_EPS)
            h_ref[...] = ((xv * r) * g_ref[...]).astype(BF16)

        proj_ref[...] = _dot(h_ref[...], w_ref[...])

    return pl.pallas_call(
        body, name=f"norm_proj_l{layer}", grid=(s // tm, N_DEV * nb),
        in_specs=[pl.BlockSpec((tm, d), lambda m, n: (m, 0)),
                  pl.BlockSpec((1, d), lambda m, n: (0, 0)),
                  pl.BlockSpec((None, d, tn), lambda m, n: (n // nb, 0, n % nb))] + [_any()] * len(ties),
        out_specs=(pl.BlockSpec((None, tm, tn), lambda m, n: (n // nb, m, n % nb)),
                   pl.BlockSpec((tm, d), lambda m, n: (m, 0))),
        out_shape=(jax.ShapeDtypeStruct((N_DEV, s, e), F32), jax.ShapeDtypeStruct((s, d), BF16)),
        compiler_params=_params("parallel", "arbitrary"),
    )(x, g, w_all, *ties)


def _sb_forward(proj, layer):
    _, s, e = proj.shape
    hp = _tile(e // HEAD_DIM, SB_HEADS)
    width = hp * HEAD_DIM
    tq = _tile(s, SB_Q)
    diag_tiles = tq // SB_K
    scale = HEAD_DIM ** -0.5

    def body(p_ref, y_ref, mix_ref, tot_ref, kb_ref, vb_ref):
        kb_ref[...] = p_ref[1].astype(BF16)
        vb_ref[...] = p_ref[2].astype(BF16)
        row = lax.broadcasted_iota(jnp.int32, (tq, SB_K), 0)
        col = lax.broadcasted_iota(jnp.int32, (tq, SB_K), 1)
        kj = lax.broadcasted_iota(jnp.int32, (SB_K, SB_K), 0)
        ks = lax.broadcasted_iota(jnp.int32, (SB_K, SB_K), 1)
        later = (kj > ks).astype(BF16)

        def q_block(qi, _):
            t0 = pl.multiple_of(qi * tq, tq)
            qb = [p_ref[0, pl.ds(t0, tq), h * HEAD_DIM:(h + 1) * HEAD_DIM].astype(BF16) for h in range(hp)]

            def tile(s0, state, causal):
                out = []
                for h in range(hp):
                    carry, acc = state[h]
                    lanes = slice(h * HEAD_DIM, (h + 1) * HEAD_DIM)
                    z = _dot(qb[h], kb_ref[pl.ds(s0, SB_K), lanes], NT) * scale
                    ls = _log_sigmoid(z)
                    stay = ls - z
                    if causal is not None:
                        stay = jnp.where(causal, stay, 0.0)
                    w = jnp.exp(ls + carry + _split_dot(stay, later, 2))
                    if causal is not None:
                        w = jnp.where(causal, w, 0.0)
                    acc = acc + _dot(w.astype(BF16), vb_ref[pl.ds(s0, SB_K), lanes])
                    out.append((carry + jnp.sum(stay, axis=1, keepdims=True), acc))
                return tuple(out)

            state = tuple((jnp.zeros((tq, 1), F32), jnp.zeros((tq, HEAD_DIM), F32)) for _ in range(hp))
            for dt in reversed(range(diag_tiles)):
                state = tile(t0 + dt * SB_K, state, col + dt * SB_K < row)

            def k_block(j, st):
                return tile(pl.multiple_of((diag_tiles * qi - 1 - j) * SB_K, SB_K), st, None)

            state = lax.fori_loop(0, diag_tiles * qi, k_block, state)
            silu, _ = _silu_and_grad(p_ref[3, pl.ds(t0, tq), :])
            for h in range(hp):
                lanes = slice(h * HEAD_DIM, (h + 1) * HEAD_DIM)
                y_ref[pl.ds(t0, tq), lanes] = state[h][1]
                mix_ref[pl.ds(t0, tq), lanes] = (state[h][1] * silu[:, lanes]).astype(BF16)
                tot_ref[h, pl.ds(t0, tq), :] = state[h][0]
            return 0

        lax.fori_loop(0, s // tq, q_block, 0)

    return pl.pallas_call(
        body, name=f"sb_forward_l{layer}", grid=(e // width,),
        in_specs=[pl.BlockSpec((4, s, width), lambda h: (0, 0, h))],
        out_specs=(pl.BlockSpec((s, width), lambda h: (0, h)),
                   pl.BlockSpec((s, width), lambda h: (0, h)),
                   pl.BlockSpec((hp, s, 1), lambda h: (h, 0, 0))),
        out_shape=(jax.ShapeDtypeStruct((s, 2 * e), F32), jax.ShapeDtypeStruct((s, 2 * e), BF16),
                   jax.ShapeDtypeStruct((e // HEAD_DIM, s, 1), F32)),
        scratch_shapes=[pltpu.VMEM((s, width), BF16), pltpu.VMEM((s, width), BF16)],
        compiler_params=_params("parallel"),
    )(proj)


def _qk_norm(t, gain):
    r = lax.rsqrt(jnp.mean(t * t, axis=-1, keepdims=True) + NORM_EPS)
    return t * r, r, (t * r) * gain


def _chunk_scores(qc, kw, bias, t0, scale):
    sc = _dot(qc, kw, NT) * scale + bias
    col = lax.broadcasted_iota(jnp.int32, (CHUNK, BAND_W), 1)
    sc = jnp.where(col + t0 >= PAD_K, sc, NEG_BIG)
    ex = jnp.exp(sc - jnp.max(sc, axis=-1, keepdims=True))
    return ex / jnp.sum(ex, axis=-1, keepdims=True)


def _chunk_forward(proj, bias, q_gain, k_gain, y, mixed, layer):
    _, s, e = proj.shape
    heads = e // HEAD_DIM
    scale = HEAD_DIM ** -0.5

    def body(p_ref, b_ref, qg_ref, kg_ref, y_in, mix_in, y_ref, mix_ref, qn_ref, kp_ref, vp_ref):
        del y_in, mix_in
        qn_ref[...] = _qk_norm(p_ref[0], qg_ref[...])[2].astype(BF16)
        kp_ref[pl.ds(0, PAD_K), :] = jnp.zeros((PAD_K, HEAD_DIM), BF16)
        vp_ref[pl.ds(0, PAD_K), :] = jnp.zeros((PAD_K, HEAD_DIM), BF16)
        kp_ref[pl.ds(PAD_K, s), :] = _qk_norm(p_ref[1], kg_ref[...])[2].astype(BF16)
        vp_ref[pl.ds(PAD_K, s), :] = p_ref[2].astype(BF16)

        def chunk(ci, _):
            t0 = pl.multiple_of(ci * CHUNK, CHUNK)
            probs = _chunk_scores(qn_ref[pl.ds(t0, CHUNK), :], kp_ref[pl.ds(t0, BAND_W), :], b_ref[...], t0, scale)
            out = _dot(probs.astype(BF16), vp_ref[pl.ds(t0, BAND_W), :])
            y_ref[pl.ds(t0, CHUNK), :] = out
            silu, _ = _silu_and_grad(p_ref[3, pl.ds(t0, CHUNK), :])
            mix_ref[pl.ds(t0, CHUNK), :] = (out * silu).astype(BF16)
            return 0

        lax.fori_loop(0, s // CHUNK, chunk, 0)

    return pl.pallas_call(
        body, name=f"chunk_forward_l{layer}", grid=(heads,),
        in_specs=[pl.BlockSpec((4, s, HEAD_DIM), lambda h: (1, 0, h)),
                  pl.BlockSpec((None, CHUNK, BAND_W), lambda h: (layer * heads + h, 0, 0)),
                  pl.BlockSpec((1, HEAD_DIM), lambda h: (0, 0)),
                  pl.BlockSpec((1, HEAD_DIM), lambda h: (0, 0)),
                  _any(), _any()],
        out_specs=(pl.BlockSpec((s, HEAD_DIM), lambda h: (0, heads + h)),
                   pl.BlockSpec((s, HEAD_DIM), lambda h: (0, heads + h))),
        out_shape=(jax.ShapeDtypeStruct(y.shape, F32), jax.ShapeDtypeStruct(mixed.shape, BF16)),
        input_output_aliases={4: 0, 5: 1},
        scratch_shapes=[pltpu.VMEM((s, HEAD_DIM), BF16), pltpu.VMEM((s + PAD_K, HEAD_DIM), BF16),
                        pltpu.VMEM((s + PAD_K, HEAD_DIM), BF16)],
        compiler_params=_params("parallel"),
    )(proj, bias, q_gain, k_gain, y, mixed)


def _out_proj(mixed, w, x, layer):
    s, d = x.shape
    tm, tn = _tile(s, 512), _tile(d, 1024)

    def body(a_ref, w_ref, x_ref, o_ref):
        o_ref[...] = x_ref[...] + _dot(a_ref[...], w_ref[...])

    return pl.pallas_call(
        body, name=f"out_proj_l{layer}", grid=(s // tm, d // tn),
        in_specs=[pl.BlockSpec((tm, d), lambda m, n: (m, 0)),
                  pl.BlockSpec((d, tn), lambda m, n: (0, n)),
                  pl.BlockSpec((tm, tn), lambda m, n: (m, n))],
        out_specs=pl.BlockSpec((tm, tn), lambda m, n: (m, n)),
        out_shape=jax.ShapeDtypeStruct((s, d), F32),
        compiler_params=_params("parallel", "parallel"),
    )(mixed, w, x)


def _loss_head(y, target):
    s, d = y.shape
    tm = _tile(s, 256)

    def body(y_ref, t_ref, dy_ref, part_ref):
        diff = y_ref[...] - t_ref[...]
        dy_ref[...] = diff * (1.0 / d)
        sq = (diff * diff).reshape(tm // 8, 8, d).sum(axis=0)
        acc = sq[:, 0:LANES]
        for j in range(1, d // LANES):
            acc = acc + sq[:, j * LANES:(j + 1) * LANES]
        part_ref[...] = acc * (0.5 / d)

    return pl.pallas_call(
        body, name="loss_head", grid=(s // tm,),
        in_specs=[pl.BlockSpec((tm, d), lambda i: (i, 0)), pl.BlockSpec((tm, d), lambda i: (i, 0))],
        out_specs=(pl.BlockSpec((tm, d), lambda i: (i, 0)), pl.BlockSpec((None, 8, LANES), lambda i: (i, 0, 0))),
        out_shape=(jax.ShapeDtypeStruct((s, d), F32), jax.ShapeDtypeStruct((s // tm, 8, LANES), F32)),
        compiler_params=_params("parallel"),
    )(y, target)


def _out_proj_bwd_input(dx, w, tie, layer):
    s, d = dx.shape
    tm, tn = _tile(s, 512), _tile(d, 1024)
    ties = [] if tie is None else [tie]

    def body(dx_ref, w_ref, *rest):
        rest[-1][...] = _dot(dx_ref[...].astype(BF16), w_ref[...], NT)

    return pl.pallas_call(
        body, name=f"out_proj_dx_l{layer}", grid=(s // tm, d // tn),
        in_specs=[pl.BlockSpec((tm, d), lambda m, n: (m, 0)),
                  pl.BlockSpec((tn, d), lambda m, n: (n, 0))] + [_any()] * len(ties),
        out_specs=pl.BlockSpec((tm, tn), lambda m, n: (m, n)),
        out_shape=jax.ShapeDtypeStruct((s, d), F32),
        compiler_params=_params("parallel", "parallel"),
    )(dx, w, *ties)


def _out_proj_bwd_weight(mixed, dx, layer):
    s, d = dx.shape
    te, tn = _tile(d, 512), _tile(d, 1024)

    def body(a_ref, dx_ref, o_ref):
        o_ref[...] = _dot(a_ref[...], dx_ref[...].astype(BF16), TN).astype(BF16)

    return pl.pallas_call(
        body, name=f"out_proj_dw_l{layer}", grid=(d // te, d // tn),
        in_specs=[pl.BlockSpec((s, te), lambda i, n: (0, i)), pl.BlockSpec((s, tn), lambda i, n: (0, n))],
        out_specs=pl.BlockSpec((te, tn), lambda i, n: (i, n)),
        out_shape=jax.ShapeDtypeStruct((d, d), BF16),
        compiler_params=_params("parallel", "parallel"),
    )(mixed, dx)


def _sb_backward(proj, y, dmixed, tot, layer):
    _, s, e = proj.shape
    hp = _tile(e // HEAD_DIM, SB_HEADS)
    width = hp * HEAD_DIM
    tq = _tile(s, SB_Q)
    diag_tiles = tq // SB_K
    scale = HEAD_DIM ** -0.5

    def body(p_ref, y_ref, dm_ref, tot_ref, o_ref, kb_ref, vb_ref, do_ref, dk_ref, dv_ref):
        kb_ref[...] = p_ref[1].astype(BF16)
        vb_ref[...] = p_ref[2].astype(BF16)
        silu, dsilu = _silu_and_grad(p_ref[3])
        dm = dm_ref[...]
        do_ref[...] = (dm * silu).astype(BF16)
        o_ref[3] = (dm * y_ref[...] * dsilu).astype(BF16)
        dk_ref[...] = jnp.zeros_like(dk_ref)
        dv_ref[...] = jnp.zeros_like(dv_ref)
        row = lax.broadcasted_iota(jnp.int32, (tq, SB_K), 0)
        col = lax.broadcasted_iota(jnp.int32, (tq, SB_K), 1)
        kj = lax.broadcasted_iota(jnp.int32, (SB_K, SB_K), 0)
        ks = lax.broadcasted_iota(jnp.int32, (SB_K, SB_K), 1)
        upto = (kj <= ks).astype(BF16)
        before = (kj < ks).astype(BF16)

        def q_block(qi, _):
            t0 = pl.multiple_of(qi * tq, tq)
            heads = [slice(h * HEAD_DIM, (h + 1) * HEAD_DIM) for h in range(hp)]
            qb = [p_ref[0, pl.ds(t0, tq), lanes].astype(BF16) for lanes in heads]
            dob = [do_ref[pl.ds(t0, tq), lanes] for lanes in heads]
            total = [tot_ref[h, pl.ds(t0, tq), :] for h in range(hp)]

            def tile(s0, state, causal):
                out = []
                for h, lanes in enumerate(heads):
                    stay_sum, dlw_sum, dq = state[h]
                    kt = kb_ref[pl.ds(s0, SB_K), lanes]
                    vt = vb_ref[pl.ds(s0, SB_K), lanes]
                    z = _dot(qb[h], kt, NT) * scale
                    ls = _log_sigmoid(z)
                    stay = ls - z
                    if causal is not None:
                        stay = jnp.where(causal, stay, 0.0)
                    after = total[h] - (stay_sum + _split_dot(stay, upto, 3))
                    w = jnp.exp(ls + after)
                    if causal is not None:
                        w = jnp.where(causal, w, 0.0)
                    dlw = _dot(dob[h], vt, NT) * w
                    prior = dlw_sum + _split_dot(dlw, before, 2)
                    sig = jnp.exp(ls)
                    dz = (dlw * (1.0 - sig) - sig * prior) * scale
                    if causal is not None:
                        dz = jnp.where(causal, dz, 0.0)
                    dzb = dz.astype(BF16)
                    dq = dq + _dot(dzb, kt)
                    dk_ref[pl.ds(s0, SB_K), lanes] += _dot(dzb, qb[h], TN)
                    dv_ref[pl.ds(s0, SB_K), lanes] += _dot(w.astype(BF16), dob[h], TN)
                    out.append((stay_sum + jnp.sum(stay, axis=1, keepdims=True),
                                dlw_sum + jnp.sum(dlw, axis=1, keepdims=True), dq))
                return tuple(out)

            def k_block(j, st):
                return tile(pl.multiple_of(j * SB_K, SB_K), st, None)

            zero = jnp.zeros((tq, 1), F32)
            state = tuple((zero, zero, jnp.zeros((tq, HEAD_DIM), F32)) for _ in range(hp))
            state = lax.fori_loop(0, diag_tiles * qi, k_block, state)
            for dt in range(diag_tiles):
                state = tile(t0 + dt * SB_K, state, col + dt * SB_K < row)
            for h, lanes in enumerate(heads):
                o_ref[0, pl.ds(t0, tq), lanes] = state[h][2].astype(BF16)
            return 0

        lax.fori_loop(0, s // tq, q_block, 0)
        o_ref[1] = dk_ref[...].astype(BF16)
        o_ref[2] = dv_ref[...].astype(BF16)

    return pl.pallas_call(
        body, name=f"sb_backward_l{layer}", grid=(e // width,),
        in_specs=[pl.BlockSpec((4, s, width), lambda h: (0, 0, h)),
                  pl.BlockSpec((s, width), lambda h: (0, h)),
                  pl.BlockSpec((s, width), lambda h: (0, h)),
                  pl.BlockSpec((hp, s, 1), lambda h: (h, 0, 0))],
        out_specs=pl.BlockSpec((4, s, width), lambda h: (0, 0, h)),
        out_shape=jax.ShapeDtypeStruct((N_DEV, s, e), BF16),
        scratch_shapes=[pltpu.VMEM((s, width), BF16), pltpu.VMEM((s, width), BF16),
                        pltpu.VMEM((s, width), BF16), pltpu.VMEM((s, width), F32),
                        pltpu.VMEM((s, width), F32)],
        compiler_params=_params("parallel"),
    )(proj, y, dmixed, tot)


def _norm_bwd(dn, xh, r, gain):
    dxh = dn * gain
    return r * (dxh - xh * jnp.mean(dxh * xh, axis=-1, keepdims=True)), dn * xh


def _chunk_backward(proj, bias, q_gain, k_gain, y, dmixed, dproj, layer):
    _, s, e = proj.shape
    heads = e // HEAD_DIM
    scale = HEAD_DIM ** -0.5

    def body(p_ref, b_ref, qg_ref, kg_ref, y_ref, dm_ref, dp_in, o_ref, db_ref, dqg_ref, dkg_ref,
             qn_ref, kp_ref, vp_ref, do_ref, dqn_ref, dkn_ref, dvp_ref):
        del dp_in
        qh, rq, qn = _qk_norm(p_ref[0], qg_ref[...])
        kh, rk, kn = _qk_norm(p_ref[1], kg_ref[...])
        qn_ref[...] = qn.astype(BF16)
        kp_ref[pl.ds(0, PAD_K), :] = jnp.zeros((PAD_K, HEAD_DIM), BF16)
        vp_ref[pl.ds(0, PAD_K), :] = jnp.zeros((PAD_K, HEAD_DIM), BF16)
        kp_ref[pl.ds(PAD_K, s), :] = kn.astype(BF16)
        vp_ref[pl.ds(PAD_K, s), :] = p_ref[2].astype(BF16)
        silu, dsilu = _silu_and_grad(p_ref[3])
        dm = dm_ref[...]
        do_ref[...] = (dm * silu).astype(BF16)
        o_ref[3] = (dm * y_ref[...] * dsilu).astype(BF16)
        dkn_ref[...] = jnp.zeros_like(dkn_ref)
        dvp_ref[...] = jnp.zeros_like(dvp_ref)
        db_ref[...] = jnp.zeros_like(db_ref)

        def chunk(ci, _):
            t0 = pl.multiple_of(ci * CHUNK, CHUNK)
            qc = qn_ref[pl.ds(t0, CHUNK), :]
            kw = kp_ref[pl.ds(t0, BAND_W), :]
            vw = vp_ref[pl.ds(t0, BAND_W), :]
            dob = do_ref[pl.ds(t0, CHUNK), :]
            probs = _chunk_scores(qc, kw, b_ref[...], t0, scale)
            dprobs = _dot(dob, vw, NT)
            dsc = probs * (dprobs - jnp.sum(probs * dprobs, axis=-1, keepdims=True))
            db_ref[...] += dsc
            dsb = (dsc * scale).astype(BF16)
            dqn_ref[pl.ds(t0, CHUNK), :] = _dot(dsb, kw)
            dkn_ref[pl.ds(t0, BAND_W), :] += _dot(dsb, qc, TN)
            dvp_ref[pl.ds(t0, BAND_W), :] += _dot(probs.astype(BF16), dob, TN)
            return 0

        lax.fori_loop(0, s // CHUNK, chunk, 0)
        dq, dqg_rows = _norm_bwd(dqn_ref[...], qh, rq, qg_ref[...])
        dk, dkg_rows = _norm_bwd(dkn_ref[pl.ds(PAD_K, s), :], kh, rk, kg_ref[...])
        o_ref[0] = dq.astype(BF16)
        o_ref[1] = dk.astype(BF16)
        o_ref[2] = dvp_ref[pl.ds(PAD_K, s), :].astype(BF16)

        @pl.when(pl.program_id(0) == 0)
        def _():
            dqg_ref[...] = jnp.zeros_like(dqg_ref)
            dkg_ref[...] = jnp.zeros_like(dkg_ref)

        dqg_ref[...] += jnp.sum(dqg_rows, axis=0, keepdims=True)
        dkg_ref[...] += jnp.sum(dkg_rows, axis=0, keepdims=True)

    return pl.pallas_call(
        body, name=f"chunk_backward_l{layer}", grid=(heads,),
        in_specs=[pl.BlockSpec((4, s, HEAD_DIM), lambda h: (1, 0, h)),
                  pl.BlockSpec((None, CHUNK, BAND_W), lambda h: (layer * heads + h, 0, 0)),
                  pl.BlockSpec((1, HEAD_DIM), lambda h: (0, 0)),
                  pl.BlockSpec((1, HEAD_DIM), lambda h: (0, 0)),
                  pl.BlockSpec((s, HEAD_DIM), lambda h: (0, heads + h)),
                  pl.BlockSpec((s, HEAD_DIM), lambda h: (0, heads + h)),
                  _any()],
        out_specs=(pl.BlockSpec((4, s, HEAD_DIM), lambda h: (1, 0, h)),
                   pl.BlockSpec((None, CHUNK, BAND_W), lambda h: (h, 0, 0)),
                   pl.BlockSpec((1, HEAD_DIM), lambda h: (0, 0)),
                   pl.BlockSpec((1, HEAD_DIM), lambda h: (0, 0))),
        out_shape=(jax.ShapeDtypeStruct(dproj.shape, BF16),
                   jax.ShapeDtypeStruct((heads, CHUNK, BAND_W), F32),
                   jax.ShapeDtypeStruct((1, HEAD_DIM), F32), jax.ShapeDtypeStruct((1, HEAD_DIM), F32)),
        input_output_aliases={6: 0},
        scratch_shapes=[pltpu.VMEM((s, HEAD_DIM), BF16), pltpu.VMEM((s + PAD_K, HEAD_DIM), BF16),
                        pltpu.VMEM((s + PAD_K, HEAD_DIM), BF16), pltpu.VMEM((s, HEAD_DIM), BF16),
                        pltpu.VMEM((s, HEAD_DIM), F32), pltpu.VMEM((s + PAD_K, HEAD_DIM), F32),
                        pltpu.VMEM((s + PAD_K, HEAD_DIM), F32)],
        compiler_params=_params("arbitrary"),
    )(proj, bias, q_gain, k_gain, y, dmixed, dproj)


def _proj_bwd_input(dproj, w_all, x, g, dx, layer):
    s, d = x.shape
    e = w_all.shape[2]
    tm = _tile(s, 512)

    def body(dp_ref, w_ref, x_ref, g_ref, dx_ref, o_ref, dg_ref, acc_ref):
        j = pl.program_id(1)

        @pl.when(j == 0)
        def _():
            acc_ref[...] = jnp.zeros_like(acc_ref)

        acc_ref[...] += _dot(dp_ref[...], w_ref[...], NT)

        @pl.when(jnp.logical_and(j == N_DEV - 1, pl.program_id(0) == 0))
        def _():
            dg_ref[...] = jnp.zeros_like(dg_ref)

        @pl.when(j == N_DEV - 1)
        def _():
            xv = x_ref[...]
            r = lax.rsqrt(jnp.mean(xv * xv, axis=-1, keepdims=True) + NORM_EPS)
            dxn, dg_rows = _norm_bwd(acc_ref[...], xv * r, r, g_ref[...])
            o_ref[...] = dx_ref[...] + dxn
            dg_ref[...] += jnp.sum(dg_rows, axis=0, keepdims=True)

    return pl.pallas_call(
        body, name=f"proj_dx_l{layer}", grid=(s // tm, N_DEV),
        in_specs=[pl.BlockSpec((None, tm, e), lambda m, j: (j, m, 0)),
                  pl.BlockSpec((None, d, e), lambda m, j: (j, 0, 0)),
                  pl.BlockSpec((tm, d), lambda m, j: (m, 0)),
                  pl.BlockSpec((1, d), lambda m, j: (0, 0)),
                  pl.BlockSpec((tm, d), lambda m, j: (m, 0))],
        out_specs=(pl.BlockSpec((tm, d), lambda m, j: (m, 0)), pl.BlockSpec((1, d), lambda m, j: (0, 0))),
        out_shape=(jax.ShapeDtypeStruct((s, d), F32), jax.ShapeDtypeStruct((1, d), F32)),
        scratch_shapes=[pltpu.VMEM((tm, d), F32)],
        compiler_params=_params("arbitrary", "arbitrary"),
    )(dproj, w_all, x, g, dx)


def _proj_bwd_weight(h, dproj, layer):
    s, d = h.shape
    e = dproj.shape[2]
    td, tn = _tile(d, 1024), _tile(e, 1024)
    nb = e // tn

    def body(h_ref, dp_ref, o_ref):
        o_ref[...] = _dot(h_ref[...], dp_ref[...], TN).astype(BF16)

    return pl.pallas_call(
        body, name=f"proj_dw_l{layer}", grid=(d // td, N_DEV * nb),
        in_specs=[pl.BlockSpec((s, td), lambda i, n: (0, i)),
                  pl.BlockSpec((None, s, tn), lambda i, n: (n // nb, 0, n % nb))],
        out_specs=pl.BlockSpec((None, td, tn), lambda i, n: (n // nb, i, n % nb)),
        out_shape=jax.ShapeDtypeStruct((N_DEV, d, e), BF16),
        compiler_params=_params("parallel", "parallel"),
    )(h, dproj)


def _adamw_math(w, g, m, v):
    m = ADAM_B1 * m + (1.0 - ADAM_B1) * g
    v = ADAM_B2 * v + (1.0 - ADAM_B2) * (g * g)
    m_hat = m / (1.0 - ADAM_B1 ** ADAM_STEP)
    v_hat = v / (1.0 - ADAM_B2 ** ADAM_STEP)
    return -ADAM_LR * (m_hat / (jnp.sqrt(v_hat) + ADAM_EPS) + ADAM_WD * w), m, v


def _adamw_layer(parts, w, m, v, prev, layer, name):
    n_layers, rows, cols = w.shape
    tr = _tile(rows, max(8, (256 * 1024) // cols))

    def body(p_ref, w_ref, m_ref, v_ref, *rest):
        g_ref, d_ref, nm_ref, nv_ref = rest[-4:]
        g = p_ref[0].astype(F32)
        for j in range(1, N_DEV):
            g = g + p_ref[j].astype(F32)
        g_ref[...] = g
        d_ref[...], nm_ref[...], nv_ref[...] = _adamw_math(w_ref[...], g, m_ref[...], v_ref[...])

    blk = pl.BlockSpec((None, tr, cols), lambda i: (layer, i, 0))
    out_shape = tuple(jax.ShapeDtypeStruct(w.shape, F32) for _ in range(4))
    in_specs = [pl.BlockSpec((N_DEV, tr, cols), lambda i: (0, i, 0)), blk, blk, blk]
    args = [parts, w, m, v]
    aliases = {}
    if prev is not None:
        in_specs += [_any()] * 4
        args += list(prev)
        aliases = {4 + k: k for k in range(4)}
    return pl.pallas_call(
        body, name=f"{name}_l{layer}", grid=(rows // tr,),
        in_specs=in_specs, out_specs=(blk, blk, blk, blk), out_shape=out_shape,
        input_output_aliases=aliases,
        compiler_params=_params("parallel"),
    )(*args)


def _sum_slots(parts):
    def body(p_ref, o_ref):
        g = p_ref[0]
        for j in range(1, N_DEV):
            g = g + p_ref[j]
        o_ref[...] = g

    return pl.pallas_call(
        body, name="sum_small_grads",
        in_specs=[_vmem()], out_specs=_vmem(),
        out_shape=jax.ShapeDtypeStruct(parts.shape[1:], F32),
        compiler_params=_params(),
    )(parts)


def _adamw_small(w, g, m, v):
    def body(w_ref, g_ref, m_ref, v_ref, d_ref, nm_ref, nv_ref):
        d_ref[...], nm_ref[...], nv_ref[...] = _adamw_math(w_ref[...], g_ref[...], m_ref[...], v_ref[...])

    return pl.pallas_call(
        body, name="adamw_small",
        in_specs=[_vmem()] * 4, out_specs=(_vmem(),) * 3,
        out_shape=tuple(jax.ShapeDtypeStruct(w.shape, F32) for _ in range(3)),
        compiler_params=_params(),
    )(w, g, m, v)


def _pack_rows(arrays):
    rows = []
    for a in arrays:
        flat = a.reshape(-1)
        pad = (-flat.shape[0]) % (8 * LANES)
        rows.append(jnp.pad(flat, (0, pad)).reshape(-1, LANES))
    return jnp.concatenate(rows, axis=0)


def _unpack_rows(packed, like):
    out, r0 = [], 0
    for a in like:
        n = a.size
        nr = -(-n // (8 * LANES)) * 8
        out.append(packed[r0:r0 + nr].reshape(-1)[:n].reshape(a.shape))
        r0 += nr
    return out


def kernel(x, norm_g, w_in, q_norm_g, k_norm_g, rel_bias, w_out, loss_target, m_norm_g, m_w_in, m_q_norm_g, m_k_norm_g, m_rel_bias, m_w_out, v_norm_g, v_w_in, v_q_norm_g, v_k_norm_g, v_rel_bias, v_w_out):
    depth, d, e = w_in.shape
    r_out = w_out.shape[1]
    heads = e // HEAD_DIM
    rel_w = rel_bias.shape[2]
    x0 = x[0]
    target = loss_target[0]
    s = x0.shape[0]

    def begin_gather(l, dep):
        shards = (_cast_layer(w_in, l, "cast_w_in"), _cast_layer(w_out, l, "cast_w_out"))
        lands = _place_own(shards, False, dep, f"place_weights_l{l}")
        return _copies_start(shards, lands, False, f"gather_start_l{l}")

    gathering = begin_gather(0, None)
    rel_all = _gather_small(rel_bias, "gather_rel_bias")
    rel_full = jnp.transpose(rel_all, (1, 2, 0, 3)).reshape(depth * heads, N_DEV * rel_w)
    bias = jnp.transpose(_bias_expand(rel_full), (1, 0, 2))

    xs, hs, projs, ys, mixes, tots, weights = [], [], [], [], [], [], []
    xl = x0
    for l in range(depth):
        win_all, wout_all = _copies_wait(gathering, xl, False, f"gather_wait_l{l}")
        tie = None
        if l + 1 < depth:
            gathering = begin_gather(l + 1, win_all)
            tie = gathering[-1]
        wout_full = wout_all.reshape(d, d)
        proj, h = _norm_proj(xl, norm_g[l:l + 1], win_all, tie, l)
        y, mixed, tot = _sb_forward(proj, l)
        y, mixed = _chunk_forward(proj, bias, q_norm_g[l:l + 1], k_norm_g[l:l + 1], y, mixed, l)
        xs.append(xl), hs.append(h), projs.append(proj), ys.append(y), mixes.append(mixed), tots.append(tot)
        weights.append((win_all, wout_full))
        xl = _out_proj(mixed, wout_full, xl, l)

    dx, loss_parts = _loss_head(xl, target)
    loss = lax.psum(jnp.sum(loss_parts), AXES)

    dbias, dng, dqg, dkg = [None] * depth, [None] * depth, [None] * depth, [None] * depth
    res_in, res_out = None, None

    def finish_exchange(exchanging, after, l):
        rin, rout = _copies_wait(exchanging, after, True, f"exchange_wait_l{l}")
        return (_adamw_layer(rin, w_in, m_w_in, v_w_in, res_in, l, "adamw_w_in"),
                _adamw_layer(rout, w_out, m_w_out, v_w_out, res_out, l, "adamw_w_out"))

    exchanging, tie = None, None
    for l in reversed(range(depth)):
        win_all, wout_full = weights[l]
        dmixed = _out_proj_bwd_input(dx, wout_full, tie, l)
        gwout = _out_proj_bwd_weight(mixes[l], dx, l).reshape(N_DEV, r_out, d)
        dproj = _sb_backward(projs[l], ys[l], dmixed, tots[l], l)
        dproj, dbias[l], dqg[l], dkg[l] = _chunk_backward(
            projs[l], bias, q_norm_g[l:l + 1], k_norm_g[l:l + 1], ys[l], dmixed, dproj, l)
        grads_l = (_proj_bwd_weight(hs[l], dproj, l), gwout)
        dx, dng[l] = _proj_bwd_input(dproj, win_all, xs[l], norm_g[l:l + 1], dx, l)
        if exchanging is not None:
            res_in, res_out = finish_exchange(exchanging, dx, l + 1)
        exchanging = _copies_start(grads_l, _place_own(grads_l, True, None, f"place_grads_l{l}"), True,
                                   f"exchange_start_l{l}")
        tie = exchanging[-1]
    drel = _bias_grad(jnp.transpose(jnp.concatenate(dbias, axis=0), (1, 0, 2)))
    small_like = [norm_g, q_norm_g, k_norm_g, drel]
    mine = _pack_rows([jnp.concatenate(dng, axis=0), jnp.concatenate(dqg, axis=0),
                       jnp.concatenate(dkg, axis=0), drel])
    g_norm, g_qn, g_kn, g_rel_full = _unpack_rows(_sum_slots(_gather_small(mine, "gather_small_grads")), small_like)
    my_block = _flat(_my_place())
    g_rel = lax.dynamic_slice_in_dim(g_rel_full.reshape(depth, heads, N_REL), my_block * rel_w, rel_w, axis=2)
    small_w = [norm_g, q_norm_g, k_norm_g, rel_bias]
    small = _adamw_small(_pack_rows(small_w), _pack_rows([g_norm, g_qn, g_kn, g_rel]),
                         _pack_rows([m_norm_g, m_q_norm_g, m_k_norm_g, m_rel_bias]),
                         _pack_rows([v_norm_g, v_q_norm_g, v_k_norm_g, v_rel_bias]))
    d_small, nm_small, nv_small = (_unpack_rows(p, small_w) for p in small)

    res_in, res_out = finish_exchange(exchanging, small[0], 0)
    g_win, d_win, nm_win, nv_win = res_in
    g_wout, d_wout, nm_wout, nv_wout = res_out
    grads = (g_norm, g_win, g_qn, g_kn, g_rel, g_wout)

    def order(sm, big_in, big_out):
        return (sm[0], big_in, sm[1], sm[2], sm[3], big_out)

    return (loss, dx[None], *grads, *order(d_small, d_win, d_wout),
            *order(nm_small, nm_win, nm_wout), *order(nv_small, nv_win, nv_wout))
```

```python
import functools

import jax
import jax.numpy as jnp
from jax import lax
from jax.experimental import pallas as pl
from jax.experimental.pallas import tpu as pltpu

F32 = jnp.float32
BF16 = jnp.bfloat16
MESH_ID = pl.DeviceIdType.MESH
AXES = ("x", "y", "c")

N_DEV = 8
HEAD_DIM = 128
CHUNK = 64
LEFT_CHUNKS = 8
BAND_W = (LEFT_CHUNKS + 1) * CHUNK
PAD_K = LEFT_CHUNKS * CHUNK
REL_CLIP = 256
N_REL = REL_CLIP + CHUNK
NORM_EPS = 1e-6
NEG_BIG = -1e30
SB_TILE = 128
SB_Q = 512
SB_K = 128
SB_HEADS = 2
LANES = 128

ADAM_LR = 0.001
ADAM_B1 = 0.9
ADAM_B2 = 0.999
ADAM_EPS = 1e-08
ADAM_WD = 0.01
ADAM_STEP = 10

VMEM_LIMIT_BYTES = 56 * 1024 * 1024

NT = (((1,), (1,)), ((), ()))
TN = (((0,), (0,)), ((), ()))


def _params(*sem, **kw):
    return pltpu.CompilerParams(dimension_semantics=sem or None, vmem_limit_bytes=VMEM_LIMIT_BYTES, **kw)


def _any():
    return pl.BlockSpec(memory_space=pl.ANY)


def _vmem():
    return pl.BlockSpec(memory_space=pltpu.VMEM)


def _tile(n, want):
    return want if n % want == 0 else n


def _dot(a, b, dims=None):
    if dims is None:
        return jnp.dot(a, b, preferred_element_type=F32)
    return lax.dot_general(a, b, dims, preferred_element_type=F32)


def _split_dot(a, b, parts, dims=None):
    acc = None
    rest = a
    for _ in range(parts):
        piece = rest.astype(BF16)
        rest = rest - piece.astype(F32)
        term = _dot(piece, b, dims)
        acc = term if acc is None else acc + term
    return acc


def _log_sigmoid(z):
    return jnp.minimum(z, 0.0) - jnp.log(1.0 + jnp.exp(-jnp.abs(z)))


def _silu_and_grad(g):
    sig = jax.nn.sigmoid(g)
    return g * sig, sig * (1.0 + g * (1.0 - sig))


def _my_place():
    return lax.axis_index("x"), lax.axis_index("y"), lax.axis_index("c")


def _flat(place):
    return 4 * place[0] + 2 * place[1] + place[2]


def _flip(place, k):
    return tuple(1 - p if (k >> s) & 1 else p for p, s in zip(place, (2, 1, 0)))


def _cast_layer(w, me, dep, layer, name):
    _, rows, cols = w.shape
    tr = _tile(rows, 1024)
    deps = [] if dep is None else [dep]

    def body(me_ref, a_ref, *rest):
        del me_ref
        shard_ref, land_ref = rest[-2:]
        shard_ref[...] = a_ref[...].astype(BF16)
        land_ref[...] = shard_ref[...]

    return pl.pallas_call(
        body, name=f"{name}_l{layer}",
        grid_spec=pltpu.PrefetchScalarGridSpec(
            num_scalar_prefetch=1, grid=(rows // tr,),
            in_specs=[pl.BlockSpec((None, tr, cols), lambda i, me_ref: (layer, i, 0))] + [_any()] * len(deps),
            out_specs=(pl.BlockSpec((tr, cols), lambda i, me_ref: (i, 0)),
                       pl.BlockSpec((None, tr, cols), lambda i, me_ref: (me_ref[0], i, 0)))),
        out_shape=(jax.ShapeDtypeStruct((rows, cols), BF16), jax.ShapeDtypeStruct((N_DEV, rows, cols), BF16)),
        compiler_params=_params("parallel"),
    )(me, w, *deps)


HBM_SPEC = pl.BlockSpec(memory_space=pltpu.HBM)
SEM_SPEC = pl.BlockSpec(memory_space=pltpu.SEMAPHORE)
N_PEERS = N_DEV - 1


def _peer_copies(srcs, lands, send_sems, recv_sems, slot_of_src):
    me = _my_place()
    copies = []
    for k in range(1, N_DEV):
        peer = _flip(me, k)
        for t in range(len(srcs)):
            copies.append(pltpu.make_async_remote_copy(
                src_ref=srcs[t].at[_flat(peer)] if slot_of_src else srcs[t], dst_ref=lands[t].at[_flat(me)],
                send_sem=send_sems.at[t * N_PEERS + k - 1], recv_sem=recv_sems.at[t * N_PEERS + k - 1],
                device_id=peer, device_id_type=MESH_ID))
    return copies


def _copies_start(srcs, lands, slot_of_src, name):
    n = len(srcs)

    def body(*refs):
        for cp in _peer_copies(refs[:n], refs[n:2 * n], refs[2 * n], refs[2 * n + 1], slot_of_src):
            cp.start()
        refs[-1][...] = jnp.zeros_like(refs[-1])

    thru = [pltpu.HBM(a.shape, a.dtype) for a in list(srcs) + list(lands)]
    out = pl.pallas_call(
        body, name=name,
        in_specs=[HBM_SPEC] * (2 * n),
        out_specs=(SEM_SPEC, SEM_SPEC, *([HBM_SPEC] * (2 * n)), _vmem()),
        out_shape=(pltpu.SemaphoreType.DMA((n * N_PEERS,)), pltpu.SemaphoreType.DMA((n * N_PEERS,)), *thru,
                   jax.ShapeDtypeStruct((8, LANES), F32)),
        input_output_aliases={i: 2 + i for i in range(2 * n)},
        compiler_params=pltpu.CompilerParams(has_side_effects=pltpu.SideEffectType.DATAFLOW_SIDE_EFFECTING),
    )(*[pltpu.with_memory_space_constraint(a, pltpu.HBM) for a in list(srcs) + list(lands)])
    return out[0], out[1], out[2:2 + n], out[2 + n:2 + 2 * n], out[-1]


def _copies_wait(pending, after, slot_of_src, name):
    send_sems, recv_sems, srcs, lands, _ = pending
    n = len(srcs)

    def body(*refs):
        for cp in _peer_copies(refs[:n], refs[n:2 * n], refs[2 * n], refs[2 * n + 1], slot_of_src):
            cp.wait_send()
            cp.wait_recv()

    thru = [pltpu.HBM(a.shape, a.dtype) for a in list(srcs) + list(lands)]
    out = pl.pallas_call(
        body, name=name,
        in_specs=[HBM_SPEC] * (2 * n) + [SEM_SPEC, SEM_SPEC, _any()],
        out_specs=tuple([HBM_SPEC] * (2 * n)),
        out_shape=tuple(thru),
        input_output_aliases={i: i for i in range(2 * n)},
        compiler_params=pltpu.CompilerParams(has_side_effects=pltpu.SideEffectType.DATAFLOW_SIDE_EFFECTING),
    )(*srcs, *lands, send_sems, recv_sems, after)
    return out[:n], out[n:]


def _gather_small(v, name):
    def body(v_ref, o_ref, send_sems, recv_sems):
        me = _my_place()
        o_ref[_flat(me)] = v_ref[...]
        copies = []
        for k in range(1, N_DEV):
            copies.append(pltpu.make_async_remote_copy(
                src_ref=v_ref, dst_ref=o_ref.at[_flat(me)],
                send_sem=send_sems.at[k - 1], recv_sem=recv_sems.at[k - 1],
                device_id=_flip(me, k), device_id_type=MESH_ID))
        for cp in copies:
            cp.start()
        for cp in copies:
            cp.wait()

    return pl.pallas_call(
        body, name=name,
        in_specs=[_vmem()], out_specs=_vmem(),
        out_shape=jax.ShapeDtypeStruct((N_DEV,) + v.shape, v.dtype),
        scratch_shapes=[pltpu.SemaphoreType.DMA((7,)), pltpu.SemaphoreType.DMA((7,))],
        compiler_params=_params(has_side_effects=True),
    )(v)


def _rel_onehot(i):
    r_io = lax.broadcasted_iota(jnp.int32, (N_REL, BAND_W), 0)
    p_io = lax.broadcasted_iota(jnp.int32, (N_REL, BAND_W), 1)
    idx = jnp.clip(PAD_K + i - p_io, -(CHUNK - 1), REL_CLIP) + (CHUNK - 1)
    return (r_io == idx).astype(BF16)


def _bias_expand(rel):
    lh = rel.shape[0]

    def body(rel_ref, o_ref):
        o_ref[...] = _split_dot(rel_ref[...], _rel_onehot(pl.program_id(0)), 3)

    return pl.pallas_call(
        body, name="bias_expand", grid=(CHUNK,),
        in_specs=[pl.BlockSpec((lh, N_REL), lambda i: (0, 0))],
        out_specs=pl.BlockSpec((None, lh, BAND_W), lambda i: (i, 0, 0)),
        out_shape=jax.ShapeDtypeStruct((CHUNK, lh, BAND_W), F32),
        compiler_params=_params("parallel"),
    )(rel)


def _bias_grad(dbias):
    lh = dbias.shape[1]

    def body(db_ref, o_ref):
        i = pl.program_id(0)

        @pl.when(i == 0)
        def _():
            o_ref[...] = jnp.zeros_like(o_ref)

        o_ref[...] += _split_dot(db_ref[...], _rel_onehot(i), 3, NT)

    return pl.pallas_call(
        body, name="bias_grad", grid=(CHUNK,),
        in_specs=[pl.BlockSpec((None, lh, BAND_W), lambda i: (i, 0, 0))],
        out_specs=pl.BlockSpec((lh, N_REL), lambda i: (0, 0)),
        out_shape=jax.ShapeDtypeStruct((lh, N_REL), F32),
        compiler_params=_params("arbitrary"),
    )(dbias)


def _norm_proj(x, g, w_all, tie, layer):
    s, d = x.shape
    e = w_all.shape[2]
    tm, tn = _tile(s, 512), _tile(e, 1024)
    nb = e // tn
    ties = [] if tie is None else [tie]

    def body(x_ref, g_ref, w_ref, *rest):
        proj_ref, h_ref = rest[-2:]

        @pl.when(pl.program_id(1) == 0)
        def _():
            xv = x_ref[...]
            r = lax.rsqrt(jnp.mean(xv * xv, axis=-1, keepdims=True) + NORM_EPS)
            h_ref[...] = ((xv * r) * g_ref[...]).astype(BF16)

        proj_ref[...] = _dot(h_ref[...], w_ref[...])

    return pl.pallas_call(
        body, name=f"norm_proj_l{layer}", grid=(s // tm, N_DEV * nb),
        in_specs=[pl.BlockSpec((tm, d), lambda m, n: (m, 0)),
                  pl.BlockSpec((1, d), lambda m, n: (0, 0)),
                  pl.BlockSpec((None, d, tn), lambda m, n: (n // nb, 0, n % nb))] + [_any()] * len(ties),
        out_specs=(pl.BlockSpec((None, tm, tn), lambda m, n: (n // nb, m, n % nb)),
                   pl.BlockSpec((tm, d), lambda m, n: (m, 0))),
        out_shape=(jax.ShapeDtypeStruct((N_DEV, s, e), F32), jax.ShapeDtypeStruct((s, d), BF16)),
        compiler_params=_params("parallel", "arbitrary"),
    )(x, g, w_all, *ties)


def _sb_forward(proj, layer):
    _, s, e = proj.shape
    hp = _tile(e // HEAD_DIM, SB_HEADS)
    width = hp * HEAD_DIM
    tq = _tile(s, SB_Q)
    diag_tiles = tq // SB_K
    scale = HEAD_DIM ** -0.5

    def body(p_ref, y_ref, mix_ref, tot_ref, kb_ref, vb_ref):
        kb_ref[...] = p_ref[1].astype(BF16)
        vb_ref[...] = p_ref[2].astype(BF16)
        row = lax.broadcasted_iota(jnp.int32, (tq, SB_K), 0)
        col = lax.broadcasted_iota(jnp.int32, (tq, SB_K), 1)
        kj = lax.broadcasted_iota(jnp.int32, (SB_K, SB_K), 0)
        ks = lax.broadcasted_iota(jnp.int32, (SB_K, SB_K), 1)
        later = (kj > ks).astype(BF16)

        def q_block(qi, _):
            t0 = pl.multiple_of(qi * tq, tq)
            qb = [p_ref[0, pl.ds(t0, tq), h * HEAD_DIM:(h + 1) * HEAD_DIM].astype(BF16) for h in range(hp)]

            def tile(s0, state, causal):
                out = []
                for h in range(hp):
                    carry, acc = state[h]
                    lanes = slice(h * HEAD_DIM, (h + 1) * HEAD_DIM)
                    z = _dot(qb[h], kb_ref[pl.ds(s0, SB_K), lanes], NT) * scale
                    ls = _log_sigmoid(z)
                    stay = ls - z
                    if causal is not None:
                        stay = jnp.where(causal, stay, 0.0)
                    w = jnp.exp(ls + carry + _split_dot(stay, later, 2))
                    if causal is not None:
                        w = jnp.where(causal, w, 0.0)
                    acc = acc + _dot(w.astype(BF16), vb_ref[pl.ds(s0, SB_K), lanes])
                    out.append((carry + jnp.sum(stay, axis=1, keepdims=True), acc))
                return tuple(out)

            state = tuple((jnp.zeros((tq, 1), F32), jnp.zeros((tq, HEAD_DIM), F32)) for _ in range(hp))
            for dt in reversed(range(diag_tiles)):
                state = tile(t0 + dt * SB_K, state, col + dt * SB_K < row)

            def k_block(j, st):
                return tile(pl.multiple_of((diag_tiles * qi - 1 - j) * SB_K, SB_K), st, None)

            state = lax.fori_loop(0, diag_tiles * qi, k_block, state)
            silu, _ = _silu_and_grad(p_ref[3, pl.ds(t0, tq), :])
            for h in range(hp):
                lanes = slice(h * HEAD_DIM, (h + 1) * HEAD_DIM)
                y_ref[pl.ds(t0, tq), lanes] = state[h][1]
                mix_ref[pl.ds(t0, tq), lanes] = (state[h][1] * silu[:, lanes]).astype(BF16)
                tot_ref[h, pl.ds(t0, tq), :] = state[h][0]
            return 0

        lax.fori_loop(0, s // tq, q_block, 0)

    return pl.pallas_call(
        body, name=f"sb_forward_l{layer}", grid=(e // width,),
        in_specs=[pl.BlockSpec((4, s, width), lambda h: (0, 0, h))],
        out_specs=(pl.BlockSpec((s, width), lambda h: (0, h)),
                   pl.BlockSpec((s, width), lambda h: (0, h)),
                   pl.BlockSpec((hp, s, 1), lambda h: (h, 0, 0))),
        out_shape=(jax.ShapeDtypeStruct((s, 2 * e), F32), jax.ShapeDtypeStruct((s, 2 * e), BF16),
                   jax.ShapeDtypeStruct((e // HEAD_DIM, s, 1), F32)),
        scratch_shapes=[pltpu.VMEM((s, width), BF16), pltpu.VMEM((s, width), BF16)],
        compiler_params=_params("parallel"),
    )(proj)


def _qk_norm(t, gain):
    r = lax.rsqrt(jnp.mean(t * t, axis=-1, keepdims=True) + NORM_EPS)
    return t * r, r, (t * r) * gain


def _chunk_scores(qc, kw, bias, t0, scale):
    sc = _dot(qc, kw, NT) * scale + bias
    col = lax.broadcasted_iota(jnp.int32, (CHUNK, BAND_W), 1)
    sc = jnp.where(col + t0 >= PAD_K, sc, NEG_BIG)
    ex = jnp.exp(sc - jnp.max(sc, axis=-1, keepdims=True))
    return ex / jnp.sum(ex, axis=-1, keepdims=True)


def _chunk_forward(proj, bias, q_gain, k_gain, y, mixed, layer):
    _, s, e = proj.shape
    heads = e // HEAD_DIM
    scale = HEAD_DIM ** -0.5

    def body(p_ref, b_ref, qg_ref, kg_ref, y_in, mix_in, y_ref, mix_ref, qn_ref, kp_ref, vp_ref):
        del y_in, mix_in
        qn_ref[...] = _qk_norm(p_ref[0], qg_ref[...])[2].astype(BF16)
        kp_ref[pl.ds(0, PAD_K), :] = jnp.zeros((PAD_K, HEAD_DIM), BF16)
        vp_ref[pl.ds(0, PAD_K), :] = jnp.zeros((PAD_K, HEAD_DIM), BF16)
        kp_ref[pl.ds(PAD_K, s), :] = _qk_norm(p_ref[1], kg_ref[...])[2].astype(BF16)
        vp_ref[pl.ds(PAD_K, s), :] = p_ref[2].astype(BF16)

        def chunk(ci, _):
            t0 = pl.multiple_of(ci * CHUNK, CHUNK)
            probs = _chunk_scores(qn_ref[pl.ds(t0, CHUNK), :], kp_ref[pl.ds(t0, BAND_W), :], b_ref[...], t0, scale)
            out = _dot(probs.astype(BF16), vp_ref[pl.ds(t0, BAND_W), :])
            y_ref[pl.ds(t0, CHUNK), :] = out
            silu, _ = _silu_and_grad(p_ref[3, pl.ds(t0, CHUNK), :])
            mix_ref[pl.ds(t0, CHUNK), :] = (out * silu).astype(BF16)
            return 0

        lax.fori_loop(0, s // CHUNK, chunk, 0)

    return pl.pallas_call(
        body, name=f"chunk_forward_l{layer}", grid=(heads,),
        in_specs=[pl.BlockSpec((4, s, HEAD_DIM), lambda h: (1, 0, h)),
                  pl.BlockSpec((None, CHUNK, BAND_W), lambda h: (layer * heads + h, 0, 0)),
                  pl.BlockSpec((1, HEAD_DIM), lambda h: (0, 0)),
                  pl.BlockSpec((1, HEAD_DIM), lambda h: (0, 0)),
                  _any(), _any()],
        out_specs=(pl.BlockSpec((s, HEAD_DIM), lambda h: (0, heads + h)),
                   pl.BlockSpec((s, HEAD_DIM), lambda h: (0, heads + h))),
        out_shape=(jax.ShapeDtypeStruct(y.shape, F32), jax.ShapeDtypeStruct(mixed.shape, BF16)),
        input_output_aliases={4: 0, 5: 1},
        scratch_shapes=[pltpu.VMEM((s, HEAD_DIM), BF16), pltpu.VMEM((s + PAD_K, HEAD_DIM), BF16),
                        pltpu.VMEM((s + PAD_K, HEAD_DIM), BF16)],
        compiler_params=_params("parallel"),
    )(proj, bias, q_gain, k_gain, y, mixed)


def _out_proj(mixed, w, x, layer):
    s, d = x.shape
    tm, tn = _tile(s, 512), _tile(d, 1024)

    def body(a_ref, w_ref, x_ref, o_ref):
        o_ref[...] = x_ref[...] + _dot(a_ref[...], w_ref[...])

    return pl.pallas_call(
        body, name=f"out_proj_l{layer}", grid=(s // tm, d // tn),
        in_specs=[pl.BlockSpec((tm, d), lambda m, n: (m, 0)),
                  pl.BlockSpec((d, tn), lambda m, n: (0, n)),
                  pl.BlockSpec((tm, tn), lambda m, n: (m, n))],
        out_specs=pl.BlockSpec((tm, tn), lambda m, n: (m, n)),
        out_shape=jax.ShapeDtypeStruct((s, d), F32),
        compiler_params=_params("parallel", "parallel"),
    )(mixed, w, x)


def _loss_head(y, target):
    s, d = y.shape
    tm = _tile(s, 256)

    def body(y_ref, t_ref, dy_ref, part_ref):
        diff = y_ref[...] - t_ref[...]
        dy_ref[...] = diff * (1.0 / d)
        sq = (diff * diff).reshape(tm // 8, 8, d).sum(axis=0)
        acc = sq[:, 0:LANES]
        for j in range(1, d // LANES):
            acc = acc + sq[:, j * LANES:(j + 1) * LANES]
        part_ref[...] = acc * (0.5 / d)

    return pl.pallas_call(
        body, name="loss_head", grid=(s // tm,),
        in_specs=[pl.BlockSpec((tm, d), lambda i: (i, 0)), pl.BlockSpec((tm, d), lambda i: (i, 0))],
        out_specs=(pl.BlockSpec((tm, d), lambda i: (i, 0)), pl.BlockSpec((None, 8, LANES), lambda i: (i, 0, 0))),
        out_shape=(jax.ShapeDtypeStruct((s, d), F32), jax.ShapeDtypeStruct((s // tm, 8, LANES), F32)),
        compiler_params=_params("parallel"),
    )(y, target)


def _out_proj_bwd_input(dx, w, tie, layer):
    s, d = dx.shape
    tm, tn = _tile(s, 512), _tile(d, 1024)
    ties = [] if tie is None else [tie]

    def body(dx_ref, w_ref, *rest):
        rest[-1][...] = _dot(dx_ref[...].astype(BF16), w_ref[...], NT)

    return pl.pallas_call(
        body, name=f"out_proj_dx_l{layer}", grid=(s // tm, d // tn),
        in_specs=[pl.BlockSpec((tm, d), lambda m, n: (m, 0)),
                  pl.BlockSpec((tn, d), lambda m, n: (n, 0))] + [_any()] * len(ties),
        out_specs=pl.BlockSpec((tm, tn), lambda m, n: (m, n)),
        out_shape=jax.ShapeDtypeStruct((s, d), F32),
        compiler_params=_params("parallel", "parallel"),
    )(dx, w, *ties)


def _out_proj_bwd_weight(mixed, dx, layer):
    s, d = dx.shape
    te, tn = _tile(d, 512), _tile(d, 1024)

    def body(a_ref, dx_ref, o_ref):
        o_ref[...] = _dot(a_ref[...], dx_ref[...].astype(BF16), TN).astype(BF16)

    return pl.pallas_call(
        body, name=f"out_proj_dw_l{layer}", grid=(d // te, d // tn),
        in_specs=[pl.BlockSpec((s, te), lambda i, n: (0, i)), pl.BlockSpec((s, tn), lambda i, n: (0, n))],
        out_specs=pl.BlockSpec((te, tn), lambda i, n: (i, n)),
        out_shape=jax.ShapeDtypeStruct((d, d), BF16),
        compiler_params=_params("parallel", "parallel"),
    )(mixed, dx)


def _sb_backward(proj, y, dmixed, tot, layer):
    _, s, e = proj.shape
    hp = _tile(e // HEAD_DIM, SB_HEADS)
    width = hp * HEAD_DIM
    tq = _tile(s, SB_Q)
    diag_tiles = tq // SB_K
    scale = HEAD_DIM ** -0.5

    def body(p_ref, y_ref, dm_ref, tot_ref, o_ref, kb_ref, vb_ref, do_ref, dk_ref, dv_ref):
        kb_ref[...] = p_ref[1].astype(BF16)
        vb_ref[...] = p_ref[2].astype(BF16)
        silu, dsilu = _silu_and_grad(p_ref[3])
        dm = dm_ref[...]
        do_ref[...] = (dm * silu).astype(BF16)
        o_ref[3] = (dm * y_ref[...] * dsilu).astype(BF16)
        dk_ref[...] = jnp.zeros_like(dk_ref)
        dv_ref[...] = jnp.zeros_like(dv_ref)
        row = lax.broadcasted_iota(jnp.int32, (tq, SB_K), 0)
        col = lax.broadcasted_iota(jnp.int32, (tq, SB_K), 1)
        kj = lax.broadcasted_iota(jnp.int32, (SB_K, SB_K), 0)
        ks = lax.broadcasted_iota(jnp.int32, (SB_K, SB_K), 1)
        upto = (kj <= ks).astype(BF16)
        before = (kj < ks).astype(BF16)

        def q_block(qi, _):
            t0 = pl.multiple_of(qi * tq, tq)
            heads = [slice(h * HEAD_DIM, (h + 1) * HEAD_DIM) for h in range(hp)]
            qb = [p_ref[0, pl.ds(t0, tq), lanes].astype(BF16) for lanes in heads]
            dob = [do_ref[pl.ds(t0, tq), lanes] for lanes in heads]
            total = [tot_ref[h, pl.ds(t0, tq), :] for h in range(hp)]

            def tile(s0, state, causal):
                out = []
                for h, lanes in enumerate(heads):
                    stay_sum, dlw_sum, dq = state[h]
                    kt = kb_ref[pl.ds(s0, SB_K), lanes]
                    vt = vb_ref[pl.ds(s0, SB_K), lanes]
                    z = _dot(qb[h], kt, NT) * scale
                    ls = _log_sigmoid(z)
                    stay = ls - z
                    if causal is not None:
                        stay = jnp.where(causal, stay, 0.0)
                    after = total[h] - (stay_sum + _split_dot(stay, upto, 3))
                    w = jnp.exp(ls + after)
                    if causal is not None:
                        w = jnp.where(causal, w, 0.0)
                    dlw = _dot(dob[h], vt, NT) * w
                    prior = dlw_sum + _split_dot(dlw, before, 2)
                    sig = jnp.exp(ls)
                    dz = (dlw * (1.0 - sig) - sig * prior) * scale
                    if causal is not None:
                        dz = jnp.where(causal, dz, 0.0)
                    dzb = dz.astype(BF16)
                    dq = dq + _dot(dzb, kt)
                    dk_ref[pl.ds(s0, SB_K), lanes] += _dot(dzb, qb[h], TN)
                    dv_ref[pl.ds(s0, SB_K), lanes] += _dot(w.astype(BF16), dob[h], TN)
                    out.append((stay_sum + jnp.sum(stay, axis=1, keepdims=True),
                                dlw_sum + jnp.sum(dlw, axis=1, keepdims=True), dq))
                return tuple(out)

            def k_block(j, st):
                return tile(pl.multiple_of(j * SB_K, SB_K), st, None)

            zero = jnp.zeros((tq, 1), F32)
            state = tuple((zero, zero, jnp.zeros((tq, HEAD_DIM), F32)) for _ in range(hp))
            state = lax.fori_loop(0, diag_tiles * qi, k_block, state)
            for dt in range(diag_tiles):
                state = tile(t0 + dt * SB_K, state, col + dt * SB_K < row)
            for h, lanes in enumerate(heads):
                o_ref[0, pl.ds(t0, tq), lanes] = state[h][2].astype(BF16)
            return 0

        lax.fori_loop(0, s // tq, q_block, 0)
        o_ref[1] = dk_ref[...].astype(BF16)
        o_ref[2] = dv_ref[...].astype(BF16)

    return pl.pallas_call(
        body, name=f"sb_backward_l{layer}", grid=(e // width,),
        in_specs=[pl.BlockSpec((4, s, width), lambda h: (0, 0, h)),
                  pl.BlockSpec((s, width), lambda h: (0, h)),
                  pl.BlockSpec((s, width), lambda h: (0, h)),
                  pl.BlockSpec((hp, s, 1), lambda h: (h, 0, 0))],
        out_specs=pl.BlockSpec((4, s, width), lambda h: (0, 0, h)),
        out_shape=jax.ShapeDtypeStruct((N_DEV, s, e), BF16),
        scratch_shapes=[pltpu.VMEM((s, width), BF16), pltpu.VMEM((s, width), BF16),
                        pltpu.VMEM((s, width), BF16), pltpu.VMEM((s, width), F32),
                        pltpu.VMEM((s, width), F32)],
        compiler_params=_params("parallel"),
    )(proj, y, dmixed, tot)


def _norm_bwd(dn, xh, r, gain):
    dxh = dn * gain
    return r * (dxh - xh * jnp.mean(dxh * xh, axis=-1, keepdims=True)), dn * xh


def _chunk_backward(proj, bias, q_gain, k_gain, y, dmixed, dproj, layer):
    _, s, e = proj.shape
    heads = e // HEAD_DIM
    scale = HEAD_DIM ** -0.5

    def body(p_ref, b_ref, qg_ref, kg_ref, y_ref, dm_ref, dp_in, o_ref, db_ref, dqg_ref, dkg_ref,
             qn_ref, kp_ref, vp_ref, do_ref, dqn_ref, dkn_ref, dvp_ref):
        del dp_in
        qh, rq, qn = _qk_norm(p_ref[0], qg_ref[...])
        kh, rk, kn = _qk_norm(p_ref[1], kg_ref[...])
        qn_ref[...] = qn.astype(BF16)
        kp_ref[pl.ds(0, PAD_K), :] = jnp.zeros((PAD_K, HEAD_DIM), BF16)
        vp_ref[pl.ds(0, PAD_K), :] = jnp.zeros((PAD_K, HEAD_DIM), BF16)
        kp_ref[pl.ds(PAD_K, s), :] = kn.astype(BF16)
        vp_ref[pl.ds(PAD_K, s), :] = p_ref[2].astype(BF16)
        silu, dsilu = _silu_and_grad(p_ref[3])
        dm = dm_ref[...]
        do_ref[...] = (dm * silu).astype(BF16)
        o_ref[3] = (dm * y_ref[...] * dsilu).astype(BF16)
        dkn_ref[...] = jnp.zeros_like(dkn_ref)
        dvp_ref[...] = jnp.zeros_like(dvp_ref)
        db_ref[...] = jnp.zeros_like(db_ref)

        def chunk(ci, _):
            t0 = pl.multiple_of(ci * CHUNK, CHUNK)
            qc = qn_ref[pl.ds(t0, CHUNK), :]
            kw = kp_ref[pl.ds(t0, BAND_W), :]
            vw = vp_ref[pl.ds(t0, BAND_W), :]
            dob = do_ref[pl.ds(t0, CHUNK), :]
            probs = _chunk_scores(qc, kw, b_ref[...], t0, scale)
            dprobs = _dot(dob, vw, NT)
            dsc = probs * (dprobs - jnp.sum(probs * dprobs, axis=-1, keepdims=True))
            db_ref[...] += dsc
            dsb = (dsc * scale).astype(BF16)
            dqn_ref[pl.ds(t0, CHUNK), :] = _dot(dsb, kw)
            dkn_ref[pl.ds(t0, BAND_W), :] += _dot(dsb, qc, TN)
            dvp_ref[pl.ds(t0, BAND_W), :] += _dot(probs.astype(BF16), dob, TN)
            return 0

        lax.fori_loop(0, s // CHUNK, chunk, 0)
        dq, dqg_rows = _norm_bwd(dqn_ref[...], qh, rq, qg_ref[...])
        dk, dkg_rows = _norm_bwd(dkn_ref[pl.ds(PAD_K, s), :], kh, rk, kg_ref[...])
        o_ref[0] = dq.astype(BF16)
        o_ref[1] = dk.astype(BF16)
        o_ref[2] = dvp_ref[pl.ds(PAD_K, s), :].astype(BF16)

        @pl.when(pl.program_id(0) == 0)
        def _():
            dqg_ref[...] = jnp.zeros_like(dqg_ref)
            dkg_ref[...] = jnp.zeros_like(dkg_ref)

        dqg_ref[...] += jnp.sum(dqg_rows, axis=0, keepdims=True)
        dkg_ref[...] += jnp.sum(dkg_rows, axis=0, keepdims=True)

    return pl.pallas_call(
        body, name=f"chunk_backward_l{layer}", grid=(heads,),
        in_specs=[pl.BlockSpec((4, s, HEAD_DIM), lambda h: (1, 0, h)),
                  pl.BlockSpec((None, CHUNK, BAND_W), lambda h: (layer * heads + h, 0, 0)),
                  pl.BlockSpec((1, HEAD_DIM), lambda h: (0, 0)),
                  pl.BlockSpec((1, HEAD_DIM), lambda h: (0, 0)),
                  pl.BlockSpec((s, HEAD_DIM), lambda h: (0, heads + h)),
                  pl.BlockSpec((s, HEAD_DIM), lambda h: (0, heads + h)),
                  _any()],
        out_specs=(pl.BlockSpec((4, s, HEAD_DIM), lambda h: (1, 0, h)),
                   pl.BlockSpec((None, CHUNK, BAND_W), lambda h: (h, 0, 0)),
                   pl.BlockSpec((1, HEAD_DIM), lambda h: (0, 0)),
                   pl.BlockSpec((1, HEAD_DIM), lambda h: (0, 0))),
        out_shape=(jax.ShapeDtypeStruct(dproj.shape, BF16),
                   jax.ShapeDtypeStruct((heads, CHUNK, BAND_W), F32),
                   jax.ShapeDtypeStruct((1, HEAD_DIM), F32), jax.ShapeDtypeStruct((1, HEAD_DIM), F32)),
        input_output_aliases={6: 0},
        scratch_shapes=[pltpu.VMEM((s, HEAD_DIM), BF16), pltpu.VMEM((s + PAD_K, HEAD_DIM), BF16),
                        pltpu.VMEM((s + PAD_K, HEAD_DIM), BF16), pltpu.VMEM((s, HEAD_DIM), BF16),
                        pltpu.VMEM((s, HEAD_DIM), F32), pltpu.VMEM((s + PAD_K, HEAD_DIM), F32),
                        pltpu.VMEM((s + PAD_K, HEAD_DIM), F32)],
        compiler_params=_params("arbitrary"),
    )(proj, bias, q_gain, k_gain, y, dmixed, dproj)


def _proj_bwd_input(dproj, w_all, x, g, dx, layer):
    s, d = x.shape
    e = w_all.shape[2]
    tm = _tile(s, 512)

    def body(dp_ref, w_ref, x_ref, g_ref, dx_ref, o_ref, dg_ref, acc_ref):
        j = pl.program_id(1)

        @pl.when(j == 0)
        def _():
            acc_ref[...] = jnp.zeros_like(acc_ref)

        acc_ref[...] += _dot(dp_ref[...], w_ref[...], NT)

        @pl.when(jnp.logical_and(j == N_DEV - 1, pl.program_id(0) == 0))
        def _():
            dg_ref[...] = jnp.zeros_like(dg_ref)

        @pl.when(j == N_DEV - 1)
        def _():
            xv = x_ref[...]
            r = lax.rsqrt(jnp.mean(xv * xv, axis=-1, keepdims=True) + NORM_EPS)
            dxn, dg_rows = _norm_bwd(acc_ref[...], xv * r, r, g_ref[...])
            o_ref[...] = dx_ref[...] + dxn
            dg_ref[...] += jnp.sum(dg_rows, axis=0, keepdims=True)

    return pl.pallas_call(
        body, name=f"proj_dx_l{layer}", grid=(s // tm, N_DEV),
        in_specs=[pl.BlockSpec((None, tm, e), lambda m, j: (j, m, 0)),
                  pl.BlockSpec((None, d, e), lambda m, j: (j, 0, 0)),
                  pl.BlockSpec((tm, d), lambda m, j: (m, 0)),
                  pl.BlockSpec((1, d), lambda m, j: (0, 0)),
                  pl.BlockSpec((tm, d), lambda m, j: (m, 0))],
        out_specs=(pl.BlockSpec((tm, d), lambda m, j: (m, 0)), pl.BlockSpec((1, d), lambda m, j: (0, 0))),
        out_shape=(jax.ShapeDtypeStruct((s, d), F32), jax.ShapeDtypeStruct((1, d), F32)),
        scratch_shapes=[pltpu.VMEM((tm, d), F32)],
        compiler_params=_params("arbitrary", "arbitrary"),
    )(dproj, w_all, x, g, dx)


def _proj_bwd_weight(h, dproj, layer):
    s, d = h.shape
    e = dproj.shape[2]
    td, tn = _tile(d, 1024), _tile(e, 1024)
    nb = e // tn

    def body(h_ref, dp_ref, o_ref):
        o_ref[...] = _dot(h_ref[...], dp_ref[...], TN).astype(BF16)

    return pl.pallas_call(
        body, name=f"proj_dw_l{layer}", grid=(d // td, N_DEV * nb),
        in_specs=[pl.BlockSpec((s, td), lambda i, n: (0, i)),
                  pl.BlockSpec((None, s, tn), lambda i, n: (n // nb, 0, n % nb))],
        out_specs=pl.BlockSpec((None, td, tn), lambda i, n: (n // nb, i, n % nb)),
        out_shape=jax.ShapeDtypeStruct((N_DEV, d, e), BF16),
        compiler_params=_params("parallel", "parallel"),
    )(h, dproj)


def _adamw_math(w, g, m, v):
    m = ADAM_B1 * m + (1.0 - ADAM_B1) * g
    v = ADAM_B2 * v + (1.0 - ADAM_B2) * (g * g)
    m_hat = m / (1.0 - ADAM_B1 ** ADAM_STEP)
    v_hat = v / (1.0 - ADAM_B2 ** ADAM_STEP)
    return -ADAM_LR * (m_hat / (jnp.sqrt(v_hat) + ADAM_EPS) + ADAM_WD * w), m, v


def _adamw_layer(parts, own, me, w, m, v, prev, layer, name):
    n_layers, rows, cols = w.shape
    tr = _tile(rows, max(8, (256 * 1024) // cols))

    def body(me_ref, p_ref, own_ref, w_ref, m_ref, v_ref, *rest):
        g_ref, d_ref, nm_ref, nv_ref = rest[-4:]
        mine = own_ref[...].astype(F32)
        g = None
        for j in range(N_DEV):
            term = jnp.where(me_ref[0] == j, mine, p_ref[j].astype(F32))
            g = term if g is None else g + term
        g_ref[...] = g
        d_ref[...], nm_ref[...], nv_ref[...] = _adamw_math(w_ref[...], g, m_ref[...], v_ref[...])

    blk = pl.BlockSpec((None, tr, cols), lambda i, me_ref: (layer, i, 0))
    out_shape = tuple(jax.ShapeDtypeStruct(w.shape, F32) for _ in range(4))
    in_specs = [pl.BlockSpec((N_DEV, tr, cols), lambda i, me_ref: (0, i, 0)),
                pl.BlockSpec((None, tr, cols), lambda i, me_ref: (me_ref[0], i, 0)), blk, blk, blk]
    args = [me, parts, own, w, m, v]
    aliases = {}
    if prev is not None:
        in_specs += [_any()] * 4
        args += list(prev)
        aliases = {6 + k: k for k in range(4)}
    return pl.pallas_call(
        body, name=f"{name}_l{layer}",
        grid_spec=pltpu.PrefetchScalarGridSpec(
            num_scalar_prefetch=1, grid=(rows // tr,), in_specs=in_specs, out_specs=(blk, blk, blk, blk)),
        out_shape=out_shape, input_output_aliases=aliases,
        compiler_params=_params("parallel"),
    )(*args)


def _sum_slots(parts):
    def body(p_ref, o_ref):
        g = p_ref[0]
        for j in range(1, N_DEV):
            g = g + p_ref[j]
        o_ref[...] = g

    return pl.pallas_call(
        body, name="sum_small_grads",
        in_specs=[_vmem()], out_specs=_vmem(),
        out_shape=jax.ShapeDtypeStruct(parts.shape[1:], F32),
        compiler_params=_params(),
    )(parts)


def _adamw_small(w, g, m, v):
    def body(w_ref, g_ref, m_ref, v_ref, d_ref, nm_ref, nv_ref):
        d_ref[...], nm_ref[...], nv_ref[...] = _adamw_math(w_ref[...], g_ref[...], m_ref[...], v_ref[...])

    return pl.pallas_call(
        body, name="adamw_small",
        in_specs=[_vmem()] * 4, out_specs=(_vmem(),) * 3,
        out_shape=tuple(jax.ShapeDtypeStruct(w.shape, F32) for _ in range(3)),
        compiler_params=_params(),
    )(w, g, m, v)


def _pack_rows(arrays):
    rows = []
    for a in arrays:
        flat = a.reshape(-1)
        pad = (-flat.shape[0]) % (8 * LANES)
        rows.append(jnp.pad(flat, (0, pad)).reshape(-1, LANES))
    return jnp.concatenate(rows, axis=0)


def _unpack_rows(packed, like):
    out, r0 = [], 0
    for a in like:
        n = a.size
        nr = -(-n // (8 * LANES)) * 8
        out.append(packed[r0:r0 + nr].reshape(-1)[:n].reshape(a.shape))
        r0 += nr
    return out


def kernel(x, norm_g, w_in, q_norm_g, k_norm_g, rel_bias, w_out, loss_target, m_norm_g, m_w_in, m_q_norm_g, m_k_norm_g, m_rel_bias, m_w_out, v_norm_g, v_w_in, v_q_norm_g, v_k_norm_g, v_rel_bias, v_w_out):
    depth, d, e = w_in.shape
    r_out = w_out.shape[1]
    heads = e // HEAD_DIM
    rel_w = rel_bias.shape[2]
    x0 = x[0]
    target = loss_target[0]
    s = x0.shape[0]

    me = jnp.reshape(_flat(_my_place()), (1,)).astype(jnp.int32)

    def begin_gather(l, dep):
        win_b, win_land = _cast_layer(w_in, me, dep, l, "cast_w_in")
        wout_b, wout_land = _cast_layer(w_out, me, dep, l, "cast_w_out")
        return _copies_start((win_b, wout_b), (win_land, wout_land), False, f"gather_start_l{l}")

    rel_all = _gather_small(rel_bias, "gather_rel_bias")
    gathering = begin_gather(0, rel_all)
    rel_full = jnp.transpose(rel_all, (1, 2, 0, 3)).reshape(depth * heads, N_DEV * rel_w)
    bias = jnp.transpose(_bias_expand(rel_full), (1, 0, 2))

    xs, hs, projs, ys, mixes, tots, weights = [], [], [], [], [], [], []
    xl = x0
    for l in range(depth):
        _, (win_all, wout_all) = _copies_wait(gathering, xl, False, f"gather_wait_l{l}")
        tie = None
        if l + 1 < depth:
            gathering = begin_gather(l + 1, win_all)
            tie = gathering[-1]
        wout_full = wout_all.reshape(d, d)
        proj, h = _norm_proj(xl, norm_g[l:l + 1], win_all, tie, l)
        y, mixed, tot = _sb_forward(proj, l)
        y, mixed = _chunk_forward(proj, bias, q_norm_g[l:l + 1], k_norm_g[l:l + 1], y, mixed, l)
        xs.append(xl), hs.append(h), projs.append(proj), ys.append(y), mixes.append(mixed), tots.append(tot)
        weights.append((win_all, wout_full))
        xl = _out_proj(mixed, wout_full, xl, l)

    dx, loss_parts = _loss_head(xl, target)
    loss = lax.psum(jnp.sum(loss_parts), AXES)

    dbias, dng, dqg, dkg = [None] * depth, [None] * depth, [None] * depth, [None] * depth
    res_in, res_out = None, None

    def finish_exchange(exchanging, after, l):
        (gwin, gwout), (rin, rout) = _copies_wait(exchanging, after, True, f"exchange_wait_l{l}")
        return (_adamw_layer(rin, gwin, me, w_in, m_w_in, v_w_in, res_in, l, "adamw_w_in"),
                _adamw_layer(rout, gwout, me, w_out, m_w_out, v_w_out, res_out, l, "adamw_w_out"))

    pending, tie = [], None
    for l in reversed(range(depth)):
        win_all, wout_full = weights[l]
        dmixed = _out_proj_bwd_input(dx, wout_full, tie, l)
        gwout = _out_proj_bwd_weight(mixes[l], dx, l).reshape(N_DEV, r_out, d)
        dproj = _sb_backward(projs[l], ys[l], dmixed, tots[l], l)
        dproj, dbias[l], dqg[l], dkg[l] = _chunk_backward(
            projs[l], bias, q_norm_g[l:l + 1], k_norm_g[l:l + 1], ys[l], dmixed, dproj, l)
        grads_l = (_proj_bwd_weight(hs[l], dproj, l), gwout)
        dx, dng[l] = _proj_bwd_input(dproj, win_all, xs[l], norm_g[l:l + 1], dx, l)
        exchanging = _copies_start(grads_l, [lax.empty(g.shape, g.dtype) for g in grads_l], True,
                                   f"exchange_start_l{l}")
        pending.append((exchanging, l))
        tie = exchanging[-1]
    for exchanging, l in pending[:-1]:
        res_in, res_out = finish_exchange(exchanging, tie, l)
    drel = _bias_grad(jnp.transpose(jnp.concatenate(dbias, axis=0), (1, 0, 2)))
    small_like = [norm_g, q_norm_g, k_norm_g, drel]
    mine = _pack_rows([jnp.concatenate(dng, axis=0), jnp.concatenate(dqg, axis=0),
                       jnp.concatenate(dkg, axis=0), drel])
    g_norm, g_qn, g_kn, g_rel_full = _unpack_rows(_sum_slots(_gather_small(mine, "gather_small_grads")), small_like)
    my_block = _flat(_my_place())
    g_rel = lax.dynamic_slice_in_dim(g_rel_full.reshape(depth, heads, N_REL), my_block * rel_w, rel_w, axis=2)
    small_w = [norm_g, q_norm_g, k_norm_g, rel_bias]
    small = _adamw_small(_pack_rows(small_w), _pack_rows([g_norm, g_qn, g_kn, g_rel]),
                         _pack_rows([m_norm_g, m_q_norm_g, m_k_norm_g, m_rel_bias]),
                         _pack_rows([v_norm_g, v_q_norm_g, v_k_norm_g, v_rel_bias]))
    d_small, nm_small, nv_small = (_unpack_rows(p, small_w) for p in small)

    res_in, res_out = finish_exchange(pending[-1][0], small[0], 0)
    g_win, d_win, nm_win, nv_win = res_in
    g_wout, d_wout, nm_wout, nv_wout = res_out
    grads = (g_norm, g_win, g_qn, g_kn, g_rel, g_wout)

    def order(sm, big_in, big_out):
        return (sm[0], big_in, sm[1], sm[2], sm[3], big_out)

    return (loss, dx[None], *grads, *order(d_small, d_win, d_wout),
            *order(nm_small, nm_win, nm_wout), *order(nv_small, nv_win, nv_wout))
```

```python
import functools

import jax
import jax.numpy as jnp
from jax import lax
from jax.experimental import pallas as pl
from jax.experimental.pallas import tpu as pltpu

F32 = jnp.float32
BF16 = jnp.bfloat16
MESH_ID = pl.DeviceIdType.MESH
AXES = ("x", "y", "c")

N_DEV = 8
HEAD_DIM = 128
CHUNK = 64
LEFT_CHUNKS = 8
BAND_W = (LEFT_CHUNKS + 1) * CHUNK
PAD_K = LEFT_CHUNKS * CHUNK
REL_CLIP = 256
N_REL = REL_CLIP + CHUNK
NORM_EPS = 1e-6
NEG_BIG = -1e30
CHUNK_HEADS = 2
CHUNK_UNROLL = 2
SB_Q = 512
SB_K = 128
SB_HEADS = 2
LANES = 128

ADAM_LR = 0.001
ADAM_B1 = 0.9
ADAM_B2 = 0.999
ADAM_EPS = 1e-08
ADAM_WD = 0.01
ADAM_STEP = 10

VMEM_LIMIT_BYTES = 56 * 1024 * 1024

NT = (((1,), (1,)), ((), ()))
TN = (((0,), (0,)), ((), ()))


def _params(*sem, **kw):
    return pltpu.CompilerParams(dimension_semantics=sem or None, vmem_limit_bytes=VMEM_LIMIT_BYTES, **kw)


def _any():
    return pl.BlockSpec(memory_space=pl.ANY)


def _vmem():
    return pl.BlockSpec(memory_space=pltpu.VMEM)


def _tile(n, want):
    return want if n % want == 0 else n


def _dot(a, b, dims=None):
    if dims is None:
        return jnp.dot(a, b, preferred_element_type=F32)
    return lax.dot_general(a, b, dims, preferred_element_type=F32)


def _split_dot(a, b, parts, dims=None):
    acc = None
    rest = a
    for _ in range(parts):
        piece = rest.astype(BF16)
        rest = rest - piece.astype(F32)
        term = _dot(piece, b, dims)
        acc = term if acc is None else acc + term
    return acc


def _log_sigmoid(z):
    return jnp.minimum(z, 0.0) - jnp.log(1.0 + jnp.exp(-jnp.abs(z)))


def _silu_and_grad(g):
    sig = jax.nn.sigmoid(g)
    return g * sig, sig * (1.0 + g * (1.0 - sig))


def _my_place():
    return lax.axis_index("x"), lax.axis_index("y"), lax.axis_index("c")


def _flat(place):
    return 4 * place[0] + 2 * place[1] + place[2]


def _flip(place, k):
    return tuple(1 - p if (k >> s) & 1 else p for p, s in zip(place, (2, 1, 0)))


def _cast_layer(w, me, layer, name):
    _, rows, cols = w.shape
    tr = _tile(rows, 1024)

    def body(me_ref, a_ref, shard_ref, land_ref):
        del me_ref
        shard_ref[...] = a_ref[...].astype(BF16)
        land_ref[...] = shard_ref[...]

    return pl.pallas_call(
        body, name=f"{name}_l{layer}",
        grid_spec=pltpu.PrefetchScalarGridSpec(
            num_scalar_prefetch=1, grid=(rows // tr,),
            in_specs=[pl.BlockSpec((None, tr, cols), lambda i, me_ref: (layer, i, 0))],
            out_specs=(pl.BlockSpec((tr, cols), lambda i, me_ref: (i, 0)),
                       pl.BlockSpec((None, tr, cols), lambda i, me_ref: (me_ref[0], i, 0)))),
        out_shape=(jax.ShapeDtypeStruct((rows, cols), BF16), jax.ShapeDtypeStruct((N_DEV, rows, cols), BF16)),
        compiler_params=_params("parallel"),
    )(me, w)


HBM_SPEC = pl.BlockSpec(memory_space=pltpu.HBM)
SEM_SPEC = pl.BlockSpec(memory_space=pltpu.SEMAPHORE)
N_PEERS = N_DEV - 1


def _peer_copies(srcs, lands, send_sems, recv_sems, slot_of_src):
    me = _my_place()
    copies = []
    for k in range(1, N_DEV):
        peer = _flip(me, k)
        for t in range(len(srcs)):
            copies.append(pltpu.make_async_remote_copy(
                src_ref=srcs[t].at[_flat(peer)] if slot_of_src else srcs[t], dst_ref=lands[t].at[_flat(me)],
                send_sem=send_sems.at[t * N_PEERS + k - 1], recv_sem=recv_sems.at[t * N_PEERS + k - 1],
                device_id=peer, device_id_type=MESH_ID))
    return copies


def _copies_start(srcs, lands, slot_of_src, after, name):
    n = len(srcs)
    after = list(after)

    def body(*refs):
        sems = 2 * n + len(after)
        for cp in _peer_copies(refs[:n], refs[n:2 * n], refs[sems], refs[sems + 1], slot_of_src):
            cp.start()
        refs[-1][...] = jnp.zeros_like(refs[-1])

    thru = [pltpu.HBM(a.shape, a.dtype) for a in list(srcs) + list(lands)]
    out = pl.pallas_call(
        body, name=name,
        in_specs=[HBM_SPEC] * (2 * n) + [_any()] * len(after),
        out_specs=(SEM_SPEC, SEM_SPEC, *([HBM_SPEC] * (2 * n)), _vmem()),
        out_shape=(pltpu.SemaphoreType.DMA((n * N_PEERS,)), pltpu.SemaphoreType.DMA((n * N_PEERS,)), *thru,
                   jax.ShapeDtypeStruct((8, LANES), F32)),
        input_output_aliases={i: 2 + i for i in range(2 * n)},
        compiler_params=pltpu.CompilerParams(has_side_effects=pltpu.SideEffectType.DATAFLOW_SIDE_EFFECTING),
    )(*[pltpu.with_memory_space_constraint(a, pltpu.HBM) for a in list(srcs) + list(lands)], *after)
    return out[0], out[1], out[2:2 + n], out[2 + n:2 + 2 * n], out[-1]


def _copies_wait(pending, after, slot_of_src, name):
    send_sems, recv_sems, srcs, lands, _ = pending
    n = len(srcs)
    after = list(after)

    def body(*refs):
        for cp in _peer_copies(refs[:n], refs[n:2 * n], refs[2 * n], refs[2 * n + 1], slot_of_src):
            cp.wait_send()
            cp.wait_recv()

    thru = [pltpu.HBM(a.shape, a.dtype) for a in list(srcs) + list(lands)]
    out = pl.pallas_call(
        body, name=name,
        in_specs=[HBM_SPEC] * (2 * n) + [SEM_SPEC, SEM_SPEC] + [_any()] * len(after),
        out_specs=tuple([HBM_SPEC] * (2 * n)),
        out_shape=tuple(thru),
        input_output_aliases={i: i for i in range(2 * n)},
        compiler_params=pltpu.CompilerParams(has_side_effects=pltpu.SideEffectType.DATAFLOW_SIDE_EFFECTING),
    )(*srcs, *lands, send_sems, recv_sems, *after)
    return out[:n], out[n:]


def _gather_small(v, name):
    def body(v_ref, o_ref, send_sems, recv_sems):
        me = _my_place()
        o_ref[_flat(me)] = v_ref[...]
        copies = []
        for k in range(1, N_DEV):
            copies.append(pltpu.make_async_remote_copy(
                src_ref=v_ref, dst_ref=o_ref.at[_flat(me)],
                send_sem=send_sems.at[k - 1], recv_sem=recv_sems.at[k - 1],
                device_id=_flip(me, k), device_id_type=MESH_ID))
        for cp in copies:
            cp.start()
        for cp in copies:
            cp.wait()

    return pl.pallas_call(
        body, name=name,
        in_specs=[_vmem()], out_specs=_vmem(),
        out_shape=jax.ShapeDtypeStruct((N_DEV,) + v.shape, v.dtype),
        scratch_shapes=[pltpu.SemaphoreType.DMA((7,)), pltpu.SemaphoreType.DMA((7,))],
        compiler_params=_params(has_side_effects=True),
    )(v)


def _rel_onehot(i):
    r_io = lax.broadcasted_iota(jnp.int32, (N_REL, BAND_W), 0)
    p_io = lax.broadcasted_iota(jnp.int32, (N_REL, BAND_W), 1)
    idx = jnp.clip(PAD_K + i - p_io, -(CHUNK - 1), REL_CLIP) + (CHUNK - 1)
    return (r_io == idx).astype(BF16)


def _bias_expand(rel):
    lh = rel.shape[0]

    def body(rel_ref, o_ref):
        o_ref[...] = _split_dot(rel_ref[...], _rel_onehot(pl.program_id(0)), 3)

    return pl.pallas_call(
        body, name="bias_expand", grid=(CHUNK,),
        in_specs=[pl.BlockSpec((lh, N_REL), lambda i: (0, 0))],
        out_specs=pl.BlockSpec((None, lh, BAND_W), lambda i: (i, 0, 0)),
        out_shape=jax.ShapeDtypeStruct((CHUNK, lh, BAND_W), F32),
        compiler_params=_params("parallel"),
    )(rel)


def _bias_grad(dbias):
    lh = dbias.shape[1]

    def body(db_ref, o_ref):
        i = pl.program_id(0)

        @pl.when(i == 0)
        def _():
            o_ref[...] = jnp.zeros_like(o_ref)

        o_ref[...] += _split_dot(db_ref[...], _rel_onehot(i), 3, NT)

    return pl.pallas_call(
        body, name="bias_grad", grid=(CHUNK,),
        in_specs=[pl.BlockSpec((None, lh, BAND_W), lambda i: (i, 0, 0))],
        out_specs=pl.BlockSpec((lh, N_REL), lambda i: (0, 0)),
        out_shape=jax.ShapeDtypeStruct((lh, N_REL), F32),
        compiler_params=_params("arbitrary"),
    )(dbias)


def _norm_proj(x, g, w_all, tie, layer):
    s, d = x.shape
    e = w_all.shape[2]
    tm, tn = _tile(s, 512), _tile(e, 1024)
    nb = e // tn
    ties = [] if tie is None else [tie]

    def body(x_ref, g_ref, w_ref, *rest):
        proj_ref, h_ref = rest[-2:]

        @pl.when(pl.program_id(1) == 0)
        def _():
            xv = x_ref[...]
            r = lax.rsqrt(jnp.mean(xv * xv, axis=-1, keepdims=True) + NORM_EPS)
            h_ref[...] = ((xv * r) * g_ref[...]).astype(BF16)

        proj_ref[...] = _dot(h_ref[...], w_ref[...])

    return pl.pallas_call(
        body, name=f"norm_proj_l{layer}", grid=(s // tm, N_DEV * nb),
        in_specs=[pl.BlockSpec((tm, d), lambda m, n: (m, 0)),
                  pl.BlockSpec((1, d), lambda m, n: (0, 0)),
                  pl.BlockSpec((None, d, tn), lambda m, n: (n // nb, 0, n % nb))] + [_any()] * len(ties),
        out_specs=(pl.BlockSpec((None, tm, tn), lambda m, n: (n // nb, m, n % nb)),
                   pl.BlockSpec((tm, d), lambda m, n: (m, 0))),
        out_shape=(jax.ShapeDtypeStruct((N_DEV, s, e), F32), jax.ShapeDtypeStruct((s, d), BF16)),
        compiler_params=_params("parallel", "arbitrary"),
    )(x, g, w_all, *ties)


def _sb_forward(proj, layer):
    _, s, e = proj.shape
    hp = _tile(e // HEAD_DIM, SB_HEADS)
    width = hp * HEAD_DIM
    tq = _tile(s, SB_Q)
    diag_tiles = tq // SB_K
    scale = HEAD_DIM ** -0.5

    def body(p_ref, y_ref, mix_ref, tot_ref, kb_ref, vb_ref):
        kb_ref[...] = p_ref[1].astype(BF16)
        vb_ref[...] = p_ref[2].astype(BF16)
        row = lax.broadcasted_iota(jnp.int32, (tq, SB_K), 0)
        col = lax.broadcasted_iota(jnp.int32, (tq, SB_K), 1)
        kj = lax.broadcasted_iota(jnp.int32, (SB_K, SB_K), 0)
        ks = lax.broadcasted_iota(jnp.int32, (SB_K, SB_K), 1)
        later = (kj > ks).astype(BF16)

        def q_block(qi, _):
            t0 = pl.multiple_of(qi * tq, tq)
            qb = [p_ref[0, pl.ds(t0, tq), h * HEAD_DIM:(h + 1) * HEAD_DIM].astype(BF16) for h in range(hp)]

            def tile(s0, state, causal):
                out = []
                for h in range(hp):
                    carry, acc = state[h]
                    lanes = slice(h * HEAD_DIM, (h + 1) * HEAD_DIM)
                    z = _dot(qb[h], kb_ref[pl.ds(s0, SB_K), lanes], NT) * scale
                    ls = _log_sigmoid(z)
                    stay = ls - z
                    if causal is not None:
                        stay = jnp.where(causal, stay, 0.0)
                    w = jnp.exp(ls + carry + _split_dot(stay, later, 2))
                    if causal is not None:
                        w = jnp.where(causal, w, 0.0)
                    acc = acc + _dot(w.astype(BF16), vb_ref[pl.ds(s0, SB_K), lanes])
                    out.append((carry + jnp.sum(stay, axis=1, keepdims=True), acc))
                return tuple(out)

            state = tuple((jnp.zeros((tq, 1), F32), jnp.zeros((tq, HEAD_DIM), F32)) for _ in range(hp))
            for dt in reversed(range(diag_tiles)):
                state = tile(t0 + dt * SB_K, state, col + dt * SB_K < row)

            def k_block(j, st):
                return tile(pl.multiple_of((diag_tiles * qi - 1 - j) * SB_K, SB_K), st, None)

            state = lax.fori_loop(0, diag_tiles * qi, k_block, state)
            silu, _ = _silu_and_grad(p_ref[3, pl.ds(t0, tq), :])
            for h in range(hp):
                lanes = slice(h * HEAD_DIM, (h + 1) * HEAD_DIM)
                y_ref[pl.ds(t0, tq), lanes] = state[h][1]
                mix_ref[pl.ds(t0, tq), lanes] = (state[h][1] * silu[:, lanes]).astype(BF16)
                tot_ref[h, pl.ds(t0, tq), :] = state[h][0]
            return 0

        lax.fori_loop(0, s // tq, q_block, 0)

    return pl.pallas_call(
        body, name=f"sb_forward_l{layer}", grid=(e // width,),
        in_specs=[pl.BlockSpec((4, s, width), lambda h: (0, 0, h))],
        out_specs=(pl.BlockSpec((s, width), lambda h: (0, h)),
                   pl.BlockSpec((s, width), lambda h: (0, h)),
                   pl.BlockSpec((hp, s, 1), lambda h: (h, 0, 0))),
        out_shape=(jax.ShapeDtypeStruct((s, 2 * e), F32), jax.ShapeDtypeStruct((s, 2 * e), BF16),
                   jax.ShapeDtypeStruct((e // HEAD_DIM, s, 1), F32)),
        scratch_shapes=[pltpu.VMEM((s, width), BF16), pltpu.VMEM((s, width), BF16)],
        compiler_params=_params("parallel"),
    )(proj)


def _qk_norm(t, gain):
    r = lax.rsqrt(jnp.mean(t * t, axis=-1, keepdims=True) + NORM_EPS)
    return t * r, r, (t * r) * gain


def _chunk_scores(qc, kw, bias, t0, scale):
    sc = _dot(qc, kw, NT) * scale + bias
    col = lax.broadcasted_iota(jnp.int32, (CHUNK, BAND_W), 1)
    sc = jnp.where(col + t0 >= PAD_K, sc, NEG_BIG)
    ex = jnp.exp(sc - jnp.max(sc, axis=-1, keepdims=True))
    return ex / jnp.sum(ex, axis=-1, keepdims=True)


def _chunk_forward(proj, bias, q_gain, k_gain, y, mixed, layer):
    _, s, e = proj.shape
    hp = _tile(e // HEAD_DIM, CHUNK_HEADS)
    width = hp * HEAD_DIM
    steps = e // width
    scale = HEAD_DIM ** -0.5
    heads = [slice(h * HEAD_DIM, (h + 1) * HEAD_DIM) for h in range(hp)]

    def body(p_ref, b_ref, qg_ref, kg_ref, y_in, mix_in, y_ref, mix_ref, qn_ref, kp_ref, vp_ref):
        del y_in, mix_in
        kp_ref[pl.ds(0, PAD_K), :] = jnp.zeros((PAD_K, width), BF16)
        vp_ref[pl.ds(0, PAD_K), :] = jnp.zeros((PAD_K, width), BF16)
        vp_ref[pl.ds(PAD_K, s), :] = p_ref[2].astype(BF16)
        for lanes in heads:
            qn_ref[:, lanes] = _qk_norm(p_ref[0, :, lanes], qg_ref[...])[2].astype(BF16)
            kp_ref[pl.ds(PAD_K, s), lanes] = _qk_norm(p_ref[1, :, lanes], kg_ref[...])[2].astype(BF16)

        def chunks(ci, _):
            for u in range(CHUNK_UNROLL):
                t0 = pl.multiple_of((ci * CHUNK_UNROLL + u) * CHUNK, CHUNK)
                silu, _ = _silu_and_grad(p_ref[3, pl.ds(t0, CHUNK), :])
                for h, lanes in enumerate(heads):
                    probs = _chunk_scores(qn_ref[pl.ds(t0, CHUNK), lanes], kp_ref[pl.ds(t0, BAND_W), lanes],
                                          b_ref[h], t0, scale)
                    out = _dot(probs.astype(BF16), vp_ref[pl.ds(t0, BAND_W), lanes])
                    y_ref[pl.ds(t0, CHUNK), lanes] = out
                    mix_ref[pl.ds(t0, CHUNK), lanes] = (out * silu[:, lanes]).astype(BF16)
            return 0

        lax.fori_loop(0, s // (CHUNK * CHUNK_UNROLL), chunks, 0)

    return pl.pallas_call(
        body, name=f"chunk_forward_l{layer}", grid=(steps,),
        in_specs=[pl.BlockSpec((4, s, width), lambda h: (1, 0, h)),
                  pl.BlockSpec((hp, CHUNK, BAND_W), lambda h: (layer * steps + h, 0, 0)),
                  pl.BlockSpec((1, HEAD_DIM), lambda h: (0, 0)),
                  pl.BlockSpec((1, HEAD_DIM), lambda h: (0, 0)),
                  _any(), _any()],
        out_specs=(pl.BlockSpec((s, width), lambda h: (0, steps + h)),
                   pl.BlockSpec((s, width), lambda h: (0, steps + h))),
        out_shape=(jax.ShapeDtypeStruct(y.shape, F32), jax.ShapeDtypeStruct(mixed.shape, BF16)),
        input_output_aliases={4: 0, 5: 1},
        scratch_shapes=[pltpu.VMEM((s, width), BF16), pltpu.VMEM((s + PAD_K, width), BF16),
                        pltpu.VMEM((s + PAD_K, width), BF16)],
        compiler_params=_params("parallel"),
    )(proj, bias, q_gain, k_gain, y, mixed)


def _out_proj(mixed, w, x, layer):
    s, d = x.shape
    tm, tn = _tile(s, 512), _tile(d, 1024)

    def body(a_ref, w_ref, x_ref, o_ref):
        o_ref[...] = x_ref[...] + _dot(a_ref[...], w_ref[...])

    return pl.pallas_call(
        body, name=f"out_proj_l{layer}", grid=(s // tm, d // tn),
        in_specs=[pl.BlockSpec((tm, d), lambda m, n: (m, 0)),
                  pl.BlockSpec((d, tn), lambda m, n: (0, n)),
                  pl.BlockSpec((tm, tn), lambda m, n: (m, n))],
        out_specs=pl.BlockSpec((tm, tn), lambda m, n: (m, n)),
        out_shape=jax.ShapeDtypeStruct((s, d), F32),
        compiler_params=_params("parallel", "parallel"),
    )(mixed, w, x)


def _loss_head(y, target):
    s, d = y.shape
    tm = _tile(s, 256)

    def body(y_ref, t_ref, dy_ref, part_ref):
        diff = y_ref[...] - t_ref[...]
        dy_ref[...] = diff * (1.0 / d)
        sq = (diff * diff).reshape(tm // 8, 8, d).sum(axis=0)
        acc = sq[:, 0:LANES]
        for j in range(1, d // LANES):
            acc = acc + sq[:, j * LANES:(j + 1) * LANES]
        part_ref[...] = acc * (0.5 / d)

    return pl.pallas_call(
        body, name="loss_head", grid=(s // tm,),
        in_specs=[pl.BlockSpec((tm, d), lambda i: (i, 0)), pl.BlockSpec((tm, d), lambda i: (i, 0))],
        out_specs=(pl.BlockSpec((tm, d), lambda i: (i, 0)), pl.BlockSpec((None, 8, LANES), lambda i: (i, 0, 0))),
        out_shape=(jax.ShapeDtypeStruct((s, d), F32), jax.ShapeDtypeStruct((s // tm, 8, LANES), F32)),
        compiler_params=_params("parallel"),
    )(y, target)


def _out_proj_bwd_input(dx, w, tie, layer):
    s, d = dx.shape
    tm, tn = _tile(s, 512), _tile(d, 1024)
    ties = [] if tie is None else [tie]

    def body(dx_ref, w_ref, *rest):
        rest[-1][...] = _dot(dx_ref[...].astype(BF16), w_ref[...], NT)

    return pl.pallas_call(
        body, name=f"out_proj_dx_l{layer}", grid=(s // tm, d // tn),
        in_specs=[pl.BlockSpec((tm, d), lambda m, n: (m, 0)),
                  pl.BlockSpec((tn, d), lambda m, n: (n, 0))] + [_any()] * len(ties),
        out_specs=pl.BlockSpec((tm, tn), lambda m, n: (m, n)),
        out_shape=jax.ShapeDtypeStruct((s, d), F32),
        compiler_params=_params("parallel", "parallel"),
    )(dx, w, *ties)


def _out_proj_bwd_weight(mixed, dx, layer):
    s, d = dx.shape
    te, tn = _tile(d, 512), _tile(d, 1024)

    def body(a_ref, dx_ref, o_ref):
        o_ref[...] = _dot(a_ref[...], dx_ref[...].astype(BF16), TN).astype(BF16)

    return pl.pallas_call(
        body, name=f"out_proj_dw_l{layer}", grid=(d // te, d // tn),
        in_specs=[pl.BlockSpec((s, te), lambda i, n: (0, i)), pl.BlockSpec((s, tn), lambda i, n: (0, n))],
        out_specs=pl.BlockSpec((te, tn), lambda i, n: (i, n)),
        out_shape=jax.ShapeDtypeStruct((d, d), BF16),
        compiler_params=_params("parallel", "parallel"),
    )(mixed, dx)


def _sb_backward(proj, y, dmixed, tot, layer):
    _, s, e = proj.shape
    hp = _tile(e // HEAD_DIM, SB_HEADS)
    width = hp * HEAD_DIM
    tq = _tile(s, SB_Q)
    diag_tiles = tq // SB_K
    scale = HEAD_DIM ** -0.5

    def body(p_ref, y_ref, dm_ref, tot_ref, o_ref, kb_ref, vb_ref, do_ref, dk_ref, dv_ref):
        kb_ref[...] = p_ref[1].astype(BF16)
        vb_ref[...] = p_ref[2].astype(BF16)
        silu, dsilu = _silu_and_grad(p_ref[3])
        dm = dm_ref[...]
        do_ref[...] = (dm * silu).astype(BF16)
        o_ref[3] = (dm * y_ref[...] * dsilu).astype(BF16)
        dk_ref[...] = jnp.zeros_like(dk_ref)
        dv_ref[...] = jnp.zeros_like(dv_ref)
        row = lax.broadcasted_iota(jnp.int32, (tq, SB_K), 0)
        col = lax.broadcasted_iota(jnp.int32, (tq, SB_K), 1)
        kj = lax.broadcasted_iota(jnp.int32, (SB_K, SB_K), 0)
        ks = lax.broadcasted_iota(jnp.int32, (SB_K, SB_K), 1)
        upto = (kj <= ks).astype(BF16)
        before = (kj < ks).astype(BF16)

        def q_block(qi, _):
            t0 = pl.multiple_of(qi * tq, tq)
            heads = [slice(h * HEAD_DIM, (h + 1) * HEAD_DIM) for h in range(hp)]
            qb = [p_ref[0, pl.ds(t0, tq), lanes].astype(BF16) for lanes in heads]
            dob = [do_ref[pl.ds(t0, tq), lanes] for lanes in heads]
            total = [tot_ref[h, pl.ds(t0, tq), :] for h in range(hp)]

            def tile(s0, state, causal):
                out = []
                for h, lanes in enumerate(heads):
                    stay_sum, dlw_sum, dq = state[h]
                    kt = kb_ref[pl.ds(s0, SB_K), lanes]
                    vt = vb_ref[pl.ds(s0, SB_K), lanes]
                    z = _dot(qb[h], kt, NT) * scale
                    ls = _log_sigmoid(z)
                    stay = ls - z
                    if causal is not None:
                        stay = jnp.where(causal, stay, 0.0)
                    after = total[h] - (stay_sum + _split_dot(stay, upto, 3))
                    w = jnp.exp(ls + after)
                    if causal is not None:
                        w = jnp.where(causal, w, 0.0)
                    dlw = _dot(dob[h], vt, NT) * w
                    prior = dlw_sum + _split_dot(dlw, before, 2)
                    sig = jnp.exp(ls)
                    dz = (dlw * (1.0 - sig) - sig * prior) * scale
                    if causal is not None:
                        dz = jnp.where(causal, dz, 0.0)
                    dzb = dz.astype(BF16)
                    dq = dq + _dot(dzb, kt)
                    dk_ref[pl.ds(s0, SB_K), lanes] += _dot(dzb, qb[h], TN)
                    dv_ref[pl.ds(s0, SB_K), lanes] += _dot(w.astype(BF16), dob[h], TN)
                    out.append((stay_sum + jnp.sum(stay, axis=1, keepdims=True),
                                dlw_sum + jnp.sum(dlw, axis=1, keepdims=True), dq))
                return tuple(out)

            def k_block(j, st):
                return tile(pl.multiple_of(j * SB_K, SB_K), st, None)

            zero = jnp.zeros((tq, 1), F32)
            state = tuple((zero, zero, jnp.zeros((tq, HEAD_DIM), F32)) for _ in range(hp))
            state = lax.fori_loop(0, diag_tiles * qi, k_block, state)
            for dt in range(diag_tiles):
                state = tile(t0 + dt * SB_K, state, col + dt * SB_K < row)
            for h, lanes in enumerate(heads):
                o_ref[0, pl.ds(t0, tq), lanes] = state[h][2].astype(BF16)
            return 0

        lax.fori_loop(0, s // tq, q_block, 0)
        o_ref[1] = dk_ref[...].astype(BF16)
        o_ref[2] = dv_ref[...].astype(BF16)

    return pl.pallas_call(
        body, name=f"sb_backward_l{layer}", grid=(e // width,),
        in_specs=[pl.BlockSpec((4, s, width), lambda h: (0, 0, h)),
                  pl.BlockSpec((s, width), lambda h: (0, h)),
                  pl.BlockSpec((s, width), lambda h: (0, h)),
                  pl.BlockSpec((hp, s, 1), lambda h: (h, 0, 0))],
        out_specs=pl.BlockSpec((4, s, width), lambda h: (0, 0, h)),
        out_shape=jax.ShapeDtypeStruct((N_DEV, s, e), BF16),
        scratch_shapes=[pltpu.VMEM((s, width), BF16), pltpu.VMEM((s, width), BF16),
                        pltpu.VMEM((s, width), BF16), pltpu.VMEM((s, width), F32),
                        pltpu.VMEM((s, width), F32)],
        compiler_params=_params("parallel"),
    )(proj, y, dmixed, tot)


def _norm_bwd(dn, xh, r, gain):
    dxh = dn * gain
    return r * (dxh - xh * jnp.mean(dxh * xh, axis=-1, keepdims=True)), dn * xh


def _chunk_backward(proj, bias, q_gain, k_gain, y, dmixed, dproj, layer):
    _, s, e = proj.shape
    hp = _tile(e // HEAD_DIM, CHUNK_HEADS)
    width = hp * HEAD_DIM
    steps = e // width
    scale = HEAD_DIM ** -0.5
    heads = [slice(h * HEAD_DIM, (h + 1) * HEAD_DIM) for h in range(hp)]

    def body(p_ref, b_ref, qg_ref, kg_ref, y_ref, dm_ref, dp_in, o_ref, db_ref, dqg_ref, dkg_ref,
             qn_ref, kp_ref, vp_ref, do_ref, dqn_ref, dkn_ref, dvp_ref):
        del dp_in
        kp_ref[pl.ds(0, PAD_K), :] = jnp.zeros((PAD_K, width), BF16)
        vp_ref[pl.ds(0, PAD_K), :] = jnp.zeros((PAD_K, width), BF16)
        vp_ref[pl.ds(PAD_K, s), :] = p_ref[2].astype(BF16)
        for lanes in heads:
            qn_ref[:, lanes] = _qk_norm(p_ref[0, :, lanes], qg_ref[...])[2].astype(BF16)
            kp_ref[pl.ds(PAD_K, s), lanes] = _qk_norm(p_ref[1, :, lanes], kg_ref[...])[2].astype(BF16)
        silu, dsilu = _silu_and_grad(p_ref[3])
        dm = dm_ref[...]
        do_ref[...] = (dm * silu).astype(BF16)
        o_ref[3] = (dm * y_ref[...] * dsilu).astype(BF16)
        dkn_ref[...] = jnp.zeros_like(dkn_ref)
        dvp_ref[...] = jnp.zeros_like(dvp_ref)
        db_ref[...] = jnp.zeros_like(db_ref)

        def chunks(ci, _):
            for u in range(CHUNK_UNROLL):
                t0 = pl.multiple_of((ci * CHUNK_UNROLL + u) * CHUNK, CHUNK)
                for h, lanes in enumerate(heads):
                    qc = qn_ref[pl.ds(t0, CHUNK), lanes]
                    kw = kp_ref[pl.ds(t0, BAND_W), lanes]
                    vw = vp_ref[pl.ds(t0, BAND_W), lanes]
                    dob = do_ref[pl.ds(t0, CHUNK), lanes]
                    probs = _chunk_scores(qc, kw, b_ref[h], t0, scale)
                    dprobs = _dot(dob, vw, NT)
                    dsc = probs * (dprobs - jnp.sum(probs * dprobs, axis=-1, keepdims=True))
                    db_ref[h] += dsc
                    dsb = (dsc * scale).astype(BF16)
                    dqn_ref[pl.ds(t0, CHUNK), lanes] = _dot(dsb, kw)
                    dkn_ref[pl.ds(t0, BAND_W), lanes] += _dot(dsb, qc, TN)
                    dvp_ref[pl.ds(t0, BAND_W), lanes] += _dot(probs.astype(BF16), dob, TN)
            return 0

        lax.fori_loop(0, s // (CHUNK * CHUNK_UNROLL), chunks, 0)
        o_ref[2] = dvp_ref[pl.ds(PAD_K, s), :].astype(BF16)

        @pl.when(pl.program_id(0) == 0)
        def _():
            dqg_ref[...] = jnp.zeros_like(dqg_ref)
            dkg_ref[...] = jnp.zeros_like(dkg_ref)

        for lanes in heads:
            qh, rq, _ = _qk_norm(p_ref[0, :, lanes], qg_ref[...])
            dq, dqg_rows = _norm_bwd(dqn_ref[:, lanes], qh, rq, qg_ref[...])
            o_ref[0, :, lanes] = dq.astype(BF16)
            dqg_ref[...] += jnp.sum(dqg_rows, axis=0, keepdims=True)
            kh, rk, _ = _qk_norm(p_ref[1, :, lanes], kg_ref[...])
            dk, dkg_rows = _norm_bwd(dkn_ref[pl.ds(PAD_K, s), lanes], kh, rk, kg_ref[...])
            o_ref[1, :, lanes] = dk.astype(BF16)
            dkg_ref[...] += jnp.sum(dkg_rows, axis=0, keepdims=True)

    return pl.pallas_call(
        body, name=f"chunk_backward_l{layer}", grid=(steps,),
        in_specs=[pl.BlockSpec((4, s, width), lambda h: (1, 0, h)),
                  pl.BlockSpec((hp, CHUNK, BAND_W), lambda h: (layer * steps + h, 0, 0)),
                  pl.BlockSpec((1, HEAD_DIM), lambda h: (0, 0)),
                  pl.BlockSpec((1, HEAD_DIM), lambda h: (0, 0)),
                  pl.BlockSpec((s, width), lambda h: (0, steps + h)),
                  pl.BlockSpec((s, width), lambda h: (0, steps + h)),
                  _any()],
        out_specs=(pl.BlockSpec((4, s, width), lambda h: (1, 0, h)),
                   pl.BlockSpec((hp, CHUNK, BAND_W), lambda h: (h, 0, 0)),
                   pl.BlockSpec((1, HEAD_DIM), lambda h: (0, 0)),
                   pl.BlockSpec((1, HEAD_DIM), lambda h: (0, 0))),
        out_shape=(jax.ShapeDtypeStruct(dproj.shape, BF16),
                   jax.ShapeDtypeStruct((e // HEAD_DIM, CHUNK, BAND_W), F32),
                   jax.ShapeDtypeStruct((1, HEAD_DIM), F32), jax.ShapeDtypeStruct((1, HEAD_DIM), F32)),
        input_output_aliases={6: 0},
        scratch_shapes=[pltpu.VMEM((s, width), BF16), pltpu.VMEM((s + PAD_K, width), BF16),
                        pltpu.VMEM((s + PAD_K, width), BF16), pltpu.VMEM((s, width), BF16),
                        pltpu.VMEM((s, width), F32), pltpu.VMEM((s + PAD_K, width), F32),
                        pltpu.VMEM((s + PAD_K, width), F32)],
        compiler_params=_params("arbitrary"),
    )(proj, bias, q_gain, k_gain, y, dmixed, dproj)


def _proj_bwd_input(dproj, w_all, x, g, dx, tie, layer):
    s, d = x.shape
    e = w_all.shape[2]
    tm = _tile(s, 512)

    def body(dp_ref, w_ref, x_ref, g_ref, dx_ref, tie_ref, o_ref, dg_ref, acc_ref):
        del tie_ref
        j = pl.program_id(1)

        @pl.when(j == 0)
        def _():
            acc_ref[...] = jnp.zeros_like(acc_ref)

        acc_ref[...] += _dot(dp_ref[...], w_ref[...], NT)

        @pl.when(jnp.logical_and(j == N_DEV - 1, pl.program_id(0) == 0))
        def _():
            dg_ref[...] = jnp.zeros_like(dg_ref)

        @pl.when(j == N_DEV - 1)
        def _():
            xv = x_ref[...]
            r = lax.rsqrt(jnp.mean(xv * xv, axis=-1, keepdims=True) + NORM_EPS)
            dxn, dg_rows = _norm_bwd(acc_ref[...], xv * r, r, g_ref[...])
            o_ref[...] = dx_ref[...] + dxn
            dg_ref[...] += jnp.sum(dg_rows, axis=0, keepdims=True)

    return pl.pallas_call(
        body, name=f"proj_dx_l{layer}", grid=(s // tm, N_DEV),
        in_specs=[pl.BlockSpec((None, tm, e), lambda m, j: (j, m, 0)),
                  pl.BlockSpec((None, d, e), lambda m, j: (j, 0, 0)),
                  pl.BlockSpec((tm, d), lambda m, j: (m, 0)),
                  pl.BlockSpec((1, d), lambda m, j: (0, 0)),
                  pl.BlockSpec((tm, d), lambda m, j: (m, 0)), _any()],
        out_specs=(pl.BlockSpec((tm, d), lambda m, j: (m, 0)), pl.BlockSpec((1, d), lambda m, j: (0, 0))),
        out_shape=(jax.ShapeDtypeStruct((s, d), F32), jax.ShapeDtypeStruct((1, d), F32)),
        scratch_shapes=[pltpu.VMEM((tm, d), F32)],
        compiler_params=_params("arbitrary", "arbitrary"),
    )(dproj, w_all, x, g, dx, tie)


def _proj_bwd_weight(h, dproj, layer):
    s, d = h.shape
    e = dproj.shape[2]
    td, tn = _tile(d, 1024), _tile(e, 1024)
    nb = e // tn

    def body(h_ref, dp_ref, o_ref):
        o_ref[...] = _dot(h_ref[...], dp_ref[...], TN).astype(BF16)

    return pl.pallas_call(
        body, name=f"proj_dw_l{layer}", grid=(d // td, N_DEV * nb),
        in_specs=[pl.BlockSpec((s, td), lambda i, n: (0, i)),
                  pl.BlockSpec((None, s, tn), lambda i, n: (n // nb, 0, n % nb))],
        out_specs=pl.BlockSpec((None, td, tn), lambda i, n: (n // nb, i, n % nb)),
        out_shape=jax.ShapeDtypeStruct((N_DEV, d, e), BF16),
        compiler_params=_params("parallel", "parallel"),
    )(h, dproj)


def _adamw_math(w, g, m, v):
    m = ADAM_B1 * m + (1.0 - ADAM_B1) * g
    v = ADAM_B2 * v + (1.0 - ADAM_B2) * (g * g)
    m_hat = m / (1.0 - ADAM_B1 ** ADAM_STEP)
    v_hat = v / (1.0 - ADAM_B2 ** ADAM_STEP)
    return -ADAM_LR * (m_hat / (jnp.sqrt(v_hat) + ADAM_EPS) + ADAM_WD * w), m, v


def _adamw_layer(parts, own, me, w, m, v, prev, layer, name):
    n_layers, rows, cols = w.shape
    tr = _tile(rows, max(8, (256 * 1024) // cols))

    def body(me_ref, p_ref, own_ref, w_ref, m_ref, v_ref, *rest):
        g_ref, d_ref, nm_ref, nv_ref = rest[-4:]
        mine = own_ref[...].astype(F32)
        g = None
        for j in range(N_DEV):
            term = jnp.where(me_ref[0] == j, mine, p_ref[j].astype(F32))
            g = term if g is None else g + term
        g_ref[...] = g
        d_ref[...], nm_ref[...], nv_ref[...] = _adamw_math(w_ref[...], g, m_ref[...], v_ref[...])

    blk = pl.BlockSpec((None, tr, cols), lambda i, me_ref: (layer, i, 0))
    out_shape = tuple(jax.ShapeDtypeStruct(w.shape, F32) for _ in range(4))
    in_specs = [pl.BlockSpec((N_DEV, tr, cols), lambda i, me_ref: (0, i, 0)),
                pl.BlockSpec((None, tr, cols), lambda i, me_ref: (me_ref[0], i, 0)), blk, blk, blk]
    args = [me, parts, own, w, m, v]
    aliases = {}
    if prev is not None:
        in_specs += [_any()] * 4
        args += list(prev)
        aliases = {6 + k: k for k in range(4)}
    return pl.pallas_call(
        body, name=f"{name}_l{layer}",
        grid_spec=pltpu.PrefetchScalarGridSpec(
            num_scalar_prefetch=1, grid=(rows // tr,), in_specs=in_specs, out_specs=(blk, blk, blk, blk)),
        out_shape=out_shape, input_output_aliases=aliases,
        compiler_params=_params("parallel"),
    )(*args)


def _sum_slots(parts):
    def body(p_ref, o_ref):
        g = p_ref[0]
        for j in range(1, N_DEV):
            g = g + p_ref[j]
        o_ref[...] = g

    return pl.pallas_call(
        body, name="sum_small_grads",
        in_specs=[_vmem()], out_specs=_vmem(),
        out_shape=jax.ShapeDtypeStruct(parts.shape[1:], F32),
        compiler_params=_params(),
    )(parts)


def _adamw_small(w, g, m, v):
    def body(w_ref, g_ref, m_ref, v_ref, d_ref, nm_ref, nv_ref):
        d_ref[...], nm_ref[...], nv_ref[...] = _adamw_math(w_ref[...], g_ref[...], m_ref[...], v_ref[...])

    return pl.pallas_call(
        body, name="adamw_small",
        in_specs=[_vmem()] * 4, out_specs=(_vmem(),) * 3,
        out_shape=tuple(jax.ShapeDtypeStruct(w.shape, F32) for _ in range(3)),
        compiler_params=_params(),
    )(w, g, m, v)


def _pack_rows(arrays):
    rows = []
    for a in arrays:
        flat = a.reshape(-1)
        pad = (-flat.shape[0]) % (8 * LANES)
        rows.append(jnp.pad(flat, (0, pad)).reshape(-1, LANES))
    return jnp.concatenate(rows, axis=0)


def _unpack_rows(packed, like):
    out, r0 = [], 0
    for a in like:
        n = a.size
        nr = -(-n // (8 * LANES)) * 8
        out.append(packed[r0:r0 + nr].reshape(-1)[:n].reshape(a.shape))
        r0 += nr
    return out


def kernel(x, norm_g, w_in, q_norm_g, k_norm_g, rel_bias, w_out, loss_target, m_norm_g, m_w_in, m_q_norm_g, m_k_norm_g, m_rel_bias, m_w_out, v_norm_g, v_w_in, v_q_norm_g, v_k_norm_g, v_rel_bias, v_w_out):
    depth, d, e = w_in.shape
    r_out = w_out.shape[1]
    heads = e // HEAD_DIM
    rel_w = rel_bias.shape[2]
    x0 = x[0]
    target = loss_target[0]
    s = x0.shape[0]

    me = jnp.reshape(_flat(_my_place()), (1,)).astype(jnp.int32)

    casts = [(_cast_layer(w_in, me, l, "cast_w_in"), _cast_layer(w_out, me, l, "cast_w_out"))
             for l in range(depth)]

    def begin_gather(l, after):
        (win_b, win_land), (wout_b, wout_land) = casts[l]
        return _copies_start((win_b, wout_b), (win_land, wout_land), False, after, f"gather_start_l{l}")

    rel_all = _gather_small(rel_bias, "gather_rel_bias")
    gathering = begin_gather(0, [rel_all])
    rel_full = jnp.transpose(rel_all, (1, 2, 0, 3)).reshape(depth * heads, N_DEV * rel_w)
    bias = jnp.transpose(_bias_expand(rel_full), (1, 0, 2))
    head_work = [bias] + [shard for cast in casts[1:] for shard, _ in cast]

    xs, hs, projs, ys, mixes, tots, weights = [], [], [], [], [], [], []
    xl = x0
    for l in range(depth):
        _, (win_all, wout_all) = _copies_wait(gathering, [xl] + (head_work if l == 0 else []), False,
                                              f"gather_wait_l{l}")
        tie = None
        if l + 1 < depth:
            gathering = begin_gather(l + 1, [win_all])
            tie = gathering[-1]
        wout_full = wout_all.reshape(d, d)
        proj, h = _norm_proj(xl, norm_g[l:l + 1], win_all, tie, l)
        y, mixed, tot = _sb_forward(proj, l)
        y, mixed = _chunk_forward(proj, bias, q_norm_g[l:l + 1], k_norm_g[l:l + 1], y, mixed, l)
        xs.append(xl), hs.append(h), projs.append(proj), ys.append(y), mixes.append(mixed), tots.append(tot)
        weights.append((win_all, wout_full))
        xl = _out_proj(mixed, wout_full, xl, l)

    dx, loss_parts = _loss_head(xl, target)
    loss = lax.psum(jnp.sum(loss_parts), AXES)

    dbias, dng, dqg, dkg = [None] * depth, [None] * depth, [None] * depth, [None] * depth
    res_in, res_out = None, None

    def finish_exchange(exchanging, after, l):
        (gwin, gwout), (rin, rout) = _copies_wait(exchanging, after, True, f"exchange_wait_l{l}")
        return (_adamw_layer(rin, gwin, me, w_in, m_w_in, v_w_in, res_in, l, "adamw_w_in"),
                _adamw_layer(rout, gwout, me, w_out, m_w_out, v_w_out, res_out, l, "adamw_w_out"))

    pending = []
    for l in reversed(range(depth)):
        win_all, wout_full = weights[l]
        dmixed = _out_proj_bwd_input(dx, wout_full, None, l)
        gwout = _out_proj_bwd_weight(mixes[l], dx, l).reshape(N_DEV, r_out, d)
        dproj = _sb_backward(projs[l], ys[l], dmixed, tots[l], l)
        dproj, dbias[l], dqg[l], dkg[l] = _chunk_backward(
            projs[l], bias, q_norm_g[l:l + 1], k_norm_g[l:l + 1], ys[l], dmixed, dproj, l)
        grads_l = (_proj_bwd_weight(hs[l], dproj, l), gwout)
        exchanging = _copies_start(grads_l, [lax.empty(g.shape, g.dtype) for g in grads_l], True, [],
                                   f"exchange_start_l{l}")
        pending.append((exchanging, l))
        dx, dng[l] = _proj_bwd_input(dproj, win_all, xs[l], norm_g[l:l + 1], dx, exchanging[-1], l)
    for exchanging, l in pending[:-1]:
        res_in, res_out = finish_exchange(exchanging, [dx], l)
    drel = _bias_grad(jnp.transpose(jnp.concatenate(dbias, axis=0), (1, 0, 2)))
    small_like = [norm_g, q_norm_g, k_norm_g, drel]
    mine = _pack_rows([jnp.concatenate(dng, axis=0), jnp.concatenate(dqg, axis=0),
                       jnp.concatenate(dkg, axis=0), drel])
    g_norm, g_qn, g_kn, g_rel_full = _unpack_rows(_sum_slots(_gather_small(mine, "gather_small_grads")), small_like)
    my_block = _flat(_my_place())
    g_rel = lax.dynamic_slice_in_dim(g_rel_full.reshape(depth, heads, N_REL), my_block * rel_w, rel_w, axis=2)
    small_w = [norm_g, q_norm_g, k_norm_g, rel_bias]
    small = _adamw_small(_pack_rows(small_w), _pack_rows([g_norm, g_qn, g_kn, g_rel]),
                         _pack_rows([m_norm_g, m_q_norm_g, m_k_norm_g, m_rel_bias]),
                         _pack_rows([v_norm_g, v_q_norm_g, v_k_norm_g, v_rel_bias]))
    d_small, nm_small, nv_small = (_unpack_rows(p, small_w) for p in small)

    res_in, res_out = finish_exchange(pending[-1][0], [small[0], res_in[0], res_out[0]], 0)
    g_win, d_win, nm_win, nv_win = res_in
    g_wout, d_wout, nm_wout, nv_wout = res_out
    grads = (g_norm, g_win, g_qn, g_kn, g_rel, g_wout)

    def order(sm, big_in, big_out):
        return (sm[0], big_in, sm[1], sm[2], sm[3], big_out)

    return (loss, dx[None], *grads, *order(d_small, d_win, d_wout),
            *order(nm_small, nm_win, nm_wout), *order(nv_small, nv_win, nv_wout))
```

```python
import functools

import jax
import jax.numpy as jnp
from jax import lax
from jax.experimental import pallas as pl
from jax.experimental.pallas import tpu as pltpu

F32 = jnp.float32
BF16 = jnp.bfloat16
MESH_ID = pl.DeviceIdType.MESH
AXES = ("x", "y", "c")

N_DEV = 8
HEAD_DIM = 128
CHUNK = 64
LEFT_CHUNKS = 8
BAND_W = (LEFT_CHUNKS + 1) * CHUNK
PAD_K = LEFT_CHUNKS * CHUNK
REL_CLIP = 256
N_REL = REL_CLIP + CHUNK
NORM_EPS = 1e-6
NEG_BIG = -1e30
PAIR = 2 * CHUNK
PAIR_W = BAND_W + CHUNK
CHUNK_HEADS = 2
PAIR_UNROLL = 2
SB_Q = 512
SB_K = 128
SB_HEADS = 2
LANES = 128

ADAM_LR = 0.001
ADAM_B1 = 0.9
ADAM_B2 = 0.999
ADAM_EPS = 1e-08
ADAM_WD = 0.01
ADAM_STEP = 10

VMEM_LIMIT_BYTES = 56 * 1024 * 1024

NT = (((1,), (1,)), ((), ()))
TN = (((0,), (0,)), ((), ()))


def _params(*sem, **kw):
    return pltpu.CompilerParams(dimension_semantics=sem or None, vmem_limit_bytes=VMEM_LIMIT_BYTES, **kw)


def _any():
    return pl.BlockSpec(memory_space=pl.ANY)


def _vmem():
    return pl.BlockSpec(memory_space=pltpu.VMEM)


def _tile(n, want):
    return want if n % want == 0 else n


def _dot(a, b, dims=None):
    if dims is None:
        return jnp.dot(a, b, preferred_element_type=F32)
    return lax.dot_general(a, b, dims, preferred_element_type=F32)


def _split_dot(a, b, parts, dims=None):
    acc = None
    rest = a
    for _ in range(parts):
        piece = rest.astype(BF16)
        rest = rest - piece.astype(F32)
        term = _dot(piece, b, dims)
        acc = term if acc is None else acc + term
    return acc


def _log_sigmoid(z):
    return jnp.minimum(z, 0.0) - jnp.log(1.0 + jnp.exp(-jnp.abs(z)))


def _silu_and_grad(g):
    sig = jax.nn.sigmoid(g)
    return g * sig, sig * (1.0 + g * (1.0 - sig))


def _my_place():
    return lax.axis_index("x"), lax.axis_index("y"), lax.axis_index("c")


def _flat(place):
    return 4 * place[0] + 2 * place[1] + place[2]


def _flip(place, k):
    return tuple(1 - p if (k >> s) & 1 else p for p, s in zip(place, (2, 1, 0)))


def _cast_layer(w, me, layer, name):
    _, rows, cols = w.shape
    tr = _tile(rows, 1024)

    def body(me_ref, a_ref, shard_ref, land_ref):
        del me_ref
        shard_ref[...] = a_ref[...].astype(BF16)
        land_ref[...] = shard_ref[...]

    return pl.pallas_call(
        body, name=f"{name}_l{layer}",
        grid_spec=pltpu.PrefetchScalarGridSpec(
            num_scalar_prefetch=1, grid=(rows // tr,),
            in_specs=[pl.BlockSpec((None, tr, cols), lambda i, me_ref: (layer, i, 0))],
            out_specs=(pl.BlockSpec((tr, cols), lambda i, me_ref: (i, 0)),
                       pl.BlockSpec((None, tr, cols), lambda i, me_ref: (me_ref[0], i, 0)))),
        out_shape=(jax.ShapeDtypeStruct((rows, cols), BF16), jax.ShapeDtypeStruct((N_DEV, rows, cols), BF16)),
        compiler_params=_params("parallel"),
    )(me, w)


HBM_SPEC = pl.BlockSpec(memory_space=pltpu.HBM)
SEM_SPEC = pl.BlockSpec(memory_space=pltpu.SEMAPHORE)
N_PEERS = N_DEV - 1


def _peer_copies(srcs, lands, send_sems, recv_sems, slot_of_src):
    me = _my_place()
    copies = []
    for k in range(1, N_DEV):
        peer = _flip(me, k)
        for t in range(len(srcs)):
            copies.append(pltpu.make_async_remote_copy(
                src_ref=srcs[t].at[_flat(peer)] if slot_of_src else srcs[t], dst_ref=lands[t].at[_flat(me)],
                send_sem=send_sems.at[t * N_PEERS + k - 1], recv_sem=recv_sems.at[t * N_PEERS + k - 1],
                device_id=peer, device_id_type=MESH_ID))
    return copies


def _copies_start(srcs, lands, slot_of_src, after, name):
    n = len(srcs)
    after = list(after)

    def body(*refs):
        sems = 2 * n + len(after)
        for cp in _peer_copies(refs[:n], refs[n:2 * n], refs[sems], refs[sems + 1], slot_of_src):
            cp.start()
        refs[-1][...] = jnp.zeros_like(refs[-1])

    thru = [pltpu.HBM(a.shape, a.dtype) for a in list(srcs) + list(lands)]
    out = pl.pallas_call(
        body, name=name,
        in_specs=[HBM_SPEC] * (2 * n) + [_any()] * len(after),
        out_specs=(SEM_SPEC, SEM_SPEC, *([HBM_SPEC] * (2 * n)), _vmem()),
        out_shape=(pltpu.SemaphoreType.DMA((n * N_PEERS,)), pltpu.SemaphoreType.DMA((n * N_PEERS,)), *thru,
                   jax.ShapeDtypeStruct((8, LANES), F32)),
        input_output_aliases={i: 2 + i for i in range(2 * n)},
        compiler_params=pltpu.CompilerParams(has_side_effects=pltpu.SideEffectType.DATAFLOW_SIDE_EFFECTING),
    )(*[pltpu.with_memory_space_constraint(a, pltpu.HBM) for a in list(srcs) + list(lands)], *after)
    return out[0], out[1], out[2:2 + n], out[2 + n:2 + 2 * n], out[-1]


def _copies_wait(pending, after, slot_of_src, name):
    send_sems, recv_sems, srcs, lands, _ = pending
    n = len(srcs)
    after = list(after)

    def body(*refs):
        for cp in _peer_copies(refs[:n], refs[n:2 * n], refs[2 * n], refs[2 * n + 1], slot_of_src):
            cp.wait_send()
            cp.wait_recv()

    thru = [pltpu.HBM(a.shape, a.dtype) for a in list(srcs) + list(lands)]
    out = pl.pallas_call(
        body, name=name,
        in_specs=[HBM_SPEC] * (2 * n) + [SEM_SPEC, SEM_SPEC] + [_any()] * len(after),
        out_specs=tuple([HBM_SPEC] * (2 * n)),
        out_shape=tuple(thru),
        input_output_aliases={i: i for i in range(2 * n)},
        compiler_params=pltpu.CompilerParams(has_side_effects=pltpu.SideEffectType.DATAFLOW_SIDE_EFFECTING),
    )(*srcs, *lands, send_sems, recv_sems, *after)
    return out[:n], out[n:]


def _gather_small(v, after, name):
    after = list(after)

    def body(v_ref, *rest):
        o_ref, send_sems, recv_sems = rest[-3:]
        me = _my_place()
        o_ref[_flat(me)] = v_ref[...]
        copies = []
        for k in range(1, N_DEV):
            copies.append(pltpu.make_async_remote_copy(
                src_ref=v_ref, dst_ref=o_ref.at[_flat(me)],
                send_sem=send_sems.at[k - 1], recv_sem=recv_sems.at[k - 1],
                device_id=_flip(me, k), device_id_type=MESH_ID))
        for cp in copies:
            cp.start()
        for cp in copies:
            cp.wait()

    return pl.pallas_call(
        body, name=name,
        in_specs=[_vmem()] + [_any()] * len(after), out_specs=_vmem(),
        out_shape=jax.ShapeDtypeStruct((N_DEV,) + v.shape, v.dtype),
        scratch_shapes=[pltpu.SemaphoreType.DMA((7,)), pltpu.SemaphoreType.DMA((7,))],
        compiler_params=_params(has_side_effects=True),
    )(v, *after)


def _rel_onehot(row):
    r_io = lax.broadcasted_iota(jnp.int32, (N_REL, PAIR_W), 0)
    p_io = lax.broadcasted_iota(jnp.int32, (N_REL, PAIR_W), 1)
    band_col = p_io - (row // CHUNK) * CHUNK
    in_band = jnp.logical_and(band_col >= 0, band_col < BAND_W)
    idx = jnp.clip(PAD_K + row % CHUNK - band_col, -(CHUNK - 1), REL_CLIP) + (CHUNK - 1)
    return jnp.logical_and(r_io == idx, in_band).astype(BF16), in_band[0:1]


def _bias_expand(rel):
    lh = rel.shape[0]

    def body(rel_ref, o_ref):
        onehot, in_band = _rel_onehot(pl.program_id(0))
        o_ref[...] = jnp.where(in_band, _split_dot(rel_ref[...], onehot, 3), NEG_BIG)

    return pl.pallas_call(
        body, name="bias_expand", grid=(PAIR,),
        in_specs=[pl.BlockSpec((lh, N_REL), lambda i: (0, 0))],
        out_specs=pl.BlockSpec((None, lh, PAIR_W), lambda i: (i, 0, 0)),
        out_shape=jax.ShapeDtypeStruct((PAIR, lh, PAIR_W), F32),
        compiler_params=_params("parallel"),
    )(rel)


def _bias_grad(dbias):
    lh = dbias.shape[1]

    def body(db_ref, o_ref):
        i = pl.program_id(0)

        @pl.when(i == 0)
        def _():
            o_ref[...] = jnp.zeros_like(o_ref)

        o_ref[...] += _split_dot(db_ref[...], _rel_onehot(i)[0], 3, NT)

    return pl.pallas_call(
        body, name="bias_grad", grid=(PAIR,),
        in_specs=[pl.BlockSpec((None, lh, PAIR_W), lambda i: (i, 0, 0))],
        out_specs=pl.BlockSpec((lh, N_REL), lambda i: (0, 0)),
        out_shape=jax.ShapeDtypeStruct((lh, N_REL), F32),
        compiler_params=_params("arbitrary"),
    )(dbias)


def _norm_proj(x, g, w_all, tie, layer):
    s, d = x.shape
    e = w_all.shape[2]
    tm, tn = _tile(s, 512), _tile(e, 1024)
    nb = e // tn
    ties = [] if tie is None else [tie]

    def body(x_ref, g_ref, w_ref, *rest):
        proj_ref, h_ref = rest[-2:]

        @pl.when(pl.program_id(1) == 0)
        def _():
            xv = x_ref[...]
            r = lax.rsqrt(jnp.mean(xv * xv, axis=-1, keepdims=True) + NORM_EPS)
            h_ref[...] = ((xv * r) * g_ref[...]).astype(BF16)

        proj_ref[...] = _dot(h_ref[...], w_ref[...])

    return pl.pallas_call(
        body, name=f"norm_proj_l{layer}", grid=(s // tm, N_DEV * nb),
        in_specs=[pl.BlockSpec((tm, d), lambda m, n: (m, 0)),
                  pl.BlockSpec((1, d), lambda m, n: (0, 0)),
                  pl.BlockSpec((None, d, tn), lambda m, n: (n // nb, 0, n % nb))] + [_any()] * len(ties),
        out_specs=(pl.BlockSpec((None, tm, tn), lambda m, n: (n // nb, m, n % nb)),
                   pl.BlockSpec((tm, d), lambda m, n: (m, 0))),
        out_shape=(jax.ShapeDtypeStruct((N_DEV, s, e), F32), jax.ShapeDtypeStruct((s, d), BF16)),
        compiler_params=_params("parallel", "arbitrary"),
    )(x, g, w_all, *ties)


def _sb_forward(proj, layer):
    _, s, e = proj.shape
    hp = _tile(e // HEAD_DIM, SB_HEADS)
    width = hp * HEAD_DIM
    tq = _tile(s, SB_Q)
    diag_tiles = tq // SB_K
    scale = HEAD_DIM ** -0.5

    def body(p_ref, y_ref, mix_ref, tot_ref, kb_ref, vb_ref):
        kb_ref[...] = p_ref[1].astype(BF16)
        vb_ref[...] = p_ref[2].astype(BF16)
        row = lax.broadcasted_iota(jnp.int32, (tq, SB_K), 0)
        col = lax.broadcasted_iota(jnp.int32, (tq, SB_K), 1)
        kj = lax.broadcasted_iota(jnp.int32, (SB_K, SB_K), 0)
        ks = lax.broadcasted_iota(jnp.int32, (SB_K, SB_K), 1)
        later = (kj > ks).astype(BF16)

        def q_block(qi, _):
            t0 = pl.multiple_of(qi * tq, tq)
            qb = [p_ref[0, pl.ds(t0, tq), h * HEAD_DIM:(h + 1) * HEAD_DIM].astype(BF16) for h in range(hp)]

            def tile(s0, state, causal):
                out = []
                for h in range(hp):
                    carry, acc = state[h]
                    lanes = slice(h * HEAD_DIM, (h + 1) * HEAD_DIM)
                    z = _dot(qb[h], kb_ref[pl.ds(s0, SB_K), lanes], NT) * scale
                    ls = _log_sigmoid(z)
                    stay = ls - z
                    if causal is not None:
                        stay = jnp.where(causal, stay, 0.0)
                    w = jnp.exp(ls + carry + _split_dot(stay, later, 2))
                    if causal is not None:
                        w = jnp.where(causal, w, 0.0)
                    acc = acc + _dot(w.astype(BF16), vb_ref[pl.ds(s0, SB_K), lanes])
                    out.append((carry + jnp.sum(stay, axis=1, keepdims=True), acc))
                return tuple(out)

            state = tuple((jnp.zeros((tq, 1), F32), jnp.zeros((tq, HEAD_DIM), F32)) for _ in range(hp))
            for dt in reversed(range(diag_tiles)):
                state = tile(t0 + dt * SB_K, state, col + dt * SB_K < row)

            def k_block(j, st):
                return tile(pl.multiple_of((diag_tiles * qi - 1 - j) * SB_K, SB_K), st, None)

            state = lax.fori_loop(0, diag_tiles * qi, k_block, state)
            silu, _ = _silu_and_grad(p_ref[3, pl.ds(t0, tq), :])
            for h in range(hp):
                lanes = slice(h * HEAD_DIM, (h + 1) * HEAD_DIM)
                y_ref[pl.ds(t0, tq), lanes] = state[h][1]
                mix_ref[pl.ds(t0, tq), lanes] = (state[h][1] * silu[:, lanes]).astype(BF16)
                tot_ref[h, pl.ds(t0, tq), :] = state[h][0]
            return 0

        lax.fori_loop(0, s // tq, q_block, 0)

    return pl.pallas_call(
        body, name=f"sb_forward_l{layer}", grid=(e // width,),
        in_specs=[pl.BlockSpec((4, s, width), lambda h: (0, 0, h))],
        out_specs=(pl.BlockSpec((s, width), lambda h: (0, h)),
                   pl.BlockSpec((s, width), lambda h: (0, h)),
                   pl.BlockSpec((hp, s, 1), lambda h: (h, 0, 0))),
        out_shape=(jax.ShapeDtypeStruct((s, 2 * e), F32), jax.ShapeDtypeStruct((s, 2 * e), BF16),
                   jax.ShapeDtypeStruct((e // HEAD_DIM, s, 1), F32)),
        scratch_shapes=[pltpu.VMEM((s, width), BF16), pltpu.VMEM((s, width), BF16)],
        compiler_params=_params("parallel"),
    )(proj)


def _qk_norm(t, gain):
    r = lax.rsqrt(jnp.mean(t * t, axis=-1, keepdims=True) + NORM_EPS)
    return t * r, r, (t * r) * gain


def _chunk_scores(qc, kw, bias, t0, scale):
    sc = _dot(qc, kw, NT) * scale + bias
    col = lax.broadcasted_iota(jnp.int32, (PAIR, PAIR_W), 1)
    sc = jnp.where(col + t0 >= PAD_K, sc, NEG_BIG)
    ex = jnp.exp(sc - jnp.max(sc, axis=-1, keepdims=True))
    return ex / jnp.sum(ex, axis=-1, keepdims=True)


def _chunk_forward(proj, bias, q_gain, k_gain, y, mixed, layer):
    _, s, e = proj.shape
    hp = _tile(e // HEAD_DIM, CHUNK_HEADS)
    width = hp * HEAD_DIM
    steps = e // width
    unroll = _tile(s // PAIR, PAIR_UNROLL)
    scale = HEAD_DIM ** -0.5
    heads = [slice(h * HEAD_DIM, (h + 1) * HEAD_DIM) for h in range(hp)]

    def body(p_ref, b_ref, qg_ref, kg_ref, y_in, mix_in, y_ref, mix_ref, qn_ref, kp_ref, vp_ref):
        del y_in, mix_in
        kp_ref[pl.ds(0, PAD_K), :] = jnp.zeros((PAD_K, width), BF16)
        vp_ref[pl.ds(0, PAD_K), :] = jnp.zeros((PAD_K, width), BF16)
        vp_ref[pl.ds(PAD_K, s), :] = p_ref[2].astype(BF16)
        for lanes in heads:
            qn_ref[:, lanes] = _qk_norm(p_ref[0, :, lanes], qg_ref[...])[2].astype(BF16)
            kp_ref[pl.ds(PAD_K, s), lanes] = _qk_norm(p_ref[1, :, lanes], kg_ref[...])[2].astype(BF16)

        def chunks(ci, _):
            done = []
            for u in range(unroll):
                t0 = pl.multiple_of((ci * unroll + u) * PAIR, PAIR)
                silu, _ = _silu_and_grad(p_ref[3, pl.ds(t0, PAIR), :])
                for h, lanes in enumerate(heads):
                    probs = _chunk_scores(qn_ref[pl.ds(t0, PAIR), lanes], kp_ref[pl.ds(t0, PAIR_W), lanes],
                                          b_ref[h], t0, scale)
                    out = _dot(probs.astype(BF16), vp_ref[pl.ds(t0, PAIR_W), lanes])
                    done.append((t0, lanes, out, (out * silu[:, lanes]).astype(BF16)))
            for t0, lanes, out, gated in done:
                y_ref[pl.ds(t0, PAIR), lanes] = out
                mix_ref[pl.ds(t0, PAIR), lanes] = gated
            return 0

        lax.fori_loop(0, s // (PAIR * unroll), chunks, 0)

    return pl.pallas_call(
        body, name=f"chunk_forward_l{layer}", grid=(steps,),
        in_specs=[pl.BlockSpec((4, s, width), lambda h: (1, 0, h)),
                  pl.BlockSpec((hp, PAIR, PAIR_W), lambda h: (layer * steps + h, 0, 0)),
                  pl.BlockSpec((1, HEAD_DIM), lambda h: (0, 0)),
                  pl.BlockSpec((1, HEAD_DIM), lambda h: (0, 0)),
                  _any(), _any()],
        out_specs=(pl.BlockSpec((s, width), lambda h: (0, steps + h)),
                   pl.BlockSpec((s, width), lambda h: (0, steps + h))),
        out_shape=(jax.ShapeDtypeStruct(y.shape, F32), jax.ShapeDtypeStruct(mixed.shape, BF16)),
        input_output_aliases={4: 0, 5: 1},
        scratch_shapes=[pltpu.VMEM((s, width), BF16), pltpu.VMEM((s + PAD_K, width), BF16),
                        pltpu.VMEM((s + PAD_K, width), BF16)],
        compiler_params=_params("parallel"),
    )(proj, bias, q_gain, k_gain, y, mixed)


def _out_proj(mixed, w, x, layer):
    s, d = x.shape
    tm, tn = _tile(s, 512), _tile(d, 1024)

    def body(a_ref, w_ref, x_ref, o_ref):
        o_ref[...] = x_ref[...] + _dot(a_ref[...], w_ref[...])

    return pl.pallas_call(
        body, name=f"out_proj_l{layer}", grid=(s // tm, d // tn),
        in_specs=[pl.BlockSpec((tm, d), lambda m, n: (m, 0)),
                  pl.BlockSpec((d, tn), lambda m, n: (0, n)),
                  pl.BlockSpec((tm, tn), lambda m, n: (m, n))],
        out_specs=pl.BlockSpec((tm, tn), lambda m, n: (m, n)),
        out_shape=jax.ShapeDtypeStruct((s, d), F32),
        compiler_params=_params("parallel", "parallel"),
    )(mixed, w, x)


def _loss_head(y, target):
    s, d = y.shape
    tm = _tile(s, 256)

    def body(y_ref, t_ref, dy_ref, part_ref):
        diff = y_ref[...] - t_ref[...]
        dy_ref[...] = diff * (1.0 / d)
        sq = (diff * diff).reshape(tm // 8, 8, d).sum(axis=0)
        acc = sq[:, 0:LANES]
        for j in range(1, d // LANES):
            acc = acc + sq[:, j * LANES:(j + 1) * LANES]
        part_ref[...] = acc * (0.5 / d)

    return pl.pallas_call(
        body, name="loss_head", grid=(s // tm,),
        in_specs=[pl.BlockSpec((tm, d), lambda i: (i, 0)), pl.BlockSpec((tm, d), lambda i: (i, 0))],
        out_specs=(pl.BlockSpec((tm, d), lambda i: (i, 0)), pl.BlockSpec((None, 8, LANES), lambda i: (i, 0, 0))),
        out_shape=(jax.ShapeDtypeStruct((s, d), F32), jax.ShapeDtypeStruct((s // tm, 8, LANES), F32)),
        compiler_params=_params("parallel"),
    )(y, target)


def _out_proj_bwd_input(dx, w, tie, layer):
    s, d = dx.shape
    tm, tn = _tile(s, 512), _tile(d, 1024)
    ties = [] if tie is None else [tie]

    def body(dx_ref, w_ref, *rest):
        rest[-1][...] = _dot(dx_ref[...].astype(BF16), w_ref[...], NT)

    return pl.pallas_call(
        body, name=f"out_proj_dx_l{layer}", grid=(s // tm, d // tn),
        in_specs=[pl.BlockSpec((tm, d), lambda m, n: (m, 0)),
                  pl.BlockSpec((tn, d), lambda m, n: (n, 0))] + [_any()] * len(ties),
        out_specs=pl.BlockSpec((tm, tn), lambda m, n: (m, n)),
        out_shape=jax.ShapeDtypeStruct((s, d), F32),
        compiler_params=_params("parallel", "parallel"),
    )(dx, w, *ties)


def _out_proj_bwd_weight(mixed, dx, layer):
    s, d = dx.shape
    te, tn = _tile(d, 512), _tile(d, 1024)

    def body(a_ref, dx_ref, o_ref):
        o_ref[...] = _dot(a_ref[...], dx_ref[...].astype(BF16), TN).astype(BF16)

    return pl.pallas_call(
        body, name=f"out_proj_dw_l{layer}", grid=(d // te, d // tn),
        in_specs=[pl.BlockSpec((s, te), lambda i, n: (0, i)), pl.BlockSpec((s, tn), lambda i, n: (0, n))],
        out_specs=pl.BlockSpec((te, tn), lambda i, n: (i, n)),
        out_shape=jax.ShapeDtypeStruct((d, d), BF16),
        compiler_params=_params("parallel", "parallel"),
    )(mixed, dx)


def _sb_backward(proj, y, dmixed, tot, layer):
    _, s, e = proj.shape
    hp = _tile(e // HEAD_DIM, SB_HEADS)
    width = hp * HEAD_DIM
    tq = _tile(s, SB_Q)
    diag_tiles = tq // SB_K
    scale = HEAD_DIM ** -0.5

    def body(p_ref, y_ref, dm_ref, tot_ref, o_ref, kb_ref, vb_ref, do_ref, dk_ref, dv_ref):
        kb_ref[...] = p_ref[1].astype(BF16)
        vb_ref[...] = p_ref[2].astype(BF16)
        silu, dsilu = _silu_and_grad(p_ref[3])
        dm = dm_ref[...]
        do_ref[...] = (dm * silu).astype(BF16)
        o_ref[3] = (dm * y_ref[...] * dsilu).astype(BF16)
        dk_ref[...] = jnp.zeros_like(dk_ref)
        dv_ref[...] = jnp.zeros_like(dv_ref)
        row = lax.broadcasted_iota(jnp.int32, (tq, SB_K), 0)
        col = lax.broadcasted_iota(jnp.int32, (tq, SB_K), 1)
        kj = lax.broadcasted_iota(jnp.int32, (SB_K, SB_K), 0)
        ks = lax.broadcasted_iota(jnp.int32, (SB_K, SB_K), 1)
        upto = (kj <= ks).astype(BF16)
        before = (kj < ks).astype(BF16)

        def q_block(qi, _):
            t0 = pl.multiple_of(qi * tq, tq)
            heads = [slice(h * HEAD_DIM, (h + 1) * HEAD_DIM) for h in range(hp)]
            qb = [p_ref[0, pl.ds(t0, tq), lanes].astype(BF16) for lanes in heads]
            dob = [do_ref[pl.ds(t0, tq), lanes] for lanes in heads]
            total = [tot_ref[h, pl.ds(t0, tq), :] for h in range(hp)]

            def tile(s0, state, causal):
                out, adds = [], []
                for h, lanes in enumerate(heads):
                    stay_sum, dlw_sum, dq = state[h]
                    kt = kb_ref[pl.ds(s0, SB_K), lanes]
                    vt = vb_ref[pl.ds(s0, SB_K), lanes]
                    z = _dot(qb[h], kt, NT) * scale
                    ls = _log_sigmoid(z)
                    stay = ls - z
                    if causal is not None:
                        stay = jnp.where(causal, stay, 0.0)
                    after = total[h] - (stay_sum + _split_dot(stay, upto, 3))
                    w = jnp.exp(ls + after)
                    if causal is not None:
                        w = jnp.where(causal, w, 0.0)
                    dlw = _dot(dob[h], vt, NT) * w
                    prior = dlw_sum + _split_dot(dlw, before, 2)
                    sig = jnp.exp(ls)
                    dz = (dlw * (1.0 - sig) - sig * prior) * scale
                    if causal is not None:
                        dz = jnp.where(causal, dz, 0.0)
                    dzb = dz.astype(BF16)
                    dq = dq + _dot(dzb, kt)
                    adds.append((lanes, _dot(dzb, qb[h], TN), _dot(w.astype(BF16), dob[h], TN)))
                    out.append((stay_sum + jnp.sum(stay, axis=1, keepdims=True),
                                dlw_sum + jnp.sum(dlw, axis=1, keepdims=True), dq))
                for lanes, dk, dv in adds:
                    dk_ref[pl.ds(s0, SB_K), lanes] += dk
                    dv_ref[pl.ds(s0, SB_K), lanes] += dv
                return tuple(out)

            def k_block(j, st):
                return tile(pl.multiple_of(j * SB_K, SB_K), st, None)

            zero = jnp.zeros((tq, 1), F32)
            state = tuple((zero, zero, jnp.zeros((tq, HEAD_DIM), F32)) for _ in range(hp))
            state = lax.fori_loop(0, diag_tiles * qi, k_block, state)
            for dt in range(diag_tiles):
                state = tile(t0 + dt * SB_K, state, col + dt * SB_K < row)
            for h, lanes in enumerate(heads):
                o_ref[0, pl.ds(t0, tq), lanes] = state[h][2].astype(BF16)
            return 0

        lax.fori_loop(0, s // tq, q_block, 0)
        o_ref[1] = dk_ref[...].astype(BF16)
        o_ref[2] = dv_ref[...].astype(BF16)

    return pl.pallas_call(
        body, name=f"sb_backward_l{layer}", grid=(e // width,),
        in_specs=[pl.BlockSpec((4, s, width), lambda h: (0, 0, h)),
                  pl.BlockSpec((s, width), lambda h: (0, h)),
                  pl.BlockSpec((s, width), lambda h: (0, h)),
                  pl.BlockSpec((hp, s, 1), lambda h: (h, 0, 0))],
        out_specs=pl.BlockSpec((4, s, width), lambda h: (0, 0, h)),
        out_shape=jax.ShapeDtypeStruct((N_DEV, s, e), BF16),
        scratch_shapes=[pltpu.VMEM((s, width), BF16), pltpu.VMEM((s, width), BF16),
                        pltpu.VMEM((s, width), BF16), pltpu.VMEM((s, width), F32),
                        pltpu.VMEM((s, width), F32)],
        compiler_params=_params("parallel"),
    )(proj, y, dmixed, tot)


def _norm_bwd(dn, xh, r, gain):
    dxh = dn * gain
    return r * (dxh - xh * jnp.mean(dxh * xh, axis=-1, keepdims=True)), dn * xh


def _chunk_backward(proj, bias, q_gain, k_gain, y, dmixed, dproj, layer):
    _, s, e = proj.shape
    hp = _tile(e // HEAD_DIM, CHUNK_HEADS)
    width = hp * HEAD_DIM
    steps = e // width
    unroll = _tile(s // PAIR, PAIR_UNROLL)
    scale = HEAD_DIM ** -0.5
    heads = [slice(h * HEAD_DIM, (h + 1) * HEAD_DIM) for h in range(hp)]

    def body(p_ref, b_ref, qg_ref, kg_ref, y_ref, dm_ref, dp_in, o_ref, db_ref, dqg_ref, dkg_ref,
             qn_ref, kp_ref, vp_ref, do_ref, dqn_ref, dkn_ref, dvp_ref):
        del dp_in
        kp_ref[pl.ds(0, PAD_K), :] = jnp.zeros((PAD_K, width), BF16)
        vp_ref[pl.ds(0, PAD_K), :] = jnp.zeros((PAD_K, width), BF16)
        vp_ref[pl.ds(PAD_K, s), :] = p_ref[2].astype(BF16)
        for lanes in heads:
            qn_ref[:, lanes] = _qk_norm(p_ref[0, :, lanes], qg_ref[...])[2].astype(BF16)
            kp_ref[pl.ds(PAD_K, s), lanes] = _qk_norm(p_ref[1, :, lanes], kg_ref[...])[2].astype(BF16)
        silu, dsilu = _silu_and_grad(p_ref[3])
        dm = dm_ref[...]
        do_ref[...] = (dm * silu).astype(BF16)
        o_ref[3] = (dm * y_ref[...] * dsilu).astype(BF16)
        dkn_ref[...] = jnp.zeros_like(dkn_ref)
        dvp_ref[...] = jnp.zeros_like(dvp_ref)
        db_ref[...] = jnp.zeros_like(db_ref)

        def chunks(ci, _):
            done = []
            for u in range(unroll):
                t0 = pl.multiple_of((ci * unroll + u) * PAIR, PAIR)
                for h, lanes in enumerate(heads):
                    qc = qn_ref[pl.ds(t0, PAIR), lanes]
                    kw = kp_ref[pl.ds(t0, PAIR_W), lanes]
                    vw = vp_ref[pl.ds(t0, PAIR_W), lanes]
                    dob = do_ref[pl.ds(t0, PAIR), lanes]
                    probs = _chunk_scores(qc, kw, b_ref[h], t0, scale)
                    dprobs = _dot(dob, vw, NT)
                    dsc = probs * (dprobs - jnp.sum(probs * dprobs, axis=-1, keepdims=True))
                    dsb = (dsc * scale).astype(BF16)
                    done.append((t0, h, lanes, dsc, _dot(dsb, kw), _dot(dsb, qc, TN),
                                 _dot(probs.astype(BF16), dob, TN)))
            for t0, h, lanes, dsc, dqn, dkn, dvp in done:
                db_ref[h] += dsc
                dqn_ref[pl.ds(t0, PAIR), lanes] = dqn
                dkn_ref[pl.ds(t0, PAIR_W), lanes] += dkn
                dvp_ref[pl.ds(t0, PAIR_W), lanes] += dvp
            return 0

        lax.fori_loop(0, s // (PAIR * unroll), chunks, 0)
        o_ref[2] = dvp_ref[pl.ds(PAD_K, s), :].astype(BF16)

        @pl.when(pl.program_id(0) == 0)
        def _():
            dqg_ref[...] = jnp.zeros_like(dqg_ref)
            dkg_ref[...] = jnp.zeros_like(dkg_ref)

        for lanes in heads:
            qh, rq, _ = _qk_norm(p_ref[0, :, lanes], qg_ref[...])
            dq, dqg_rows = _norm_bwd(dqn_ref[:, lanes], qh, rq, qg_ref[...])
            o_ref[0, :, lanes] = dq.astype(BF16)
            dqg_ref[...] += jnp.sum(dqg_rows, axis=0, keepdims=True)
            kh, rk, _ = _qk_norm(p_ref[1, :, lanes], kg_ref[...])
            dk, dkg_rows = _norm_bwd(dkn_ref[pl.ds(PAD_K, s), lanes], kh, rk, kg_ref[...])
            o_ref[1, :, lanes] = dk.astype(BF16)
            dkg_ref[...] += jnp.sum(dkg_rows, axis=0, keepdims=True)

    return pl.pallas_call(
        body, name=f"chunk_backward_l{layer}", grid=(steps,),
        in_specs=[pl.BlockSpec((4, s, width), lambda h: (1, 0, h)),
                  pl.BlockSpec((hp, PAIR, PAIR_W), lambda h: (layer * steps + h, 0, 0)),
                  pl.BlockSpec((1, HEAD_DIM), lambda h: (0, 0)),
                  pl.BlockSpec((1, HEAD_DIM), lambda h: (0, 0)),
                  pl.BlockSpec((s, width), lambda h: (0, steps + h)),
                  pl.BlockSpec((s, width), lambda h: (0, steps + h)),
                  _any()],
        out_specs=(pl.BlockSpec((4, s, width), lambda h: (1, 0, h)),
                   pl.BlockSpec((hp, PAIR, PAIR_W), lambda h: (h, 0, 0)),
                   pl.BlockSpec((1, HEAD_DIM), lambda h: (0, 0)),
                   pl.BlockSpec((1, HEAD_DIM), lambda h: (0, 0))),
        out_shape=(jax.ShapeDtypeStruct(dproj.shape, BF16),
                   jax.ShapeDtypeStruct((e // HEAD_DIM, PAIR, PAIR_W), F32),
                   jax.ShapeDtypeStruct((1, HEAD_DIM), F32), jax.ShapeDtypeStruct((1, HEAD_DIM), F32)),
        input_output_aliases={6: 0},
        scratch_shapes=[pltpu.VMEM((s, width), BF16), pltpu.VMEM((s + PAD_K, width), BF16),
                        pltpu.VMEM((s + PAD_K, width), BF16), pltpu.VMEM((s, width), BF16),
                        pltpu.VMEM((s, width), F32), pltpu.VMEM((s + PAD_K, width), F32),
                        pltpu.VMEM((s + PAD_K, width), F32)],
        compiler_params=_params("arbitrary"),
    )(proj, bias, q_gain, k_gain, y, dmixed, dproj)


def _proj_bwd_input(dproj, w_all, x, g, dx, tie, layer):
    s, d = x.shape
    e = w_all.shape[2]
    tm = _tile(s, 512)

    def body(dp_ref, w_ref, x_ref, g_ref, dx_ref, tie_ref, o_ref, dg_ref, acc_ref):
        del tie_ref
        j = pl.program_id(1)

        @pl.when(j == 0)
        def _():
            acc_ref[...] = jnp.zeros_like(acc_ref)

        acc_ref[...] += _dot(dp_ref[...], w_ref[...], NT)

        @pl.when(jnp.logical_and(j == N_DEV - 1, pl.program_id(0) == 0))
        def _():
            dg_ref[...] = jnp.zeros_like(dg_ref)

        @pl.when(j == N_DEV - 1)
        def _():
            xv = x_ref[...]
            r = lax.rsqrt(jnp.mean(xv * xv, axis=-1, keepdims=True) + NORM_EPS)
            dxn, dg_rows = _norm_bwd(acc_ref[...], xv * r, r, g_ref[...])
            o_ref[...] = dx_ref[...] + dxn
            dg_ref[...] += jnp.sum(dg_rows, axis=0, keepdims=True)

    return pl.pallas_call(
        body, name=f"proj_dx_l{layer}", grid=(s // tm, N_DEV),
        in_specs=[pl.BlockSpec((None, tm, e), lambda m, j: (j, m, 0)),
                  pl.BlockSpec((None, d, e), lambda m, j: (j, 0, 0)),
                  pl.BlockSpec((tm, d), lambda m, j: (m, 0)),
                  pl.BlockSpec((1, d), lambda m, j: (0, 0)),
                  pl.BlockSpec((tm, d), lambda m, j: (m, 0)), _any()],
        out_specs=(pl.BlockSpec((tm, d), lambda m, j: (m, 0)), pl.BlockSpec((1, d), lambda m, j: (0, 0))),
        out_shape=(jax.ShapeDtypeStruct((s, d), F32), jax.ShapeDtypeStruct((1, d), F32)),
        scratch_shapes=[pltpu.VMEM((tm, d), F32)],
        compiler_params=_params("arbitrary", "arbitrary"),
    )(dproj, w_all, x, g, dx, tie)


def _proj_bwd_weight(h, dproj, layer):
    s, d = h.shape
    e = dproj.shape[2]
    td, tn = _tile(d, 1024), _tile(e, 1024)
    nb = e // tn

    def body(h_ref, dp_ref, o_ref):
        o_ref[...] = _dot(h_ref[...], dp_ref[...], TN).astype(BF16)

    return pl.pallas_call(
        body, name=f"proj_dw_l{layer}", grid=(d // td, N_DEV * nb),
        in_specs=[pl.BlockSpec((s, td), lambda i, n: (0, i)),
                  pl.BlockSpec((None, s, tn), lambda i, n: (n // nb, 0, n % nb))],
        out_specs=pl.BlockSpec((None, td, tn), lambda i, n: (n // nb, i, n % nb)),
        out_shape=jax.ShapeDtypeStruct((N_DEV, d, e), BF16),
        compiler_params=_params("parallel", "parallel"),
    )(h, dproj)


def _adamw_math(w, g, m, v):
    m = ADAM_B1 * m + (1.0 - ADAM_B1) * g
    v = ADAM_B2 * v + (1.0 - ADAM_B2) * (g * g)
    m_hat = m / (1.0 - ADAM_B1 ** ADAM_STEP)
    v_hat = v / (1.0 - ADAM_B2 ** ADAM_STEP)
    return -ADAM_LR * (m_hat / (jnp.sqrt(v_hat) + ADAM_EPS) + ADAM_WD * w), m, v


def _adamw_layer(parts, own, me, w, m, v, prev, layer, name):
    n_layers, rows, cols = w.shape
    tr = _tile(rows, max(8, (256 * 1024) // cols))

    def body(me_ref, p_ref, own_ref, w_ref, m_ref, v_ref, *rest):
        g_ref, d_ref, nm_ref, nv_ref = rest[-4:]
        mine = own_ref[...].astype(F32)
        g = None
        for j in range(N_DEV):
            term = jnp.where(me_ref[0] == j, mine, p_ref[j].astype(F32))
            g = term if g is None else g + term
        g_ref[...] = g
        d_ref[...], nm_ref[...], nv_ref[...] = _adamw_math(w_ref[...], g, m_ref[...], v_ref[...])

    blk = pl.BlockSpec((None, tr, cols), lambda i, me_ref: (layer, i, 0))
    out_shape = tuple(jax.ShapeDtypeStruct(w.shape, F32) for _ in range(4))
    in_specs = [pl.BlockSpec((N_DEV, tr, cols), lambda i, me_ref: (0, i, 0)),
                pl.BlockSpec((None, tr, cols), lambda i, me_ref: (me_ref[0], i, 0)), blk, blk, blk]
    args = [me, parts, own, w, m, v]
    aliases = {}
    if prev is not None:
        in_specs += [_any()] * 4
        args += list(prev)
        aliases = {6 + k: k for k in range(4)}
    return pl.pallas_call(
        body, name=f"{name}_l{layer}",
        grid_spec=pltpu.PrefetchScalarGridSpec(
            num_scalar_prefetch=1, grid=(rows // tr,), in_specs=in_specs, out_specs=(blk, blk, blk, blk)),
        out_shape=out_shape, input_output_aliases=aliases,
        compiler_params=_params("parallel"),
    )(*args)


def _sum_slots(parts):
    def body(p_ref, o_ref):
        g = p_ref[0]
        for j in range(1, N_DEV):
            g = g + p_ref[j]
        o_ref[...] = g

    return pl.pallas_call(
        body, name="sum_small_grads",
        in_specs=[_vmem()], out_specs=_vmem(),
        out_shape=jax.ShapeDtypeStruct(parts.shape[1:], F32),
        compiler_params=_params(),
    )(parts)


def _adamw_small(w, g, m, v):
    def body(w_ref, g_ref, m_ref, v_ref, d_ref, nm_ref, nv_ref):
        d_ref[...], nm_ref[...], nv_ref[...] = _adamw_math(w_ref[...], g_ref[...], m_ref[...], v_ref[...])

    return pl.pallas_call(
        body, name="adamw_small",
        in_specs=[_vmem()] * 4, out_specs=(_vmem(),) * 3,
        out_shape=tuple(jax.ShapeDtypeStruct(w.shape, F32) for _ in range(3)),
        compiler_params=_params(),
    )(w, g, m, v)


def _pack_rows(arrays):
    rows = []
    for a in arrays:
        flat = a.reshape(-1)
        pad = (-flat.shape[0]) % (8 * LANES)
        rows.append(jnp.pad(flat, (0, pad)).reshape(-1, LANES))
    return jnp.concatenate(rows, axis=0)


def _unpack_rows(packed, like):
    out, r0 = [], 0
    for a in like:
        n = a.size
        nr = -(-n // (8 * LANES)) * 8
        out.append(packed[r0:r0 + nr].reshape(-1)[:n].reshape(a.shape))
        r0 += nr
    return out


def kernel(x, norm_g, w_in, q_norm_g, k_norm_g, rel_bias, w_out, loss_target, m_norm_g, m_w_in, m_q_norm_g, m_k_norm_g, m_rel_bias, m_w_out, v_norm_g, v_w_in, v_q_norm_g, v_k_norm_g, v_rel_bias, v_w_out):
    depth, d, e = w_in.shape
    r_out = w_out.shape[1]
    heads = e // HEAD_DIM
    rel_w = rel_bias.shape[2]
    x0 = x[0]
    target = loss_target[0]
    s = x0.shape[0]

    me = jnp.reshape(_flat(_my_place()), (1,)).astype(jnp.int32)

    casts = [(_cast_layer(w_in, me, l, "cast_w_in"), _cast_layer(w_out, me, l, "cast_w_out"))
             for l in range(depth)]

    def begin_gather(l, after):
        (win_b, win_land), (wout_b, wout_land) = casts[l]
        return _copies_start((win_b, wout_b), (win_land, wout_land), False, after, f"gather_start_l{l}")

    rel_all = _gather_small(rel_bias, [], "gather_rel_bias")
    gathering = begin_gather(0, [rel_all])
    rel_full = jnp.transpose(rel_all, (1, 2, 0, 3)).reshape(depth * heads, N_DEV * rel_w)
    bias = jnp.transpose(_bias_expand(rel_full), (1, 0, 2))
    head_work = [bias] + [shard for cast in casts[1:] for shard, _ in cast]

    xs, hs, projs, ys, mixes, tots, weights = [], [], [], [], [], [], []
    xl = x0
    for l in range(depth):
        _, (win_all, wout_all) = _copies_wait(gathering, [xl] + (head_work if l == 0 else []), False,
                                              f"gather_wait_l{l}")
        tie = None
        if l + 1 < depth:
            gathering = begin_gather(l + 1, [win_all])
            tie = gathering[-1]
        wout_full = wout_all.reshape(d, d)
        proj, h = _norm_proj(xl, norm_g[l:l + 1], win_all, tie, l)
        y, mixed, tot = _sb_forward(proj, l)
        y, mixed = _chunk_forward(proj, bias, q_norm_g[l:l + 1], k_norm_g[l:l + 1], y, mixed, l)
        xs.append(xl), hs.append(h), projs.append(proj), ys.append(y), mixes.append(mixed), tots.append(tot)
        weights.append((win_all, wout_full))
        xl = _out_proj(mixed, wout_full, xl, l)

    dx, loss_parts = _loss_head(xl, target)
    loss = lax.psum(jnp.sum(loss_parts), AXES)

    dbias, dng, dqg, dkg = [None] * depth, [None] * depth, [None] * depth, [None] * depth
    res_in, res_out = None, None

    def finish_exchange(exchanging, after, l):
        (gwin, gwout), (rin, rout) = _copies_wait(exchanging, after, True, f"exchange_wait_l{l}")
        return (_adamw_layer(rin, gwin, me, w_in, m_w_in, v_w_in, res_in, l, "adamw_w_in"),
                _adamw_layer(rout, gwout, me, w_out, m_w_out, v_w_out, res_out, l, "adamw_w_out"))

    pending = []
    for l in reversed(range(depth)):
        win_all, wout_full = weights[l]
        dmixed = _out_proj_bwd_input(dx, wout_full, None, l)
        gwout = _out_proj_bwd_weight(mixes[l], dx, l).reshape(N_DEV, r_out, d)
        dproj = _sb_backward(projs[l], ys[l], dmixed, tots[l], l)
        dproj, dbias[l], dqg[l], dkg[l] = _chunk_backward(
            projs[l], bias, q_norm_g[l:l + 1], k_norm_g[l:l + 1], ys[l], dmixed, dproj, l)
        grads_l = (_proj_bwd_weight(hs[l], dproj, l), gwout)
        exchanging = _copies_start(grads_l, [lax.empty(g.shape, g.dtype) for g in grads_l], True, [],
                                   f"exchange_start_l{l}")
        pending.append((exchanging, l))
        dx, dng[l] = _proj_bwd_input(dproj, win_all, xs[l], norm_g[l:l + 1], dx, exchanging[-1], l)
    for exchanging, l in pending[:-1]:
        res_in, res_out = finish_exchange(exchanging, [dx], l)
    drel = _bias_grad(jnp.transpose(jnp.concatenate(dbias, axis=0), (1, 0, 2)))
    small_like = [norm_g, q_norm_g, k_norm_g, drel]
    mine = _pack_rows([jnp.concatenate(dng, axis=0), jnp.concatenate(dqg, axis=0),
                       jnp.concatenate(dkg, axis=0), drel])
    gathered = _gather_small(mine, [res_in[0], res_out[0]], "gather_small_grads")
    g_norm, g_qn, g_kn, g_rel_full = _unpack_rows(_sum_slots(gathered), small_like)
    my_block = _flat(_my_place())
    g_rel = lax.dynamic_slice_in_dim(g_rel_full.reshape(depth, heads, N_REL), my_block * rel_w, rel_w, axis=2)
    small_w = [norm_g, q_norm_g, k_norm_g, rel_bias]
    small = _adamw_small(_pack_rows(small_w), _pack_rows([g_norm, g_qn, g_kn, g_rel]),
                         _pack_rows([m_norm_g, m_q_norm_g, m_k_norm_g, m_rel_bias]),
                         _pack_rows([v_norm_g, v_q_norm_g, v_k_norm_g, v_rel_bias]))
    d_small, nm_small, nv_small = (_unpack_rows(p, small_w) for p in small)

    res_in, res_out = finish_exchange(pending[-1][0], [small[0], res_in[0], res_out[0]], 0)
    g_win, d_win, nm_win, nv_win = res_in
    g_wout, d_wout, nm_wout, nv_wout = res_out
    grads = (g_norm, g_win, g_qn, g_kn, g_rel, g_wout)

    def order(sm, big_in, big_out):
        return (sm[0], big_in, sm[1], sm[2], sm[3], big_out)

    return (loss, dx[None], *grads, *order(d_small, d_win, d_wout),
            *order(nm_small, nm_win, nm_wout), *order(nv_small, nv_win, nv_wout))
```

```python
import functools

import jax
import jax.numpy as jnp
from jax import lax
from jax.experimental import pallas as pl
from jax.experimental.pallas import tpu as pltpu

F32 = jnp.float32
BF16 = jnp.bfloat16
MESH_ID = pl.DeviceIdType.MESH
AXES = ("x", "y", "c")

N_DEV = 8
HEAD_DIM = 128
CHUNK = 64
LEFT_CHUNKS = 8
BAND_W = (LEFT_CHUNKS + 1) * CHUNK
PAD_K = LEFT_CHUNKS * CHUNK
REL_CLIP = 256
N_REL = REL_CLIP + CHUNK
NORM_EPS = 1e-6
NEG_BIG = -1e30
PAIR = 2 * CHUNK
PAIR_W = BAND_W + CHUNK
CHUNK_HEADS = 2
PAIR_UNROLL = 2
SB_Q = 512
SB_K = 128
SB_HEADS = 2
LANES = 128

ADAM_LR = 0.001
ADAM_B1 = 0.9
ADAM_B2 = 0.999
ADAM_EPS = 1e-08
ADAM_WD = 0.01
ADAM_STEP = 10

VMEM_LIMIT_BYTES = 56 * 1024 * 1024

NT = (((1,), (1,)), ((), ()))
TN = (((0,), (0,)), ((), ()))


def _params(*sem, **kw):
    return pltpu.CompilerParams(dimension_semantics=sem or None, vmem_limit_bytes=VMEM_LIMIT_BYTES, **kw)


def _any():
    return pl.BlockSpec(memory_space=pl.ANY)


def _vmem():
    return pl.BlockSpec(memory_space=pltpu.VMEM)


def _tile(n, want):
    return want if n % want == 0 else n


def _dot(a, b, dims=None):
    if dims is None:
        return jnp.dot(a, b, preferred_element_type=F32)
    return lax.dot_general(a, b, dims, preferred_element_type=F32)


def _split_dot(a, b, parts, dims=None):
    acc = None
    rest = a
    for _ in range(parts):
        piece = rest.astype(BF16)
        rest = rest - piece.astype(F32)
        term = _dot(piece, b, dims)
        acc = term if acc is None else acc + term
    return acc


def _log_sigmoid(z):
    return jnp.minimum(z, 0.0) - jnp.log(1.0 + jnp.exp(-jnp.abs(z)))


def _silu_and_grad(g):
    sig = jax.nn.sigmoid(g)
    return g * sig, sig * (1.0 + g * (1.0 - sig))


def _my_place():
    return lax.axis_index("x"), lax.axis_index("y"), lax.axis_index("c")


def _flat(place):
    return 4 * place[0] + 2 * place[1] + place[2]


def _flip(place, k):
    return tuple(1 - p if (k >> s) & 1 else p for p, s in zip(place, (2, 1, 0)))


def _cast_layer(w, me, layer, name):
    _, rows, cols = w.shape
    tr = _tile(rows, 1024)

    def body(me_ref, a_ref, shard_ref, land_ref):
        del me_ref
        shard_ref[...] = a_ref[...].astype(BF16)
        land_ref[...] = shard_ref[...]

    return pl.pallas_call(
        body, name=f"{name}_l{layer}",
        grid_spec=pltpu.PrefetchScalarGridSpec(
            num_scalar_prefetch=1, grid=(rows // tr,),
            in_specs=[pl.BlockSpec((None, tr, cols), lambda i, me_ref: (layer, i, 0))],
            out_specs=(pl.BlockSpec((tr, cols), lambda i, me_ref: (i, 0)),
                       pl.BlockSpec((None, tr, cols), lambda i, me_ref: (me_ref[0], i, 0)))),
        out_shape=(jax.ShapeDtypeStruct((rows, cols), BF16), jax.ShapeDtypeStruct((N_DEV, rows, cols), BF16)),
        compiler_params=_params("parallel"),
    )(me, w)


HBM_SPEC = pl.BlockSpec(memory_space=pltpu.HBM)
SEM_SPEC = pl.BlockSpec(memory_space=pltpu.SEMAPHORE)
N_PEERS = N_DEV - 1


def _peer_copies(srcs, lands, send_sems, recv_sems, slot_of_src):
    me = _my_place()
    copies = []
    for k in range(1, N_DEV):
        peer = _flip(me, k)
        for t in range(len(srcs)):
            copies.append(pltpu.make_async_remote_copy(
                src_ref=srcs[t].at[_flat(peer)] if slot_of_src else srcs[t], dst_ref=lands[t].at[_flat(me)],
                send_sem=send_sems.at[t * N_PEERS + k - 1], recv_sem=recv_sems.at[t * N_PEERS + k - 1],
                device_id=peer, device_id_type=MESH_ID))
    return copies


def _copies_start(srcs, lands, slot_of_src, after, name):
    n = len(srcs)
    after = list(after)

    def body(*refs):
        sems = 2 * n + len(after)
        for cp in _peer_copies(refs[:n], refs[n:2 * n], refs[sems], refs[sems + 1], slot_of_src):
            cp.start()
        refs[-1][...] = jnp.zeros_like(refs[-1])

    thru = [pltpu.HBM(a.shape, a.dtype) for a in list(srcs) + list(lands)]
    out = pl.pallas_call(
        body, name=name,
        in_specs=[HBM_SPEC] * (2 * n) + [_any()] * len(after),
        out_specs=(SEM_SPEC, SEM_SPEC, *([HBM_SPEC] * (2 * n)), _vmem()),
        out_shape=(pltpu.SemaphoreType.DMA((n * N_PEERS,)), pltpu.SemaphoreType.DMA((n * N_PEERS,)), *thru,
                   jax.ShapeDtypeStruct((8, LANES), F32)),
        input_output_aliases={i: 2 + i for i in range(2 * n)},
        compiler_params=pltpu.CompilerParams(has_side_effects=pltpu.SideEffectType.DATAFLOW_SIDE_EFFECTING),
    )(*[pltpu.with_memory_space_constraint(a, pltpu.HBM) for a in list(srcs) + list(lands)], *after)
    return out[0], out[1], out[2:2 + n], out[2 + n:2 + 2 * n], out[-1]


def _copies_wait(pending, after, slot_of_src, name):
    send_sems, recv_sems, srcs, lands, _ = pending
    n = len(srcs)
    after = list(after)

    def body(*refs):
        for cp in _peer_copies(refs[:n], refs[n:2 * n], refs[2 * n], refs[2 * n + 1], slot_of_src):
            cp.wait_send()
            cp.wait_recv()

    thru = [pltpu.HBM(a.shape, a.dtype) for a in list(srcs) + list(lands)]
    out = pl.pallas_call(
        body, name=name,
        in_specs=[HBM_SPEC] * (2 * n) + [SEM_SPEC, SEM_SPEC] + [_any()] * len(after),
        out_specs=tuple([HBM_SPEC] * (2 * n)),
        out_shape=tuple(thru),
        input_output_aliases={i: i for i in range(2 * n)},
        compiler_params=pltpu.CompilerParams(has_side_effects=pltpu.SideEffectType.DATAFLOW_SIDE_EFFECTING),
    )(*srcs, *lands, send_sems, recv_sems, *after)
    return out[:n], out[n:]


SAME_CORE = (2, 4, 6)
SIBLING = 1
SPLIT_EFFECT = pltpu.SideEffectType.DATAFLOW_SIDE_EFFECTING


def _hbm_call(body, name, n_hbm, sems_in, sems_out, after, token, like):
    after = list(after)
    in_specs = [HBM_SPEC] * n_hbm + [SEM_SPEC] * len(sems_in) + [_any()] * len(after)
    out_specs = [SEM_SPEC] * len(sems_out) + [HBM_SPEC] * n_hbm + ([_vmem()] if token else [])
    out_shape = ([pltpu.SemaphoreType.DMA((c,)) for c in sems_out] + [pltpu.HBM(a.shape, a.dtype) for a in like]
                 + ([jax.ShapeDtypeStruct((8, LANES), F32)] if token else []))
    return in_specs, tuple(out_specs), tuple(out_shape), {i: len(sems_out) + i for i in range(n_hbm)}, after


def _gather_send(shards, lands, after, name):
    n = len(shards)
    peers = (SIBLING,) + SAME_CORE
    after = list(after)

    def body(*refs):
        me = _my_place()
        send_sems, recv_sems = refs[2 * n + len(after)], refs[2 * n + len(after) + 1]
        for a, k in enumerate(peers):
            for t in range(n):
                pltpu.make_async_remote_copy(
                    src_ref=refs[t], dst_ref=refs[n + t].at[_flat(me)],
                    send_sem=send_sems.at[t * 4 + a], recv_sem=recv_sems.at[t * 4 + a],
                    device_id=_flip(me, k), device_id_type=MESH_ID).start()
        refs[-1][...] = jnp.zeros_like(refs[-1])

    bufs = list(shards) + list(lands)
    in_specs, out_specs, out_shape, aliases, after = _hbm_call(body, name, 2 * n, (), (4 * n, 4 * n), after, True, bufs)
    out = pl.pallas_call(
        body, name=name, in_specs=in_specs, out_specs=out_specs, out_shape=out_shape,
        input_output_aliases=aliases, compiler_params=pltpu.CompilerParams(has_side_effects=SPLIT_EFFECT),
    )(*[pltpu.with_memory_space_constraint(a, pltpu.HBM) for a in bufs], *after)
    return out[0], out[1], out[2:2 + n], out[2 + n:2 + 2 * n], out[-1]


def _gather_forward(sent, after, name):
    send1, recv1, shards, lands, _ = sent
    n = len(shards)
    after = list(after)

    def body(*refs):
        me = _my_place()
        recv1_ref = refs[2 * n + 1]
        out0 = 2 * n + 2 + len(after)
        send2_ref, recv2_ref = refs[out0], refs[out0 + 1]
        for a, k in enumerate(SAME_CORE):
            owner = _flat(_flip(me, k))
            for t in range(n):
                slot = refs[n + t].at[owner]
                pltpu.make_async_remote_copy(
                    src_ref=refs[t], dst_ref=slot, send_sem=refs[2 * n].at[t * 4 + 1 + a],
                    recv_sem=recv1_ref.at[t * 4 + 1 + a], device_id=_flip(me, k), device_id_type=MESH_ID).wait_recv()
                pltpu.make_async_remote_copy(
                    src_ref=slot, dst_ref=slot, send_sem=send2_ref.at[t * 3 + a], recv_sem=recv2_ref.at[t * 3 + a],
                    device_id=_flip(me, SIBLING), device_id_type=MESH_ID).start()
        refs[-1][...] = jnp.zeros_like(refs[-1])

    bufs = list(shards) + list(lands)
    in_specs, out_specs, out_shape, aliases, after = _hbm_call(body, name, 2 * n, (4 * n, 4 * n), (3 * n, 3 * n), after,
                                                               True, bufs)
    out = pl.pallas_call(
        body, name=name, in_specs=in_specs, out_specs=out_specs, out_shape=out_shape,
        input_output_aliases=aliases, compiler_params=pltpu.CompilerParams(has_side_effects=SPLIT_EFFECT),
    )(*bufs, send1, recv1, *after)
    return (send1, recv1), (out[0], out[1]), out[2:2 + n], out[2 + n:2 + 2 * n], out[-1]


def _gather_finish(forwarded, after, name):
    (send1, recv1), (send2, recv2), shards, lands, _ = forwarded
    n = len(shards)
    after = list(after)

    def body(*refs):
        me = _my_place()
        send1_ref, recv1_ref, send2_ref, recv2_ref = refs[2 * n:2 * n + 4]
        sib = _flip(me, SIBLING)
        for t in range(n):
            for a in range(4):
                cp = pltpu.make_async_remote_copy(
                    src_ref=refs[t], dst_ref=refs[n + t].at[_flat(sib)], send_sem=send1_ref.at[t * 4 + a],
                    recv_sem=recv1_ref.at[t * 4 + a], device_id=sib, device_id_type=MESH_ID)
                cp.wait_send()
                if a == 0:
                    cp.wait_recv()
            for a in range(3):
                cp = pltpu.make_async_remote_copy(
                    src_ref=refs[t], dst_ref=refs[n + t].at[_flat(sib)], send_sem=send2_ref.at[t * 3 + a],
                    recv_sem=recv2_ref.at[t * 3 + a], device_id=sib, device_id_type=MESH_ID)
                cp.wait_send()
                cp.wait_recv()

    bufs = list(shards) + list(lands)
    in_specs, out_specs, out_shape, aliases, after = _hbm_call(body, name, 2 * n, (4 * n, 4 * n, 3 * n, 3 * n), (), after,
                                                               False, bufs)
    out = pl.pallas_call(
        body, name=name, in_specs=in_specs, out_specs=out_specs, out_shape=out_shape,
        input_output_aliases=aliases, compiler_params=pltpu.CompilerParams(has_side_effects=SPLIT_EFFECT),
    )(*bufs, send1, recv1, send2, recv2, *after)
    return out[n:]


def _gather_small(v, after, name):
    after = list(after)

    def body(v_ref, *rest):
        o_ref, send_sems, recv_sems = rest[-3:]
        me = _my_place()
        o_ref[_flat(me)] = v_ref[...]
        copies = []
        for k in range(1, N_DEV):
            copies.append(pltpu.make_async_remote_copy(
                src_ref=v_ref, dst_ref=o_ref.at[_flat(me)],
                send_sem=send_sems.at[k - 1], recv_sem=recv_sems.at[k - 1],
                device_id=_flip(me, k), device_id_type=MESH_ID))
        for cp in copies:
            cp.start()
        for cp in copies:
            cp.wait()

    return pl.pallas_call(
        body, name=name,
        in_specs=[_vmem()] + [_any()] * len(after), out_specs=_vmem(),
        out_shape=jax.ShapeDtypeStruct((N_DEV,) + v.shape, v.dtype),
        scratch_shapes=[pltpu.SemaphoreType.DMA((7,)), pltpu.SemaphoreType.DMA((7,))],
        compiler_params=_params(has_side_effects=True),
    )(v, *after)


def _rel_onehot(row):
    r_io = lax.broadcasted_iota(jnp.int32, (N_REL, PAIR_W), 0)
    p_io = lax.broadcasted_iota(jnp.int32, (N_REL, PAIR_W), 1)
    band_col = p_io - (row // CHUNK) * CHUNK
    in_band = jnp.logical_and(band_col >= 0, band_col < BAND_W)
    idx = jnp.clip(PAD_K + row % CHUNK - band_col, -(CHUNK - 1), REL_CLIP) + (CHUNK - 1)
    return jnp.logical_and(r_io == idx, in_band).astype(BF16), in_band[0:1]


def _bias_expand(rel):
    lh = rel.shape[0]

    def body(rel_ref, o_ref):
        onehot, in_band = _rel_onehot(pl.program_id(0))
        o_ref[...] = jnp.where(in_band, _split_dot(rel_ref[...], onehot, 3), NEG_BIG)

    return pl.pallas_call(
        body, name="bias_expand", grid=(PAIR,),
        in_specs=[pl.BlockSpec((lh, N_REL), lambda i: (0, 0))],
        out_specs=pl.BlockSpec((None, lh, PAIR_W), lambda i: (i, 0, 0)),
        out_shape=jax.ShapeDtypeStruct((PAIR, lh, PAIR_W), F32),
        compiler_params=_params("parallel"),
    )(rel)


def _bias_grad(dbias):
    lh = dbias.shape[1]

    def body(db_ref, o_ref):
        i = pl.program_id(0)

        @pl.when(i == 0)
        def _():
            o_ref[...] = jnp.zeros_like(o_ref)

        o_ref[...] += _split_dot(db_ref[...], _rel_onehot(i)[0], 3, NT)

    return pl.pallas_call(
        body, name="bias_grad", grid=(PAIR,),
        in_specs=[pl.BlockSpec((None, lh, PAIR_W), lambda i: (i, 0, 0))],
        out_specs=pl.BlockSpec((lh, N_REL), lambda i: (0, 0)),
        out_shape=jax.ShapeDtypeStruct((lh, N_REL), F32),
        compiler_params=_params("arbitrary"),
    )(dbias)


def _norm_proj(x, g, w_all, tie, layer):
    s, d = x.shape
    e = w_all.shape[2]
    tm, tn = _tile(s, 512), _tile(e, 1024)
    nb = e // tn
    ties = [] if tie is None else [tie]

    def body(x_ref, g_ref, w_ref, *rest):
        proj_ref, h_ref = rest[-2:]

        @pl.when(pl.program_id(1) == 0)
        def _():
            xv = x_ref[...]
            r = lax.rsqrt(jnp.mean(xv * xv, axis=-1, keepdims=True) + NORM_EPS)
            h_ref[...] = ((xv * r) * g_ref[...]).astype(BF16)

        proj_ref[...] = _dot(h_ref[...], w_ref[...])

    return pl.pallas_call(
        body, name=f"norm_proj_l{layer}", grid=(s // tm, N_DEV * nb),
        in_specs=[pl.BlockSpec((tm, d), lambda m, n: (m, 0)),
                  pl.BlockSpec((1, d), lambda m, n: (0, 0)),
                  pl.BlockSpec((None, d, tn), lambda m, n: (n // nb, 0, n % nb))] + [_any()] * len(ties),
        out_specs=(pl.BlockSpec((None, tm, tn), lambda m, n: (n // nb, m, n % nb)),
                   pl.BlockSpec((tm, d), lambda m, n: (m, 0))),
        out_shape=(jax.ShapeDtypeStruct((N_DEV, s, e), F32), jax.ShapeDtypeStruct((s, d), BF16)),
        compiler_params=_params("parallel", "arbitrary"),
    )(x, g, w_all, *ties)


def _sb_forward(proj, layer):
    _, s, e = proj.shape
    hp = _tile(e // HEAD_DIM, SB_HEADS)
    width = hp * HEAD_DIM
    tq = _tile(s, SB_Q)
    diag_tiles = tq // SB_K
    scale = HEAD_DIM ** -0.5

    def body(p_ref, y_ref, mix_ref, tot_ref, kb_ref, vb_ref):
        kb_ref[...] = p_ref[1].astype(BF16)
        vb_ref[...] = p_ref[2].astype(BF16)
        row = lax.broadcasted_iota(jnp.int32, (tq, SB_K), 0)
        col = lax.broadcasted_iota(jnp.int32, (tq, SB_K), 1)
        kj = lax.broadcasted_iota(jnp.int32, (SB_K, SB_K), 0)
        ks = lax.broadcasted_iota(jnp.int32, (SB_K, SB_K), 1)
        later = (kj > ks).astype(BF16)

        def q_block(qi, _):
            t0 = pl.multiple_of(qi * tq, tq)
            qb = [p_ref[0, pl.ds(t0, tq), h * HEAD_DIM:(h + 1) * HEAD_DIM].astype(BF16) for h in range(hp)]

            def tile(s0, state, causal):
                out = []
                for h in range(hp):
                    carry, acc = state[h]
                    lanes = slice(h * HEAD_DIM, (h + 1) * HEAD_DIM)
                    z = _dot(qb[h], kb_ref[pl.ds(s0, SB_K), lanes], NT) * scale
                    ls = _log_sigmoid(z)
                    stay = ls - z
                    if causal is not None:
                        stay = jnp.where(causal, stay, 0.0)
                    w = jnp.exp(ls + carry + _split_dot(stay, later, 2))
                    if causal is not None:
                        w = jnp.where(causal, w, 0.0)
                    acc = acc + _dot(w.astype(BF16), vb_ref[pl.ds(s0, SB_K), lanes])
                    out.append((carry + jnp.sum(stay, axis=1, keepdims=True), acc))
                return tuple(out)

            state = tuple((jnp.zeros((tq, 1), F32), jnp.zeros((tq, HEAD_DIM), F32)) for _ in range(hp))
            for dt in reversed(range(diag_tiles)):
                state = tile(t0 + dt * SB_K, state, col + dt * SB_K < row)

            def k_block(j, st):
                return tile(pl.multiple_of((diag_tiles * qi - 1 - j) * SB_K, SB_K), st, None)

            state = lax.fori_loop(0, diag_tiles * qi, k_block, state)
            silu, _ = _silu_and_grad(p_ref[3, pl.ds(t0, tq), :])
            for h in range(hp):
                lanes = slice(h * HEAD_DIM, (h + 1) * HEAD_DIM)
                y_ref[pl.ds(t0, tq), lanes] = state[h][1]
                mix_ref[pl.ds(t0, tq), lanes] = (state[h][1] * silu[:, lanes]).astype(BF16)
                tot_ref[h, pl.ds(t0, tq), :] = state[h][0]
            return 0

        lax.fori_loop(0, s // tq, q_block, 0)

    return pl.pallas_call(
        body, name=f"sb_forward_l{layer}", grid=(e // width,),
        in_specs=[pl.BlockSpec((4, s, width), lambda h: (0, 0, h))],
        out_specs=(pl.BlockSpec((s, width), lambda h: (0, h)),
                   pl.BlockSpec((s, width), lambda h: (0, h)),
                   pl.BlockSpec((hp, s, 1), lambda h: (h, 0, 0))),
        out_shape=(jax.ShapeDtypeStruct((s, 2 * e), F32), jax.ShapeDtypeStruct((s, 2 * e), BF16),
                   jax.ShapeDtypeStruct((e // HEAD_DIM, s, 1), F32)),
        scratch_shapes=[pltpu.VMEM((s, width), BF16), pltpu.VMEM((s, width), BF16)],
        compiler_params=_params("parallel"),
    )(proj)


def _qk_norm(t, gain):
    r = lax.rsqrt(jnp.mean(t * t, axis=-1, keepdims=True) + NORM_EPS)
    return t * r, r, (t * r) * gain


def _chunk_scores(qc, kw, bias, t0, scale):
    sc = _dot(qc, kw, NT) * scale + bias
    col = lax.broadcasted_iota(jnp.int32, (PAIR, PAIR_W), 1)
    sc = jnp.where(col + t0 >= PAD_K, sc, NEG_BIG)
    ex = jnp.exp(sc - jnp.max(sc, axis=-1, keepdims=True))
    return ex / jnp.sum(ex, axis=-1, keepdims=True)


def _chunk_forward(proj, bias, q_gain, k_gain, y, mixed, tie, layer):
    ties = [] if tie is None else [tie]
    _, s, e = proj.shape
    hp = _tile(e // HEAD_DIM, CHUNK_HEADS)
    width = hp * HEAD_DIM
    steps = e // width
    unroll = _tile(s // PAIR, PAIR_UNROLL)
    scale = HEAD_DIM ** -0.5
    heads = [slice(h * HEAD_DIM, (h + 1) * HEAD_DIM) for h in range(hp)]

    def body(p_ref, b_ref, qg_ref, kg_ref, *rest):
        y_ref, mix_ref, qn_ref, kp_ref, vp_ref = rest[-5:]
        kp_ref[pl.ds(0, PAD_K), :] = jnp.zeros((PAD_K, width), BF16)
        vp_ref[pl.ds(0, PAD_K), :] = jnp.zeros((PAD_K, width), BF16)
        vp_ref[pl.ds(PAD_K, s), :] = p_ref[2].astype(BF16)
        for lanes in heads:
            qn_ref[:, lanes] = _qk_norm(p_ref[0, :, lanes], qg_ref[...])[2].astype(BF16)
            kp_ref[pl.ds(PAD_K, s), lanes] = _qk_norm(p_ref[1, :, lanes], kg_ref[...])[2].astype(BF16)

        def chunks(ci, _):
            done = []
            for u in range(unroll):
                t0 = pl.multiple_of((ci * unroll + u) * PAIR, PAIR)
                silu, _ = _silu_and_grad(p_ref[3, pl.ds(t0, PAIR), :])
                for h, lanes in enumerate(heads):
                    probs = _chunk_scores(qn_ref[pl.ds(t0, PAIR), lanes], kp_ref[pl.ds(t0, PAIR_W), lanes],
                                          b_ref[h], t0, scale)
                    out = _dot(probs.astype(BF16), vp_ref[pl.ds(t0, PAIR_W), lanes])
                    done.append((t0, lanes, out, (out * silu[:, lanes]).astype(BF16)))
            for t0, lanes, out, gated in done:
                y_ref[pl.ds(t0, PAIR), lanes] = out
                mix_ref[pl.ds(t0, PAIR), lanes] = gated
            return 0

        lax.fori_loop(0, s // (PAIR * unroll), chunks, 0)

    return pl.pallas_call(
        body, name=f"chunk_forward_l{layer}", grid=(steps,),
        in_specs=[pl.BlockSpec((4, s, width), lambda h: (1, 0, h)),
                  pl.BlockSpec((hp, PAIR, PAIR_W), lambda h: (layer * steps + h, 0, 0)),
                  pl.BlockSpec((1, HEAD_DIM), lambda h: (0, 0)),
                  pl.BlockSpec((1, HEAD_DIM), lambda h: (0, 0)),
                  _any(), _any()] + [_any()] * len(ties),
        out_specs=(pl.BlockSpec((s, width), lambda h: (0, steps + h)),
                   pl.BlockSpec((s, width), lambda h: (0, steps + h))),
        out_shape=(jax.ShapeDtypeStruct(y.shape, F32), jax.ShapeDtypeStruct(mixed.shape, BF16)),
        input_output_aliases={4: 0, 5: 1},
        scratch_shapes=[pltpu.VMEM((s, width), BF16), pltpu.VMEM((s + PAD_K, width), BF16),
                        pltpu.VMEM((s + PAD_K, width), BF16)],
        compiler_params=_params("parallel"),
    )(proj, bias, q_gain, k_gain, y, mixed, *ties)


def _out_proj(mixed, w, x, layer):
    s, d = x.shape
    tm, tn = _tile(s, 512), _tile(d, 1024)

    def body(a_ref, w_ref, x_ref, o_ref):
        o_ref[...] = x_ref[...] + _dot(a_ref[...], w_ref[...])

    return pl.pallas_call(
        body, name=f"out_proj_l{layer}", grid=(s // tm, d // tn),
        in_specs=[pl.BlockSpec((tm, d), lambda m, n: (m, 0)),
                  pl.BlockSpec((d, tn), lambda m, n: (0, n)),
                  pl.BlockSpec((tm, tn), lambda m, n: (m, n))],
        out_specs=pl.BlockSpec((tm, tn), lambda m, n: (m, n)),
        out_shape=jax.ShapeDtypeStruct((s, d), F32),
        compiler_params=_params("parallel", "parallel"),
    )(mixed, w, x)


def _loss_head(y, target):
    s, d = y.shape
    tm = _tile(s, 256)

    def body(y_ref, t_ref, dy_ref, part_ref):
        diff = y_ref[...] - t_ref[...]
        dy_ref[...] = diff * (1.0 / d)
        sq = (diff * diff).reshape(tm // 8, 8, d).sum(axis=0)
        acc = sq[:, 0:LANES]
        for j in range(1, d // LANES):
            acc = acc + sq[:, j * LANES:(j + 1) * LANES]
        part_ref[...] = acc * (0.5 / d)

    return pl.pallas_call(
        body, name="loss_head", grid=(s // tm,),
        in_specs=[pl.BlockSpec((tm, d), lambda i: (i, 0)), pl.BlockSpec((tm, d), lambda i: (i, 0))],
        out_specs=(pl.BlockSpec((tm, d), lambda i: (i, 0)), pl.BlockSpec((None, 8, LANES), lambda i: (i, 0, 0))),
        out_shape=(jax.ShapeDtypeStruct((s, d), F32), jax.ShapeDtypeStruct((s // tm, 8, LANES), F32)),
        compiler_params=_params("parallel"),
    )(y, target)


def _out_proj_bwd_input(dx, w, tie, layer):
    s, d = dx.shape
    tm, tn = _tile(s, 512), _tile(d, 1024)
    ties = [] if tie is None else [tie]

    def body(dx_ref, w_ref, *rest):
        rest[-1][...] = _dot(dx_ref[...].astype(BF16), w_ref[...], NT)

    return pl.pallas_call(
        body, name=f"out_proj_dx_l{layer}", grid=(s // tm, d // tn),
        in_specs=[pl.BlockSpec((tm, d), lambda m, n: (m, 0)),
                  pl.BlockSpec((tn, d), lambda m, n: (n, 0))] + [_any()] * len(ties),
        out_specs=pl.BlockSpec((tm, tn), lambda m, n: (m, n)),
        out_shape=jax.ShapeDtypeStruct((s, d), F32),
        compiler_params=_params("parallel", "parallel"),
    )(dx, w, *ties)


def _out_proj_bwd_weight(mixed, dx, layer):
    s, d = dx.shape
    te, tn = _tile(d, 512), _tile(d, 1024)

    def body(a_ref, dx_ref, o_ref):
        o_ref[...] = _dot(a_ref[...], dx_ref[...].astype(BF16), TN).astype(BF16)

    return pl.pallas_call(
        body, name=f"out_proj_dw_l{layer}", grid=(d // te, d // tn),
        in_specs=[pl.BlockSpec((s, te), lambda i, n: (0, i)), pl.BlockSpec((s, tn), lambda i, n: (0, n))],
        out_specs=pl.BlockSpec((te, tn), lambda i, n: (i, n)),
        out_shape=jax.ShapeDtypeStruct((d, d), BF16),
        compiler_params=_params("parallel", "parallel"),
    )(mixed, dx)


def _sb_backward(proj, y, dmixed, tot, layer):
    _, s, e = proj.shape
    hp = _tile(e // HEAD_DIM, SB_HEADS)
    width = hp * HEAD_DIM
    tq = _tile(s, SB_Q)
    diag_tiles = tq // SB_K
    scale = HEAD_DIM ** -0.5

    def body(p_ref, y_ref, dm_ref, tot_ref, o_ref, kb_ref, vb_ref, do_ref, dk_ref, dv_ref):
        kb_ref[...] = p_ref[1].astype(BF16)
        vb_ref[...] = p_ref[2].astype(BF16)
        silu, dsilu = _silu_and_grad(p_ref[3])
        dm = dm_ref[...]
        do_ref[...] = (dm * silu).astype(BF16)
        o_ref[3] = (dm * y_ref[...] * dsilu).astype(BF16)
        dk_ref[...] = jnp.zeros_like(dk_ref)
        dv_ref[...] = jnp.zeros_like(dv_ref)
        row = lax.broadcasted_iota(jnp.int32, (tq, SB_K), 0)
        col = lax.broadcasted_iota(jnp.int32, (tq, SB_K), 1)
        kj = lax.broadcasted_iota(jnp.int32, (SB_K, SB_K), 0)
        ks = lax.broadcasted_iota(jnp.int32, (SB_K, SB_K), 1)
        upto = (kj <= ks).astype(BF16)
        before = (kj < ks).astype(BF16)

        def q_block(qi, _):
            t0 = pl.multiple_of(qi * tq, tq)
            heads = [slice(h * HEAD_DIM, (h + 1) * HEAD_DIM) for h in range(hp)]
            qb = [p_ref[0, pl.ds(t0, tq), lanes].astype(BF16) for lanes in heads]
            dob = [do_ref[pl.ds(t0, tq), lanes] for lanes in heads]
            total = [tot_ref[h, pl.ds(t0, tq), :] for h in range(hp)]

            def tile(s0, state, causal):
                out, adds = [], []
                for h, lanes in enumerate(heads):
                    stay_sum, dlw_sum, dq = state[h]
                    kt = kb_ref[pl.ds(s0, SB_K), lanes]
                    vt = vb_ref[pl.ds(s0, SB_K), lanes]
                    z = _dot(qb[h], kt, NT) * scale
                    ls = _log_sigmoid(z)
                    stay = ls - z
                    if causal is not None:
                        stay = jnp.where(causal, stay, 0.0)
                    after = total[h] - (stay_sum + _split_dot(stay, upto, 3))
                    w = jnp.exp(ls + after)
                    if causal is not None:
                        w = jnp.where(causal, w, 0.0)
                    dlw = _dot(dob[h], vt, NT) * w
                    prior = dlw_sum + _split_dot(dlw, before, 2)
                    sig = jnp.exp(ls)
                    dz = (dlw * (1.0 - sig) - sig * prior) * scale
                    if causal is not None:
                        dz = jnp.where(causal, dz, 0.0)
                    dzb = dz.astype(BF16)
                    dq = dq + _dot(dzb, kt)
                    adds.append((lanes, _dot(dzb, qb[h], TN), _dot(w.astype(BF16), dob[h], TN)))
                    out.append((stay_sum + jnp.sum(stay, axis=1, keepdims=True),
                                dlw_sum + jnp.sum(dlw, axis=1, keepdims=True), dq))
                for lanes, dk, dv in adds:
                    dk_ref[pl.ds(s0, SB_K), lanes] += dk
                    dv_ref[pl.ds(s0, SB_K), lanes] += dv
                return tuple(out)

            def k_block(j, st):
                return tile(pl.multiple_of(j * SB_K, SB_K), st, None)

            zero = jnp.zeros((tq, 1), F32)
            state = tuple((zero, zero, jnp.zeros((tq, HEAD_DIM), F32)) for _ in range(hp))
            state = lax.fori_loop(0, diag_tiles * qi, k_block, state)
            for dt in range(diag_tiles):
                state = tile(t0 + dt * SB_K, state, col + dt * SB_K < row)
            for h, lanes in enumerate(heads):
                o_ref[0, pl.ds(t0, tq), lanes] = state[h][2].astype(BF16)
            return 0

        lax.fori_loop(0, s // tq, q_block, 0)
        o_ref[1] = dk_ref[...].astype(BF16)
        o_ref[2] = dv_ref[...].astype(BF16)

    return pl.pallas_call(
        body, name=f"sb_backward_l{layer}", grid=(e // width,),
        in_specs=[pl.BlockSpec((4, s, width), lambda h: (0, 0, h)),
                  pl.BlockSpec((s, width), lambda h: (0, h)),
                  pl.BlockSpec((s, width), lambda h: (0, h)),
                  pl.BlockSpec((hp, s, 1), lambda h: (h, 0, 0))],
        out_specs=pl.BlockSpec((4, s, width), lambda h: (0, 0, h)),
        out_shape=jax.ShapeDtypeStruct((N_DEV, s, e), BF16),
        scratch_shapes=[pltpu.VMEM((s, width), BF16), pltpu.VMEM((s, width), BF16),
                        pltpu.VMEM((s, width), BF16), pltpu.VMEM((s, width), F32),
                        pltpu.VMEM((s, width), F32)],
        compiler_params=_params("parallel"),
    )(proj, y, dmixed, tot)


def _norm_bwd(dn, xh, r, gain):
    dxh = dn * gain
    return r * (dxh - xh * jnp.mean(dxh * xh, axis=-1, keepdims=True)), dn * xh


def _chunk_backward(proj, bias, q_gain, k_gain, y, dmixed, dproj, layer):
    _, s, e = proj.shape
    hp = _tile(e // HEAD_DIM, CHUNK_HEADS)
    width = hp * HEAD_DIM
    steps = e // width
    unroll = _tile(s // PAIR, PAIR_UNROLL)
    scale = HEAD_DIM ** -0.5
    heads = [slice(h * HEAD_DIM, (h + 1) * HEAD_DIM) for h in range(hp)]

    def body(p_ref, b_ref, qg_ref, kg_ref, y_ref, dm_ref, dp_in, o_ref, db_ref, dqg_ref, dkg_ref,
             qn_ref, kp_ref, vp_ref, do_ref, dqn_ref, dkn_ref, dvp_ref):
        del dp_in
        kp_ref[pl.ds(0, PAD_K), :] = jnp.zeros((PAD_K, width), BF16)
        vp_ref[pl.ds(0, PAD_K), :] = jnp.zeros((PAD_K, width), BF16)
        vp_ref[pl.ds(PAD_K, s), :] = p_ref[2].astype(BF16)
        for lanes in heads:
            qn_ref[:, lanes] = _qk_norm(p_ref[0, :, lanes], qg_ref[...])[2].astype(BF16)
            kp_ref[pl.ds(PAD_K, s), lanes] = _qk_norm(p_ref[1, :, lanes], kg_ref[...])[2].astype(BF16)
        silu, dsilu = _silu_and_grad(p_ref[3])
        dm = dm_ref[...]
        do_ref[...] = (dm * silu).astype(BF16)
        o_ref[3] = (dm * y_ref[...] * dsilu).astype(BF16)
        dkn_ref[...] = jnp.zeros_like(dkn_ref)
        dvp_ref[...] = jnp.zeros_like(dvp_ref)
        db_ref[...] = jnp.zeros_like(db_ref)

        def chunks(ci, _):
            done = []
            for u in range(unroll):
                t0 = pl.multiple_of((ci * unroll + u) * PAIR, PAIR)
                for h, lanes in enumerate(heads):
                    qc = qn_ref[pl.ds(t0, PAIR), lanes]
                    kw = kp_ref[pl.ds(t0, PAIR_W), lanes]
                    vw = vp_ref[pl.ds(t0, PAIR_W), lanes]
                    dob = do_ref[pl.ds(t0, PAIR), lanes]
                    probs = _chunk_scores(qc, kw, b_ref[h], t0, scale)
                    dprobs = _dot(dob, vw, NT)
                    dsc = probs * (dprobs - jnp.sum(probs * dprobs, axis=-1, keepdims=True))
                    dsb = (dsc * scale).astype(BF16)
                    done.append((t0, h, lanes, dsc, _dot(dsb, kw), _dot(dsb, qc, TN),
                                 _dot(probs.astype(BF16), dob, TN)))
            for t0, h, lanes, dsc, dqn, dkn, dvp in done:
                db_ref[h] += dsc
                dqn_ref[pl.ds(t0, PAIR), lanes] = dqn
                dkn_ref[pl.ds(t0, PAIR_W), lanes] += dkn
                dvp_ref[pl.ds(t0, PAIR_W), lanes] += dvp
            return 0

        lax.fori_loop(0, s // (PAIR * unroll), chunks, 0)
        o_ref[2] = dvp_ref[pl.ds(PAD_K, s), :].astype(BF16)

        @pl.when(pl.program_id(0) == 0)
        def _():
            dqg_ref[...] = jnp.zeros_like(dqg_ref)
            dkg_ref[...] = jnp.zeros_like(dkg_ref)

        for lanes in heads:
            qh, rq, _ = _qk_norm(p_ref[0, :, lanes], qg_ref[...])
            dq, dqg_rows = _norm_bwd(dqn_ref[:, lanes], qh, rq, qg_ref[...])
            o_ref[0, :, lanes] = dq.astype(BF16)
            dqg_ref[...] += jnp.sum(dqg_rows, axis=0, keepdims=True)
            kh, rk, _ = _qk_norm(p_ref[1, :, lanes], kg_ref[...])
            dk, dkg_rows = _norm_bwd(dkn_ref[pl.ds(PAD_K, s), lanes], kh, rk, kg_ref[...])
            o_ref[1, :, lanes] = dk.astype(BF16)
            dkg_ref[...] += jnp.sum(dkg_rows, axis=0, keepdims=True)

    return pl.pallas_call(
        body, name=f"chunk_backward_l{layer}", grid=(steps,),
        in_specs=[pl.BlockSpec((4, s, width), lambda h: (1, 0, h)),
                  pl.BlockSpec((hp, PAIR, PAIR_W), lambda h: (layer * steps + h, 0, 0)),
                  pl.BlockSpec((1, HEAD_DIM), lambda h: (0, 0)),
                  pl.BlockSpec((1, HEAD_DIM), lambda h: (0, 0)),
                  pl.BlockSpec((s, width), lambda h: (0, steps + h)),
                  pl.BlockSpec((s, width), lambda h: (0, steps + h)),
                  _any()],
        out_specs=(pl.BlockSpec((4, s, width), lambda h: (1, 0, h)),
                   pl.BlockSpec((hp, PAIR, PAIR_W), lambda h: (h, 0, 0)),
                   pl.BlockSpec((1, HEAD_DIM), lambda h: (0, 0)),
                   pl.BlockSpec((1, HEAD_DIM), lambda h: (0, 0))),
        out_shape=(jax.ShapeDtypeStruct(dproj.shape, BF16),
                   jax.ShapeDtypeStruct((e // HEAD_DIM, PAIR, PAIR_W), F32),
                   jax.ShapeDtypeStruct((1, HEAD_DIM), F32), jax.ShapeDtypeStruct((1, HEAD_DIM), F32)),
        input_output_aliases={6: 0},
        scratch_shapes=[pltpu.VMEM((s, width), BF16), pltpu.VMEM((s + PAD_K, width), BF16),
                        pltpu.VMEM((s + PAD_K, width), BF16), pltpu.VMEM((s, width), BF16),
                        pltpu.VMEM((s, width), F32), pltpu.VMEM((s + PAD_K, width), F32),
                        pltpu.VMEM((s + PAD_K, width), F32)],
        compiler_params=_params("arbitrary"),
    )(proj, bias, q_gain, k_gain, y, dmixed, dproj)


def _proj_bwd_input(dproj, w_all, x, g, dx, tie, layer):
    s, d = x.shape
    e = w_all.shape[2]
    tm = _tile(s, 512)

    def body(dp_ref, w_ref, x_ref, g_ref, dx_ref, tie_ref, o_ref, dg_ref, acc_ref):
        del tie_ref
        j = pl.program_id(1)

        @pl.when(j == 0)
        def _():
            acc_ref[...] = jnp.zeros_like(acc_ref)

        acc_ref[...] += _dot(dp_ref[...], w_ref[...], NT)

        @pl.when(jnp.logical_and(j == N_DEV - 1, pl.program_id(0) == 0))
        def _():
            dg_ref[...] = jnp.zeros_like(dg_ref)

        @pl.when(j == N_DEV - 1)
        def _():
            xv = x_ref[...]
            r = lax.rsqrt(jnp.mean(xv * xv, axis=-1, keepdims=True) + NORM_EPS)
            dxn, dg_rows = _norm_bwd(acc_ref[...], xv * r, r, g_ref[...])
            o_ref[...] = dx_ref[...] + dxn
            dg_ref[...] += jnp.sum(dg_rows, axis=0, keepdims=True)

    return pl.pallas_call(
        body, name=f"proj_dx_l{layer}", grid=(s // tm, N_DEV),
        in_specs=[pl.BlockSpec((None, tm, e), lambda m, j: (j, m, 0)),
                  pl.BlockSpec((None, d, e), lambda m, j: (j, 0, 0)),
                  pl.BlockSpec((tm, d), lambda m, j: (m, 0)),
                  pl.BlockSpec((1, d), lambda m, j: (0, 0)),
                  pl.BlockSpec((tm, d), lambda m, j: (m, 0)), _any()],
        out_specs=(pl.BlockSpec((tm, d), lambda m, j: (m, 0)), pl.BlockSpec((1, d), lambda m, j: (0, 0))),
        out_shape=(jax.ShapeDtypeStruct((s, d), F32), jax.ShapeDtypeStruct((1, d), F32)),
        scratch_shapes=[pltpu.VMEM((tm, d), F32)],
        compiler_params=_params("arbitrary", "arbitrary"),
    )(dproj, w_all, x, g, dx, tie)


def _proj_bwd_weight(h, dproj, layer):
    s, d = h.shape
    e = dproj.shape[2]
    td, tn = _tile(d, 1024), _tile(e, 1024)
    nb = e // tn

    def body(h_ref, dp_ref, o_ref):
        o_ref[...] = _dot(h_ref[...], dp_ref[...], TN).astype(BF16)

    return pl.pallas_call(
        body, name=f"proj_dw_l{layer}", grid=(d // td, N_DEV * nb),
        in_specs=[pl.BlockSpec((s, td), lambda i, n: (0, i)),
                  pl.BlockSpec((None, s, tn), lambda i, n: (n // nb, 0, n % nb))],
        out_specs=pl.BlockSpec((None, td, tn), lambda i, n: (n // nb, i, n % nb)),
        out_shape=jax.ShapeDtypeStruct((N_DEV, d, e), BF16),
        compiler_params=_params("parallel", "parallel"),
    )(h, dproj)


def _adamw_math(w, g, m, v):
    m = ADAM_B1 * m + (1.0 - ADAM_B1) * g
    v = ADAM_B2 * v + (1.0 - ADAM_B2) * (g * g)
    m_hat = m / (1.0 - ADAM_B1 ** ADAM_STEP)
    v_hat = v / (1.0 - ADAM_B2 ** ADAM_STEP)
    return -ADAM_LR * (m_hat / (jnp.sqrt(v_hat) + ADAM_EPS) + ADAM_WD * w), m, v


def _adamw_layer(parts, own, me, w, m, v, prev, layer, name):
    n_layers, rows, cols = w.shape
    tr = _tile(rows, max(8, (256 * 1024) // cols))

    def body(me_ref, p_ref, own_ref, w_ref, m_ref, v_ref, *rest):
        g_ref, d_ref, nm_ref, nv_ref = rest[-4:]
        mine = own_ref[...].astype(F32)
        g = None
        for j in range(N_DEV):
            term = jnp.where(me_ref[0] == j, mine, p_ref[j].astype(F32))
            g = term if g is None else g + term
        g_ref[...] = g
        d_ref[...], nm_ref[...], nv_ref[...] = _adamw_math(w_ref[...], g, m_ref[...], v_ref[...])

    blk = pl.BlockSpec((None, tr, cols), lambda i, me_ref: (layer, i, 0))
    out_shape = tuple(jax.ShapeDtypeStruct(w.shape, F32) for _ in range(4))
    in_specs = [pl.BlockSpec((N_DEV, tr, cols), lambda i, me_ref: (0, i, 0)),
                pl.BlockSpec((None, tr, cols), lambda i, me_ref: (me_ref[0], i, 0)), blk, blk, blk]
    args = [me, parts, own, w, m, v]
    aliases = {}
    if prev is not None:
        in_specs += [_any()] * 4
        args += list(prev)
        aliases = {6 + k: k for k in range(4)}
    return pl.pallas_call(
        body, name=f"{name}_l{layer}",
        grid_spec=pltpu.PrefetchScalarGridSpec(
            num_scalar_prefetch=1, grid=(rows // tr,), in_specs=in_specs, out_specs=(blk, blk, blk, blk)),
        out_shape=out_shape, input_output_aliases=aliases,
        compiler_params=_params("parallel"),
    )(*args)


def _sum_slots(parts):
    def body(p_ref, o_ref):
        g = p_ref[0]
        for j in range(1, N_DEV):
            g = g + p_ref[j]
        o_ref[...] = g

    return pl.pallas_call(
        body, name="sum_small_grads",
        in_specs=[_vmem()], out_specs=_vmem(),
        out_shape=jax.ShapeDtypeStruct(parts.shape[1:], F32),
        compiler_params=_params(),
    )(parts)


def _adamw_small(w, g, m, v):
    def body(w_ref, g_ref, m_ref, v_ref, d_ref, nm_ref, nv_ref):
        d_ref[...], nm_ref[...], nv_ref[...] = _adamw_math(w_ref[...], g_ref[...], m_ref[...], v_ref[...])

    return pl.pallas_call(
        body, name="adamw_small",
        in_specs=[_vmem()] * 4, out_specs=(_vmem(),) * 3,
        out_shape=tuple(jax.ShapeDtypeStruct(w.shape, F32) for _ in range(3)),
        compiler_params=_params(),
    )(w, g, m, v)


def _pack_rows(arrays):
    rows = []
    for a in arrays:
        flat = a.reshape(-1)
        pad = (-flat.shape[0]) % (8 * LANES)
        rows.append(jnp.pad(flat, (0, pad)).reshape(-1, LANES))
    return jnp.concatenate(rows, axis=0)


def _unpack_rows(packed, like):
    out, r0 = [], 0
    for a in like:
        n = a.size
        nr = -(-n // (8 * LANES)) * 8
        out.append(packed[r0:r0 + nr].reshape(-1)[:n].reshape(a.shape))
        r0 += nr
    return out


def kernel(x, norm_g, w_in, q_norm_g, k_norm_g, rel_bias, w_out, loss_target, m_norm_g, m_w_in, m_q_norm_g, m_k_norm_g, m_rel_bias, m_w_out, v_norm_g, v_w_in, v_q_norm_g, v_k_norm_g, v_rel_bias, v_w_out):
    depth, d, e = w_in.shape
    r_out = w_out.shape[1]
    heads = e // HEAD_DIM
    rel_w = rel_bias.shape[2]
    x0 = x[0]
    target = loss_target[0]
    s = x0.shape[0]

    me = jnp.reshape(_flat(_my_place()), (1,)).astype(jnp.int32)

    casts = [(_cast_layer(w_in, me, l, "cast_w_in"), _cast_layer(w_out, me, l, "cast_w_out"))
             for l in range(depth)]

    def begin_gather(l, after):
        (win_b, win_land), (wout_b, wout_land) = casts[l]
        return _gather_send((win_b, wout_b), (win_land, wout_land), after, f"gather_send_l{l}")

    rel_all = _gather_small(rel_bias, [], "gather_rel_bias")
    sent = begin_gather(0, [rel_all])
    rel_full = jnp.transpose(rel_all, (1, 2, 0, 3)).reshape(depth * heads, N_DEV * rel_w)
    bias = jnp.transpose(_bias_expand(rel_full), (1, 0, 2))
    head_work = [bias] + [shard for cast in casts[1:] for shard, _ in cast]
    forwarded = _gather_forward(sent, head_work, "gather_forward_l0")

    xs, hs, projs, ys, mixes, tots, weights = [], [], [], [], [], [], []
    xl = x0
    for l in range(depth):
        win_all, wout_all = _gather_finish(forwarded, [xl, forwarded[-1]], f"gather_finish_l{l}")
        more = l + 1 < depth
        if more:
            sent = begin_gather(l + 1, [win_all])
        wout_full = wout_all.reshape(d, d)
        proj, h = _norm_proj(xl, norm_g[l:l + 1], win_all, sent[-1] if more else None, l)
        y, mixed, tot = _sb_forward(proj, l)
        if more:
            forwarded = _gather_forward(sent, [tot], f"gather_forward_l{l + 1}")
        y, mixed = _chunk_forward(proj, bias, q_norm_g[l:l + 1], k_norm_g[l:l + 1], y, mixed,
                                  forwarded[-1] if more else None, l)
        xs.append(xl), hs.append(h), projs.append(proj), ys.append(y), mixes.append(mixed), tots.append(tot)
        weights.append((win_all, wout_full))
        xl = _out_proj(mixed, wout_full, xl, l)

    dx, loss_parts = _loss_head(xl, target)
    loss = lax.psum(jnp.sum(loss_parts), AXES)

    dbias, dng, dqg, dkg = [None] * depth, [None] * depth, [None] * depth, [None] * depth
    res_in, res_out = None, None

    def finish_exchange(exchanging, after, l):
        (gwin, gwout), (rin, rout) = _copies_wait(exchanging, after, True, f"exchange_wait_l{l}")
        return (_adamw_layer(rin, gwin, me, w_in, m_w_in, v_w_in, res_in, l, "adamw_w_in"),
                _adamw_layer(rout, gwout, me, w_out, m_w_out, v_w_out, res_out, l, "adamw_w_out"))

    pending = []
    for l in reversed(range(depth)):
        win_all, wout_full = weights[l]
        dmixed = _out_proj_bwd_input(dx, wout_full, None, l)
        gwout = _out_proj_bwd_weight(mixes[l], dx, l).reshape(N_DEV, r_out, d)
        dproj = _sb_backward(projs[l], ys[l], dmixed, tots[l], l)
        dproj, dbias[l], dqg[l], dkg[l] = _chunk_backward(
            projs[l], bias, q_norm_g[l:l + 1], k_norm_g[l:l + 1], ys[l], dmixed, dproj, l)
        grads_l = (_proj_bwd_weight(hs[l], dproj, l), gwout)
        exchanging = _copies_start(grads_l, [lax.empty(g.shape, g.dtype) for g in grads_l], True, [],
                                   f"exchange_start_l{l}")
        pending.append((exchanging, l))
        dx, dng[l] = _proj_bwd_input(dproj, win_all, xs[l], norm_g[l:l + 1], dx, exchanging[-1], l)
    for exchanging, l in pending[:-1]:
        res_in, res_out = finish_exchange(exchanging, [dx], l)
    drel = _bias_grad(jnp.transpose(jnp.concatenate(dbias, axis=0), (1, 0, 2)))
    small_like = [norm_g, q_norm_g, k_norm_g, drel]
    mine = _pack_rows([jnp.concatenate(dng, axis=0), jnp.concatenate(dqg, axis=0),
                       jnp.concatenate(dkg, axis=0), drel])
    gathered = _gather_small(mine, [res_in[0], res_out[0]], "gather_small_grads")
    g_norm, g_qn, g_kn, g_rel_full = _unpack_rows(_sum_slots(gathered), small_like)
    my_block = _flat(_my_place())
    g_rel = lax.dynamic_slice_in_dim(g_rel_full.reshape(depth, heads, N_REL), my_block * rel_w, rel_w, axis=2)
    small_w = [norm_g, q_norm_g, k_norm_g, rel_bias]
    small = _adamw_small(_pack_rows(small_w), _pack_rows([g_norm, g_qn, g_kn, g_rel]),
                         _pack_rows([m_norm_g, m_q_norm_g, m_k_norm_g, m_rel_bias]),
                         _pack_rows([v_norm_g, v_q_norm_g, v_k_norm_g, v_rel_bias]))
    d_small, nm_small, nv_small = (_unpack_rows(p, small_w) for p in small)

    res_in, res_out = finish_exchange(pending[-1][0], [small[0], res_in[0], res_out[0]], 0)
    g_win, d_win, nm_win, nv_win = res_in
    g_wout, d_wout, nm_wout, nv_wout = res_out
    grads = (g_norm, g_win, g_qn, g_kn, g_rel, g_wout)

    def order(sm, big_in, big_out):
        return (sm[0], big_in, sm[1], sm[2], sm[3], big_out)

    return (loss, dx[None], *grads, *order(d_small, d_win, d_wout),
            *order(nm_small, nm_win, nm_wout), *order(nv_small, nv_win, nv_wout))
```

```python
import functools

import jax
import jax.numpy as jnp
from jax import lax
from jax.experimental import pallas as pl
from jax.experimental.pallas import tpu as pltpu

F32 = jnp.float32
BF16 = jnp.bfloat16
MESH_ID = pl.DeviceIdType.MESH
AXES = ("x", "y", "c")

N_DEV = 8
HEAD_DIM = 128
CHUNK = 64
LEFT_CHUNKS = 8
BAND_W = (LEFT_CHUNKS + 1) * CHUNK
PAD_K = LEFT_CHUNKS * CHUNK
REL_CLIP = 256
N_REL = REL_CLIP + CHUNK
NORM_EPS = 1e-6
NEG_BIG = -1e30
PAIR = 2 * CHUNK
PAIR_W = BAND_W + CHUNK
CHUNK_HEADS = 2
PAIR_UNROLL = 2
SB_Q = 512
SB_K = 128
SB_HEADS = 2
LANES = 128

ADAM_LR = 0.001
ADAM_B1 = 0.9
ADAM_B2 = 0.999
ADAM_EPS = 1e-08
ADAM_WD = 0.01
ADAM_STEP = 10

VMEM_LIMIT_BYTES = 56 * 1024 * 1024

NT = (((1,), (1,)), ((), ()))
TN = (((0,), (0,)), ((), ()))


def _params(*sem, **kw):
    return pltpu.CompilerParams(dimension_semantics=sem or None, vmem_limit_bytes=VMEM_LIMIT_BYTES, **kw)


def _any():
    return pl.BlockSpec(memory_space=pl.ANY)


def _vmem():
    return pl.BlockSpec(memory_space=pltpu.VMEM)


def _tile(n, want):
    return want if n % want == 0 else n


def _dot(a, b, dims=None):
    if dims is None:
        return jnp.dot(a, b, preferred_element_type=F32)
    return lax.dot_general(a, b, dims, preferred_element_type=F32)


def _split_dot(a, b, parts, dims=None):
    acc = None
    rest = a
    for _ in range(parts):
        piece = rest.astype(BF16)
        rest = rest - piece.astype(F32)
        term = _dot(piece, b, dims)
        acc = term if acc is None else acc + term
    return acc


def _log_sigmoid(z):
    return jnp.minimum(z, 0.0) - jnp.log(1.0 + jnp.exp(-jnp.abs(z)))


def _silu_and_grad(g):
    sig = jax.nn.sigmoid(g)
    return g * sig, sig * (1.0 + g * (1.0 - sig))


def _my_place():
    return lax.axis_index("x"), lax.axis_index("y"), lax.axis_index("c")


def _flat(place):
    return 4 * place[0] + 2 * place[1] + place[2]


def _flip(place, k):
    return tuple(1 - p if (k >> s) & 1 else p for p, s in zip(place, (2, 1, 0)))


def _cast_layer(w, me, layer, name):
    _, rows, cols = w.shape
    tr = _tile(rows, 1024)

    def body(me_ref, a_ref, shard_ref, land_ref):
        del me_ref
        shard_ref[...] = a_ref[...].astype(BF16)
        land_ref[...] = shard_ref[...]

    return pl.pallas_call(
        body, name=f"{name}_l{layer}",
        grid_spec=pltpu.PrefetchScalarGridSpec(
            num_scalar_prefetch=1, grid=(rows // tr,),
            in_specs=[pl.BlockSpec((None, tr, cols), lambda i, me_ref: (layer, i, 0))],
            out_specs=(pl.BlockSpec((tr, cols), lambda i, me_ref: (i, 0)),
                       pl.BlockSpec((None, tr, cols), lambda i, me_ref: (me_ref[0], i, 0)))),
        out_shape=(jax.ShapeDtypeStruct((rows, cols), BF16), jax.ShapeDtypeStruct((N_DEV, rows, cols), BF16)),
        compiler_params=_params("parallel"),
    )(me, w)


HBM_SPEC = pl.BlockSpec(memory_space=pltpu.HBM)
SEM_SPEC = pl.BlockSpec(memory_space=pltpu.SEMAPHORE)
SAME_CORE = (2, 4, 6)
SIBLING = 1
SPLIT_EFFECT = pltpu.SideEffectType.DATAFLOW_SIDE_EFFECTING


def _split_call(body, name, bufs, sems_in, sem_counts_out, after, token):
    bufs, sems_in, after = list(bufs), list(sems_in), list(after)
    nb, ni, no = len(bufs), len(sems_in), len(sem_counts_out)

    def wrapped(*refs):
        outs = nb + ni + len(after)
        body(refs[:nb], refs[nb:nb + ni], refs[outs:outs + no])
        if token:
            refs[-1][...] = jnp.zeros_like(refs[-1])

    out = pl.pallas_call(
        wrapped, name=name,
        in_specs=[HBM_SPEC] * nb + [SEM_SPEC] * ni + [_any()] * len(after),
        out_specs=tuple([SEM_SPEC] * no + [HBM_SPEC] * nb + ([_vmem()] if token else [])),
        out_shape=tuple([pltpu.SemaphoreType.DMA((c,)) for c in sem_counts_out]
                        + [pltpu.HBM(a.shape, a.dtype) for a in bufs]
                        + ([jax.ShapeDtypeStruct((8, LANES), F32)] if token else [])),
        input_output_aliases={i: no + i for i in range(nb)},
        compiler_params=pltpu.CompilerParams(has_side_effects=SPLIT_EFFECT),
    )(*[pltpu.with_memory_space_constraint(a, pltpu.HBM) for a in bufs], *sems_in, *after)
    return list(out[:no]), list(out[no:no + nb]), (out[-1] if token else None)


def _remote(src, dst, send_sems, recv_sems, i, to):
    return pltpu.make_async_remote_copy(src_ref=src, dst_ref=dst, send_sem=send_sems.at[i], recv_sem=recv_sems.at[i],
                                        device_id=to, device_id_type=MESH_ID)


def _wait_copies(name, bufs, sems, n, slot_of, count, after):
    def body(refs, sems_in, _):
        me = _my_place()
        for t in range(n):
            slot = slot_of(refs, t)
            for a in range(count):
                cp = _remote(slot, slot, sems_in[0], sems_in[1], t * count + a, me)
                cp.wait_send()
                cp.wait_recv()

    return _split_call(body, name, bufs, sems, (), after, False)[1]


def _exchange_to_sibling(grads, after, name):
    n = len(grads)
    parts = [lax.empty((3,) + g.shape[1:], g.dtype) for g in grads]
    lands = [lax.empty((4,) + g.shape[1:], g.dtype) for g in grads]

    def body(refs, _, sems_out):
        me = _my_place()
        sib = _flip(me, SIBLING)
        for t in range(n):
            g, part, land = refs[t], refs[n + t], refs[2 * n + t]
            _remote(g.at[_flat(sib)], land.at[0], *sems_out, 4 * t, sib).start()
            for a, k in enumerate(SAME_CORE):
                _remote(g.at[_flat(_flip(sib, k))], part.at[a], *sems_out, 4 * t + 1 + a, sib).start()

    return _split_call(body, name, list(grads) + parts + lands, (), (4 * n, 4 * n), after, True)


def _chip_sums(grads, parts, slots, name):
    _, rows, cols = grads.shape
    tr = _tile(rows, max(8, (512 * 1024) // cols))

    def body(slots_ref, g_ref, p_ref, o_ref):
        del slots_ref
        o_ref[...] = (g_ref[...].astype(F32) + p_ref[...].astype(F32)).astype(BF16)

    return pl.pallas_call(
        body, name=name,
        grid_spec=pltpu.PrefetchScalarGridSpec(
            num_scalar_prefetch=1, grid=(3, rows // tr),
            in_specs=[pl.BlockSpec((None, tr, cols), lambda a, i, slots_ref: (slots_ref[a], i, 0)),
                      pl.BlockSpec((None, tr, cols), lambda a, i, slots_ref: (a, i, 0))],
            out_specs=pl.BlockSpec((None, tr, cols), lambda a, i, slots_ref: (a, i, 0))),
        out_shape=jax.ShapeDtypeStruct((3, rows, cols), BF16),
        compiler_params=_params("parallel", "parallel"),
    )(slots, grads, parts)


def _exchange_to_chips(csums, lands, after, name):
    n = len(csums)

    def body(refs, _, sems_out):
        me = _my_place()
        for t in range(n):
            for a, k in enumerate(SAME_CORE):
                _remote(refs[t].at[a], refs[n + t].at[1 + a], *sems_out, 3 * t + a, _flip(me, k)).start()

    return _split_call(body, name, list(csums) + list(lands), (), (3 * n, 3 * n), after, True)


def _hbm_call(body, name, n_hbm, sems_in, sems_out, after, token, like):
    after = list(after)
    in_specs = [HBM_SPEC] * n_hbm + [SEM_SPEC] * len(sems_in) + [_any()] * len(after)
    out_specs = [SEM_SPEC] * len(sems_out) + [HBM_SPEC] * n_hbm + ([_vmem()] if token else [])
    out_shape = ([pltpu.SemaphoreType.DMA((c,)) for c in sems_out] + [pltpu.HBM(a.shape, a.dtype) for a in like]
                 + ([jax.ShapeDtypeStruct((8, LANES), F32)] if token else []))
    return in_specs, tuple(out_specs), tuple(out_shape), {i: len(sems_out) + i for i in range(n_hbm)}, after


def _gather_send(shards, lands, after, name):
    n = len(shards)
    peers = (SIBLING,) + SAME_CORE
    after = list(after)

    def body(*refs):
        me = _my_place()
        send_sems, recv_sems = refs[2 * n + len(after)], refs[2 * n + len(after) + 1]
        for a, k in enumerate(peers):
            for t in range(n):
                pltpu.make_async_remote_copy(
                    src_ref=refs[t], dst_ref=refs[n + t].at[_flat(me)],
                    send_sem=send_sems.at[t * 4 + a], recv_sem=recv_sems.at[t * 4 + a],
                    device_id=_flip(me, k), device_id_type=MESH_ID).start()
        refs[-1][...] = jnp.zeros_like(refs[-1])

    bufs = list(shards) + list(lands)
    in_specs, out_specs, out_shape, aliases, after = _hbm_call(body, name, 2 * n, (), (4 * n, 4 * n), after, True, bufs)
    out = pl.pallas_call(
        body, name=name, in_specs=in_specs, out_specs=out_specs, out_shape=out_shape,
        input_output_aliases=aliases, compiler_params=pltpu.CompilerParams(has_side_effects=SPLIT_EFFECT),
    )(*[pltpu.with_memory_space_constraint(a, pltpu.HBM) for a in bufs], *after)
    return out[0], out[1], out[2:2 + n], out[2 + n:2 + 2 * n], out[-1]


def _gather_forward(sent, after, name):
    send1, recv1, shards, lands, _ = sent
    n = len(shards)
    after = list(after)

    def body(*refs):
        me = _my_place()
        recv1_ref = refs[2 * n + 1]
        out0 = 2 * n + 2 + len(after)
        send2_ref, recv2_ref = refs[out0], refs[out0 + 1]
        for a, k in enumerate(SAME_CORE):
            owner = _flat(_flip(me, k))
            for t in range(n):
                slot = refs[n + t].at[owner]
                pltpu.make_async_remote_copy(
                    src_ref=refs[t], dst_ref=slot, send_sem=refs[2 * n].at[t * 4 + 1 + a],
                    recv_sem=recv1_ref.at[t * 4 + 1 + a], device_id=_flip(me, k), device_id_type=MESH_ID).wait_recv()
                pltpu.make_async_remote_copy(
                    src_ref=slot, dst_ref=slot, send_sem=send2_ref.at[t * 3 + a], recv_sem=recv2_ref.at[t * 3 + a],
                    device_id=_flip(me, SIBLING), device_id_type=MESH_ID).start()
        refs[-1][...] = jnp.zeros_like(refs[-1])

    bufs = list(shards) + list(lands)
    in_specs, out_specs, out_shape, aliases, after = _hbm_call(body, name, 2 * n, (4 * n, 4 * n), (3 * n, 3 * n), after,
                                                               True, bufs)
    out = pl.pallas_call(
        body, name=name, in_specs=in_specs, out_specs=out_specs, out_shape=out_shape,
        input_output_aliases=aliases, compiler_params=pltpu.CompilerParams(has_side_effects=SPLIT_EFFECT),
    )(*bufs, send1, recv1, *after)
    return (send1, recv1), (out[0], out[1]), out[2:2 + n], out[2 + n:2 + 2 * n], out[-1]


def _gather_finish(forwarded, after, name):
    (send1, recv1), (send2, recv2), shards, lands, _ = forwarded
    n = len(shards)
    after = list(after)

    def body(*refs):
        me = _my_place()
        send1_ref, recv1_ref, send2_ref, recv2_ref = refs[2 * n:2 * n + 4]
        sib = _flip(me, SIBLING)
        for t in range(n):
            for a in range(4):
                cp = pltpu.make_async_remote_copy(
                    src_ref=refs[t], dst_ref=refs[n + t].at[_flat(sib)], send_sem=send1_ref.at[t * 4 + a],
                    recv_sem=recv1_ref.at[t * 4 + a], device_id=sib, device_id_type=MESH_ID)
                cp.wait_send()
                if a == 0:
                    cp.wait_recv()
            for a in range(3):
                cp = pltpu.make_async_remote_copy(
                    src_ref=refs[t], dst_ref=refs[n + t].at[_flat(sib)], send_sem=send2_ref.at[t * 3 + a],
                    recv_sem=recv2_ref.at[t * 3 + a], device_id=sib, device_id_type=MESH_ID)
                cp.wait_send()
                cp.wait_recv()

    bufs = list(shards) + list(lands)
    in_specs, out_specs, out_shape, aliases, after = _hbm_call(body, name, 2 * n, (4 * n, 4 * n, 3 * n, 3 * n), (), after,
                                                               False, bufs)
    out = pl.pallas_call(
        body, name=name, in_specs=in_specs, out_specs=out_specs, out_shape=out_shape,
        input_output_aliases=aliases, compiler_params=pltpu.CompilerParams(has_side_effects=SPLIT_EFFECT),
    )(*bufs, send1, recv1, send2, recv2, *after)
    return out[n:]


def _gather_small(v, after, name):
    after = list(after)

    def body(v_ref, *rest):
        o_ref, send_sems, recv_sems = rest[-3:]
        me = _my_place()
        o_ref[_flat(me)] = v_ref[...]
        copies = []
        for k in range(1, N_DEV):
            copies.append(pltpu.make_async_remote_copy(
                src_ref=v_ref, dst_ref=o_ref.at[_flat(me)],
                send_sem=send_sems.at[k - 1], recv_sem=recv_sems.at[k - 1],
                device_id=_flip(me, k), device_id_type=MESH_ID))
        for cp in copies:
            cp.start()
        for cp in copies:
            cp.wait()

    return pl.pallas_call(
        body, name=name,
        in_specs=[_vmem()] + [_any()] * len(after), out_specs=_vmem(),
        out_shape=jax.ShapeDtypeStruct((N_DEV,) + v.shape, v.dtype),
        scratch_shapes=[pltpu.SemaphoreType.DMA((7,)), pltpu.SemaphoreType.DMA((7,))],
        compiler_params=_params(has_side_effects=True),
    )(v, *after)


def _rel_onehot(row):
    r_io = lax.broadcasted_iota(jnp.int32, (N_REL, PAIR_W), 0)
    p_io = lax.broadcasted_iota(jnp.int32, (N_REL, PAIR_W), 1)
    band_col = p_io - (row // CHUNK) * CHUNK
    in_band = jnp.logical_and(band_col >= 0, band_col < BAND_W)
    idx = jnp.clip(PAD_K + row % CHUNK - band_col, -(CHUNK - 1), REL_CLIP) + (CHUNK - 1)
    return jnp.logical_and(r_io == idx, in_band).astype(BF16), in_band[0:1]


def _bias_expand(rel):
    lh = rel.shape[0]

    def body(rel_ref, o_ref):
        onehot, in_band = _rel_onehot(pl.program_id(0))
        o_ref[...] = jnp.where(in_band, _split_dot(rel_ref[...], onehot, 3), NEG_BIG)

    return pl.pallas_call(
        body, name="bias_expand", grid=(PAIR,),
        in_specs=[pl.BlockSpec((lh, N_REL), lambda i: (0, 0))],
        out_specs=pl.BlockSpec((None, lh, PAIR_W), lambda i: (i, 0, 0)),
        out_shape=jax.ShapeDtypeStruct((PAIR, lh, PAIR_W), F32),
        compiler_params=_params("parallel"),
    )(rel)


def _bias_grad(dbias):
    lh = dbias.shape[1]

    def body(db_ref, o_ref):
        i = pl.program_id(0)

        @pl.when(i == 0)
        def _():
            o_ref[...] = jnp.zeros_like(o_ref)

        o_ref[...] += _split_dot(db_ref[...], _rel_onehot(i)[0], 3, NT)

    return pl.pallas_call(
        body, name="bias_grad", grid=(PAIR,),
        in_specs=[pl.BlockSpec((None, lh, PAIR_W), lambda i: (i, 0, 0))],
        out_specs=pl.BlockSpec((lh, N_REL), lambda i: (0, 0)),
        out_shape=jax.ShapeDtypeStruct((lh, N_REL), F32),
        compiler_params=_params("arbitrary"),
    )(dbias)


def _norm_proj(x, g, w_all, tie, layer):
    s, d = x.shape
    e = w_all.shape[2]
    tm, tn = _tile(s, 512), _tile(e, 1024)
    nb = e // tn
    ties = [] if tie is None else [tie]

    def body(x_ref, g_ref, w_ref, *rest):
        proj_ref, h_ref = rest[-2:]

        @pl.when(pl.program_id(1) == 0)
        def _():
            xv = x_ref[...]
            r = lax.rsqrt(jnp.mean(xv * xv, axis=-1, keepdims=True) + NORM_EPS)
            h_ref[...] = ((xv * r) * g_ref[...]).astype(BF16)

        proj_ref[...] = _dot(h_ref[...], w_ref[...])

    return pl.pallas_call(
        body, name=f"norm_proj_l{layer}", grid=(s // tm, N_DEV * nb),
        in_specs=[pl.BlockSpec((tm, d), lambda m, n: (m, 0)),
                  pl.BlockSpec((1, d), lambda m, n: (0, 0)),
                  pl.BlockSpec((None, d, tn), lambda m, n: (n // nb, 0, n % nb))] + [_any()] * len(ties),
        out_specs=(pl.BlockSpec((None, tm, tn), lambda m, n: (n // nb, m, n % nb)),
                   pl.BlockSpec((tm, d), lambda m, n: (m, 0))),
        out_shape=(jax.ShapeDtypeStruct((N_DEV, s, e), F32), jax.ShapeDtypeStruct((s, d), BF16)),
        compiler_params=_params("parallel", "arbitrary"),
    )(x, g, w_all, *ties)


def _sb_forward(proj, layer):
    _, s, e = proj.shape
    hp = _tile(e // HEAD_DIM, SB_HEADS)
    width = hp * HEAD_DIM
    tq = _tile(s, SB_Q)
    diag_tiles = tq // SB_K
    scale = HEAD_DIM ** -0.5

    def body(p_ref, y_ref, mix_ref, tot_ref, kb_ref, vb_ref):
        kb_ref[...] = p_ref[1].astype(BF16)
        vb_ref[...] = p_ref[2].astype(BF16)
        row = lax.broadcasted_iota(jnp.int32, (tq, SB_K), 0)
        col = lax.broadcasted_iota(jnp.int32, (tq, SB_K), 1)
        kj = lax.broadcasted_iota(jnp.int32, (SB_K, SB_K), 0)
        ks = lax.broadcasted_iota(jnp.int32, (SB_K, SB_K), 1)
        later = (kj > ks).astype(BF16)

        def q_block(qi, _):
            t0 = pl.multiple_of(qi * tq, tq)
            qb = [p_ref[0, pl.ds(t0, tq), h * HEAD_DIM:(h + 1) * HEAD_DIM].astype(BF16) for h in range(hp)]

            def tile(s0, state, causal):
                out = []
                for h in range(hp):
                    carry, acc = state[h]
                    lanes = slice(h * HEAD_DIM, (h + 1) * HEAD_DIM)
                    z = _dot(qb[h], kb_ref[pl.ds(s0, SB_K), lanes], NT) * scale
                    ls = _log_sigmoid(z)
                    stay = ls - z
                    if causal is not None:
                        stay = jnp.where(causal, stay, 0.0)
                    w = jnp.exp(ls + carry + _split_dot(stay, later, 2))
                    if causal is not None:
                        w = jnp.where(causal, w, 0.0)
                    acc = acc + _dot(w.astype(BF16), vb_ref[pl.ds(s0, SB_K), lanes])
                    out.append((carry + jnp.sum(stay, axis=1, keepdims=True), acc))
                return tuple(out)

            state = tuple((jnp.zeros((tq, 1), F32), jnp.zeros((tq, HEAD_DIM), F32)) for _ in range(hp))
            for dt in reversed(range(diag_tiles)):
                state = tile(t0 + dt * SB_K, state, col + dt * SB_K < row)

            def k_block(j, st):
                return tile(pl.multiple_of((diag_tiles * qi - 1 - j) * SB_K, SB_K), st, None)

            state = lax.fori_loop(0, diag_tiles * qi, k_block, state)
            silu, _ = _silu_and_grad(p_ref[3, pl.ds(t0, tq), :])
            for h in range(hp):
                lanes = slice(h * HEAD_DIM, (h + 1) * HEAD_DIM)
                y_ref[pl.ds(t0, tq), lanes] = state[h][1]
                mix_ref[pl.ds(t0, tq), lanes] = (state[h][1] * silu[:, lanes]).astype(BF16)
                tot_ref[h, pl.ds(t0, tq), :] = state[h][0]
            return 0

        lax.fori_loop(0, s // tq, q_block, 0)

    return pl.pallas_call(
        body, name=f"sb_forward_l{layer}", grid=(e // width,),
        in_specs=[pl.BlockSpec((4, s, width), lambda h: (0, 0, h))],
        out_specs=(pl.BlockSpec((s, width), lambda h: (0, h)),
                   pl.BlockSpec((s, width), lambda h: (0, h)),
                   pl.BlockSpec((hp, s, 1), lambda h: (h, 0, 0))),
        out_shape=(jax.ShapeDtypeStruct((s, 2 * e), F32), jax.ShapeDtypeStruct((s, 2 * e), BF16),
                   jax.ShapeDtypeStruct((e // HEAD_DIM, s, 1), F32)),
        scratch_shapes=[pltpu.VMEM((s, width), BF16), pltpu.VMEM((s, width), BF16)],
        compiler_params=_params("parallel"),
    )(proj)


def _qk_norm(t, gain):
    r = lax.rsqrt(jnp.mean(t * t, axis=-1, keepdims=True) + NORM_EPS)
    return t * r, r, (t * r) * gain


def _chunk_scores(qc, kw, bias, t0, scale):
    sc = _dot(qc, kw, NT) * scale + bias
    col = lax.broadcasted_iota(jnp.int32, (PAIR, PAIR_W), 1)
    sc = jnp.where(col + t0 >= PAD_K, sc, NEG_BIG)
    ex = jnp.exp(sc - jnp.max(sc, axis=-1, keepdims=True))
    return ex / jnp.sum(ex, axis=-1, keepdims=True)


def _chunk_forward(proj, bias, q_gain, k_gain, y, mixed, tie, layer):
    ties = [] if tie is None else [tie]
    _, s, e = proj.shape
    hp = _tile(e // HEAD_DIM, CHUNK_HEADS)
    width = hp * HEAD_DIM
    steps = e // width
    unroll = _tile(s // PAIR, PAIR_UNROLL)
    scale = HEAD_DIM ** -0.5
    heads = [slice(h * HEAD_DIM, (h + 1) * HEAD_DIM) for h in range(hp)]

    def body(p_ref, b_ref, qg_ref, kg_ref, *rest):
        y_ref, mix_ref, qn_ref, kp_ref, vp_ref = rest[-5:]
        kp_ref[pl.ds(0, PAD_K), :] = jnp.zeros((PAD_K, width), BF16)
        vp_ref[pl.ds(0, PAD_K), :] = jnp.zeros((PAD_K, width), BF16)
        vp_ref[pl.ds(PAD_K, s), :] = p_ref[2].astype(BF16)
        for lanes in heads:
            qn_ref[:, lanes] = _qk_norm(p_ref[0, :, lanes], qg_ref[...])[2].astype(BF16)
            kp_ref[pl.ds(PAD_K, s), lanes] = _qk_norm(p_ref[1, :, lanes], kg_ref[...])[2].astype(BF16)

        def chunks(ci, _):
            done = []
            for u in range(unroll):
                t0 = pl.multiple_of((ci * unroll + u) * PAIR, PAIR)
                silu, _ = _silu_and_grad(p_ref[3, pl.ds(t0, PAIR), :])
                for h, lanes in enumerate(heads):
                    probs = _chunk_scores(qn_ref[pl.ds(t0, PAIR), lanes], kp_ref[pl.ds(t0, PAIR_W), lanes],
                                          b_ref[h], t0, scale)
                    out = _dot(probs.astype(BF16), vp_ref[pl.ds(t0, PAIR_W), lanes])
                    done.append((t0, lanes, out, (out * silu[:, lanes]).astype(BF16)))
            for t0, lanes, out, gated in done:
                y_ref[pl.ds(t0, PAIR), lanes] = out
                mix_ref[pl.ds(t0, PAIR), lanes] = gated
            return 0

        lax.fori_loop(0, s // (PAIR * unroll), chunks, 0)

    return pl.pallas_call(
        body, name=f"chunk_forward_l{layer}", grid=(steps,),
        in_specs=[pl.BlockSpec((4, s, width), lambda h: (1, 0, h)),
                  pl.BlockSpec((hp, PAIR, PAIR_W), lambda h: (layer * steps + h, 0, 0)),
                  pl.BlockSpec((1, HEAD_DIM), lambda h: (0, 0)),
                  pl.BlockSpec((1, HEAD_DIM), lambda h: (0, 0)),
                  _any(), _any()] + [_any()] * len(ties),
        out_specs=(pl.BlockSpec((s, width), lambda h: (0, steps + h)),
                   pl.BlockSpec((s, width), lambda h: (0, steps + h))),
        out_shape=(jax.ShapeDtypeStruct(y.shape, F32), jax.ShapeDtypeStruct(mixed.shape, BF16)),
        input_output_aliases={4: 0, 5: 1},
        scratch_shapes=[pltpu.VMEM((s, width), BF16), pltpu.VMEM((s + PAD_K, width), BF16),
                        pltpu.VMEM((s + PAD_K, width), BF16)],
        compiler_params=_params("parallel"),
    )(proj, bias, q_gain, k_gain, y, mixed, *ties)


def _out_proj(mixed, w, x, layer):
    s, d = x.shape
    tm, tn = _tile(s, 512), _tile(d, 1024)

    def body(a_ref, w_ref, x_ref, o_ref):
        o_ref[...] = x_ref[...] + _dot(a_ref[...], w_ref[...])

    return pl.pallas_call(
        body, name=f"out_proj_l{layer}", grid=(s // tm, d // tn),
        in_specs=[pl.BlockSpec((tm, d), lambda m, n: (m, 0)),
                  pl.BlockSpec((d, tn), lambda m, n: (0, n)),
                  pl.BlockSpec((tm, tn), lambda m, n: (m, n))],
        out_specs=pl.BlockSpec((tm, tn), lambda m, n: (m, n)),
        out_shape=jax.ShapeDtypeStruct((s, d), F32),
        compiler_params=_params("parallel", "parallel"),
    )(mixed, w, x)


def _loss_head(y, target):
    s, d = y.shape
    tm = _tile(s, 256)

    def body(y_ref, t_ref, dy_ref, part_ref):
        diff = y_ref[...] - t_ref[...]
        dy_ref[...] = diff * (1.0 / d)
        sq = (diff * diff).reshape(tm // 8, 8, d).sum(axis=0)
        acc = sq[:, 0:LANES]
        for j in range(1, d // LANES):
            acc = acc + sq[:, j * LANES:(j + 1) * LANES]
        part_ref[...] = acc * (0.5 / d)

    return pl.pallas_call(
        body, name="loss_head", grid=(s // tm,),
        in_specs=[pl.BlockSpec((tm, d), lambda i: (i, 0)), pl.BlockSpec((tm, d), lambda i: (i, 0))],
        out_specs=(pl.BlockSpec((tm, d), lambda i: (i, 0)), pl.BlockSpec((None, 8, LANES), lambda i: (i, 0, 0))),
        out_shape=(jax.ShapeDtypeStruct((s, d), F32), jax.ShapeDtypeStruct((s // tm, 8, LANES), F32)),
        compiler_params=_params("parallel"),
    )(y, target)


def _out_proj_bwd_input(dx, w, tie, layer):
    s, d = dx.shape
    tm, tn = _tile(s, 512), _tile(d, 1024)
    ties = [] if tie is None else [tie]

    def body(dx_ref, w_ref, *rest):
        rest[-1][...] = _dot(dx_ref[...].astype(BF16), w_ref[...], NT)

    return pl.pallas_call(
        body, name=f"out_proj_dx_l{layer}", grid=(s // tm, d // tn),
        in_specs=[pl.BlockSpec((tm, d), lambda m, n: (m, 0)),
                  pl.BlockSpec((tn, d), lambda m, n: (n, 0))] + [_any()] * len(ties),
        out_specs=pl.BlockSpec((tm, tn), lambda m, n: (m, n)),
        out_shape=jax.ShapeDtypeStruct((s, d), F32),
        compiler_params=_params("parallel", "parallel"),
    )(dx, w, *ties)


def _out_proj_bwd_weight(mixed, dx, layer):
    s, d = dx.shape
    te, tn = _tile(d, 512), _tile(d, 1024)

    def body(a_ref, dx_ref, o_ref):
        o_ref[...] = _dot(a_ref[...], dx_ref[...].astype(BF16), TN).astype(BF16)

    return pl.pallas_call(
        body, name=f"out_proj_dw_l{layer}", grid=(d // te, d // tn),
        in_specs=[pl.BlockSpec((s, te), lambda i, n: (0, i)), pl.BlockSpec((s, tn), lambda i, n: (0, n))],
        out_specs=pl.BlockSpec((te, tn), lambda i, n: (i, n)),
        out_shape=jax.ShapeDtypeStruct((d, d), BF16),
        compiler_params=_params("parallel", "parallel"),
    )(mixed, dx)


def _sb_backward(proj, y, dmixed, tot, layer):
    _, s, e = proj.shape
    hp = _tile(e // HEAD_DIM, SB_HEADS)
    width = hp * HEAD_DIM
    tq = _tile(s, SB_Q)
    diag_tiles = tq // SB_K
    scale = HEAD_DIM ** -0.5

    def body(p_ref, y_ref, dm_ref, tot_ref, o_ref, kb_ref, vb_ref, do_ref, dk_ref, dv_ref):
        kb_ref[...] = p_ref[1].astype(BF16)
        vb_ref[...] = p_ref[2].astype(BF16)
        silu, dsilu = _silu_and_grad(p_ref[3])
        dm = dm_ref[...]
        do_ref[...] = (dm * silu).astype(BF16)
        o_ref[3] = (dm * y_ref[...] * dsilu).astype(BF16)
        dk_ref[...] = jnp.zeros_like(dk_ref)
        dv_ref[...] = jnp.zeros_like(dv_ref)
        row = lax.broadcasted_iota(jnp.int32, (tq, SB_K), 0)
        col = lax.broadcasted_iota(jnp.int32, (tq, SB_K), 1)
        kj = lax.broadcasted_iota(jnp.int32, (SB_K, SB_K), 0)
        ks = lax.broadcasted_iota(jnp.int32, (SB_K, SB_K), 1)
        upto = (kj <= ks).astype(BF16)
        before = (kj < ks).astype(BF16)

        def q_block(qi, _):
            t0 = pl.multiple_of(qi * tq, tq)
            heads = [slice(h * HEAD_DIM, (h + 1) * HEAD_DIM) for h in range(hp)]
            qb = [p_ref[0, pl.ds(t0, tq), lanes].astype(BF16) for lanes in heads]
            dob = [do_ref[pl.ds(t0, tq), lanes] for lanes in heads]
            total = [tot_ref[h, pl.ds(t0, tq), :] for h in range(hp)]

            def tile(s0, state, causal):
                out, adds = [], []
                for h, lanes in enumerate(heads):
                    stay_sum, dlw_sum, dq = state[h]
                    kt = kb_ref[pl.ds(s0, SB_K), lanes]
                    vt = vb_ref[pl.ds(s0, SB_K), lanes]
                    z = _dot(qb[h], kt, NT) * scale
                    ls = _log_sigmoid(z)
                    stay = ls - z
                    if causal is not None:
                        stay = jnp.where(causal, stay, 0.0)
                    after = total[h] - (stay_sum + _split_dot(stay, upto, 3))
                    w = jnp.exp(ls + after)
                    if causal is not None:
                        w = jnp.where(causal, w, 0.0)
                    dlw = _dot(dob[h], vt, NT) * w
                    prior = dlw_sum + _split_dot(dlw, before, 2)
                    sig = jnp.exp(ls)
                    dz = (dlw * (1.0 - sig) - sig * prior) * scale
                    if causal is not None:
                        dz = jnp.where(causal, dz, 0.0)
                    dzb = dz.astype(BF16)
                    dq = dq + _dot(dzb, kt)
                    adds.append((lanes, _dot(dzb, qb[h], TN), _dot(w.astype(BF16), dob[h], TN)))
                    out.append((stay_sum + jnp.sum(stay, axis=1, keepdims=True),
                                dlw_sum + jnp.sum(dlw, axis=1, keepdims=True), dq))
                for lanes, dk, dv in adds:
                    dk_ref[pl.ds(s0, SB_K), lanes] += dk
                    dv_ref[pl.ds(s0, SB_K), lanes] += dv
                return tuple(out)

            def k_block(j, st):
                return tile(pl.multiple_of(j * SB_K, SB_K), st, None)

            zero = jnp.zeros((tq, 1), F32)
            state = tuple((zero, zero, jnp.zeros((tq, HEAD_DIM), F32)) for _ in range(hp))
            state = lax.fori_loop(0, diag_tiles * qi, k_block, state)
            for dt in range(diag_tiles):
                state = tile(t0 + dt * SB_K, state, col + dt * SB_K < row)
            for h, lanes in enumerate(heads):
                o_ref[0, pl.ds(t0, tq), lanes] = state[h][2].astype(BF16)
            return 0

        lax.fori_loop(0, s // tq, q_block, 0)
        o_ref[1] = dk_ref[...].astype(BF16)
        o_ref[2] = dv_ref[...].astype(BF16)

    return pl.pallas_call(
        body, name=f"sb_backward_l{layer}", grid=(e // width,),
        in_specs=[pl.BlockSpec((4, s, width), lambda h: (0, 0, h)),
                  pl.BlockSpec((s, width), lambda h: (0, h)),
                  pl.BlockSpec((s, width), lambda h: (0, h)),
                  pl.BlockSpec((hp, s, 1), lambda h: (h, 0, 0))],
        out_specs=pl.BlockSpec((4, s, width), lambda h: (0, 0, h)),
        out_shape=jax.ShapeDtypeStruct((N_DEV, s, e), BF16),
        scratch_shapes=[pltpu.VMEM((s, width), BF16), pltpu.VMEM((s, width), BF16),
                        pltpu.VMEM((s, width), BF16), pltpu.VMEM((s, width), F32),
                        pltpu.VMEM((s, width), F32)],
        compiler_params=_params("parallel"),
    )(proj, y, dmixed, tot)


def _norm_bwd(dn, xh, r, gain):
    dxh = dn * gain
    return r * (dxh - xh * jnp.mean(dxh * xh, axis=-1, keepdims=True)), dn * xh


def _chunk_backward(proj, bias, q_gain, k_gain, y, dmixed, dproj, layer):
    _, s, e = proj.shape
    hp = _tile(e // HEAD_DIM, CHUNK_HEADS)
    width = hp * HEAD_DIM
    steps = e // width
    unroll = _tile(s // PAIR, PAIR_UNROLL)
    scale = HEAD_DIM ** -0.5
    heads = [slice(h * HEAD_DIM, (h + 1) * HEAD_DIM) for h in range(hp)]

    def body(p_ref, b_ref, qg_ref, kg_ref, y_ref, dm_ref, dp_in, o_ref, db_ref, dqg_ref, dkg_ref,
             qn_ref, kp_ref, vp_ref, do_ref, dqn_ref, dkn_ref, dvp_ref):
        del dp_in
        kp_ref[pl.ds(0, PAD_K), :] = jnp.zeros((PAD_K, width), BF16)
        vp_ref[pl.ds(0, PAD_K), :] = jnp.zeros((PAD_K, width), BF16)
        vp_ref[pl.ds(PAD_K, s), :] = p_ref[2].astype(BF16)
        for lanes in heads:
            qn_ref[:, lanes] = _qk_norm(p_ref[0, :, lanes], qg_ref[...])[2].astype(BF16)
            kp_ref[pl.ds(PAD_K, s), lanes] = _qk_norm(p_ref[1, :, lanes], kg_ref[...])[2].astype(BF16)
        silu, dsilu = _silu_and_grad(p_ref[3])
        dm = dm_ref[...]
        do_ref[...] = (dm * silu).astype(BF16)
        o_ref[3] = (dm * y_ref[...] * dsilu).astype(BF16)
        dkn_ref[...] = jnp.zeros_like(dkn_ref)
        dvp_ref[...] = jnp.zeros_like(dvp_ref)
        db_ref[...] = jnp.zeros_like(db_ref)

        def chunks(ci, _):
            done = []
            for u in range(unroll):
                t0 = pl.multiple_of((ci * unroll + u) * PAIR, PAIR)
                for h, lanes in enumerate(heads):
                    qc = qn_ref[pl.ds(t0, PAIR), lanes]
                    kw = kp_ref[pl.ds(t0, PAIR_W), lanes]
                    vw = vp_ref[pl.ds(t0, PAIR_W), lanes]
                    dob = do_ref[pl.ds(t0, PAIR), lanes]
                    probs = _chunk_scores(qc, kw, b_ref[h], t0, scale)
                    dprobs = _dot(dob, vw, NT)
                    dsc = probs * (dprobs - jnp.sum(probs * dprobs, axis=-1, keepdims=True))
                    dsb = (dsc * scale).astype(BF16)
                    done.append((t0, h, lanes, dsc, _dot(dsb, kw), _dot(dsb, qc, TN),
                                 _dot(probs.astype(BF16), dob, TN)))
            for t0, h, lanes, dsc, dqn, dkn, dvp in done:
                db_ref[h] += dsc
                dqn_ref[pl.ds(t0, PAIR), lanes] = dqn
                dkn_ref[pl.ds(t0, PAIR_W), lanes] += dkn
                dvp_ref[pl.ds(t0, PAIR_W), lanes] += dvp
            return 0

        lax.fori_loop(0, s // (PAIR * unroll), chunks, 0)
        o_ref[2] = dvp_ref[pl.ds(PAD_K, s), :].astype(BF16)

        @pl.when(pl.program_id(0) == 0)
        def _():
            dqg_ref[...] = jnp.zeros_like(dqg_ref)
            dkg_ref[...] = jnp.zeros_like(dkg_ref)

        for lanes in heads:
            qh, rq, _ = _qk_norm(p_ref[0, :, lanes], qg_ref[...])
            dq, dqg_rows = _norm_bwd(dqn_ref[:, lanes], qh, rq, qg_ref[...])
            o_ref[0, :, lanes] = dq.astype(BF16)
            dqg_ref[...] += jnp.sum(dqg_rows, axis=0, keepdims=True)
            kh, rk, _ = _qk_norm(p_ref[1, :, lanes], kg_ref[...])
            dk, dkg_rows = _norm_bwd(dkn_ref[pl.ds(PAD_K, s), lanes], kh, rk, kg_ref[...])
            o_ref[1, :, lanes] = dk.astype(BF16)
            dkg_ref[...] += jnp.sum(dkg_rows, axis=0, keepdims=True)

    return pl.pallas_call(
        body, name=f"chunk_backward_l{layer}", grid=(steps,),
        in_specs=[pl.BlockSpec((4, s, width), lambda h: (1, 0, h)),
                  pl.BlockSpec((hp, PAIR, PAIR_W), lambda h: (layer * steps + h, 0, 0)),
                  pl.BlockSpec((1, HEAD_DIM), lambda h: (0, 0)),
                  pl.BlockSpec((1, HEAD_DIM), lambda h: (0, 0)),
                  pl.BlockSpec((s, width), lambda h: (0, steps + h)),
                  pl.BlockSpec((s, width), lambda h: (0, steps + h)),
                  _any()],
        out_specs=(pl.BlockSpec((4, s, width), lambda h: (1, 0, h)),
                   pl.BlockSpec((hp, PAIR, PAIR_W), lambda h: (h, 0, 0)),
                   pl.BlockSpec((1, HEAD_DIM), lambda h: (0, 0)),
                   pl.BlockSpec((1, HEAD_DIM), lambda h: (0, 0))),
        out_shape=(jax.ShapeDtypeStruct(dproj.shape, BF16),
                   jax.ShapeDtypeStruct((e // HEAD_DIM, PAIR, PAIR_W), F32),
                   jax.ShapeDtypeStruct((1, HEAD_DIM), F32), jax.ShapeDtypeStruct((1, HEAD_DIM), F32)),
        input_output_aliases={6: 0},
        scratch_shapes=[pltpu.VMEM((s, width), BF16), pltpu.VMEM((s + PAD_K, width), BF16),
                        pltpu.VMEM((s + PAD_K, width), BF16), pltpu.VMEM((s, width), BF16),
                        pltpu.VMEM((s, width), F32), pltpu.VMEM((s + PAD_K, width), F32),
                        pltpu.VMEM((s + PAD_K, width), F32)],
        compiler_params=_params("arbitrary"),
    )(proj, bias, q_gain, k_gain, y, dmixed, dproj)


def _proj_bwd_input(dproj, w_all, x, g, dx, tie, layer):
    s, d = x.shape
    e = w_all.shape[2]
    tm = _tile(s, 512)

    def body(dp_ref, w_ref, x_ref, g_ref, dx_ref, tie_ref, o_ref, dg_ref, acc_ref):
        del tie_ref
        j = pl.program_id(1)

        @pl.when(j == 0)
        def _():
            acc_ref[...] = jnp.zeros_like(acc_ref)

        acc_ref[...] += _dot(dp_ref[...], w_ref[...], NT)

        @pl.when(jnp.logical_and(j == N_DEV - 1, pl.program_id(0) == 0))
        def _():
            dg_ref[...] = jnp.zeros_like(dg_ref)

        @pl.when(j == N_DEV - 1)
        def _():
            xv = x_ref[...]
            r = lax.rsqrt(jnp.mean(xv * xv, axis=-1, keepdims=True) + NORM_EPS)
            dxn, dg_rows = _norm_bwd(acc_ref[...], xv * r, r, g_ref[...])
            o_ref[...] = dx_ref[...] + dxn
            dg_ref[...] += jnp.sum(dg_rows, axis=0, keepdims=True)

    return pl.pallas_call(
        body, name=f"proj_dx_l{layer}", grid=(s // tm, N_DEV),
        in_specs=[pl.BlockSpec((None, tm, e), lambda m, j: (j, m, 0)),
                  pl.BlockSpec((None, d, e), lambda m, j: (j, 0, 0)),
                  pl.BlockSpec((tm, d), lambda m, j: (m, 0)),
                  pl.BlockSpec((1, d), lambda m, j: (0, 0)),
                  pl.BlockSpec((tm, d), lambda m, j: (m, 0)), _any()],
        out_specs=(pl.BlockSpec((tm, d), lambda m, j: (m, 0)), pl.BlockSpec((1, d), lambda m, j: (0, 0))),
        out_shape=(jax.ShapeDtypeStruct((s, d), F32), jax.ShapeDtypeStruct((1, d), F32)),
        scratch_shapes=[pltpu.VMEM((tm, d), F32)],
        compiler_params=_params("arbitrary", "arbitrary"),
    )(dproj, w_all, x, g, dx, tie)


def _proj_bwd_weight(h, dproj, layer):
    s, d = h.shape
    e = dproj.shape[2]
    td, tn = _tile(d, 1024), _tile(e, 1024)
    nb = e // tn

    def body(h_ref, dp_ref, o_ref):
        o_ref[...] = _dot(h_ref[...], dp_ref[...], TN).astype(BF16)

    return pl.pallas_call(
        body, name=f"proj_dw_l{layer}", grid=(d // td, N_DEV * nb),
        in_specs=[pl.BlockSpec((s, td), lambda i, n: (0, i)),
                  pl.BlockSpec((None, s, tn), lambda i, n: (n // nb, 0, n % nb))],
        out_specs=pl.BlockSpec((None, td, tn), lambda i, n: (n // nb, i, n % nb)),
        out_shape=jax.ShapeDtypeStruct((N_DEV, d, e), BF16),
        compiler_params=_params("parallel", "parallel"),
    )(h, dproj)


def _adamw_math(w, g, m, v):
    m = ADAM_B1 * m + (1.0 - ADAM_B1) * g
    v = ADAM_B2 * v + (1.0 - ADAM_B2) * (g * g)
    m_hat = m / (1.0 - ADAM_B1 ** ADAM_STEP)
    v_hat = v / (1.0 - ADAM_B2 ** ADAM_STEP)
    return -ADAM_LR * (m_hat / (jnp.sqrt(v_hat) + ADAM_EPS) + ADAM_WD * w), m, v


def _adamw_layer(parts, own, me, w, m, v, prev, layer, name):
    n_layers, rows, cols = w.shape
    tr = _tile(rows, max(8, (256 * 1024) // cols))

    def body(me_ref, p_ref, own_ref, w_ref, m_ref, v_ref, *rest):
        del me_ref
        g_ref, d_ref, nm_ref, nv_ref = rest[-4:]
        g = own_ref[...].astype(F32)
        for j in range(4):
            g = g + p_ref[j].astype(F32)
        g_ref[...] = g
        d_ref[...], nm_ref[...], nv_ref[...] = _adamw_math(w_ref[...], g, m_ref[...], v_ref[...])

    blk = pl.BlockSpec((None, tr, cols), lambda i, me_ref: (layer, i, 0))
    out_shape = tuple(jax.ShapeDtypeStruct(w.shape, F32) for _ in range(4))
    in_specs = [pl.BlockSpec((4, tr, cols), lambda i, me_ref: (0, i, 0)),
                pl.BlockSpec((None, tr, cols), lambda i, me_ref: (me_ref[0], i, 0)), blk, blk, blk]
    args = [me, parts, own, w, m, v]
    aliases = {}
    if prev is not None:
        in_specs += [_any()] * 4
        args += list(prev)
        aliases = {6 + k: k for k in range(4)}
    return pl.pallas_call(
        body, name=f"{name}_l{layer}",
        grid_spec=pltpu.PrefetchScalarGridSpec(
            num_scalar_prefetch=1, grid=(rows // tr,), in_specs=in_specs, out_specs=(blk, blk, blk, blk)),
        out_shape=out_shape, input_output_aliases=aliases,
        compiler_params=_params("parallel"),
    )(*args)


def _sum_slots(parts):
    def body(p_ref, o_ref):
        g = p_ref[0]
        for j in range(1, N_DEV):
            g = g + p_ref[j]
        o_ref[...] = g

    return pl.pallas_call(
        body, name="sum_small_grads",
        in_specs=[_vmem()], out_specs=_vmem(),
        out_shape=jax.ShapeDtypeStruct(parts.shape[1:], F32),
        compiler_params=_params(),
    )(parts)


def _adamw_small(w, g, m, v):
    def body(w_ref, g_ref, m_ref, v_ref, d_ref, nm_ref, nv_ref):
        d_ref[...], nm_ref[...], nv_ref[...] = _adamw_math(w_ref[...], g_ref[...], m_ref[...], v_ref[...])

    return pl.pallas_call(
        body, name="adamw_small",
        in_specs=[_vmem()] * 4, out_specs=(_vmem(),) * 3,
        out_shape=tuple(jax.ShapeDtypeStruct(w.shape, F32) for _ in range(3)),
        compiler_params=_params(),
    )(w, g, m, v)


def _pack_rows(arrays):
    rows = []
    for a in arrays:
        flat = a.reshape(-1)
        pad = (-flat.shape[0]) % (8 * LANES)
        rows.append(jnp.pad(flat, (0, pad)).reshape(-1, LANES))
    return jnp.concatenate(rows, axis=0)


def _unpack_rows(packed, like):
    out, r0 = [], 0
    for a in like:
        n = a.size
        nr = -(-n // (8 * LANES)) * 8
        out.append(packed[r0:r0 + nr].reshape(-1)[:n].reshape(a.shape))
        r0 += nr
    return out


def kernel(x, norm_g, w_in, q_norm_g, k_norm_g, rel_bias, w_out, loss_target, m_norm_g, m_w_in, m_q_norm_g, m_k_norm_g, m_rel_bias, m_w_out, v_norm_g, v_w_in, v_q_norm_g, v_k_norm_g, v_rel_bias, v_w_out):
    depth, d, e = w_in.shape
    r_out = w_out.shape[1]
    heads = e // HEAD_DIM
    rel_w = rel_bias.shape[2]
    x0 = x[0]
    target = loss_target[0]
    s = x0.shape[0]

    me = jnp.reshape(_flat(_my_place()), (1,)).astype(jnp.int32)

    casts = [(_cast_layer(w_in, me, l, "cast_w_in"), _cast_layer(w_out, me, l, "cast_w_out"))
             for l in range(depth)]

    def begin_gather(l, after):
        (win_b, win_land), (wout_b, wout_land) = casts[l]
        return _gather_send((win_b, wout_b), (win_land, wout_land), after, f"gather_send_l{l}")

    rel_all = _gather_small(rel_bias, [], "gather_rel_bias")
    sent = begin_gather(0, [rel_all])
    rel_full = jnp.transpose(rel_all, (1, 2, 0, 3)).reshape(depth * heads, N_DEV * rel_w)
    bias = jnp.transpose(_bias_expand(rel_full), (1, 0, 2))
    head_work = [bias] + [shard for cast in casts[1:] for shard, _ in cast]
    forwarded = _gather_forward(sent, head_work, "gather_forward_l0")

    xs, hs, projs, ys, mixes, tots, weights = [], [], [], [], [], [], []
    xl = x0
    for l in range(depth):
        win_all, wout_all = _gather_finish(forwarded, [xl, forwarded[-1]], f"gather_finish_l{l}")
        more = l + 1 < depth
        if more:
            sent = begin_gather(l + 1, [win_all])
        wout_full = wout_all.reshape(d, d)
        proj, h = _norm_proj(xl, norm_g[l:l + 1], win_all, sent[-1] if more else None, l)
        y, mixed, tot = _sb_forward(proj, l)
        if more:
            forwarded = _gather_forward(sent, [tot], f"gather_forward_l{l + 1}")
        y, mixed = _chunk_forward(proj, bias, q_norm_g[l:l + 1], k_norm_g[l:l + 1], y, mixed,
                                  forwarded[-1] if more else None, l)
        xs.append(xl), hs.append(h), projs.append(proj), ys.append(y), mixes.append(mixed), tots.append(tot)
        weights.append((win_all, wout_full))
        xl = _out_proj(mixed, wout_full, xl, l)

    dx, loss_parts = _loss_head(xl, target)
    loss = lax.psum(jnp.sum(loss_parts), AXES)

    dbias, dng, dqg, dkg = [None] * depth, [None] * depth, [None] * depth, [None] * depth
    res_in, res_out = None, None

    peer_slots = jnp.stack([_flat(_flip(_my_place(), k)) for k in SAME_CORE]).astype(jnp.int32)

    def finish_exchange(exchanging, after, l):
        sems, csums_lands, own = exchanging
        rin, rout = _wait_copies(f"exchange_finish_l{l}", csums_lands, sems, 2, lambda refs, t: refs[t].at[0], 3,
                                 after)[2:]
        return (_adamw_layer(rin, own[0], me, w_in, m_w_in, v_w_in, res_in, l, "adamw_w_in"),
                _adamw_layer(rout, own[1], me, w_out, m_w_out, v_w_out, res_out, l, "adamw_w_out"))

    pending, tie = [], None
    for l in reversed(range(depth)):
        win_all, wout_full = weights[l]
        dmixed = _out_proj_bwd_input(dx, wout_full, tie, l)
        gwout = _out_proj_bwd_weight(mixes[l], dx, l).reshape(N_DEV, r_out, d)
        dproj = _sb_backward(projs[l], ys[l], dmixed, tots[l], l)
        dproj, dbias[l], dqg[l], dkg[l] = _chunk_backward(
            projs[l], bias, q_norm_g[l:l + 1], k_norm_g[l:l + 1], ys[l], dmixed, dproj, l)
        grads_l = (_proj_bwd_weight(hs[l], dproj, l), gwout)
        sems, bufs, token = _exchange_to_sibling(grads_l, [], f"exchange_sibling_l{l}")
        dx, dng[l] = _proj_bwd_input(dproj, win_all, xs[l], norm_g[l:l + 1], dx, token, l)
        bufs = _wait_copies(f"exchange_sibling_wait_l{l}", bufs, sems, 2, lambda refs, t: refs[t].at[0], 4, [dx])
        own, parts, lands = bufs[0:2], bufs[2:4], bufs[4:6]
        csums = [_chip_sums(own[t], parts[t], peer_slots, f"chip_sums_{t}_l{l}") for t in range(2)]
        sems, csums_lands, tie = _exchange_to_chips(csums, lands, [], f"exchange_chips_l{l}")
        pending.append(((sems, csums_lands, own), l))
    for exchanging, l in pending[:-1]:
        res_in, res_out = finish_exchange(exchanging, [dx, tie], l)
    drel = _bias_grad(jnp.transpose(jnp.concatenate(dbias, axis=0), (1, 0, 2)))
    small_like = [norm_g, q_norm_g, k_norm_g, drel]
    mine = _pack_rows([jnp.concatenate(dng, axis=0), jnp.concatenate(dqg, axis=0),
                       jnp.concatenate(dkg, axis=0), drel])
    gathered = _gather_small(mine, [res_in[0], res_out[0]], "gather_small_grads")
    g_norm, g_qn, g_kn, g_rel_full = _unpack_rows(_sum_slots(gathered), small_like)
    my_block = _flat(_my_place())
    g_rel = lax.dynamic_slice_in_dim(g_rel_full.reshape(depth, heads, N_REL), my_block * rel_w, rel_w, axis=2)
    small_w = [norm_g, q_norm_g, k_norm_g, rel_bias]
    small = _adamw_small(_pack_rows(small_w), _pack_rows([g_norm, g_qn, g_kn, g_rel]),
                         _pack_rows([m_norm_g, m_q_norm_g, m_k_norm_g, m_rel_bias]),
                         _pack_rows([v_norm_g, v_q_norm_g, v_k_norm_g, v_rel_bias]))
    d_small, nm_small, nv_small = (_unpack_rows(p, small_w) for p in small)

    res_in, res_out = finish_exchange(pending[-1][0], [small[0], res_in[0], res_out[0]], 0)
    g_win, d_win, nm_win, nv_win = res_in
    g_wout, d_wout, nm_wout, nv_wout = res_out
    grads = (g_norm, g_win, g_qn, g_kn, g_rel, g_wout)

    def order(sm, big_in, big_out):
        return (sm[0], big_in, sm[1], sm[2], sm[3], big_out)

    return (loss, dx[None], *grads, *order(d_small, d_win, d_wout),
            *order(nm_small, nm_win, nm_wout), *order(nv_small, nv_win, nv_wout))
```

```python
import functools

import jax
import jax.numpy as jnp
from jax import lax
from jax.experimental import pallas as pl
from jax.experimental.pallas import tpu as pltpu

F32 = jnp.float32
BF16 = jnp.bfloat16
MESH_ID = pl.DeviceIdType.MESH
AXES = ("x", "y", "c")

N_DEV = 8
HEAD_DIM = 128
CHUNK = 64
LEFT_CHUNKS = 8
BAND_W = (LEFT_CHUNKS + 1) * CHUNK
PAD_K = LEFT_CHUNKS * CHUNK
REL_CLIP = 256
N_REL = REL_CLIP + CHUNK
NORM_EPS = 1e-6
NEG_BIG = -1e30
PAIR = 2 * CHUNK
PAIR_W = BAND_W + CHUNK
CHUNK_HEADS = 2
PAIR_UNROLL = 2
SB_Q = 512
SB_K = 128
SB_HEADS = 2
LANES = 128

ADAM_LR = 0.001
ADAM_B1 = 0.9
ADAM_B2 = 0.999
ADAM_EPS = 1e-08
ADAM_WD = 0.01
ADAM_STEP = 10

VMEM_LIMIT_BYTES = 56 * 1024 * 1024

NT = (((1,), (1,)), ((), ()))
TN = (((0,), (0,)), ((), ()))


def _params(*sem, **kw):
    return pltpu.CompilerParams(dimension_semantics=sem or None, vmem_limit_bytes=VMEM_LIMIT_BYTES, **kw)


def _any():
    return pl.BlockSpec(memory_space=pl.ANY)


def _vmem():
    return pl.BlockSpec(memory_space=pltpu.VMEM)


def _tile(n, want):
    return want if n % want == 0 else n


def _dot(a, b, dims=None):
    if dims is None:
        return jnp.dot(a, b, preferred_element_type=F32)
    return lax.dot_general(a, b, dims, preferred_element_type=F32)


def _split_dot(a, b, parts, dims=None):
    acc = None
    rest = a
    for _ in range(parts):
        piece = rest.astype(BF16)
        rest = rest - piece.astype(F32)
        term = _dot(piece, b, dims)
        acc = term if acc is None else acc + term
    return acc


def _log_sigmoid(z):
    return jnp.minimum(z, 0.0) - jnp.log(1.0 + jnp.exp(-jnp.abs(z)))


def _silu_and_grad(g):
    sig = jax.nn.sigmoid(g)
    return g * sig, sig * (1.0 + g * (1.0 - sig))


def _my_place():
    return lax.axis_index("x"), lax.axis_index("y"), lax.axis_index("c")


def _flat(place):
    return 4 * place[0] + 2 * place[1] + place[2]


def _flip(place, k):
    return tuple(1 - p if (k >> s) & 1 else p for p, s in zip(place, (2, 1, 0)))


def _cast_layer(w, me, layer, name):
    _, rows, cols = w.shape
    tr = _tile(rows, 1024)

    def body(me_ref, a_ref, shard_ref, land_ref):
        del me_ref
        shard_ref[...] = a_ref[...].astype(BF16)
        land_ref[...] = shard_ref[...]

    return pl.pallas_call(
        body, name=f"{name}_l{layer}",
        grid_spec=pltpu.PrefetchScalarGridSpec(
            num_scalar_prefetch=1, grid=(rows // tr,),
            in_specs=[pl.BlockSpec((None, tr, cols), lambda i, me_ref: (layer, i, 0))],
            out_specs=(pl.BlockSpec((tr, cols), lambda i, me_ref: (i, 0)),
                       pl.BlockSpec((None, tr, cols), lambda i, me_ref: (me_ref[0], i, 0)))),
        out_shape=(jax.ShapeDtypeStruct((rows, cols), BF16), jax.ShapeDtypeStruct((N_DEV, rows, cols), BF16)),
        compiler_params=_params("parallel"),
    )(me, w)


HBM_SPEC = pl.BlockSpec(memory_space=pltpu.HBM)
SEM_SPEC = pl.BlockSpec(memory_space=pltpu.SEMAPHORE)
SAME_CORE = (2, 4, 6)
SIBLING = 1
SPLIT_EFFECT = pltpu.SideEffectType.DATAFLOW_SIDE_EFFECTING


def _split_call(body, name, bufs, sems_in, sem_counts_out, after, token):
    bufs, sems_in, after = list(bufs), list(sems_in), list(after)
    nb, ni, no = len(bufs), len(sems_in), len(sem_counts_out)

    def wrapped(*refs):
        outs = nb + ni + len(after)
        body(refs[:nb], refs[nb:nb + ni], refs[outs:outs + no])
        if token:
            refs[-1][...] = jnp.zeros_like(refs[-1])

    out = pl.pallas_call(
        wrapped, name=name,
        in_specs=[HBM_SPEC] * nb + [SEM_SPEC] * ni + [_any()] * len(after),
        out_specs=tuple([SEM_SPEC] * no + [HBM_SPEC] * nb + ([_vmem()] if token else [])),
        out_shape=tuple([pltpu.SemaphoreType.DMA((c,)) for c in sem_counts_out]
                        + [pltpu.HBM(a.shape, a.dtype) for a in bufs]
                        + ([jax.ShapeDtypeStruct((8, LANES), F32)] if token else [])),
        input_output_aliases={i: no + i for i in range(nb)},
        compiler_params=pltpu.CompilerParams(has_side_effects=SPLIT_EFFECT),
    )(*[pltpu.with_memory_space_constraint(a, pltpu.HBM) for a in bufs], *sems_in, *after)
    return list(out[:no]), list(out[no:no + nb]), (out[-1] if token else None)


def _remote(src, dst, send_sems, recv_sems, i, to):
    return pltpu.make_async_remote_copy(src_ref=src, dst_ref=dst, send_sem=send_sems.at[i], recv_sem=recv_sems.at[i],
                                        device_id=to, device_id_type=MESH_ID)


def _wait_copies(name, bufs, sems, n, slot_of, count, after):
    def body(refs, sems_in, _):
        me = _my_place()
        for t in range(n):
            slot = slot_of(refs, t)
            for a in range(count):
                cp = _remote(slot, slot, sems_in[0], sems_in[1], t * count + a, me)
                cp.wait_send()
                cp.wait_recv()

    return _split_call(body, name, bufs, sems, (), after, False)[1]


def _exchange_to_sibling(grads, after, name):
    n = len(grads)
    parts = [lax.empty((3,) + g.shape[1:], g.dtype) for g in grads]
    lands = [lax.empty((4,) + g.shape[1:], g.dtype) for g in grads]

    def body(refs, _, sems_out):
        me = _my_place()
        sib = _flip(me, SIBLING)
        for t in range(n):
            g, part, land = refs[t], refs[n + t], refs[2 * n + t]
            _remote(g.at[_flat(sib)], land.at[0], *sems_out, 4 * t, sib).start()
            for a, k in enumerate(SAME_CORE):
                _remote(g.at[_flat(_flip(sib, k))], part.at[a], *sems_out, 4 * t + 1 + a, sib).start()

    return _split_call(body, name, list(grads) + parts + lands, (), (4 * n, 4 * n), after, True)


def _chip_sums(grads, parts, slots, name):
    _, rows, cols = grads.shape
    tr = _tile(rows, max(8, (512 * 1024) // cols))

    def body(slots_ref, g_ref, p_ref, o_ref):
        del slots_ref
        o_ref[...] = (g_ref[...].astype(F32) + p_ref[...].astype(F32)).astype(BF16)

    return pl.pallas_call(
        body, name=name,
        grid_spec=pltpu.PrefetchScalarGridSpec(
            num_scalar_prefetch=1, grid=(3, rows // tr),
            in_specs=[pl.BlockSpec((None, tr, cols), lambda a, i, slots_ref: (slots_ref[a], i, 0)),
                      pl.BlockSpec((None, tr, cols), lambda a, i, slots_ref: (a, i, 0))],
            out_specs=pl.BlockSpec((None, tr, cols), lambda a, i, slots_ref: (a, i, 0))),
        out_shape=jax.ShapeDtypeStruct((3, rows, cols), BF16),
        compiler_params=_params("parallel", "parallel"),
    )(slots, grads, parts)


def _exchange_to_chips(csums, lands, after, name):
    n = len(csums)

    def body(refs, _, sems_out):
        me = _my_place()
        for t in range(n):
            for a, k in enumerate(SAME_CORE):
                _remote(refs[t].at[a], refs[n + t].at[1 + a], *sems_out, 3 * t + a, _flip(me, k)).start()

    return _split_call(body, name, list(csums) + list(lands), (), (3 * n, 3 * n), after, True)


def _hbm_call(body, name, n_hbm, sems_in, sems_out, after, token, like):
    after = list(after)
    in_specs = [HBM_SPEC] * n_hbm + [SEM_SPEC] * len(sems_in) + [_any()] * len(after)
    out_specs = [SEM_SPEC] * len(sems_out) + [HBM_SPEC] * n_hbm + ([_vmem()] if token else [])
    out_shape = ([pltpu.SemaphoreType.DMA((c,)) for c in sems_out] + [pltpu.HBM(a.shape, a.dtype) for a in like]
                 + ([jax.ShapeDtypeStruct((8, LANES), F32)] if token else []))
    return in_specs, tuple(out_specs), tuple(out_shape), {i: len(sems_out) + i for i in range(n_hbm)}, after


def _gather_send(shards, lands, after, name):
    n = len(shards)
    peers = (SIBLING,) + SAME_CORE
    after = list(after)

    def body(*refs):
        me = _my_place()
        send_sems, recv_sems = refs[2 * n + len(after)], refs[2 * n + len(after) + 1]
        for a, k in enumerate(peers):
            for t in range(n):
                pltpu.make_async_remote_copy(
                    src_ref=refs[t], dst_ref=refs[n + t].at[_flat(me)],
                    send_sem=send_sems.at[t * 4 + a], recv_sem=recv_sems.at[t * 4 + a],
                    device_id=_flip(me, k), device_id_type=MESH_ID).start()
        refs[-1][...] = jnp.zeros_like(refs[-1])

    bufs = list(shards) + list(lands)
    in_specs, out_specs, out_shape, aliases, after = _hbm_call(body, name, 2 * n, (), (4 * n, 4 * n), after, True, bufs)
    out = pl.pallas_call(
        body, name=name, in_specs=in_specs, out_specs=out_specs, out_shape=out_shape,
        input_output_aliases=aliases, compiler_params=pltpu.CompilerParams(has_side_effects=SPLIT_EFFECT),
    )(*[pltpu.with_memory_space_constraint(a, pltpu.HBM) for a in bufs], *after)
    return out[0], out[1], out[2:2 + n], out[2 + n:2 + 2 * n], out[-1]


def _gather_forward(sent, after, name):
    send1, recv1, shards, lands, _ = sent
    n = len(shards)
    after = list(after)

    def body(*refs):
        me = _my_place()
        recv1_ref = refs[2 * n + 1]
        out0 = 2 * n + 2 + len(after)
        send2_ref, recv2_ref = refs[out0], refs[out0 + 1]
        for a, k in enumerate(SAME_CORE):
            owner = _flat(_flip(me, k))
            for t in range(n):
                slot = refs[n + t].at[owner]
                pltpu.make_async_remote_copy(
                    src_ref=refs[t], dst_ref=slot, send_sem=refs[2 * n].at[t * 4 + 1 + a],
                    recv_sem=recv1_ref.at[t * 4 + 1 + a], device_id=_flip(me, k), device_id_type=MESH_ID).wait_recv()
                pltpu.make_async_remote_copy(
                    src_ref=slot, dst_ref=slot, send_sem=send2_ref.at[t * 3 + a], recv_sem=recv2_ref.at[t * 3 + a],
                    device_id=_flip(me, SIBLING), device_id_type=MESH_ID).start()
        refs[-1][...] = jnp.zeros_like(refs[-1])

    bufs = list(shards) + list(lands)
    in_specs, out_specs, out_shape, aliases, after = _hbm_call(body, name, 2 * n, (4 * n, 4 * n), (3 * n, 3 * n), after,
                                                               True, bufs)
    out = pl.pallas_call(
        body, name=name, in_specs=in_specs, out_specs=out_specs, out_shape=out_shape,
        input_output_aliases=aliases, compiler_params=pltpu.CompilerParams(has_side_effects=SPLIT_EFFECT),
    )(*bufs, send1, recv1, *after)
    return (send1, recv1), (out[0], out[1]), out[2:2 + n], out[2 + n:2 + 2 * n], out[-1]


def _gather_finish(forwarded, after, name):
    (send1, recv1), (send2, recv2), shards, lands, _ = forwarded
    n = len(shards)
    after = list(after)

    def body(*refs):
        me = _my_place()
        send1_ref, recv1_ref, send2_ref, recv2_ref = refs[2 * n:2 * n + 4]
        sib = _flip(me, SIBLING)
        for t in range(n):
            for a in range(4):
                cp = pltpu.make_async_remote_copy(
                    src_ref=refs[t], dst_ref=refs[n + t].at[_flat(sib)], send_sem=send1_ref.at[t * 4 + a],
                    recv_sem=recv1_ref.at[t * 4 + a], device_id=sib, device_id_type=MESH_ID)
                cp.wait_send()
                if a == 0:
                    cp.wait_recv()
            for a in range(3):
                cp = pltpu.make_async_remote_copy(
                    src_ref=refs[t], dst_ref=refs[n + t].at[_flat(sib)], send_sem=send2_ref.at[t * 3 + a],
                    recv_sem=recv2_ref.at[t * 3 + a], device_id=sib, device_id_type=MESH_ID)
                cp.wait_send()
                cp.wait_recv()

    bufs = list(shards) + list(lands)
    in_specs, out_specs, out_shape, aliases, after = _hbm_call(body, name, 2 * n, (4 * n, 4 * n, 3 * n, 3 * n), (), after,
                                                               False, bufs)
    out = pl.pallas_call(
        body, name=name, in_specs=in_specs, out_specs=out_specs, out_shape=out_shape,
        input_output_aliases=aliases, compiler_params=pltpu.CompilerParams(has_side_effects=SPLIT_EFFECT),
    )(*bufs, send1, recv1, send2, recv2, *after)
    return out[n:]


def _gather_small(v, after, name):
    after = list(after)

    def body(v_ref, *rest):
        o_ref, send_sems, recv_sems = rest[-3:]
        me = _my_place()
        o_ref[_flat(me)] = v_ref[...]
        copies = []
        for k in range(1, N_DEV):
            copies.append(pltpu.make_async_remote_copy(
                src_ref=v_ref, dst_ref=o_ref.at[_flat(me)],
                send_sem=send_sems.at[k - 1], recv_sem=recv_sems.at[k - 1],
                device_id=_flip(me, k), device_id_type=MESH_ID))
        for cp in copies:
            cp.start()
        for cp in copies:
            cp.wait()

    return pl.pallas_call(
        body, name=name,
        in_specs=[_vmem()] + [_any()] * len(after), out_specs=_vmem(),
        out_shape=jax.ShapeDtypeStruct((N_DEV,) + v.shape, v.dtype),
        scratch_shapes=[pltpu.SemaphoreType.DMA((7,)), pltpu.SemaphoreType.DMA((7,))],
        compiler_params=_params(has_side_effects=True),
    )(v, *after)


def _rel_onehot(row):
    r_io = lax.broadcasted_iota(jnp.int32, (N_REL, PAIR_W), 0)
    p_io = lax.broadcasted_iota(jnp.int32, (N_REL, PAIR_W), 1)
    band_col = p_io - (row // CHUNK) * CHUNK
    in_band = jnp.logical_and(band_col >= 0, band_col < BAND_W)
    idx = jnp.clip(PAD_K + row % CHUNK - band_col, -(CHUNK - 1), REL_CLIP) + (CHUNK - 1)
    return jnp.logical_and(r_io == idx, in_band).astype(BF16), in_band[0:1]


def _bias_expand(rel):
    lh = rel.shape[0]

    def body(rel_ref, o_ref):
        onehot, in_band = _rel_onehot(pl.program_id(0))
        o_ref[...] = jnp.where(in_band, _split_dot(rel_ref[...], onehot, 3), NEG_BIG)

    return pl.pallas_call(
        body, name="bias_expand", grid=(PAIR,),
        in_specs=[pl.BlockSpec((lh, N_REL), lambda i: (0, 0))],
        out_specs=pl.BlockSpec((None, lh, PAIR_W), lambda i: (i, 0, 0)),
        out_shape=jax.ShapeDtypeStruct((PAIR, lh, PAIR_W), F32),
        compiler_params=_params("parallel"),
    )(rel)


def _bias_grad(dbias, after):
    lh = dbias.shape[1]
    after = list(after)

    def body(db_ref, *rest):
        o_ref = rest[-1]
        i = pl.program_id(0)

        @pl.when(i == 0)
        def _():
            o_ref[...] = jnp.zeros_like(o_ref)

        o_ref[...] += _split_dot(db_ref[...], _rel_onehot(i)[0], 2, NT)

    return pl.pallas_call(
        body, name="bias_grad", grid=(PAIR,),
        in_specs=[pl.BlockSpec((None, lh, PAIR_W), lambda i: (i, 0, 0))] + [_any()] * len(after),
        out_specs=pl.BlockSpec((lh, N_REL), lambda i: (0, 0)),
        out_shape=jax.ShapeDtypeStruct((lh, N_REL), F32),
        compiler_params=_params("arbitrary"),
    )(dbias, *after)


def _norm_proj(x, g, w_all, tie, layer):
    s, d = x.shape
    e = w_all.shape[2]
    tm, tn = _tile(s, 1024), _tile(e, 1024)
    nb = e // tn
    ties = [] if tie is None else [tie]

    def body(x_ref, g_ref, w_ref, *rest):
        proj_ref, h_ref = rest[-2:]

        @pl.when(pl.program_id(1) == 0)
        def _():
            xv = x_ref[...]
            r = lax.rsqrt(jnp.mean(xv * xv, axis=-1, keepdims=True) + NORM_EPS)
            h_ref[...] = ((xv * r) * g_ref[...]).astype(BF16)

        proj_ref[...] = _dot(h_ref[...], w_ref[...])

    return pl.pallas_call(
        body, name=f"norm_proj_l{layer}", grid=(s // tm, N_DEV * nb),
        in_specs=[pl.BlockSpec((tm, d), lambda m, n: (m, 0)),
                  pl.BlockSpec((1, d), lambda m, n: (0, 0)),
                  pl.BlockSpec((None, d, tn), lambda m, n: (n // nb, 0, n % nb))] + [_any()] * len(ties),
        out_specs=(pl.BlockSpec((None, tm, tn), lambda m, n: (n // nb, m, n % nb)),
                   pl.BlockSpec((tm, d), lambda m, n: (m, 0))),
        out_shape=(jax.ShapeDtypeStruct((N_DEV, s, e), F32), jax.ShapeDtypeStruct((s, d), BF16)),
        compiler_params=_params("parallel", "arbitrary"),
    )(x, g, w_all, *ties)


def _sb_forward(proj, layer):
    _, s, e = proj.shape
    hp = _tile(e // HEAD_DIM, SB_HEADS)
    width = hp * HEAD_DIM
    tq = _tile(s, SB_Q)
    diag_tiles = tq // SB_K
    scale = HEAD_DIM ** -0.5

    def body(p_ref, y_ref, mix_ref, tot_ref, kb_ref, vb_ref):
        kb_ref[...] = p_ref[1].astype(BF16)
        vb_ref[...] = p_ref[2].astype(BF16)
        row = lax.broadcasted_iota(jnp.int32, (tq, SB_K), 0)
        col = lax.broadcasted_iota(jnp.int32, (tq, SB_K), 1)
        kj = lax.broadcasted_iota(jnp.int32, (SB_K, SB_K), 0)
        ks = lax.broadcasted_iota(jnp.int32, (SB_K, SB_K), 1)
        later = (kj > ks).astype(BF16)

        def q_block(qi, _):
            t0 = pl.multiple_of(qi * tq, tq)
            qb = [p_ref[0, pl.ds(t0, tq), h * HEAD_DIM:(h + 1) * HEAD_DIM].astype(BF16) for h in range(hp)]

            def tile(s0, state, causal):
                out = []
                for h in range(hp):
                    carry, acc = state[h]
                    lanes = slice(h * HEAD_DIM, (h + 1) * HEAD_DIM)
                    z = _dot(qb[h], kb_ref[pl.ds(s0, SB_K), lanes], NT) * scale
                    ls = _log_sigmoid(z)
                    stay = ls - z
                    if causal is not None:
                        stay = jnp.where(causal, stay, 0.0)
                    w = jnp.exp(ls + carry + _split_dot(stay, later, 2))
                    if causal is not None:
                        w = jnp.where(causal, w, 0.0)
                    acc = acc + _dot(w.astype(BF16), vb_ref[pl.ds(s0, SB_K), lanes])
                    out.append((carry + jnp.sum(stay, axis=1, keepdims=True), acc))
                return tuple(out)

            state = tuple((jnp.zeros((tq, 1), F32), jnp.zeros((tq, HEAD_DIM), F32)) for _ in range(hp))
            for dt in reversed(range(diag_tiles)):
                state = tile(t0 + dt * SB_K, state, col + dt * SB_K < row)

            def k_block(j, st):
                return tile(pl.multiple_of((diag_tiles * qi - 1 - j) * SB_K, SB_K), st, None)

            state = lax.fori_loop(0, diag_tiles * qi, k_block, state)
            silu, _ = _silu_and_grad(p_ref[3, pl.ds(t0, tq), :])
            for h in range(hp):
                lanes = slice(h * HEAD_DIM, (h + 1) * HEAD_DIM)
                y_ref[pl.ds(t0, tq), lanes] = state[h][1]
                mix_ref[pl.ds(t0, tq), lanes] = (state[h][1] * silu[:, lanes]).astype(BF16)
                tot_ref[h, pl.ds(t0, tq), :] = state[h][0]
            return 0

        lax.fori_loop(0, s // tq, q_block, 0)

    return pl.pallas_call(
        body, name=f"sb_forward_l{layer}", grid=(e // width,),
        in_specs=[pl.BlockSpec((4, s, width), lambda h: (0, 0, h))],
        out_specs=(pl.BlockSpec((s, width), lambda h: (0, h)),
                   pl.BlockSpec((s, width), lambda h: (0, h)),
                   pl.BlockSpec((hp, s, 1), lambda h: (h, 0, 0))),
        out_shape=(jax.ShapeDtypeStruct((s, 2 * e), F32), jax.ShapeDtypeStruct((s, 2 * e), BF16),
                   jax.ShapeDtypeStruct((e // HEAD_DIM, s, 1), F32)),
        scratch_shapes=[pltpu.VMEM((s, width), BF16), pltpu.VMEM((s, width), BF16)],
        compiler_params=_params("parallel"),
    )(proj)


def _qk_norm(t, gain):
    r = lax.rsqrt(jnp.mean(t * t, axis=-1, keepdims=True) + NORM_EPS)
    return t * r, r, (t * r) * gain


def _chunk_scores(qc, kw, bias, t0, scale):
    sc = _dot(qc, kw, NT) * scale + bias
    col = lax.broadcasted_iota(jnp.int32, (PAIR, PAIR_W), 1)
    sc = jnp.where(col + t0 >= PAD_K, sc, NEG_BIG)
    ex = jnp.exp(sc - jnp.max(sc, axis=-1, keepdims=True))
    return ex / jnp.sum(ex, axis=-1, keepdims=True)


def _chunk_forward(proj, bias, q_gain, k_gain, y, mixed, tie, layer):
    ties = [] if tie is None else [tie]
    _, s, e = proj.shape
    hp = _tile(e // HEAD_DIM, CHUNK_HEADS)
    width = hp * HEAD_DIM
    steps = e // width
    unroll = _tile(s // PAIR, PAIR_UNROLL)
    scale = HEAD_DIM ** -0.5
    heads = [slice(h * HEAD_DIM, (h + 1) * HEAD_DIM) for h in range(hp)]

    def body(p_ref, b_ref, qg_ref, kg_ref, *rest):
        y_ref, mix_ref, qn_ref, kp_ref, vp_ref = rest[-5:]
        kp_ref[pl.ds(0, PAD_K), :] = jnp.zeros((PAD_K, width), BF16)
        vp_ref[pl.ds(0, PAD_K), :] = jnp.zeros((PAD_K, width), BF16)
        vp_ref[pl.ds(PAD_K, s), :] = p_ref[2].astype(BF16)
        for lanes in heads:
            qn_ref[:, lanes] = _qk_norm(p_ref[0, :, lanes], qg_ref[...])[2].astype(BF16)
            kp_ref[pl.ds(PAD_K, s), lanes] = _qk_norm(p_ref[1, :, lanes], kg_ref[...])[2].astype(BF16)

        def chunks(ci, _):
            done = []
            for u in range(unroll):
                t0 = pl.multiple_of((ci * unroll + u) * PAIR, PAIR)
                silu, _ = _silu_and_grad(p_ref[3, pl.ds(t0, PAIR), :])
                for h, lanes in enumerate(heads):
                    probs = _chunk_scores(qn_ref[pl.ds(t0, PAIR), lanes], kp_ref[pl.ds(t0, PAIR_W), lanes],
                                          b_ref[h], t0, scale)
                    out = _dot(probs.astype(BF16), vp_ref[pl.ds(t0, PAIR_W), lanes])
                    done.append((t0, lanes, out, (out * silu[:, lanes]).astype(BF16)))
            for t0, lanes, out, gated in done:
                y_ref[pl.ds(t0, PAIR), lanes] = out
                mix_ref[pl.ds(t0, PAIR), lanes] = gated
            return 0

        lax.fori_loop(0, s // (PAIR * unroll), chunks, 0)

    return pl.pallas_call(
        body, name=f"chunk_forward_l{layer}", grid=(steps,),
        in_specs=[pl.BlockSpec((4, s, width), lambda h: (1, 0, h)),
                  pl.BlockSpec((hp, PAIR, PAIR_W), lambda h: (layer * steps + h, 0, 0)),
                  pl.BlockSpec((1, HEAD_DIM), lambda h: (0, 0)),
                  pl.BlockSpec((1, HEAD_DIM), lambda h: (0, 0)),
                  _any(), _any()] + [_any()] * len(ties),
        out_specs=(pl.BlockSpec((s, width), lambda h: (0, steps + h)),
                   pl.BlockSpec((s, width), lambda h: (0, steps + h))),
        out_shape=(jax.ShapeDtypeStruct(y.shape, F32), jax.ShapeDtypeStruct(mixed.shape, BF16)),
        input_output_aliases={4: 0, 5: 1},
        scratch_shapes=[pltpu.VMEM((s, width), BF16), pltpu.VMEM((s + PAD_K, width), BF16),
                        pltpu.VMEM((s + PAD_K, width), BF16)],
        compiler_params=_params("parallel"),
    )(proj, bias, q_gain, k_gain, y, mixed, *ties)


def _out_proj(mixed, w, x, layer):
    s, d = x.shape
    tm, tn = _tile(s, 512), _tile(d, 1024)

    def body(a_ref, w_ref, x_ref, o_ref):
        o_ref[...] = x_ref[...] + _dot(a_ref[...], w_ref[...])

    return pl.pallas_call(
        body, name=f"out_proj_l{layer}", grid=(s // tm, d // tn),
        in_specs=[pl.BlockSpec((tm, d), lambda m, n: (m, 0)),
                  pl.BlockSpec((d, tn), lambda m, n: (0, n)),
                  pl.BlockSpec((tm, tn), lambda m, n: (m, n))],
        out_specs=pl.BlockSpec((tm, tn), lambda m, n: (m, n)),
        out_shape=jax.ShapeDtypeStruct((s, d), F32),
        compiler_params=_params("parallel", "parallel"),
    )(mixed, w, x)


def _loss_head(y, target):
    s, d = y.shape
    tm = _tile(s, 256)

    def body(y_ref, t_ref, dy_ref, part_ref):
        diff = y_ref[...] - t_ref[...]
        dy_ref[...] = diff * (1.0 / d)
        sq = (diff * diff).reshape(tm // 8, 8, d).sum(axis=0)
        acc = sq[:, 0:LANES]
        for j in range(1, d // LANES):
            acc = acc + sq[:, j * LANES:(j + 1) * LANES]
        part_ref[...] = acc * (0.5 / d)

    return pl.pallas_call(
        body, name="loss_head", grid=(s // tm,),
        in_specs=[pl.BlockSpec((tm, d), lambda i: (i, 0)), pl.BlockSpec((tm, d), lambda i: (i, 0))],
        out_specs=(pl.BlockSpec((tm, d), lambda i: (i, 0)), pl.BlockSpec((None, 8, LANES), lambda i: (i, 0, 0))),
        out_shape=(jax.ShapeDtypeStruct((s, d), F32), jax.ShapeDtypeStruct((s // tm, 8, LANES), F32)),
        compiler_params=_params("parallel"),
    )(y, target)


def _out_proj_bwd_input(dx, w, tie, layer):
    s, d = dx.shape
    tm, tn = _tile(s, 512), _tile(d, 1024)
    ties = [] if tie is None else [tie]

    def body(dx_ref, w_ref, *rest):
        rest[-1][...] = _dot(dx_ref[...].astype(BF16), w_ref[...], NT)

    return pl.pallas_call(
        body, name=f"out_proj_dx_l{layer}", grid=(s // tm, d // tn),
        in_specs=[pl.BlockSpec((tm, d), lambda m, n: (m, 0)),
                  pl.BlockSpec((tn, d), lambda m, n: (n, 0))] + [_any()] * len(ties),
        out_specs=pl.BlockSpec((tm, tn), lambda m, n: (m, n)),
        out_shape=jax.ShapeDtypeStruct((s, d), F32),
        compiler_params=_params("parallel", "parallel"),
    )(dx, w, *ties)


def _out_proj_bwd_weight(mixed, dx, layer):
    s, d = dx.shape
    te, tn = _tile(d, 512), _tile(d, 1024)

    def body(a_ref, dx_ref, o_ref):
        o_ref[...] = _dot(a_ref[...], dx_ref[...].astype(BF16), TN).astype(BF16)

    return pl.pallas_call(
        body, name=f"out_proj_dw_l{layer}", grid=(d // te, d // tn),
        in_specs=[pl.BlockSpec((s, te), lambda i, n: (0, i)), pl.BlockSpec((s, tn), lambda i, n: (0, n))],
        out_specs=pl.BlockSpec((te, tn), lambda i, n: (i, n)),
        out_shape=jax.ShapeDtypeStruct((d, d), BF16),
        compiler_params=_params("parallel", "parallel"),
    )(mixed, dx)


def _sb_backward(proj, y, dmixed, tot, layer):
    _, s, e = proj.shape
    hp = _tile(e // HEAD_DIM, SB_HEADS)
    width = hp * HEAD_DIM
    tq = _tile(s, SB_Q)
    diag_tiles = tq // SB_K
    scale = HEAD_DIM ** -0.5

    def body(p_ref, y_ref, dm_ref, tot_ref, o_ref, kb_ref, vb_ref, do_ref, dk_ref, dv_ref):
        kb_ref[...] = p_ref[1].astype(BF16)
        vb_ref[...] = p_ref[2].astype(BF16)
        silu, dsilu = _silu_and_grad(p_ref[3])
        dm = dm_ref[...]
        do_ref[...] = (dm * silu).astype(BF16)
        o_ref[3] = (dm * y_ref[...] * dsilu).astype(BF16)
        dk_ref[...] = jnp.zeros_like(dk_ref)
        dv_ref[...] = jnp.zeros_like(dv_ref)
        row = lax.broadcasted_iota(jnp.int32, (tq, SB_K), 0)
        col = lax.broadcasted_iota(jnp.int32, (tq, SB_K), 1)
        kj = lax.broadcasted_iota(jnp.int32, (SB_K, SB_K), 0)
        ks = lax.broadcasted_iota(jnp.int32, (SB_K, SB_K), 1)
        upto = (kj <= ks).astype(BF16)
        before = (kj < ks).astype(BF16)

        def q_block(qi, _):
            t0 = pl.multiple_of(qi * tq, tq)
            heads = [slice(h * HEAD_DIM, (h + 1) * HEAD_DIM) for h in range(hp)]
            qb = [p_ref[0, pl.ds(t0, tq), lanes].astype(BF16) for lanes in heads]
            dob = [do_ref[pl.ds(t0, tq), lanes] for lanes in heads]
            total = [tot_ref[h, pl.ds(t0, tq), :] for h in range(hp)]

            def tile(s0, state, causal):
                out, adds = [], []
                for h, lanes in enumerate(heads):
                    stay_sum, dlw_sum, dq = state[h]
                    kt = kb_ref[pl.ds(s0, SB_K), lanes]
                    vt = vb_ref[pl.ds(s0, SB_K), lanes]
                    z = _dot(qb[h], kt, NT) * scale
                    ls = _log_sigmoid(z)
                    stay = ls - z
                    if causal is not None:
                        stay = jnp.where(causal, stay, 0.0)
                    after = total[h] - (stay_sum + _split_dot(stay, upto, 2))
                    w = jnp.exp(ls + after)
                    if causal is not None:
                        w = jnp.where(causal, w, 0.0)
                    dlw = _dot(dob[h], vt, NT) * w
                    prior = dlw_sum + _split_dot(dlw, before, 2)
                    sig = jnp.exp(ls)
                    dz = (dlw * (1.0 - sig) - sig * prior) * scale
                    if causal is not None:
                        dz = jnp.where(causal, dz, 0.0)
                    dzb = dz.astype(BF16)
                    dq = dq + _dot(dzb, kt)
                    adds.append((lanes, _dot(dzb, qb[h], TN), _dot(w.astype(BF16), dob[h], TN)))
                    out.append((stay_sum + jnp.sum(stay, axis=1, keepdims=True),
                                dlw_sum + jnp.sum(dlw, axis=1, keepdims=True), dq))
                for lanes, dk, dv in adds:
                    dk_ref[pl.ds(s0, SB_K), lanes] += dk
                    dv_ref[pl.ds(s0, SB_K), lanes] += dv
                return tuple(out)

            def k_block(j, st):
                return tile(pl.multiple_of(j * SB_K, SB_K), st, None)

            zero = jnp.zeros((tq, 1), F32)
            state = tuple((zero, zero, jnp.zeros((tq, HEAD_DIM), F32)) for _ in range(hp))
            state = lax.fori_loop(0, diag_tiles * qi, k_block, state)
            for dt in range(diag_tiles):
                state = tile(t0 + dt * SB_K, state, col + dt * SB_K < row)
            for h, lanes in enumerate(heads):
                o_ref[0, pl.ds(t0, tq), lanes] = state[h][2].astype(BF16)
            return 0

        lax.fori_loop(0, s // tq, q_block, 0)
        o_ref[1] = dk_ref[...].astype(BF16)
        o_ref[2] = dv_ref[...].astype(BF16)

    return pl.pallas_call(
        body, name=f"sb_backward_l{layer}", grid=(e // width,),
        in_specs=[pl.BlockSpec((4, s, width), lambda h: (0, 0, h)),
                  pl.BlockSpec((s, width), lambda h: (0, h)),
                  pl.BlockSpec((s, width), lambda h: (0, h)),
                  pl.BlockSpec((hp, s, 1), lambda h: (h, 0, 0))],
        out_specs=pl.BlockSpec((4, s, width), lambda h: (0, 0, h)),
        out_shape=jax.ShapeDtypeStruct((N_DEV, s, e), BF16),
        scratch_shapes=[pltpu.VMEM((s, width), BF16), pltpu.VMEM((s, width), BF16),
                        pltpu.VMEM((s, width), BF16), pltpu.VMEM((s, width), F32),
                        pltpu.VMEM((s, width), F32)],
        compiler_params=_params("parallel"),
    )(proj, y, dmixed, tot)


def _norm_bwd(dn, xh, r, gain):
    dxh = dn * gain
    return r * (dxh - xh * jnp.mean(dxh * xh, axis=-1, keepdims=True)), dn * xh


def _chunk_backward(proj, bias, q_gain, k_gain, y, dmixed, dproj, layer):
    _, s, e = proj.shape
    hp = _tile(e // HEAD_DIM, CHUNK_HEADS)
    width = hp * HEAD_DIM
    steps = e // width
    unroll = _tile(s // PAIR, PAIR_UNROLL)
    scale = HEAD_DIM ** -0.5
    heads = [slice(h * HEAD_DIM, (h + 1) * HEAD_DIM) for h in range(hp)]

    def body(p_ref, b_ref, qg_ref, kg_ref, y_ref, dm_ref, dp_in, o_ref, db_ref, dqg_ref, dkg_ref,
             qn_ref, kp_ref, vp_ref, do_ref, dqn_ref, dkn_ref, dvp_ref):
        del dp_in
        kp_ref[pl.ds(0, PAD_K), :] = jnp.zeros((PAD_K, width), BF16)
        vp_ref[pl.ds(0, PAD_K), :] = jnp.zeros((PAD_K, width), BF16)
        vp_ref[pl.ds(PAD_K, s), :] = p_ref[2].astype(BF16)
        for lanes in heads:
            qn_ref[:, lanes] = _qk_norm(p_ref[0, :, lanes], qg_ref[...])[2].astype(BF16)
            kp_ref[pl.ds(PAD_K, s), lanes] = _qk_norm(p_ref[1, :, lanes], kg_ref[...])[2].astype(BF16)
        silu, dsilu = _silu_and_grad(p_ref[3])
        dm = dm_ref[...]
        do_ref[...] = (dm * silu).astype(BF16)
        o_ref[3] = (dm * y_ref[...] * dsilu).astype(BF16)
        dkn_ref[...] = jnp.zeros_like(dkn_ref)
        dvp_ref[...] = jnp.zeros_like(dvp_ref)
        db_ref[...] = jnp.zeros_like(db_ref)

        def chunks(ci, _):
            done = []
            for u in range(unroll):
                t0 = pl.multiple_of((ci * unroll + u) * PAIR, PAIR)
                for h, lanes in enumerate(heads):
                    qc = qn_ref[pl.ds(t0, PAIR), lanes]
                    kw = kp_ref[pl.ds(t0, PAIR_W), lanes]
                    vw = vp_ref[pl.ds(t0, PAIR_W), lanes]
                    dob = do_ref[pl.ds(t0, PAIR), lanes]
                    probs = _chunk_scores(qc, kw, b_ref[h], t0, scale)
                    dprobs = _dot(dob, vw, NT)
                    dsc = probs * (dprobs - jnp.sum(probs * dprobs, axis=-1, keepdims=True))
                    dsb = (dsc * scale).astype(BF16)
                    done.append((t0, h, lanes, dsc, _dot(dsb, kw), _dot(dsb, qc, TN),
                                 _dot(probs.astype(BF16), dob, TN)))
            for t0, h, lanes, dsc, dqn, dkn, dvp in done:
                db_ref[h] += dsc
                dqn_ref[pl.ds(t0, PAIR), lanes] = dqn
                dkn_ref[pl.ds(t0, PAIR_W), lanes] += dkn
                dvp_ref[pl.ds(t0, PAIR_W), lanes] += dvp
            return 0

        lax.fori_loop(0, s // (PAIR * unroll), chunks, 0)
        o_ref[2] = dvp_ref[pl.ds(PAD_K, s), :].astype(BF16)

        @pl.when(pl.program_id(0) == 0)
        def _():
            dqg_ref[...] = jnp.zeros_like(dqg_ref)
            dkg_ref[...] = jnp.zeros_like(dkg_ref)

        for lanes in heads:
            qh, rq, _ = _qk_norm(p_ref[0, :, lanes], qg_ref[...])
            dq, dqg_rows = _norm_bwd(dqn_ref[:, lanes], qh, rq, qg_ref[...])
            o_ref[0, :, lanes] = dq.astype(BF16)
            dqg_ref[...] += jnp.sum(dqg_rows, axis=0, keepdims=True)
            kh, rk, _ = _qk_norm(p_ref[1, :, lanes], kg_ref[...])
            dk, dkg_rows = _norm_bwd(dkn_ref[pl.ds(PAD_K, s), lanes], kh, rk, kg_ref[...])
            o_ref[1, :, lanes] = dk.astype(BF16)
            dkg_ref[...] += jnp.sum(dkg_rows, axis=0, keepdims=True)

    return pl.pallas_call(
        body, name=f"chunk_backward_l{layer}", grid=(steps,),
        in_specs=[pl.BlockSpec((4, s, width), lambda h: (1, 0, h)),
                  pl.BlockSpec((hp, PAIR, PAIR_W), lambda h: (layer * steps + h, 0, 0)),
                  pl.BlockSpec((1, HEAD_DIM), lambda h: (0, 0)),
                  pl.BlockSpec((1, HEAD_DIM), lambda h: (0, 0)),
                  pl.BlockSpec((s, width), lambda h: (0, steps + h)),
                  pl.BlockSpec((s, width), lambda h: (0, steps + h)),
                  _any()],
        out_specs=(pl.BlockSpec((4, s, width), lambda h: (1, 0, h)),
                   pl.BlockSpec((hp, PAIR, PAIR_W), lambda h: (h, 0, 0)),
                   pl.BlockSpec((1, HEAD_DIM), lambda h: (0, 0)),
                   pl.BlockSpec((1, HEAD_DIM), lambda h: (0, 0))),
        out_shape=(jax.ShapeDtypeStruct(dproj.shape, BF16),
                   jax.ShapeDtypeStruct((e // HEAD_DIM, PAIR, PAIR_W), F32),
                   jax.ShapeDtypeStruct((1, HEAD_DIM), F32), jax.ShapeDtypeStruct((1, HEAD_DIM), F32)),
        input_output_aliases={6: 0},
        scratch_shapes=[pltpu.VMEM((s, width), BF16), pltpu.VMEM((s + PAD_K, width), BF16),
                        pltpu.VMEM((s + PAD_K, width), BF16), pltpu.VMEM((s, width), BF16),
                        pltpu.VMEM((s, width), F32), pltpu.VMEM((s + PAD_K, width), F32),
                        pltpu.VMEM((s + PAD_K, width), F32)],
        compiler_params=_params("arbitrary"),
    )(proj, bias, q_gain, k_gain, y, dmixed, dproj)


def _proj_bwd_input(dproj, w_all, x, g, dx, tie, layer):
    s, d = x.shape
    e = w_all.shape[2]
    tm = _tile(s, 512)

    def body(dp_ref, w_ref, x_ref, g_ref, dx_ref, tie_ref, o_ref, dg_ref, acc_ref):
        del tie_ref
        j = pl.program_id(1)

        @pl.when(j == 0)
        def _():
            acc_ref[...] = jnp.zeros_like(acc_ref)

        acc_ref[...] += _dot(dp_ref[...], w_ref[...], NT)

        @pl.when(jnp.logical_and(j == N_DEV - 1, pl.program_id(0) == 0))
        def _():
            dg_ref[...] = jnp.zeros_like(dg_ref)

        @pl.when(j == N_DEV - 1)
        def _():
            xv = x_ref[...]
            r = lax.rsqrt(jnp.mean(xv * xv, axis=-1, keepdims=True) + NORM_EPS)
            dxn, dg_rows = _norm_bwd(acc_ref[...], xv * r, r, g_ref[...])
            o_ref[...] = dx_ref[...] + dxn
            dg_ref[...] += jnp.sum(dg_rows, axis=0, keepdims=True)

    return pl.pallas_call(
        body, name=f"proj_dx_l{layer}", grid=(s // tm, N_DEV),
        in_specs=[pl.BlockSpec((None, tm, e), lambda m, j: (j, m, 0)),
                  pl.BlockSpec((None, d, e), lambda m, j: (j, 0, 0)),
                  pl.BlockSpec((tm, d), lambda m, j: (m, 0)),
                  pl.BlockSpec((1, d), lambda m, j: (0, 0)),
                  pl.BlockSpec((tm, d), lambda m, j: (m, 0)), _any()],
        out_specs=(pl.BlockSpec((tm, d), lambda m, j: (m, 0)), pl.BlockSpec((1, d), lambda m, j: (0, 0))),
        out_shape=(jax.ShapeDtypeStruct((s, d), F32), jax.ShapeDtypeStruct((1, d), F32)),
        scratch_shapes=[pltpu.VMEM((tm, d), F32)],
        compiler_params=_params("arbitrary", "arbitrary"),
    )(dproj, w_all, x, g, dx, tie)


def _proj_bwd_weight(h, dproj, layer):
    s, d = h.shape
    e = dproj.shape[2]
    td, tn = _tile(d, 1024), _tile(e, 1024)
    nb = e // tn

    def body(h_ref, dp_ref, o_ref):
        o_ref[...] = _dot(h_ref[...], dp_ref[...], TN).astype(BF16)

    return pl.pallas_call(
        body, name=f"proj_dw_l{layer}", grid=(d // td, N_DEV * nb),
        in_specs=[pl.BlockSpec((s, td), lambda i, n: (0, i)),
                  pl.BlockSpec((None, s, tn), lambda i, n: (n // nb, 0, n % nb))],
        out_specs=pl.BlockSpec((None, td, tn), lambda i, n: (n // nb, i, n % nb)),
        out_shape=jax.ShapeDtypeStruct((N_DEV, d, e), BF16),
        compiler_params=_params("parallel", "parallel"),
    )(h, dproj)


def _adamw_math(w, g, m, v):
    m = ADAM_B1 * m + (1.0 - ADAM_B1) * g
    v = ADAM_B2 * v + (1.0 - ADAM_B2) * (g * g)
    m_hat = m / (1.0 - ADAM_B1 ** ADAM_STEP)
    v_hat = v / (1.0 - ADAM_B2 ** ADAM_STEP)
    return -ADAM_LR * (m_hat / (jnp.sqrt(v_hat) + ADAM_EPS) + ADAM_WD * w), m, v


def _adamw_layer(parts, own, me, w, m, v, prev, layer, name):
    n_layers, rows, cols = w.shape
    tr = _tile(rows, max(8, (256 * 1024) // cols))

    def body(me_ref, p_ref, own_ref, w_ref, m_ref, v_ref, *rest):
        del me_ref
        g_ref, d_ref, nm_ref, nv_ref = rest[-4:]
        g = own_ref[...].astype(F32)
        for j in range(4):
            g = g + p_ref[j].astype(F32)
        g_ref[...] = g
        d_ref[...], nm_ref[...], nv_ref[...] = _adamw_math(w_ref[...], g, m_ref[...], v_ref[...])

    blk = pl.BlockSpec((None, tr, cols), lambda i, me_ref: (layer, i, 0))
    out_shape = tuple(jax.ShapeDtypeStruct(w.shape, F32) for _ in range(4))
    in_specs = [pl.BlockSpec((4, tr, cols), lambda i, me_ref: (0, i, 0)),
                pl.BlockSpec((None, tr, cols), lambda i, me_ref: (me_ref[0], i, 0)), blk, blk, blk]
    args = [me, parts, own, w, m, v]
    aliases = {}
    if prev is not None:
        in_specs += [_any()] * 4
        args += list(prev)
        aliases = {6 + k: k for k in range(4)}
    return pl.pallas_call(
        body, name=f"{name}_l{layer}",
        grid_spec=pltpu.PrefetchScalarGridSpec(
            num_scalar_prefetch=1, grid=(rows // tr,), in_specs=in_specs, out_specs=(blk, blk, blk, blk)),
        out_shape=out_shape, input_output_aliases=aliases,
        compiler_params=_params("parallel"),
    )(*args)


def _sum_slots(parts):
    def body(p_ref, o_ref):
        g = p_ref[0]
        for j in range(1, N_DEV):
            g = g + p_ref[j]
        o_ref[...] = g

    return pl.pallas_call(
        body, name="sum_small_grads",
        in_specs=[_vmem()], out_specs=_vmem(),
        out_shape=jax.ShapeDtypeStruct(parts.shape[1:], F32),
        compiler_params=_params(),
    )(parts)


def _adamw_small(w, g, m, v):
    def body(w_ref, g_ref, m_ref, v_ref, d_ref, nm_ref, nv_ref):
        d_ref[...], nm_ref[...], nv_ref[...] = _adamw_math(w_ref[...], g_ref[...], m_ref[...], v_ref[...])

    return pl.pallas_call(
        body, name="adamw_small",
        in_specs=[_vmem()] * 4, out_specs=(_vmem(),) * 3,
        out_shape=tuple(jax.ShapeDtypeStruct(w.shape, F32) for _ in range(3)),
        compiler_params=_params(),
    )(w, g, m, v)


def _pack_rows(arrays):
    rows = []
    for a in arrays:
        flat = a.reshape(-1)
        pad = (-flat.shape[0]) % (8 * LANES)
        rows.append(jnp.pad(flat, (0, pad)).reshape(-1, LANES))
    return jnp.concatenate(rows, axis=0)


def _unpack_rows(packed, like):
    out, r0 = [], 0
    for a in like:
        n = a.size
        nr = -(-n // (8 * LANES)) * 8
        out.append(packed[r0:r0 + nr].reshape(-1)[:n].reshape(a.shape))
        r0 += nr
    return out


def kernel(x, norm_g, w_in, q_norm_g, k_norm_g, rel_bias, w_out, loss_target, m_norm_g, m_w_in, m_q_norm_g, m_k_norm_g, m_rel_bias, m_w_out, v_norm_g, v_w_in, v_q_norm_g, v_k_norm_g, v_rel_bias, v_w_out):
    depth, d, e = w_in.shape
    r_out = w_out.shape[1]
    heads = e // HEAD_DIM
    rel_w = rel_bias.shape[2]
    x0 = x[0]
    target = loss_target[0]
    s = x0.shape[0]

    me = jnp.reshape(_flat(_my_place()), (1,)).astype(jnp.int32)

    casts = [(_cast_layer(w_in, me, l, "cast_w_in"), _cast_layer(w_out, me, l, "cast_w_out"))
             for l in range(depth)]

    def begin_gather(l, after):
        (win_b, win_land), (wout_b, wout_land) = casts[l]
        return _gather_send((win_b, wout_b), (win_land, wout_land), after, f"gather_send_l{l}")

    rel_all = _gather_small(rel_bias, [], "gather_rel_bias")
    sent = begin_gather(0, [rel_all])
    rel_full = jnp.transpose(rel_all, (1, 2, 0, 3)).reshape(depth * heads, N_DEV * rel_w)
    bias = jnp.transpose(_bias_expand(rel_full), (1, 0, 2))
    head_work = [bias] + [shard for cast in casts[1:] for shard, _ in cast]
    forwarded = _gather_forward(sent, head_work, "gather_forward_l0")

    xs, hs, projs, ys, mixes, tots, weights = [], [], [], [], [], [], []
    xl = x0
    for l in range(depth):
        win_all, wout_all = _gather_finish(forwarded, [xl, forwarded[-1]], f"gather_finish_l{l}")
        more = l + 1 < depth
        if more:
            sent = begin_gather(l + 1, [win_all])
        wout_full = wout_all.reshape(d, d)
        proj, h = _norm_proj(xl, norm_g[l:l + 1], win_all, sent[-1] if more else None, l)
        y, mixed, tot = _sb_forward(proj, l)
        if more:
            forwarded = _gather_forward(sent, [tot], f"gather_forward_l{l + 1}")
        y, mixed = _chunk_forward(proj, bias, q_norm_g[l:l + 1], k_norm_g[l:l + 1], y, mixed,
                                  forwarded[-1] if more else None, l)
        xs.append(xl), hs.append(h), projs.append(proj), ys.append(y), mixes.append(mixed), tots.append(tot)
        weights.append((win_all, wout_full))
        xl = _out_proj(mixed, wout_full, xl, l)

    dx, loss_parts = _loss_head(xl, target)
    loss = lax.psum(jnp.sum(loss_parts), AXES)

    dbias, dng, dqg, dkg = [None] * depth, [None] * depth, [None] * depth, [None] * depth
    res_in, res_out = None, None

    peer_slots = jnp.stack([_flat(_flip(_my_place(), k)) for k in SAME_CORE]).astype(jnp.int32)

    def finish_exchange(exchanging, after, l):
        sems, csums_lands, own = exchanging
        rin, rout = _wait_copies(f"exchange_finish_l{l}", csums_lands, sems, 2, lambda refs, t: refs[t].at[0], 3,
                                 after)[2:]
        return (_adamw_layer(rin, own[0], me, w_in, m_w_in, v_w_in, res_in, l, "adamw_w_in"),
                _adamw_layer(rout, own[1], me, w_out, m_w_out, v_w_out, res_out, l, "adamw_w_out"))

    pending, tie = [], None
    for l in reversed(range(depth)):
        win_all, wout_full = weights[l]
        dmixed = _out_proj_bwd_input(dx, wout_full, tie, l)
        gwout = _out_proj_bwd_weight(mixes[l], dx, l).reshape(N_DEV, r_out, d)
        dproj = _sb_backward(projs[l], ys[l], dmixed, tots[l], l)
        dproj, dbias[l], dqg[l], dkg[l] = _chunk_backward(
            projs[l], bias, q_norm_g[l:l + 1], k_norm_g[l:l + 1], ys[l], dmixed, dproj, l)
        grads_l = (_proj_bwd_weight(hs[l], dproj, l), gwout)
        sems, bufs, token = _exchange_to_sibling(grads_l, [], f"exchange_sibling_l{l}")
        if l > 0:
            dx, dng[l] = _proj_bwd_input(dproj, win_all, xs[l], norm_g[l:l + 1], dx, token, l)
            token = dx
        bufs = _wait_copies(f"exchange_sibling_wait_l{l}", bufs, sems, 2, lambda refs, t: refs[t].at[0], 4, [token])
        own, parts, lands = bufs[0:2], bufs[2:4], bufs[4:6]
        csums = [_chip_sums(own[t], parts[t], peer_slots, f"chip_sums_{t}_l{l}") for t in range(2)]
        sems, csums_lands, tie = _exchange_to_chips(csums, lands, [], f"exchange_chips_l{l}")
        pending.append(((sems, csums_lands, own), l))
        if l == 0:
            dx, dng[l] = _proj_bwd_input(dproj, win_all, xs[l], norm_g[l:l + 1], dx, tie, l)
    for exchanging, l in pending[:-1]:
        res_in, res_out = finish_exchange(exchanging, [dx, tie], l)
    drel = _bias_grad(jnp.transpose(jnp.concatenate(dbias, axis=0), (1, 0, 2)), [tie])
    small_like = [norm_g, q_norm_g, k_norm_g, drel]
    mine = _pack_rows([jnp.concatenate(dng, axis=0), jnp.concatenate(dqg, axis=0),
                       jnp.concatenate(dkg, axis=0), drel])
    gathered = _gather_small(mine, [res_in[0], res_out[0]], "gather_small_grads")
    g_norm, g_qn, g_kn, g_rel_full = _unpack_rows(_sum_slots(gathered), small_like)
    my_block = _flat(_my_place())
    g_rel = lax.dynamic_slice_in_dim(g_rel_full.reshape(depth, heads, N_REL), my_block * rel_w, rel_w, axis=2)
    small_w = [norm_g, q_norm_g, k_norm_g, rel_bias]
    small = _adamw_small(_pack_rows(small_w), _pack_rows([g_norm, g_qn, g_kn, g_rel]),
                         _pack_rows([m_norm_g, m_q_norm_g, m_k_norm_g, m_rel_bias]),
                         _pack_rows([v_norm_g, v_q_norm_g, v_k_norm_g, v_rel_bias]))
    d_small, nm_small, nv_small = (_unpack_rows(p, small_w) for p in small)

    res_in, res_out = finish_exchange(pending[-1][0], [small[0], res_in[0], res_out[0]], 0)
    g_win, d_win, nm_win, nv_win = res_in
    g_wout, d_wout, nm_wout, nv_wout = res_out
    grads = (g_norm, g_win, g_qn, g_kn, g_rel, g_wout)

    def order(sm, big_in, big_out):
        return (sm[0], big_in, sm[1], sm[2], sm[3], big_out)

    return (loss, dx[None], *grads, *order(d_small, d_win, d_wout),
            *order(nm_small, nm_win, nm_wout), *order(nv_small, nv_win, nv_wout))
```

```python
import functools

import jax
import jax.numpy as jnp
from jax import lax
from jax.experimental import pallas as pl
from jax.experimental.pallas import tpu as pltpu

F32 = jnp.float32
BF16 = jnp.bfloat16
MESH_ID = pl.DeviceIdType.MESH
AXES = ("x", "y", "c")

N_DEV = 8
HEAD_DIM = 128
CHUNK = 64
LEFT_CHUNKS = 8
BAND_W = (LEFT_CHUNKS + 1) * CHUNK
PAD_K = LEFT_CHUNKS * CHUNK
REL_CLIP = 256
N_REL = REL_CLIP + CHUNK
NORM_EPS = 1e-6
NEG_BIG = -1e30
PAIR = 2 * CHUNK
PAIR_W = BAND_W + CHUNK
CHUNK_HEADS = 2
PAIR_UNROLL = 2
SB_Q = 512
SB_K = 128
SB_HEADS = 2
SB_UNROLL = 4
LANES = 128

ADAM_LR = 0.001
ADAM_B1 = 0.9
ADAM_B2 = 0.999
ADAM_EPS = 1e-08
ADAM_WD = 0.01
ADAM_STEP = 10

VMEM_LIMIT_BYTES = 56 * 1024 * 1024

NT = (((1,), (1,)), ((), ()))
TN = (((0,), (0,)), ((), ()))


def _params(*sem, **kw):
    return pltpu.CompilerParams(dimension_semantics=sem or None, vmem_limit_bytes=VMEM_LIMIT_BYTES, **kw)


def _any():
    return pl.BlockSpec(memory_space=pl.ANY)


def _vmem():
    return pl.BlockSpec(memory_space=pltpu.VMEM)


def _tile(n, want):
    return want if n % want == 0 else n


def _dot(a, b, dims=None):
    if dims is None:
        return jnp.dot(a, b, preferred_element_type=F32)
    return lax.dot_general(a, b, dims, preferred_element_type=F32)


def _split_dot(a, b, parts, dims=None):
    acc = None
    rest = a
    for _ in range(parts):
        piece = rest.astype(BF16)
        rest = rest - piece.astype(F32)
        term = _dot(piece, b, dims)
        acc = term if acc is None else acc + term
    return acc


def _log_sigmoid(z):
    return jnp.minimum(z, 0.0) - jnp.log(1.0 + jnp.exp(-jnp.abs(z)))


def _silu_and_grad(g):
    sig = jax.nn.sigmoid(g)
    return g * sig, sig * (1.0 + g * (1.0 - sig))


def _my_place():
    return lax.axis_index("x"), lax.axis_index("y"), lax.axis_index("c")


def _flat(place):
    return 4 * place[0] + 2 * place[1] + place[2]


def _flip(place, k):
    return tuple(1 - p if (k >> s) & 1 else p for p, s in zip(place, (2, 1, 0)))


def _cast_layer(w, me, layer, name):
    _, rows, cols = w.shape
    tr = _tile(rows, 1024)

    def body(me_ref, a_ref, shard_ref, land_ref):
        del me_ref
        shard_ref[...] = a_ref[...].astype(BF16)
        land_ref[...] = shard_ref[...]

    return pl.pallas_call(
        body, name=f"{name}_l{layer}",
        grid_spec=pltpu.PrefetchScalarGridSpec(
            num_scalar_prefetch=1, grid=(rows // tr,),
            in_specs=[pl.BlockSpec((None, tr, cols), lambda i, me_ref: (layer, i, 0))],
            out_specs=(pl.BlockSpec((tr, cols), lambda i, me_ref: (i, 0)),
                       pl.BlockSpec((None, tr, cols), lambda i, me_ref: (me_ref[0], i, 0)))),
        out_shape=(jax.ShapeDtypeStruct((rows, cols), BF16), jax.ShapeDtypeStruct((N_DEV, rows, cols), BF16)),
        compiler_params=_params("parallel"),
    )(me, w)


HBM_SPEC = pl.BlockSpec(memory_space=pltpu.HBM)
SEM_SPEC = pl.BlockSpec(memory_space=pltpu.SEMAPHORE)
SAME_CORE = (2, 4, 6)
SIBLING = 1
SPLIT_EFFECT = pltpu.SideEffectType.DATAFLOW_SIDE_EFFECTING


def _split_call(body, name, bufs, sems_in, sem_counts_out, after, token):
    bufs, sems_in, after = list(bufs), list(sems_in), list(after)
    nb, ni, no = len(bufs), len(sems_in), len(sem_counts_out)

    def wrapped(*refs):
        outs = nb + ni + len(after)
        body(refs[:nb], refs[nb:nb + ni], refs[outs:outs + no])
        if token:
            refs[-1][...] = jnp.zeros_like(refs[-1])

    out = pl.pallas_call(
        wrapped, name=name,
        in_specs=[HBM_SPEC] * nb + [SEM_SPEC] * ni + [_any()] * len(after),
        out_specs=tuple([SEM_SPEC] * no + [HBM_SPEC] * nb + ([_vmem()] if token else [])),
        out_shape=tuple([pltpu.SemaphoreType.DMA((c,)) for c in sem_counts_out]
                        + [pltpu.HBM(a.shape, a.dtype) for a in bufs]
                        + ([jax.ShapeDtypeStruct((8, LANES), F32)] if token else [])),
        input_output_aliases={i: no + i for i in range(nb)},
        compiler_params=pltpu.CompilerParams(has_side_effects=SPLIT_EFFECT),
    )(*[pltpu.with_memory_space_constraint(a, pltpu.HBM) for a in bufs], *sems_in, *after)
    return list(out[:no]), list(out[no:no + nb]), (out[-1] if token else None)


def _remote(src, dst, send_sems, recv_sems, i, to):
    return pltpu.make_async_remote_copy(src_ref=src, dst_ref=dst, send_sem=send_sems.at[i], recv_sem=recv_sems.at[i],
                                        device_id=to, device_id_type=MESH_ID)


def _wait_copies(name, bufs, sems, n, slot_of, count, after):
    def body(refs, sems_in, _):
        me = _my_place()
        for t in range(n):
            slot = slot_of(refs, t)
            for a in range(count):
                cp = _remote(slot, slot, sems_in[0], sems_in[1], t * count + a, me)
                cp.wait_send()
                cp.wait_recv()

    return _split_call(body, name, bufs, sems, (), after, False)[1]


def _exchange_to_sibling(grads, after, name):
    n = len(grads)
    parts = [lax.empty((3,) + g.shape[1:], g.dtype) for g in grads]
    lands = [lax.empty((4,) + g.shape[1:], g.dtype) for g in grads]

    def body(refs, _, sems_out):
        me = _my_place()
        sib = _flip(me, SIBLING)
        for t in range(n):
            g, part, land = refs[t], refs[n + t], refs[2 * n + t]
            _remote(g.at[_flat(sib)], land.at[0], *sems_out, 4 * t, sib).start()
            for a, k in enumerate(SAME_CORE):
                _remote(g.at[_flat(_flip(sib, k))], part.at[a], *sems_out, 4 * t + 1 + a, sib).start()

    return _split_call(body, name, list(grads) + parts + lands, (), (4 * n, 4 * n), after, True)


def _chip_sums(grads, parts, slots, name):
    _, rows, cols = grads.shape
    tr = _tile(rows, max(8, (512 * 1024) // cols))

    def body(slots_ref, g_ref, p_ref, o_ref):
        del slots_ref
        o_ref[...] = (g_ref[...].astype(F32) + p_ref[...].astype(F32)).astype(BF16)

    return pl.pallas_call(
        body, name=name,
        grid_spec=pltpu.PrefetchScalarGridSpec(
            num_scalar_prefetch=1, grid=(3, rows // tr),
            in_specs=[pl.BlockSpec((None, tr, cols), lambda a, i, slots_ref: (slots_ref[a], i, 0)),
                      pl.BlockSpec((None, tr, cols), lambda a, i, slots_ref: (a, i, 0))],
            out_specs=pl.BlockSpec((None, tr, cols), lambda a, i, slots_ref: (a, i, 0))),
        out_shape=jax.ShapeDtypeStruct((3, rows, cols), BF16),
        compiler_params=_params("parallel", "parallel"),
    )(slots, grads, parts)


def _exchange_to_chips(csums, lands, after, name):
    n = len(csums)

    def body(refs, _, sems_out):
        me = _my_place()
        for t in range(n):
            for a, k in enumerate(SAME_CORE):
                _remote(refs[t].at[a], refs[n + t].at[1 + a], *sems_out, 3 * t + a, _flip(me, k)).start()

    return _split_call(body, name, list(csums) + list(lands), (), (3 * n, 3 * n), after, True)


def _hbm_call(body, name, n_hbm, sems_in, sems_out, after, token, like):
    after = list(after)
    in_specs = [HBM_SPEC] * n_hbm + [SEM_SPEC] * len(sems_in) + [_any()] * len(after)
    out_specs = [SEM_SPEC] * len(sems_out) + [HBM_SPEC] * n_hbm + ([_vmem()] if token else [])
    out_shape = ([pltpu.SemaphoreType.DMA((c,)) for c in sems_out] + [pltpu.HBM(a.shape, a.dtype) for a in like]
                 + ([jax.ShapeDtypeStruct((8, LANES), F32)] if token else []))
    return in_specs, tuple(out_specs), tuple(out_shape), {i: len(sems_out) + i for i in range(n_hbm)}, after


def _gather_send(shards, lands, after, name):
    n = len(shards)
    peers = (SIBLING,) + SAME_CORE
    after = list(after)

    def body(*refs):
        me = _my_place()
        send_sems, recv_sems = refs[2 * n + len(after)], refs[2 * n + len(after) + 1]
        for a, k in enumerate(peers):
            for t in range(n):
                pltpu.make_async_remote_copy(
                    src_ref=refs[t], dst_ref=refs[n + t].at[_flat(me)],
                    send_sem=send_sems.at[t * 4 + a], recv_sem=recv_sems.at[t * 4 + a],
                    device_id=_flip(me, k), device_id_type=MESH_ID).start()
        refs[-1][...] = jnp.zeros_like(refs[-1])

    bufs = list(shards) + list(lands)
    in_specs, out_specs, out_shape, aliases, after = _hbm_call(body, name, 2 * n, (), (4 * n, 4 * n), after, True, bufs)
    out = pl.pallas_call(
        body, name=name, in_specs=in_specs, out_specs=out_specs, out_shape=out_shape,
        input_output_aliases=aliases, compiler_params=pltpu.CompilerParams(has_side_effects=SPLIT_EFFECT),
    )(*[pltpu.with_memory_space_constraint(a, pltpu.HBM) for a in bufs], *after)
    return out[0], out[1], out[2:2 + n], out[2 + n:2 + 2 * n], out[-1]


def _gather_forward(sent, after, name):
    send1, recv1, shards, lands, _ = sent
    n = len(shards)
    after = list(after)

    def body(*refs):
        me = _my_place()
        recv1_ref = refs[2 * n + 1]
        out0 = 2 * n + 2 + len(after)
        send2_ref, recv2_ref = refs[out0], refs[out0 + 1]
        for a, k in enumerate(SAME_CORE):
            owner = _flat(_flip(me, k))
            for t in range(n):
                slot = refs[n + t].at[owner]
                pltpu.make_async_remote_copy(
                    src_ref=refs[t], dst_ref=slot, send_sem=refs[2 * n].at[t * 4 + 1 + a],
                    recv_sem=recv1_ref.at[t * 4 + 1 + a], device_id=_flip(me, k), device_id_type=MESH_ID).wait_recv()
                pltpu.make_async_remote_copy(
                    src_ref=slot, dst_ref=slot, send_sem=send2_ref.at[t * 3 + a], recv_sem=recv2_ref.at[t * 3 + a],
                    device_id=_flip(me, SIBLING), device_id_type=MESH_ID).start()
        refs[-1][...] = jnp.zeros_like(refs[-1])

    bufs = list(shards) + list(lands)
    in_specs, out_specs, out_shape, aliases, after = _hbm_call(body, name, 2 * n, (4 * n, 4 * n), (3 * n, 3 * n), after,
                                                               True, bufs)
    out = pl.pallas_call(
        body, name=name, in_specs=in_specs, out_specs=out_specs, out_shape=out_shape,
        input_output_aliases=aliases, compiler_params=pltpu.CompilerParams(has_side_effects=SPLIT_EFFECT),
    )(*bufs, send1, recv1, *after)
    return (send1, recv1), (out[0], out[1]), out[2:2 + n], out[2 + n:2 + 2 * n], out[-1]


def _gather_finish(forwarded, after, name):
    (send1, recv1), (send2, recv2), shards, lands, _ = forwarded
    n = len(shards)
    after = list(after)

    def body(*refs):
        me = _my_place()
        send1_ref, recv1_ref, send2_ref, recv2_ref = refs[2 * n:2 * n + 4]
        sib = _flip(me, SIBLING)
        for t in range(n):
            for a in range(4):
                cp = pltpu.make_async_remote_copy(
                    src_ref=refs[t], dst_ref=refs[n + t].at[_flat(sib)], send_sem=send1_ref.at[t * 4 + a],
                    recv_sem=recv1_ref.at[t * 4 + a], device_id=sib, device_id_type=MESH_ID)
                cp.wait_send()
                if a == 0:
                    cp.wait_recv()
            for a in range(3):
                cp = pltpu.make_async_remote_copy(
                    src_ref=refs[t], dst_ref=refs[n + t].at[_flat(sib)], send_sem=send2_ref.at[t * 3 + a],
                    recv_sem=recv2_ref.at[t * 3 + a], device_id=sib, device_id_type=MESH_ID)
                cp.wait_send()
                cp.wait_recv()

    bufs = list(shards) + list(lands)
    in_specs, out_specs, out_shape, aliases, after = _hbm_call(body, name, 2 * n, (4 * n, 4 * n, 3 * n, 3 * n), (), after,
                                                               False, bufs)
    out = pl.pallas_call(
        body, name=name, in_specs=in_specs, out_specs=out_specs, out_shape=out_shape,
        input_output_aliases=aliases, compiler_params=pltpu.CompilerParams(has_side_effects=SPLIT_EFFECT),
    )(*bufs, send1, recv1, send2, recv2, *after)
    return out[n:]


def _gather_small(v, after, name):
    after = list(after)

    def body(v_ref, *rest):
        o_ref, send_sems, recv_sems = rest[-3:]
        me = _my_place()
        o_ref[_flat(me)] = v_ref[...]
        copies = []
        for k in range(1, N_DEV):
            copies.append(pltpu.make_async_remote_copy(
                src_ref=v_ref, dst_ref=o_ref.at[_flat(me)],
                send_sem=send_sems.at[k - 1], recv_sem=recv_sems.at[k - 1],
                device_id=_flip(me, k), device_id_type=MESH_ID))
        for cp in copies:
            cp.start()
        for cp in copies:
            cp.wait()

    return pl.pallas_call(
        body, name=name,
        in_specs=[_vmem()] + [_any()] * len(after), out_specs=_vmem(),
        out_shape=jax.ShapeDtypeStruct((N_DEV,) + v.shape, v.dtype),
        scratch_shapes=[pltpu.SemaphoreType.DMA((7,)), pltpu.SemaphoreType.DMA((7,))],
        compiler_params=_params(has_side_effects=True),
    )(v, *after)


def _rel_onehot(row):
    r_io = lax.broadcasted_iota(jnp.int32, (N_REL, PAIR_W), 0)
    p_io = lax.broadcasted_iota(jnp.int32, (N_REL, PAIR_W), 1)
    band_col = p_io - (row // CHUNK) * CHUNK
    in_band = jnp.logical_and(band_col >= 0, band_col < BAND_W)
    idx = jnp.clip(PAD_K + row % CHUNK - band_col, -(CHUNK - 1), REL_CLIP) + (CHUNK - 1)
    return jnp.logical_and(r_io == idx, in_band).astype(BF16), in_band[0:1]


def _bias_expand(rel):
    lh = rel.shape[0]

    def body(rel_ref, o_ref):
        onehot, in_band = _rel_onehot(pl.program_id(0))
        o_ref[...] = jnp.where(in_band, _split_dot(rel_ref[...], onehot, 3), NEG_BIG)

    return pl.pallas_call(
        body, name="bias_expand", grid=(PAIR,),
        in_specs=[pl.BlockSpec((lh, N_REL), lambda i: (0, 0))],
        out_specs=pl.BlockSpec((None, lh, PAIR_W), lambda i: (i, 0, 0)),
        out_shape=jax.ShapeDtypeStruct((PAIR, lh, PAIR_W), F32),
        compiler_params=_params("parallel"),
    )(rel)


def _bias_grad(dbias, after):
    lh = dbias.shape[1]
    after = list(after)

    def body(db_ref, *rest):
        o_ref = rest[-1]
        i = pl.program_id(0)

        @pl.when(i == 0)
        def _():
            o_ref[...] = jnp.zeros_like(o_ref)

        o_ref[...] += _split_dot(db_ref[...], _rel_onehot(i)[0], 2, NT)

    return pl.pallas_call(
        body, name="bias_grad", grid=(PAIR,),
        in_specs=[pl.BlockSpec((None, lh, PAIR_W), lambda i: (i, 0, 0))] + [_any()] * len(after),
        out_specs=pl.BlockSpec((lh, N_REL), lambda i: (0, 0)),
        out_shape=jax.ShapeDtypeStruct((lh, N_REL), F32),
        compiler_params=_params("arbitrary"),
    )(dbias, *after)


def _norm_proj(x, g, w_all, tie, layer):
    s, d = x.shape
    e = w_all.shape[2]
    tm, tn = _tile(s, 1024), _tile(e, 1024)
    nb = e // tn
    ties = [] if tie is None else [tie]

    def body(x_ref, g_ref, w_ref, *rest):
        proj_ref, h_ref = rest[-2:]

        @pl.when(pl.program_id(1) == 0)
        def _():
            xv = x_ref[...]
            r = lax.rsqrt(jnp.mean(xv * xv, axis=-1, keepdims=True) + NORM_EPS)
            h_ref[...] = ((xv * r) * g_ref[...]).astype(BF16)

        proj_ref[...] = _dot(h_ref[...], w_ref[...])

    return pl.pallas_call(
        body, name=f"norm_proj_l{layer}", grid=(s // tm, N_DEV * nb),
        in_specs=[pl.BlockSpec((tm, d), lambda m, n: (m, 0)),
                  pl.BlockSpec((1, d), lambda m, n: (0, 0)),
                  pl.BlockSpec((None, d, tn), lambda m, n: (n // nb, 0, n % nb))] + [_any()] * len(ties),
        out_specs=(pl.BlockSpec((None, tm, tn), lambda m, n: (n // nb, m, n % nb)),
                   pl.BlockSpec((tm, d), lambda m, n: (m, 0))),
        out_shape=(jax.ShapeDtypeStruct((N_DEV, s, e), F32), jax.ShapeDtypeStruct((s, d), BF16)),
        compiler_params=_params("parallel", "arbitrary"),
    )(x, g, w_all, *ties)


def _sb_forward(proj, layer):
    _, s, e = proj.shape
    hp = _tile(e // HEAD_DIM, SB_HEADS)
    width = hp * HEAD_DIM
    tq = _tile(s, SB_Q)
    diag_tiles = tq // SB_K
    scale = HEAD_DIM ** -0.5

    def body(p_ref, y_ref, mix_ref, tot_ref, kb_ref, vb_ref):
        kb_ref[...] = p_ref[1].astype(BF16)
        vb_ref[...] = p_ref[2].astype(BF16)
        row = lax.broadcasted_iota(jnp.int32, (tq, SB_K), 0)
        col = lax.broadcasted_iota(jnp.int32, (tq, SB_K), 1)
        kj = lax.broadcasted_iota(jnp.int32, (SB_K, SB_K), 0)
        ks = lax.broadcasted_iota(jnp.int32, (SB_K, SB_K), 1)
        later = (kj > ks).astype(BF16)

        def q_block(qi, _):
            t0 = pl.multiple_of(qi * tq, tq)
            qb = [p_ref[0, pl.ds(t0, tq), h * HEAD_DIM:(h + 1) * HEAD_DIM].astype(BF16) for h in range(hp)]

            def tile(s0, state, causal):
                out = []
                for h in range(hp):
                    carry, acc = state[h]
                    lanes = slice(h * HEAD_DIM, (h + 1) * HEAD_DIM)
                    z = _dot(qb[h], kb_ref[pl.ds(s0, SB_K), lanes], NT) * scale
                    ls = _log_sigmoid(z)
                    stay = ls - z
                    if causal is not None:
                        stay = jnp.where(causal, stay, 0.0)
                    w = jnp.exp(ls + carry + _split_dot(stay, later, 2))
                    if causal is not None:
                        w = jnp.where(causal, w, 0.0)
                    acc = acc + _dot(w.astype(BF16), vb_ref[pl.ds(s0, SB_K), lanes])
                    out.append((carry + jnp.sum(stay, axis=1, keepdims=True), acc))
                return tuple(out)

            state = tuple((jnp.zeros((tq, 1), F32), jnp.zeros((tq, HEAD_DIM), F32)) for _ in range(hp))
            for dt in reversed(range(diag_tiles)):
                state = tile(t0 + dt * SB_K, state, col + dt * SB_K < row)

            def k_blocks(j, st):
                for u in range(SB_UNROLL):
                    st = tile(pl.multiple_of((diag_tiles * qi - 1 - SB_UNROLL * j - u) * SB_K, SB_K), st, None)
                return st

            state = lax.fori_loop(0, diag_tiles * qi // SB_UNROLL, k_blocks, state)
            silu, _ = _silu_and_grad(p_ref[3, pl.ds(t0, tq), :])
            for h in range(hp):
                lanes = slice(h * HEAD_DIM, (h + 1) * HEAD_DIM)
                y_ref[pl.ds(t0, tq), lanes] = state[h][1]
                mix_ref[pl.ds(t0, tq), lanes] = (state[h][1] * silu[:, lanes]).astype(BF16)
                tot_ref[h, pl.ds(t0, tq), :] = state[h][0]
            return 0

        lax.fori_loop(0, s // tq, q_block, 0)

    return pl.pallas_call(
        body, name=f"sb_forward_l{layer}", grid=(e // width,),
        in_specs=[pl.BlockSpec((4, s, width), lambda h: (0, 0, h))],
        out_specs=(pl.BlockSpec((s, width), lambda h: (0, h)),
                   pl.BlockSpec((s, width), lambda h: (0, h)),
                   pl.BlockSpec((hp, s, 1), lambda h: (h, 0, 0))),
        out_shape=(jax.ShapeDtypeStruct((s, 2 * e), F32), jax.ShapeDtypeStruct((s, 2 * e), BF16),
                   jax.ShapeDtypeStruct((e // HEAD_DIM, s, 1), F32)),
        scratch_shapes=[pltpu.VMEM((s, width), BF16), pltpu.VMEM((s, width), BF16)],
        compiler_params=_params("parallel"),
    )(proj)


def _qk_norm(t, gain):
    r = lax.rsqrt(jnp.mean(t * t, axis=-1, keepdims=True) + NORM_EPS)
    return t * r, r, (t * r) * gain


def _chunk_scores(qc, kw, bias, t0, scale):
    sc = _dot(qc, kw, NT) * scale + bias
    col = lax.broadcasted_iota(jnp.int32, (PAIR, PAIR_W), 1)
    sc = jnp.where(col + t0 >= PAD_K, sc, NEG_BIG)
    ex = jnp.exp(sc - jnp.max(sc, axis=-1, keepdims=True))
    return ex / jnp.sum(ex, axis=-1, keepdims=True)


def _chunk_forward(proj, bias, q_gain, k_gain, y, mixed, tie, layer):
    ties = [] if tie is None else [tie]
    _, s, e = proj.shape
    hp = _tile(e // HEAD_DIM, CHUNK_HEADS)
    width = hp * HEAD_DIM
    steps = e // width
    unroll = _tile(s // PAIR, PAIR_UNROLL)
    scale = HEAD_DIM ** -0.5
    heads = [slice(h * HEAD_DIM, (h + 1) * HEAD_DIM) for h in range(hp)]

    def body(p_ref, b_ref, qg_ref, kg_ref, *rest):
        y_ref, mix_ref, qn_ref, kp_ref, vp_ref = rest[-5:]
        kp_ref[pl.ds(0, PAD_K), :] = jnp.zeros((PAD_K, width), BF16)
        vp_ref[pl.ds(0, PAD_K), :] = jnp.zeros((PAD_K, width), BF16)
        vp_ref[pl.ds(PAD_K, s), :] = p_ref[2].astype(BF16)
        for lanes in heads:
            qn_ref[:, lanes] = _qk_norm(p_ref[0, :, lanes], qg_ref[...])[2].astype(BF16)
            kp_ref[pl.ds(PAD_K, s), lanes] = _qk_norm(p_ref[1, :, lanes], kg_ref[...])[2].astype(BF16)

        def chunks(ci, _):
            done = []
            for u in range(unroll):
                t0 = pl.multiple_of((ci * unroll + u) * PAIR, PAIR)
                silu, _ = _silu_and_grad(p_ref[3, pl.ds(t0, PAIR), :])
                for h, lanes in enumerate(heads):
                    probs = _chunk_scores(qn_ref[pl.ds(t0, PAIR), lanes], kp_ref[pl.ds(t0, PAIR_W), lanes],
                                          b_ref[h], t0, scale)
                    out = _dot(probs.astype(BF16), vp_ref[pl.ds(t0, PAIR_W), lanes])
                    done.append((t0, lanes, out, (out * silu[:, lanes]).astype(BF16)))
            for t0, lanes, out, gated in done:
                y_ref[pl.ds(t0, PAIR), lanes] = out
                mix_ref[pl.ds(t0, PAIR), lanes] = gated
            return 0

        lax.fori_loop(0, s // (PAIR * unroll), chunks, 0)

    return pl.pallas_call(
        body, name=f"chunk_forward_l{layer}", grid=(steps,),
        in_specs=[pl.BlockSpec((4, s, width), lambda h: (1, 0, h)),
                  pl.BlockSpec((hp, PAIR, PAIR_W), lambda h: (layer * steps + h, 0, 0)),
                  pl.BlockSpec((1, HEAD_DIM), lambda h: (0, 0)),
                  pl.BlockSpec((1, HEAD_DIM), lambda h: (0, 0)),
                  _any(), _any()] + [_any()] * len(ties),
        out_specs=(pl.BlockSpec((s, width), lambda h: (0, steps + h)),
                   pl.BlockSpec((s, width), lambda h: (0, steps + h))),
        out_shape=(jax.ShapeDtypeStruct(y.shape, F32), jax.ShapeDtypeStruct(mixed.shape, BF16)),
        input_output_aliases={4: 0, 5: 1},
        scratch_shapes=[pltpu.VMEM((s, width), BF16), pltpu.VMEM((s + PAD_K, width), BF16),
                        pltpu.VMEM((s + PAD_K, width), BF16)],
        compiler_params=_params("parallel"),
    )(proj, bias, q_gain, k_gain, y, mixed, *ties)


def _out_proj(mixed, w, x, layer):
    s, d = x.shape
    tm, tn = _tile(s, 512), _tile(d, 1024)

    def body(a_ref, w_ref, x_ref, o_ref):
        o_ref[...] = x_ref[...] + _dot(a_ref[...], w_ref[...])

    return pl.pallas_call(
        body, name=f"out_proj_l{layer}", grid=(s // tm, d // tn),
        in_specs=[pl.BlockSpec((tm, d), lambda m, n: (m, 0)),
                  pl.BlockSpec((d, tn), lambda m, n: (0, n)),
                  pl.BlockSpec((tm, tn), lambda m, n: (m, n))],
        out_specs=pl.BlockSpec((tm, tn), lambda m, n: (m, n)),
        out_shape=jax.ShapeDtypeStruct((s, d), F32),
        compiler_params=_params("parallel", "parallel"),
    )(mixed, w, x)


def _loss_head(y, target):
    s, d = y.shape
    tm = _tile(s, 256)

    def body(y_ref, t_ref, dy_ref, part_ref):
        diff = y_ref[...] - t_ref[...]
        dy_ref[...] = diff * (1.0 / d)
        sq = (diff * diff).reshape(tm // 8, 8, d).sum(axis=0)
        acc = sq[:, 0:LANES]
        for j in range(1, d // LANES):
            acc = acc + sq[:, j * LANES:(j + 1) * LANES]
        part_ref[...] = acc * (0.5 / d)

    return pl.pallas_call(
        body, name="loss_head", grid=(s // tm,),
        in_specs=[pl.BlockSpec((tm, d), lambda i: (i, 0)), pl.BlockSpec((tm, d), lambda i: (i, 0))],
        out_specs=(pl.BlockSpec((tm, d), lambda i: (i, 0)), pl.BlockSpec((None, 8, LANES), lambda i: (i, 0, 0))),
        out_shape=(jax.ShapeDtypeStruct((s, d), F32), jax.ShapeDtypeStruct((s // tm, 8, LANES), F32)),
        compiler_params=_params("parallel"),
    )(y, target)


def _out_proj_bwd_input(dx, w, tie, layer):
    s, d = dx.shape
    tm, tn = _tile(s, 512), _tile(d, 1024)
    ties = [] if tie is None else [tie]

    def body(dx_ref, w_ref, *rest):
        rest[-1][...] = _dot(dx_ref[...].astype(BF16), w_ref[...], NT)

    return pl.pallas_call(
        body, name=f"out_proj_dx_l{layer}", grid=(s // tm, d // tn),
        in_specs=[pl.BlockSpec((tm, d), lambda m, n: (m, 0)),
                  pl.BlockSpec((tn, d), lambda m, n: (n, 0))] + [_any()] * len(ties),
        out_specs=pl.BlockSpec((tm, tn), lambda m, n: (m, n)),
        out_shape=jax.ShapeDtypeStruct((s, d), F32),
        compiler_params=_params("parallel", "parallel"),
    )(dx, w, *ties)


def _out_proj_bwd_weight(mixed, dx, layer):
    s, d = dx.shape
    te, tn = _tile(d, 512), _tile(d, 1024)

    def body(a_ref, dx_ref, o_ref):
        o_ref[...] = _dot(a_ref[...], dx_ref[...].astype(BF16), TN).astype(BF16)

    return pl.pallas_call(
        body, name=f"out_proj_dw_l{layer}", grid=(d // te, d // tn),
        in_specs=[pl.BlockSpec((s, te), lambda i, n: (0, i)), pl.BlockSpec((s, tn), lambda i, n: (0, n))],
        out_specs=pl.BlockSpec((te, tn), lambda i, n: (i, n)),
        out_shape=jax.ShapeDtypeStruct((d, d), BF16),
        compiler_params=_params("parallel", "parallel"),
    )(mixed, dx)


def _sb_backward(proj, y, dmixed, tot, layer):
    _, s, e = proj.shape
    hp = _tile(e // HEAD_DIM, SB_HEADS)
    width = hp * HEAD_DIM
    tq = _tile(s, SB_Q)
    diag_tiles = tq // SB_K
    scale = HEAD_DIM ** -0.5

    def body(p_ref, y_ref, dm_ref, tot_ref, o_ref, kb_ref, vb_ref, do_ref, dk_ref, dv_ref):
        kb_ref[...] = p_ref[1].astype(BF16)
        vb_ref[...] = p_ref[2].astype(BF16)
        silu, dsilu = _silu_and_grad(p_ref[3])
        dm = dm_ref[...]
        do_ref[...] = (dm * silu).astype(BF16)
        o_ref[3] = (dm * y_ref[...] * dsilu).astype(BF16)
        dk_ref[...] = jnp.zeros_like(dk_ref)
        dv_ref[...] = jnp.zeros_like(dv_ref)
        row = lax.broadcasted_iota(jnp.int32, (tq, SB_K), 0)
        col = lax.broadcasted_iota(jnp.int32, (tq, SB_K), 1)
        kj = lax.broadcasted_iota(jnp.int32, (SB_K, SB_K), 0)
        ks = lax.broadcasted_iota(jnp.int32, (SB_K, SB_K), 1)
        upto = (kj <= ks).astype(BF16)
        before = (kj < ks).astype(BF16)

        def q_block(qi, _):
            t0 = pl.multiple_of(qi * tq, tq)
            heads = [slice(h * HEAD_DIM, (h + 1) * HEAD_DIM) for h in range(hp)]
            qb = [p_ref[0, pl.ds(t0, tq), lanes].astype(BF16) for lanes in heads]
            dob = [do_ref[pl.ds(t0, tq), lanes] for lanes in heads]
            total = [tot_ref[h, pl.ds(t0, tq), :] for h in range(hp)]

            def tile(s0, state, causal):
                out, adds = [], []
                for h, lanes in enumerate(heads):
                    stay_sum, dlw_sum, dq = state[h]
                    kt = kb_ref[pl.ds(s0, SB_K), lanes]
                    vt = vb_ref[pl.ds(s0, SB_K), lanes]
                    z = _dot(qb[h], kt, NT) * scale
                    ls = _log_sigmoid(z)
                    stay = ls - z
                    if causal is not None:
                        stay = jnp.where(causal, stay, 0.0)
                    after = total[h] - (stay_sum + _split_dot(stay, upto, 2))
                    w = jnp.exp(ls + after)
                    if causal is not None:
                        w = jnp.where(causal, w, 0.0)
                    dlw = _dot(dob[h], vt, NT) * w
                    prior = dlw_sum + _split_dot(dlw, before, 2)
                    sig = jnp.exp(ls)
                    dz = (dlw * (1.0 - sig) - sig * prior) * scale
                    if causal is not None:
                        dz = jnp.where(causal, dz, 0.0)
                    dzb = dz.astype(BF16)
                    dq = dq + _dot(dzb, kt)
                    adds.append((lanes, _dot(dzb, qb[h], TN), _dot(w.astype(BF16), dob[h], TN)))
                    out.append((stay_sum + jnp.sum(stay, axis=1, keepdims=True),
                                dlw_sum + jnp.sum(dlw, axis=1, keepdims=True), dq))
                for lanes, dk, dv in adds:
                    dk_ref[pl.ds(s0, SB_K), lanes] += dk
                    dv_ref[pl.ds(s0, SB_K), lanes] += dv
                return tuple(out)

            def k_blocks(j, st):
                for u in range(SB_UNROLL):
                    st = tile(pl.multiple_of((SB_UNROLL * j + u) * SB_K, SB_K), st, None)
                return st

            zero = jnp.zeros((tq, 1), F32)
            state = tuple((zero, zero, jnp.zeros((tq, HEAD_DIM), F32)) for _ in range(hp))
            state = lax.fori_loop(0, diag_tiles * qi // SB_UNROLL, k_blocks, state)
            for dt in range(diag_tiles):
                state = tile(t0 + dt * SB_K, state, col + dt * SB_K < row)
            for h, lanes in enumerate(heads):
                o_ref[0, pl.ds(t0, tq), lanes] = state[h][2].astype(BF16)
            return 0

        lax.fori_loop(0, s // tq, q_block, 0)
        o_ref[1] = dk_ref[...].astype(BF16)
        o_ref[2] = dv_ref[...].astype(BF16)

    return pl.pallas_call(
        body, name=f"sb_backward_l{layer}", grid=(e // width,),
        in_specs=[pl.BlockSpec((4, s, width), lambda h: (0, 0, h)),
                  pl.BlockSpec((s, width), lambda h: (0, h)),
                  pl.BlockSpec((s, width), lambda h: (0, h)),
                  pl.BlockSpec((hp, s, 1), lambda h: (h, 0, 0))],
        out_specs=pl.BlockSpec((4, s, width), lambda h: (0, 0, h)),
        out_shape=jax.ShapeDtypeStruct((N_DEV, s, e), BF16),
        scratch_shapes=[pltpu.VMEM((s, width), BF16), pltpu.VMEM((s, width), BF16),
                        pltpu.VMEM((s, width), BF16), pltpu.VMEM((s, width), F32),
                        pltpu.VMEM((s, width), F32)],
        compiler_params=_params("parallel"),
    )(proj, y, dmixed, tot)


def _norm_bwd(dn, xh, r, gain):
    dxh = dn * gain
    return r * (dxh - xh * jnp.mean(dxh * xh, axis=-1, keepdims=True)), dn * xh


def _chunk_backward(proj, bias, q_gain, k_gain, y, dmixed, dproj, layer):
    _, s, e = proj.shape
    hp = _tile(e // HEAD_DIM, CHUNK_HEADS)
    width = hp * HEAD_DIM
    steps = e // width
    unroll = _tile(s // PAIR, PAIR_UNROLL)
    scale = HEAD_DIM ** -0.5
    heads = [slice(h * HEAD_DIM, (h + 1) * HEAD_DIM) for h in range(hp)]

    def body(p_ref, b_ref, qg_ref, kg_ref, y_ref, dm_ref, dp_in, o_ref, db_ref, dqg_ref, dkg_ref,
             qn_ref, kp_ref, vp_ref, do_ref, dqn_ref, dkn_ref, dvp_ref):
        del dp_in
        kp_ref[pl.ds(0, PAD_K), :] = jnp.zeros((PAD_K, width), BF16)
        vp_ref[pl.ds(0, PAD_K), :] = jnp.zeros((PAD_K, width), BF16)
        vp_ref[pl.ds(PAD_K, s), :] = p_ref[2].astype(BF16)
        for lanes in heads:
            qn_ref[:, lanes] = _qk_norm(p_ref[0, :, lanes], qg_ref[...])[2].astype(BF16)
            kp_ref[pl.ds(PAD_K, s), lanes] = _qk_norm(p_ref[1, :, lanes], kg_ref[...])[2].astype(BF16)
        silu, dsilu = _silu_and_grad(p_ref[3])
        dm = dm_ref[...]
        do_ref[...] = (dm * silu).astype(BF16)
        o_ref[3] = (dm * y_ref[...] * dsilu).astype(BF16)
        dkn_ref[...] = jnp.zeros_like(dkn_ref)
        dvp_ref[...] = jnp.zeros_like(dvp_ref)
        db_ref[...] = jnp.zeros_like(db_ref)

        def chunks(ci, _):
            done = []
            for u in range(unroll):
                t0 = pl.multiple_of((ci * unroll + u) * PAIR, PAIR)
                for h, lanes in enumerate(heads):
                    qc = qn_ref[pl.ds(t0, PAIR), lanes]
                    kw = kp_ref[pl.ds(t0, PAIR_W), lanes]
                    vw = vp_ref[pl.ds(t0, PAIR_W), lanes]
                    dob = do_ref[pl.ds(t0, PAIR), lanes]
                    probs = _chunk_scores(qc, kw, b_ref[h], t0, scale)
                    dprobs = _dot(dob, vw, NT)
                    dsc = probs * (dprobs - jnp.sum(probs * dprobs, axis=-1, keepdims=True))
                    dsb = (dsc * scale).astype(BF16)
                    done.append((t0, h, lanes, dsc, _dot(dsb, kw), _dot(dsb, qc, TN),
                                 _dot(probs.astype(BF16), dob, TN)))
            for t0, h, lanes, dsc, dqn, dkn, dvp in done:
                db_ref[h] += dsc
                dqn_ref[pl.ds(t0, PAIR), lanes] = dqn
                dkn_ref[pl.ds(t0, PAIR_W), lanes] += dkn
                dvp_ref[pl.ds(t0, PAIR_W), lanes] += dvp
            return 0

        lax.fori_loop(0, s // (PAIR * unroll), chunks, 0)
        o_ref[2] = dvp_ref[pl.ds(PAD_K, s), :].astype(BF16)

        @pl.when(pl.program_id(0) == 0)
        def _():
            dqg_ref[...] = jnp.zeros_like(dqg_ref)
            dkg_ref[...] = jnp.zeros_like(dkg_ref)

        for lanes in heads:
            qh, rq, _ = _qk_norm(p_ref[0, :, lanes], qg_ref[...])
            dq, dqg_rows = _norm_bwd(dqn_ref[:, lanes], qh, rq, qg_ref[...])
            o_ref[0, :, lanes] = dq.astype(BF16)
            dqg_ref[...] += jnp.sum(dqg_rows, axis=0, keepdims=True)
            kh, rk, _ = _qk_norm(p_ref[1, :, lanes], kg_ref[...])
            dk, dkg_rows = _norm_bwd(dkn_ref[pl.ds(PAD_K, s), lanes], kh, rk, kg_ref[...])
            o_ref[1, :, lanes] = dk.astype(BF16)
            dkg_ref[...] += jnp.sum(dkg_rows, axis=0, keepdims=True)

    return pl.pallas_call(
        body, name=f"chunk_backward_l{layer}", grid=(steps,),
        in_specs=[pl.BlockSpec((4, s, width), lambda h: (1, 0, h)),
                  pl.BlockSpec((hp, PAIR, PAIR_W), lambda h: (layer * steps + h, 0, 0)),
                  pl.BlockSpec((1, HEAD_DIM), lambda h: (0, 0)),
                  pl.BlockSpec((1, HEAD_DIM), lambda h: (0, 0)),
                  pl.BlockSpec((s, width), lambda h: (0, steps + h)),
                  pl.BlockSpec((s, width), lambda h: (0, steps + h)),
                  _any()],
        out_specs=(pl.BlockSpec((4, s, width), lambda h: (1, 0, h)),
                   pl.BlockSpec((hp, PAIR, PAIR_W), lambda h: (h, 0, 0)),
                   pl.BlockSpec((1, HEAD_DIM), lambda h: (0, 0)),
                   pl.BlockSpec((1, HEAD_DIM), lambda h: (0, 0))),
        out_shape=(jax.ShapeDtypeStruct(dproj.shape, BF16),
                   jax.ShapeDtypeStruct((e // HEAD_DIM, PAIR, PAIR_W), F32),
                   jax.ShapeDtypeStruct((1, HEAD_DIM), F32), jax.ShapeDtypeStruct((1, HEAD_DIM), F32)),
        input_output_aliases={6: 0},
        scratch_shapes=[pltpu.VMEM((s, width), BF16), pltpu.VMEM((s + PAD_K, width), BF16),
                        pltpu.VMEM((s + PAD_K, width), BF16), pltpu.VMEM((s, width), BF16),
                        pltpu.VMEM((s, width), F32), pltpu.VMEM((s + PAD_K, width), F32),
                        pltpu.VMEM((s + PAD_K, width), F32)],
        compiler_params=_params("arbitrary"),
    )(proj, bias, q_gain, k_gain, y, dmixed, dproj)


def _proj_bwd_input(dproj, w_all, x, g, dx, tie, layer):
    s, d = x.shape
    e = w_all.shape[2]
    tm = _tile(s, 512)

    def body(dp_ref, w_ref, x_ref, g_ref, dx_ref, tie_ref, o_ref, dg_ref, acc_ref):
        del tie_ref
        j = pl.program_id(1)

        @pl.when(j == 0)
        def _():
            acc_ref[...] = jnp.zeros_like(acc_ref)

        acc_ref[...] += _dot(dp_ref[...], w_ref[...], NT)

        @pl.when(jnp.logical_and(j == N_DEV - 1, pl.program_id(0) == 0))
        def _():
            dg_ref[...] = jnp.zeros_like(dg_ref)

        @pl.when(j == N_DEV - 1)
        def _():
            xv = x_ref[...]
            r = lax.rsqrt(jnp.mean(xv * xv, axis=-1, keepdims=True) + NORM_EPS)
            dxn, dg_rows = _norm_bwd(acc_ref[...], xv * r, r, g_ref[...])
            o_ref[...] = dx_ref[...] + dxn
            dg_ref[...] += jnp.sum(dg_rows, axis=0, keepdims=True)

    return pl.pallas_call(
        body, name=f"proj_dx_l{layer}", grid=(s // tm, N_DEV),
        in_specs=[pl.BlockSpec((None, tm, e), lambda m, j: (j, m, 0)),
                  pl.BlockSpec((None, d, e), lambda m, j: (j, 0, 0)),
                  pl.BlockSpec((tm, d), lambda m, j: (m, 0)),
                  pl.BlockSpec((1, d), lambda m, j: (0, 0)),
                  pl.BlockSpec((tm, d), lambda m, j: (m, 0)), _any()],
        out_specs=(pl.BlockSpec((tm, d), lambda m, j: (m, 0)), pl.BlockSpec((1, d), lambda m, j: (0, 0))),
        out_shape=(jax.ShapeDtypeStruct((s, d), F32), jax.ShapeDtypeStruct((1, d), F32)),
        scratch_shapes=[pltpu.VMEM((tm, d), F32)],
        compiler_params=_params("arbitrary", "arbitrary"),
    )(dproj, w_all, x, g, dx, tie)


def _proj_bwd_weight(h, dproj, layer):
    s, d = h.shape
    e = dproj.shape[2]
    td, tn = _tile(d, 1024), _tile(e, 1024)
    nb = e // tn

    def body(h_ref, dp_ref, o_ref):
        o_ref[...] = _dot(h_ref[...], dp_ref[...], TN).astype(BF16)

    return pl.pallas_call(
        body, name=f"proj_dw_l{layer}", grid=(d // td, N_DEV * nb),
        in_specs=[pl.BlockSpec((s, td), lambda i, n: (0, i)),
                  pl.BlockSpec((None, s, tn), lambda i, n: (n // nb, 0, n % nb))],
        out_specs=pl.BlockSpec((None, td, tn), lambda i, n: (n // nb, i, n % nb)),
        out_shape=jax.ShapeDtypeStruct((N_DEV, d, e), BF16),
        compiler_params=_params("parallel", "parallel"),
    )(h, dproj)


def _adamw_math(w, g, m, v):
    m = ADAM_B1 * m + (1.0 - ADAM_B1) * g
    v = ADAM_B2 * v + (1.0 - ADAM_B2) * (g * g)
    m_hat = m / (1.0 - ADAM_B1 ** ADAM_STEP)
    v_hat = v / (1.0 - ADAM_B2 ** ADAM_STEP)
    return -ADAM_LR * (m_hat / (jnp.sqrt(v_hat) + ADAM_EPS) + ADAM_WD * w), m, v


def _adamw_layer(parts, own, me, w, m, v, prev, layer, name):
    n_layers, rows, cols = w.shape
    tr = _tile(rows, max(8, (256 * 1024) // cols))

    def body(me_ref, p_ref, own_ref, w_ref, m_ref, v_ref, *rest):
        del me_ref
        g_ref, d_ref, nm_ref, nv_ref = rest[-4:]
        g = own_ref[...].astype(F32)
        for j in range(4):
            g = g + p_ref[j].astype(F32)
        g_ref[...] = g
        d_ref[...], nm_ref[...], nv_ref[...] = _adamw_math(w_ref[...], g, m_ref[...], v_ref[...])

    blk = pl.BlockSpec((None, tr, cols), lambda i, me_ref: (layer, i, 0))
    out_shape = tuple(jax.ShapeDtypeStruct(w.shape, F32) for _ in range(4))
    in_specs = [pl.BlockSpec((4, tr, cols), lambda i, me_ref: (0, i, 0)),
                pl.BlockSpec((None, tr, cols), lambda i, me_ref: (me_ref[0], i, 0)), blk, blk, blk]
    args = [me, parts, own, w, m, v]
    aliases = {}
    if prev is not None:
        in_specs += [_any()] * 4
        args += list(prev)
        aliases = {6 + k: k for k in range(4)}
    return pl.pallas_call(
        body, name=f"{name}_l{layer}",
        grid_spec=pltpu.PrefetchScalarGridSpec(
            num_scalar_prefetch=1, grid=(rows // tr,), in_specs=in_specs, out_specs=(blk, blk, blk, blk)),
        out_shape=out_shape, input_output_aliases=aliases,
        compiler_params=_params("parallel"),
    )(*args)


def _sum_slots(parts):
    def body(p_ref, o_ref):
        g = p_ref[0]
        for j in range(1, N_DEV):
            g = g + p_ref[j]
        o_ref[...] = g

    return pl.pallas_call(
        body, name="sum_small_grads",
        in_specs=[_vmem()], out_specs=_vmem(),
        out_shape=jax.ShapeDtypeStruct(parts.shape[1:], F32),
        compiler_params=_params(),
    )(parts)


def _adamw_small(w, g, m, v):
    def body(w_ref, g_ref, m_ref, v_ref, d_ref, nm_ref, nv_ref):
        d_ref[...], nm_ref[...], nv_ref[...] = _adamw_math(w_ref[...], g_ref[...], m_ref[...], v_ref[...])

    return pl.pallas_call(
        body, name="adamw_small",
        in_specs=[_vmem()] * 4, out_specs=(_vmem(),) * 3,
        out_shape=tuple(jax.ShapeDtypeStruct(w.shape, F32) for _ in range(3)),
        compiler_params=_params(),
    )(w, g, m, v)


def _pack_rows(arrays):
    rows = []
    for a in arrays:
        flat = a.reshape(-1)
        pad = (-flat.shape[0]) % (8 * LANES)
        rows.append(jnp.pad(flat, (0, pad)).reshape(-1, LANES))
    return jnp.concatenate(rows, axis=0)


def _unpack_rows(packed, like):
    out, r0 = [], 0
    for a in like:
        n = a.size
        nr = -(-n // (8 * LANES)) * 8
        out.append(packed[r0:r0 + nr].reshape(-1)[:n].reshape(a.shape))
        r0 += nr
    return out


def kernel(x, norm_g, w_in, q_norm_g, k_norm_g, rel_bias, w_out, loss_target, m_norm_g, m_w_in, m_q_norm_g, m_k_norm_g, m_rel_bias, m_w_out, v_norm_g, v_w_in, v_q_norm_g, v_k_norm_g, v_rel_bias, v_w_out):
    depth, d, e = w_in.shape
    r_out = w_out.shape[1]
    heads = e // HEAD_DIM
    rel_w = rel_bias.shape[2]
    x0 = x[0]
    target = loss_target[0]
    s = x0.shape[0]

    me = jnp.reshape(_flat(_my_place()), (1,)).astype(jnp.int32)

    casts = [(_cast_layer(w_in, me, l, "cast_w_in"), _cast_layer(w_out, me, l, "cast_w_out"))
             for l in range(depth)]

    def begin_gather(l, after):
        (win_b, win_land), (wout_b, wout_land) = casts[l]
        return _gather_send((win_b, wout_b), (win_land, wout_land), after, f"gather_send_l{l}")

    rel_all = _gather_small(rel_bias, [], "gather_rel_bias")
    sent = begin_gather(0, [rel_all])
    rel_full = jnp.transpose(rel_all, (1, 2, 0, 3)).reshape(depth * heads, N_DEV * rel_w)
    bias = jnp.transpose(_bias_expand(rel_full), (1, 0, 2))
    head_work = [bias] + [shard for cast in casts[1:] for shard, _ in cast]
    forwarded = _gather_forward(sent, head_work, "gather_forward_l0")

    xs, hs, projs, ys, mixes, tots, weights = [], [], [], [], [], [], []
    xl = x0
    for l in range(depth):
        win_all, wout_all = _gather_finish(forwarded, [xl, forwarded[-1]], f"gather_finish_l{l}")
        more = l + 1 < depth
        if more:
            sent = begin_gather(l + 1, [win_all])
        wout_full = wout_all.reshape(d, d)
        proj, h = _norm_proj(xl, norm_g[l:l + 1], win_all, sent[-1] if more else None, l)
        y, mixed, tot = _sb_forward(proj, l)
        if more:
            forwarded = _gather_forward(sent, [tot], f"gather_forward_l{l + 1}")
        y, mixed = _chunk_forward(proj, bias, q_norm_g[l:l + 1], k_norm_g[l:l + 1], y, mixed,
                                  forwarded[-1] if more else None, l)
        xs.append(xl), hs.append(h), projs.append(proj), ys.append(y), mixes.append(mixed), tots.append(tot)
        weights.append((win_all, wout_full))
        xl = _out_proj(mixed, wout_full, xl, l)

    dx, loss_parts = _loss_head(xl, target)
    loss = lax.psum(jnp.sum(loss_parts), AXES)

    dbias, dng, dqg, dkg = [None] * depth, [None] * depth, [None] * depth, [None] * depth
    res_in, res_out = None, None

    peer_slots = jnp.stack([_flat(_flip(_my_place(), k)) for k in SAME_CORE]).astype(jnp.int32)

    def finish_exchange(exchanging, after, l):
        sems, csums_lands, own = exchanging
        rin, rout = _wait_copies(f"exchange_finish_l{l}", csums_lands, sems, 2, lambda refs, t: refs[t].at[0], 3,
                                 after)[2:]
        return (_adamw_layer(rin, own[0], me, w_in, m_w_in, v_w_in, res_in, l, "adamw_w_in"),
                _adamw_layer(rout, own[1], me, w_out, m_w_out, v_w_out, res_out, l, "adamw_w_out"))

    pending, tie = [], None
    for l in reversed(range(depth)):
        win_all, wout_full = weights[l]
        dmixed = _out_proj_bwd_input(dx, wout_full, tie, l)
        gwout = _out_proj_bwd_weight(mixes[l], dx, l).reshape(N_DEV, r_out, d)
        dproj = _sb_backward(projs[l], ys[l], dmixed, tots[l], l)
        dproj, dbias[l], dqg[l], dkg[l] = _chunk_backward(
            projs[l], bias, q_norm_g[l:l + 1], k_norm_g[l:l + 1], ys[l], dmixed, dproj, l)
        grads_l = (_proj_bwd_weight(hs[l], dproj, l), gwout)
        sems, bufs, token = _exchange_to_sibling(grads_l, [], f"exchange_sibling_l{l}")
        if l > 0:
            dx, dng[l] = _proj_bwd_input(dproj, win_all, xs[l], norm_g[l:l + 1], dx, token, l)
            token = dx
        bufs = _wait_copies(f"exchange_sibling_wait_l{l}", bufs, sems, 2, lambda refs, t: refs[t].at[0], 4, [token])
        own, parts, lands = bufs[0:2], bufs[2:4], bufs[4:6]
        csums = [_chip_sums(own[t], parts[t], peer_slots, f"chip_sums_{t}_l{l}") for t in range(2)]
        sems, csums_lands, tie = _exchange_to_chips(csums, lands, [], f"exchange_chips_l{l}")
        pending.append(((sems, csums_lands, own), l))
        if l == 0:
            dx, dng[l] = _proj_bwd_input(dproj, win_all, xs[l], norm_g[l:l + 1], dx, tie, l)
    for exchanging, l in pending[:-1]:
        res_in, res_out = finish_exchange(exchanging, [dx, tie], l)
    drel = _bias_grad(jnp.transpose(jnp.concatenate(dbias, axis=0), (1, 0, 2)), [tie])
    small_like = [norm_g, q_norm_g, k_norm_g, drel]
    mine = _pack_rows([jnp.concatenate(dng, axis=0), jnp.concatenate(dqg, axis=0),
                       jnp.concatenate(dkg, axis=0), drel])
    gathered = _gather_small(mine, [res_in[0], res_out[0]], "gather_small_grads")
    g_norm, g_qn, g_kn, g_rel_full = _unpack_rows(_sum_slots(gathered), small_like)
    my_block = _flat(_my_place())
    g_rel = lax.dynamic_slice_in_dim(g_rel_full.reshape(depth, heads, N_REL), my_block * rel_w, rel_w, axis=2)
    small_w = [norm_g, q_norm_g, k_norm_g, rel_bias]
    small = _adamw_small(_pack_rows(small_w), _pack_rows([g_norm, g_qn, g_kn, g_rel]),
                         _pack_rows([m_norm_g, m_q_norm_g, m_k_norm_g, m_rel_bias]),
                         _pack_rows([v_norm_g, v_q_norm_g, v_k_norm_g, v_rel_bias]))
    d_small, nm_small, nv_small = (_unpack_rows(p, small_w) for p in small)

    res_in, res_out = finish_exchange(pending[-1][0], [small[0], res_in[0], res_out[0]], 0)
    g_win, d_win, nm_win, nv_win = res_in
    g_wout, d_wout, nm_wout, nv_wout = res_out
    grads = (g_norm, g_win, g_qn, g_kn, g_rel, g_wout)

    def order(sm, big_in, big_out):
        return (sm[0], big_in, sm[1], sm[2], sm[3], big_out)

    return (loss, dx[None], *grads, *order(d_small, d_win, d_wout),
            *order(nm_small, nm_win, nm_wout), *order(nv_small, nv_win, nv_wout))
```

```python
import functools

import jax
import jax.numpy as jnp
from jax import lax
from jax.experimental import pallas as pl
from jax.experimental.pallas import tpu as pltpu

F32 = jnp.float32
BF16 = jnp.bfloat16
MESH_ID = pl.DeviceIdType.MESH
AXES = ("x", "y", "c")

N_DEV = 8
HEAD_DIM = 128
CHUNK = 64
LEFT_CHUNKS = 8
BAND_W = (LEFT_CHUNKS + 1) * CHUNK
PAD_K = LEFT_CHUNKS * CHUNK
REL_CLIP = 256
N_REL = REL_CLIP + CHUNK
NORM_EPS = 1e-6
NEG_BIG = -1e30
PAIR = 2 * CHUNK
PAIR_W = BAND_W + CHUNK
CHUNK_HEADS = 2
PAIR_UNROLL = 2
SB_Q = 512
SB_K = 128
SB_HEADS = 2
SB_UNROLL = 4
LANES = 128

ADAM_LR = 0.001
ADAM_B1 = 0.9
ADAM_B2 = 0.999
ADAM_EPS = 1e-08
ADAM_WD = 0.01
ADAM_STEP = 10

VMEM_LIMIT_BYTES = 56 * 1024 * 1024

NT = (((1,), (1,)), ((), ()))
TN = (((0,), (0,)), ((), ()))


def _params(*sem, **kw):
    return pltpu.CompilerParams(dimension_semantics=sem or None, vmem_limit_bytes=VMEM_LIMIT_BYTES, **kw)


def _any():
    return pl.BlockSpec(memory_space=pl.ANY)


def _vmem():
    return pl.BlockSpec(memory_space=pltpu.VMEM)


def _tile(n, want):
    return want if n % want == 0 else n


def _dot(a, b, dims=None):
    if dims is None:
        return jnp.dot(a, b, preferred_element_type=F32)
    return lax.dot_general(a, b, dims, preferred_element_type=F32)


def _split_dot(a, b, parts, dims=None):
    acc = None
    rest = a
    for _ in range(parts):
        piece = rest.astype(BF16)
        rest = rest - piece.astype(F32)
        term = _dot(piece, b, dims)
        acc = term if acc is None else acc + term
    return acc


def _log_sigmoid(z):
    return jnp.minimum(z, 0.0) - jnp.log(1.0 + jnp.exp(-jnp.abs(z)))


def _silu_and_grad(g):
    sig = jax.nn.sigmoid(g)
    return g * sig, sig * (1.0 + g * (1.0 - sig))


def _my_place():
    return lax.axis_index("x"), lax.axis_index("y"), lax.axis_index("c")


def _flat(place):
    return 4 * place[0] + 2 * place[1] + place[2]


def _flip(place, k):
    return tuple(1 - p if (k >> s) & 1 else p for p, s in zip(place, (2, 1, 0)))


def _cast_layer(w, me, layer, name):
    _, rows, cols = w.shape
    tr = _tile(rows, 1024)

    def body(me_ref, a_ref, shard_ref, land_ref):
        del me_ref
        shard_ref[...] = a_ref[...].astype(BF16)
        land_ref[...] = shard_ref[...]

    return pl.pallas_call(
        body, name=f"{name}_l{layer}",
        grid_spec=pltpu.PrefetchScalarGridSpec(
            num_scalar_prefetch=1, grid=(rows // tr,),
            in_specs=[pl.BlockSpec((None, tr, cols), lambda i, me_ref: (layer, i, 0))],
            out_specs=(pl.BlockSpec((tr, cols), lambda i, me_ref: (i, 0)),
                       pl.BlockSpec((None, tr, cols), lambda i, me_ref: (me_ref[0], i, 0)))),
        out_shape=(jax.ShapeDtypeStruct((rows, cols), BF16), jax.ShapeDtypeStruct((N_DEV, rows, cols), BF16)),
        compiler_params=_params("parallel"),
    )(me, w)


HBM_SPEC = pl.BlockSpec(memory_space=pltpu.HBM)
SEM_SPEC = pl.BlockSpec(memory_space=pltpu.SEMAPHORE)
SAME_CORE = (2, 4, 6)
SIBLING = 1
SPLIT_EFFECT = pltpu.SideEffectType.DATAFLOW_SIDE_EFFECTING


def _split_call(body, name, bufs, sems_in, sem_counts_out, after, token):
    bufs, sems_in, after = list(bufs), list(sems_in), list(after)
    nb, ni, no = len(bufs), len(sems_in), len(sem_counts_out)

    def wrapped(*refs):
        outs = nb + ni + len(after)
        body(refs[:nb], refs[nb:nb + ni], refs[outs:outs + no])
        if token:
            refs[-1][...] = jnp.zeros_like(refs[-1])

    out = pl.pallas_call(
        wrapped, name=name,
        in_specs=[HBM_SPEC] * nb + [SEM_SPEC] * ni + [_any()] * len(after),
        out_specs=tuple([SEM_SPEC] * no + [HBM_SPEC] * nb + ([_vmem()] if token else [])),
        out_shape=tuple([pltpu.SemaphoreType.DMA((c,)) for c in sem_counts_out]
                        + [pltpu.HBM(a.shape, a.dtype) for a in bufs]
                        + ([jax.ShapeDtypeStruct((8, LANES), F32)] if token else [])),
        input_output_aliases={i: no + i for i in range(nb)},
        compiler_params=pltpu.CompilerParams(has_side_effects=SPLIT_EFFECT),
    )(*[pltpu.with_memory_space_constraint(a, pltpu.HBM) for a in bufs], *sems_in, *after)
    return list(out[:no]), list(out[no:no + nb]), (out[-1] if token else None)


def _remote(src, dst, send_sems, recv_sems, i, to):
    return pltpu.make_async_remote_copy(src_ref=src, dst_ref=dst, send_sem=send_sems.at[i], recv_sem=recv_sems.at[i],
                                        device_id=to, device_id_type=MESH_ID)


def _wait_copies(name, bufs, sems, n, slot_of, count, after):
    def body(refs, sems_in, _):
        me = _my_place()
        for t in range(n):
            slot = slot_of(refs, t)
            for a in range(count):
                cp = _remote(slot, slot, sems_in[0], sems_in[1], t * count + a, me)
                cp.wait_send()
                cp.wait_recv()

    return _split_call(body, name, bufs, sems, (), after, False)[1]


def _exchange_direct(grads, after, name):
    n = len(grads)
    lands = [lax.empty(g.shape, g.dtype) for g in grads]

    def body(refs, _, sems_out):
        me = _my_place()
        for k in range(1, N_DEV):
            peer = _flip(me, k)
            for t in range(n):
                _remote(refs[t].at[_flat(peer)], refs[n + t].at[_flat(me)], *sems_out, t * (N_DEV - 1) + k - 1,
                        peer).start()

    return _split_call(body, name, list(grads) + lands, (), (n * (N_DEV - 1), n * (N_DEV - 1)), after, True)


def _exchange_to_sibling(grads, after, name):
    n = len(grads)
    parts = [lax.empty((3,) + g.shape[1:], g.dtype) for g in grads]
    lands = [lax.empty((4,) + g.shape[1:], g.dtype) for g in grads]

    def body(refs, _, sems_out):
        me = _my_place()
        sib = _flip(me, SIBLING)
        for t in range(n):
            g, part, land = refs[t], refs[n + t], refs[2 * n + t]
            _remote(g.at[_flat(sib)], land.at[0], *sems_out, 4 * t, sib).start()
            for a, k in enumerate(SAME_CORE):
                _remote(g.at[_flat(_flip(sib, k))], part.at[a], *sems_out, 4 * t + 1 + a, sib).start()

    return _split_call(body, name, list(grads) + parts + lands, (), (4 * n, 4 * n), after, True)


def _chip_sums(grads, parts, slots, name):
    _, rows, cols = grads.shape
    tr = _tile(rows, max(8, (512 * 1024) // cols))

    def body(slots_ref, g_ref, p_ref, o_ref):
        del slots_ref
        o_ref[...] = (g_ref[...].astype(F32) + p_ref[...].astype(F32)).astype(BF16)

    return pl.pallas_call(
        body, name=name,
        grid_spec=pltpu.PrefetchScalarGridSpec(
            num_scalar_prefetch=1, grid=(3, rows // tr),
            in_specs=[pl.BlockSpec((None, tr, cols), lambda a, i, slots_ref: (slots_ref[a], i, 0)),
                      pl.BlockSpec((None, tr, cols), lambda a, i, slots_ref: (a, i, 0))],
            out_specs=pl.BlockSpec((None, tr, cols), lambda a, i, slots_ref: (a, i, 0))),
        out_shape=jax.ShapeDtypeStruct((3, rows, cols), BF16),
        compiler_params=_params("parallel", "parallel"),
    )(slots, grads, parts)


def _exchange_to_chips(csums, lands, after, name):
    n = len(csums)

    def body(refs, _, sems_out):
        me = _my_place()
        for t in range(n):
            for a, k in enumerate(SAME_CORE):
                _remote(refs[t].at[a], refs[n + t].at[1 + a], *sems_out, 3 * t + a, _flip(me, k)).start()

    return _split_call(body, name, list(csums) + list(lands), (), (3 * n, 3 * n), after, True)


def _hbm_call(body, name, n_hbm, sems_in, sems_out, after, token, like):
    after = list(after)
    in_specs = [HBM_SPEC] * n_hbm + [SEM_SPEC] * len(sems_in) + [_any()] * len(after)
    out_specs = [SEM_SPEC] * len(sems_out) + [HBM_SPEC] * n_hbm + ([_vmem()] if token else [])
    out_shape = ([pltpu.SemaphoreType.DMA((c,)) for c in sems_out] + [pltpu.HBM(a.shape, a.dtype) for a in like]
                 + ([jax.ShapeDtypeStruct((8, LANES), F32)] if token else []))
    return in_specs, tuple(out_specs), tuple(out_shape), {i: len(sems_out) + i for i in range(n_hbm)}, after


def _gather_send(shards, lands, after, name):
    n = len(shards)
    peers = (SIBLING,) + SAME_CORE
    after = list(after)

    def body(*refs):
        me = _my_place()
        send_sems, recv_sems = refs[2 * n + len(after)], refs[2 * n + len(after) + 1]
        for a, k in enumerate(peers):
            for t in range(n):
                pltpu.make_async_remote_copy(
                    src_ref=refs[t], dst_ref=refs[n + t].at[_flat(me)],
                    send_sem=send_sems.at[t * 4 + a], recv_sem=recv_sems.at[t * 4 + a],
                    device_id=_flip(me, k), device_id_type=MESH_ID).start()
        refs[-1][...] = jnp.zeros_like(refs[-1])

    bufs = list(shards) + list(lands)
    in_specs, out_specs, out_shape, aliases, after = _hbm_call(body, name, 2 * n, (), (4 * n, 4 * n), after, True, bufs)
    out = pl.pallas_call(
        body, name=name, in_specs=in_specs, out_specs=out_specs, out_shape=out_shape,
        input_output_aliases=aliases, compiler_params=pltpu.CompilerParams(has_side_effects=SPLIT_EFFECT),
    )(*[pltpu.with_memory_space_constraint(a, pltpu.HBM) for a in bufs], *after)
    return out[0], out[1], out[2:2 + n], out[2 + n:2 + 2 * n], out[-1]


def _gather_forward(sent, after, name):
    send1, recv1, shards, lands, _ = sent
    n = len(shards)
    after = list(after)

    def body(*refs):
        me = _my_place()
        recv1_ref = refs[2 * n + 1]
        out0 = 2 * n + 2 + len(after)
        send2_ref, recv2_ref = refs[out0], refs[out0 + 1]
        for a, k in enumerate(SAME_CORE):
            owner = _flat(_flip(me, k))
            for t in range(n):
                slot = refs[n + t].at[owner]
                pltpu.make_async_remote_copy(
                    src_ref=refs[t], dst_ref=slot, send_sem=refs[2 * n].at[t * 4 + 1 + a],
                    recv_sem=recv1_ref.at[t * 4 + 1 + a], device_id=_flip(me, k), device_id_type=MESH_ID).wait_recv()
                pltpu.make_async_remote_copy(
                    src_ref=slot, dst_ref=slot, send_sem=send2_ref.at[t * 3 + a], recv_sem=recv2_ref.at[t * 3 + a],
                    device_id=_flip(me, SIBLING), device_id_type=MESH_ID).start()
        refs[-1][...] = jnp.zeros_like(refs[-1])

    bufs = list(shards) + list(lands)
    in_specs, out_specs, out_shape, aliases, after = _hbm_call(body, name, 2 * n, (4 * n, 4 * n), (3 * n, 3 * n), after,
                                                               True, bufs)
    out = pl.pallas_call(
        body, name=name, in_specs=in_specs, out_specs=out_specs, out_shape=out_shape,
        input_output_aliases=aliases, compiler_params=pltpu.CompilerParams(has_side_effects=SPLIT_EFFECT),
    )(*bufs, send1, recv1, *after)
    return (send1, recv1), (out[0], out[1]), out[2:2 + n], out[2 + n:2 + 2 * n], out[-1]


def _gather_finish(forwarded, after, name):
    (send1, recv1), (send2, recv2), shards, lands, _ = forwarded
    n = len(shards)
    after = list(after)

    def body(*refs):
        me = _my_place()
        send1_ref, recv1_ref, send2_ref, recv2_ref = refs[2 * n:2 * n + 4]
        sib = _flip(me, SIBLING)
        for t in range(n):
            for a in range(4):
                cp = pltpu.make_async_remote_copy(
                    src_ref=refs[t], dst_ref=refs[n + t].at[_flat(sib)], send_sem=send1_ref.at[t * 4 + a],
                    recv_sem=recv1_ref.at[t * 4 + a], device_id=sib, device_id_type=MESH_ID)
                cp.wait_send()
                if a == 0:
                    cp.wait_recv()
            for a in range(3):
                cp = pltpu.make_async_remote_copy(
                    src_ref=refs[t], dst_ref=refs[n + t].at[_flat(sib)], send_sem=send2_ref.at[t * 3 + a],
                    recv_sem=recv2_ref.at[t * 3 + a], device_id=sib, device_id_type=MESH_ID)
                cp.wait_send()
                cp.wait_recv()

    bufs = list(shards) + list(lands)
    in_specs, out_specs, out_shape, aliases, after = _hbm_call(body, name, 2 * n, (4 * n, 4 * n, 3 * n, 3 * n), (), after,
                                                               False, bufs)
    out = pl.pallas_call(
        body, name=name, in_specs=in_specs, out_specs=out_specs, out_shape=out_shape,
        input_output_aliases=aliases, compiler_params=pltpu.CompilerParams(has_side_effects=SPLIT_EFFECT),
    )(*bufs, send1, recv1, send2, recv2, *after)
    return out[n:]


def _gather_small(v, after, name):
    after = list(after)

    def body(v_ref, *rest):
        o_ref, send_sems, recv_sems = rest[-3:]
        me = _my_place()
        o_ref[_flat(me)] = v_ref[...]
        copies = []
        for k in range(1, N_DEV):
            copies.append(pltpu.make_async_remote_copy(
                src_ref=v_ref, dst_ref=o_ref.at[_flat(me)],
                send_sem=send_sems.at[k - 1], recv_sem=recv_sems.at[k - 1],
                device_id=_flip(me, k), device_id_type=MESH_ID))
        for cp in copies:
            cp.start()
        for cp in copies:
            cp.wait()

    return pl.pallas_call(
        body, name=name,
        in_specs=[_vmem()] + [_any()] * len(after), out_specs=_vmem(),
        out_shape=jax.ShapeDtypeStruct((N_DEV,) + v.shape, v.dtype),
        scratch_shapes=[pltpu.SemaphoreType.DMA((7,)), pltpu.SemaphoreType.DMA((7,))],
        compiler_params=_params(has_side_effects=True),
    )(v, *after)


def _rel_onehot(row):
    r_io = lax.broadcasted_iota(jnp.int32, (N_REL, PAIR_W), 0)
    p_io = lax.broadcasted_iota(jnp.int32, (N_REL, PAIR_W), 1)
    band_col = p_io - (row // CHUNK) * CHUNK
    in_band = jnp.logical_and(band_col >= 0, band_col < BAND_W)
    idx = jnp.clip(PAD_K + row % CHUNK - band_col, -(CHUNK - 1), REL_CLIP) + (CHUNK - 1)
    return jnp.logical_and(r_io == idx, in_band).astype(BF16), in_band[0:1]


def _bias_expand(rel):
    lh = rel.shape[0]

    def body(rel_ref, o_ref):
        onehot, in_band = _rel_onehot(pl.program_id(0))
        o_ref[...] = jnp.where(in_band, _split_dot(rel_ref[...], onehot, 3), NEG_BIG)

    return pl.pallas_call(
        body, name="bias_expand", grid=(PAIR,),
        in_specs=[pl.BlockSpec((lh, N_REL), lambda i: (0, 0))],
        out_specs=pl.BlockSpec((None, lh, PAIR_W), lambda i: (i, 0, 0)),
        out_shape=jax.ShapeDtypeStruct((PAIR, lh, PAIR_W), F32),
        compiler_params=_params("parallel"),
    )(rel)


def _bias_grad(dbias, after):
    lh = dbias.shape[1]
    after = list(after)

    def body(db_ref, *rest):
        o_ref = rest[-1]
        i = pl.program_id(0)

        @pl.when(i == 0)
        def _():
            o_ref[...] = jnp.zeros_like(o_ref)

        o_ref[...] += _split_dot(db_ref[...], _rel_onehot(i)[0], 2, NT)

    return pl.pallas_call(
        body, name="bias_grad", grid=(PAIR,),
        in_specs=[pl.BlockSpec((None, lh, PAIR_W), lambda i: (i, 0, 0))] + [_any()] * len(after),
        out_specs=pl.BlockSpec((lh, N_REL), lambda i: (0, 0)),
        out_shape=jax.ShapeDtypeStruct((lh, N_REL), F32),
        compiler_params=_params("arbitrary"),
    )(dbias, *after)


def _norm_proj(x, g, w_all, tie, layer):
    s, d = x.shape
    e = w_all.shape[2]
    tm, tn = _tile(s, 1024), _tile(e, 1024)
    nb = e // tn
    ties = [] if tie is None else [tie]

    def body(x_ref, g_ref, w_ref, *rest):
        proj_ref, h_ref = rest[-2:]

        @pl.when(pl.program_id(1) == 0)
        def _():
            xv = x_ref[...]
            r = lax.rsqrt(jnp.mean(xv * xv, axis=-1, keepdims=True) + NORM_EPS)
            h_ref[...] = ((xv * r) * g_ref[...]).astype(BF16)

        proj_ref[...] = _dot(h_ref[...], w_ref[...])

    return pl.pallas_call(
        body, name=f"norm_proj_l{layer}", grid=(s // tm, N_DEV * nb),
        in_specs=[pl.BlockSpec((tm, d), lambda m, n: (m, 0)),
                  pl.BlockSpec((1, d), lambda m, n: (0, 0)),
                  pl.BlockSpec((None, d, tn), lambda m, n: (n // nb, 0, n % nb))] + [_any()] * len(ties),
        out_specs=(pl.BlockSpec((None, tm, tn), lambda m, n: (n // nb, m, n % nb)),
                   pl.BlockSpec((tm, d), lambda m, n: (m, 0))),
        out_shape=(jax.ShapeDtypeStruct((N_DEV, s, e), F32), jax.ShapeDtypeStruct((s, d), BF16)),
        compiler_params=_params("parallel", "arbitrary"),
    )(x, g, w_all, *ties)


def _sb_forward(proj, layer):
    _, s, e = proj.shape
    hp = _tile(e // HEAD_DIM, SB_HEADS)
    width = hp * HEAD_DIM
    tq = _tile(s, SB_Q)
    diag_tiles = tq // SB_K
    scale = HEAD_DIM ** -0.5

    def body(p_ref, y_ref, mix_ref, tot_ref, kb_ref, vb_ref):
        kb_ref[...] = p_ref[1].astype(BF16)
        vb_ref[...] = p_ref[2].astype(BF16)
        row = lax.broadcasted_iota(jnp.int32, (tq, SB_K), 0)
        col = lax.broadcasted_iota(jnp.int32, (tq, SB_K), 1)
        kj = lax.broadcasted_iota(jnp.int32, (SB_K, SB_K), 0)
        ks = lax.broadcasted_iota(jnp.int32, (SB_K, SB_K), 1)
        later = (kj > ks).astype(BF16)

        def q_block(qi, _):
            t0 = pl.multiple_of(qi * tq, tq)
            qb = [p_ref[0, pl.ds(t0, tq), h * HEAD_DIM:(h + 1) * HEAD_DIM].astype(BF16) for h in range(hp)]

            def tile(s0, state, causal):
                out = []
                for h in range(hp):
                    carry, acc = state[h]
                    lanes = slice(h * HEAD_DIM, (h + 1) * HEAD_DIM)
                    z = _dot(qb[h], kb_ref[pl.ds(s0, SB_K), lanes], NT) * scale
                    ls = _log_sigmoid(z)
                    stay = ls - z
                    if causal is not None:
                        stay = jnp.where(causal, stay, 0.0)
                    w = jnp.exp(ls + carry + _split_dot(stay, later, 2))
                    if causal is not None:
                        w = jnp.where(causal, w, 0.0)
                    acc = acc + _dot(w.astype(BF16), vb_ref[pl.ds(s0, SB_K), lanes])
                    out.append((carry + jnp.sum(stay, axis=1, keepdims=True), acc))
                return tuple(out)

            state = tuple((jnp.zeros((tq, 1), F32), jnp.zeros((tq, HEAD_DIM), F32)) for _ in range(hp))
            for dt in reversed(range(diag_tiles)):
                state = tile(t0 + dt * SB_K, state, col + dt * SB_K < row)

            def k_blocks(j, st):
                for u in range(SB_UNROLL):
                    st = tile(pl.multiple_of((diag_tiles * qi - 1 - SB_UNROLL * j - u) * SB_K, SB_K), st, None)
                return st

            state = lax.fori_loop(0, diag_tiles * qi // SB_UNROLL, k_blocks, state)
            silu, _ = _silu_and_grad(p_ref[3, pl.ds(t0, tq), :])
            for h in range(hp):
                lanes = slice(h * HEAD_DIM, (h + 1) * HEAD_DIM)
                y_ref[pl.ds(t0, tq), lanes] = state[h][1]
                mix_ref[pl.ds(t0, tq), lanes] = (state[h][1] * silu[:, lanes]).astype(BF16)
                tot_ref[h, pl.ds(t0, tq), :] = state[h][0]
            return 0

        lax.fori_loop(0, s // tq, q_block, 0)

    return pl.pallas_call(
        body, name=f"sb_forward_l{layer}", grid=(e // width,),
        in_specs=[pl.BlockSpec((4, s, width), lambda h: (0, 0, h))],
        out_specs=(pl.BlockSpec((s, width), lambda h: (0, h)),
                   pl.BlockSpec((s, width), lambda h: (0, h)),
                   pl.BlockSpec((hp, s, 1), lambda h: (h, 0, 0))),
        out_shape=(jax.ShapeDtypeStruct((s, 2 * e), F32), jax.ShapeDtypeStruct((s, 2 * e), BF16),
                   jax.ShapeDtypeStruct((e // HEAD_DIM, s, 1), F32)),
        scratch_shapes=[pltpu.VMEM((s, width), BF16), pltpu.VMEM((s, width), BF16)],
        compiler_params=_params("parallel"),
    )(proj)


def _qk_norm(t, gain):
    r = lax.rsqrt(jnp.mean(t * t, axis=-1, keepdims=True) + NORM_EPS)
    return t * r, r, (t * r) * gain


def _chunk_scores(qc, kw, bias, t0, scale):
    sc = _dot(qc, kw, NT) * scale + bias
    col = lax.broadcasted_iota(jnp.int32, (PAIR, PAIR_W), 1)
    sc = jnp.where(col + t0 >= PAD_K, sc, NEG_BIG)
    ex = jnp.exp(sc - jnp.max(sc, axis=-1, keepdims=True))
    return ex / jnp.sum(ex, axis=-1, keepdims=True)


def _chunk_forward(proj, bias, q_gain, k_gain, y, mixed, tie, layer):
    ties = [] if tie is None else [tie]
    _, s, e = proj.shape
    hp = _tile(e // HEAD_DIM, CHUNK_HEADS)
    width = hp * HEAD_DIM
    steps = e // width
    unroll = _tile(s // PAIR, PAIR_UNROLL)
    scale = HEAD_DIM ** -0.5
    heads = [slice(h * HEAD_DIM, (h + 1) * HEAD_DIM) for h in range(hp)]

    def body(p_ref, b_ref, qg_ref, kg_ref, *rest):
        y_ref, mix_ref, qn_ref, kp_ref, vp_ref = rest[-5:]
        kp_ref[pl.ds(0, PAD_K), :] = jnp.zeros((PAD_K, width), BF16)
        vp_ref[pl.ds(0, PAD_K), :] = jnp.zeros((PAD_K, width), BF16)
        vp_ref[pl.ds(PAD_K, s), :] = p_ref[2].astype(BF16)
        for lanes in heads:
            qn_ref[:, lanes] = _qk_norm(p_ref[0, :, lanes], qg_ref[...])[2].astype(BF16)
            kp_ref[pl.ds(PAD_K, s), lanes] = _qk_norm(p_ref[1, :, lanes], kg_ref[...])[2].astype(BF16)

        def chunks(ci, _):
            done = []
            for u in range(unroll):
                t0 = pl.multiple_of((ci * unroll + u) * PAIR, PAIR)
                silu, _ = _silu_and_grad(p_ref[3, pl.ds(t0, PAIR), :])
                for h, lanes in enumerate(heads):
                    probs = _chunk_scores(qn_ref[pl.ds(t0, PAIR), lanes], kp_ref[pl.ds(t0, PAIR_W), lanes],
                                          b_ref[h], t0, scale)
                    out = _dot(probs.astype(BF16), vp_ref[pl.ds(t0, PAIR_W), lanes])
                    done.append((t0, lanes, out, (out * silu[:, lanes]).astype(BF16)))
            for t0, lanes, out, gated in done:
                y_ref[pl.ds(t0, PAIR), lanes] = out
                mix_ref[pl.ds(t0, PAIR), lanes] = gated
            return 0

        lax.fori_loop(0, s // (PAIR * unroll), chunks, 0)

    return pl.pallas_call(
        body, name=f"chunk_forward_l{layer}", grid=(steps,),
        in_specs=[pl.BlockSpec((4, s, width), lambda h: (1, 0, h)),
                  pl.BlockSpec((hp, PAIR, PAIR_W), lambda h: (layer * steps + h, 0, 0)),
                  pl.BlockSpec((1, HEAD_DIM), lambda h: (0, 0)),
                  pl.BlockSpec((1, HEAD_DIM), lambda h: (0, 0)),
                  _any(), _any()] + [_any()] * len(ties),
        out_specs=(pl.BlockSpec((s, width), lambda h: (0, steps + h)),
                   pl.BlockSpec((s, width), lambda h: (0, steps + h))),
        out_shape=(jax.ShapeDtypeStruct(y.shape, F32), jax.ShapeDtypeStruct(mixed.shape, BF16)),
        input_output_aliases={4: 0, 5: 1},
        scratch_shapes=[pltpu.VMEM((s, width), BF16), pltpu.VMEM((s + PAD_K, width), BF16),
                        pltpu.VMEM((s + PAD_K, width), BF16)],
        compiler_params=_params("parallel"),
    )(proj, bias, q_gain, k_gain, y, mixed, *ties)


def _out_proj(mixed, w, x, layer):
    s, d = x.shape
    tm, tn = _tile(s, 1024), _tile(d, 1024)

    def body(a_ref, w_ref, x_ref, o_ref):
        o_ref[...] = x_ref[...] + _dot(a_ref[...], w_ref[...])

    return pl.pallas_call(
        body, name=f"out_proj_l{layer}", grid=(s // tm, d // tn),
        in_specs=[pl.BlockSpec((tm, d), lambda m, n: (m, 0)),
                  pl.BlockSpec((d, tn), lambda m, n: (0, n)),
                  pl.BlockSpec((tm, tn), lambda m, n: (m, n))],
        out_specs=pl.BlockSpec((tm, tn), lambda m, n: (m, n)),
        out_shape=jax.ShapeDtypeStruct((s, d), F32),
        compiler_params=_params("parallel", "parallel"),
    )(mixed, w, x)


def _loss_head(y, target):
    s, d = y.shape
    tm = _tile(s, 256)

    def body(y_ref, t_ref, dy_ref, part_ref):
        diff = y_ref[...] - t_ref[...]
        dy_ref[...] = diff * (1.0 / d)
        sq = (diff * diff).reshape(tm // 8, 8, d).sum(axis=0)
        acc = sq[:, 0:LANES]
        for j in range(1, d // LANES):
            acc = acc + sq[:, j * LANES:(j + 1) * LANES]
        part_ref[...] = acc * (0.5 / d)

    return pl.pallas_call(
        body, name="loss_head", grid=(s // tm,),
        in_specs=[pl.BlockSpec((tm, d), lambda i: (i, 0)), pl.BlockSpec((tm, d), lambda i: (i, 0))],
        out_specs=(pl.BlockSpec((tm, d), lambda i: (i, 0)), pl.BlockSpec((None, 8, LANES), lambda i: (i, 0, 0))),
        out_shape=(jax.ShapeDtypeStruct((s, d), F32), jax.ShapeDtypeStruct((s // tm, 8, LANES), F32)),
        compiler_params=_params("parallel"),
    )(y, target)


def _out_proj_bwd_input(dx, w, tie, layer):
    s, d = dx.shape
    tm, tn = _tile(s, 1024), _tile(d, 1024)
    ties = [] if tie is None else [tie]

    def body(dx_ref, w_ref, *rest):
        rest[-1][...] = _dot(dx_ref[...].astype(BF16), w_ref[...], NT)

    return pl.pallas_call(
        body, name=f"out_proj_dx_l{layer}", grid=(s // tm, d // tn),
        in_specs=[pl.BlockSpec((tm, d), lambda m, n: (m, 0)),
                  pl.BlockSpec((tn, d), lambda m, n: (n, 0))] + [_any()] * len(ties),
        out_specs=pl.BlockSpec((tm, tn), lambda m, n: (m, n)),
        out_shape=jax.ShapeDtypeStruct((s, d), F32),
        compiler_params=_params("parallel", "parallel"),
    )(dx, w, *ties)


def _out_proj_bwd_weight(mixed, dx, layer):
    s, d = dx.shape
    te, tn = _tile(d, 1024), _tile(d, 1024)

    def body(a_ref, dx_ref, o_ref):
        o_ref[...] = _dot(a_ref[...], dx_ref[...].astype(BF16), TN).astype(BF16)

    return pl.pallas_call(
        body, name=f"out_proj_dw_l{layer}", grid=(d // te, d // tn),
        in_specs=[pl.BlockSpec((s, te), lambda i, n: (0, i)), pl.BlockSpec((s, tn), lambda i, n: (0, n))],
        out_specs=pl.BlockSpec((te, tn), lambda i, n: (i, n)),
        out_shape=jax.ShapeDtypeStruct((d, d), BF16),
        compiler_params=_params("parallel", "parallel"),
    )(mixed, dx)


def _sb_backward(proj, y, dmixed, tot, layer):
    _, s, e = proj.shape
    hp = _tile(e // HEAD_DIM, SB_HEADS)
    width = hp * HEAD_DIM
    tq = _tile(s, SB_Q)
    diag_tiles = tq // SB_K
    scale = HEAD_DIM ** -0.5

    def body(p_ref, y_ref, dm_ref, tot_ref, o_ref, kb_ref, vb_ref, do_ref, dk_ref, dv_ref):
        kb_ref[...] = p_ref[1].astype(BF16)
        vb_ref[...] = p_ref[2].astype(BF16)
        silu, dsilu = _silu_and_grad(p_ref[3])
        dm = dm_ref[...]
        do_ref[...] = (dm * silu).astype(BF16)
        o_ref[3] = (dm * y_ref[...] * dsilu).astype(BF16)
        dk_ref[...] = jnp.zeros_like(dk_ref)
        dv_ref[...] = jnp.zeros_like(dv_ref)
        row = lax.broadcasted_iota(jnp.int32, (tq, SB_K), 0)
        col = lax.broadcasted_iota(jnp.int32, (tq, SB_K), 1)
        kj = lax.broadcasted_iota(jnp.int32, (SB_K, SB_K), 0)
        ks = lax.broadcasted_iota(jnp.int32, (SB_K, SB_K), 1)
        upto = (kj <= ks).astype(BF16)
        before = (kj < ks).astype(BF16)

        def q_block(qi, _):
            t0 = pl.multiple_of(qi * tq, tq)
            heads = [slice(h * HEAD_DIM, (h + 1) * HEAD_DIM) for h in range(hp)]
            qb = [p_ref[0, pl.ds(t0, tq), lanes].astype(BF16) for lanes in heads]
            dob = [do_ref[pl.ds(t0, tq), lanes] for lanes in heads]
            total = [tot_ref[h, pl.ds(t0, tq), :] for h in range(hp)]

            def tile(s0, state, causal):
                out, adds = [], []
                for h, lanes in enumerate(heads):
                    stay_sum, dlw_sum, dq = state[h]
                    kt = kb_ref[pl.ds(s0, SB_K), lanes]
                    vt = vb_ref[pl.ds(s0, SB_K), lanes]
                    z = _dot(qb[h], kt, NT) * scale
                    ls = _log_sigmoid(z)
                    stay = ls - z
                    if causal is not None:
                        stay = jnp.where(causal, stay, 0.0)
                    after = total[h] - (stay_sum + _split_dot(stay, upto, 2))
                    w = jnp.exp(ls + after)
                    if causal is not None:
                        w = jnp.where(causal, w, 0.0)
                    dlw = _dot(dob[h], vt, NT) * w
                    prior = dlw_sum + _split_dot(dlw, before, 2)
                    sig = jnp.exp(ls)
                    dz = (dlw * (1.0 - sig) - sig * prior) * scale
                    if causal is not None:
                        dz = jnp.where(causal, dz, 0.0)
                    dzb = dz.astype(BF16)
                    dq = dq + _dot(dzb, kt)
                    adds.append((lanes, _dot(dzb, qb[h], TN), _dot(w.astype(BF16), dob[h], TN)))
                    out.append((stay_sum + jnp.sum(stay, axis=1, keepdims=True),
                                dlw_sum + jnp.sum(dlw, axis=1, keepdims=True), dq))
                for lanes, dk, dv in adds:
                    dk_ref[pl.ds(s0, SB_K), lanes] += dk
                    dv_ref[pl.ds(s0, SB_K), lanes] += dv
                return tuple(out)

            def k_blocks(j, st):
                for u in range(SB_UNROLL):
                    st = tile(pl.multiple_of((SB_UNROLL * j + u) * SB_K, SB_K), st, None)
                return st

            zero = jnp.zeros((tq, 1), F32)
            state = tuple((zero, zero, jnp.zeros((tq, HEAD_DIM), F32)) for _ in range(hp))
            state = lax.fori_loop(0, diag_tiles * qi // SB_UNROLL, k_blocks, state)
            for dt in range(diag_tiles):
                state = tile(t0 + dt * SB_K, state, col + dt * SB_K < row)
            for h, lanes in enumerate(heads):
                o_ref[0, pl.ds(t0, tq), lanes] = state[h][2].astype(BF16)
            return 0

        lax.fori_loop(0, s // tq, q_block, 0)
        o_ref[1] = dk_ref[...].astype(BF16)
        o_ref[2] = dv_ref[...].astype(BF16)

    return pl.pallas_call(
        body, name=f"sb_backward_l{layer}", grid=(e // width,),
        in_specs=[pl.BlockSpec((4, s, width), lambda h: (0, 0, h)),
                  pl.BlockSpec((s, width), lambda h: (0, h)),
                  pl.BlockSpec((s, width), lambda h: (0, h)),
                  pl.BlockSpec((hp, s, 1), lambda h: (h, 0, 0))],
        out_specs=pl.BlockSpec((4, s, width), lambda h: (0, 0, h)),
        out_shape=jax.ShapeDtypeStruct((N_DEV, s, e), BF16),
        scratch_shapes=[pltpu.VMEM((s, width), BF16), pltpu.VMEM((s, width), BF16),
                        pltpu.VMEM((s, width), BF16), pltpu.VMEM((s, width), F32),
                        pltpu.VMEM((s, width), F32)],
        compiler_params=_params("parallel"),
    )(proj, y, dmixed, tot)


def _norm_bwd(dn, xh, r, gain):
    dxh = dn * gain
    return r * (dxh - xh * jnp.mean(dxh * xh, axis=-1, keepdims=True)), dn * xh


def _chunk_backward(proj, bias, q_gain, k_gain, y, dmixed, dproj, layer):
    _, s, e = proj.shape
    hp = _tile(e // HEAD_DIM, CHUNK_HEADS)
    width = hp * HEAD_DIM
    steps = e // width
    unroll = _tile(s // PAIR, PAIR_UNROLL)
    scale = HEAD_DIM ** -0.5
    heads = [slice(h * HEAD_DIM, (h + 1) * HEAD_DIM) for h in range(hp)]

    def body(p_ref, b_ref, qg_ref, kg_ref, y_ref, dm_ref, dp_in, o_ref, db_ref, dqg_ref, dkg_ref,
             qn_ref, kp_ref, vp_ref, do_ref, dqn_ref, dkn_ref, dvp_ref):
        del dp_in
        kp_ref[pl.ds(0, PAD_K), :] = jnp.zeros((PAD_K, width), BF16)
        vp_ref[pl.ds(0, PAD_K), :] = jnp.zeros((PAD_K, width), BF16)
        vp_ref[pl.ds(PAD_K, s), :] = p_ref[2].astype(BF16)
        for lanes in heads:
            qn_ref[:, lanes] = _qk_norm(p_ref[0, :, lanes], qg_ref[...])[2].astype(BF16)
            kp_ref[pl.ds(PAD_K, s), lanes] = _qk_norm(p_ref[1, :, lanes], kg_ref[...])[2].astype(BF16)
        silu, dsilu = _silu_and_grad(p_ref[3])
        dm = dm_ref[...]
        do_ref[...] = (dm * silu).astype(BF16)
        o_ref[3] = (dm * y_ref[...] * dsilu).astype(BF16)
        dkn_ref[...] = jnp.zeros_like(dkn_ref)
        dvp_ref[...] = jnp.zeros_like(dvp_ref)
        db_ref[...] = jnp.zeros_like(db_ref)

        def chunks(ci, _):
            done = []
            for u in range(unroll):
                t0 = pl.multiple_of((ci * unroll + u) * PAIR, PAIR)
                for h, lanes in enumerate(heads):
                    qc = qn_ref[pl.ds(t0, PAIR), lanes]
                    kw = kp_ref[pl.ds(t0, PAIR_W), lanes]
                    vw = vp_ref[pl.ds(t0, PAIR_W), lanes]
                    dob = do_ref[pl.ds(t0, PAIR), lanes]
                    probs = _chunk_scores(qc, kw, b_ref[h], t0, scale)
                    dprobs = _dot(dob, vw, NT)
                    dsc = probs * (dprobs - jnp.sum(probs * dprobs, axis=-1, keepdims=True))
                    dsb = (dsc * scale).astype(BF16)
                    done.append((t0, h, lanes, dsc, _dot(dsb, kw), _dot(dsb, qc, TN),
                                 _dot(probs.astype(BF16), dob, TN)))
            for t0, h, lanes, dsc, dqn, dkn, dvp in done:
                db_ref[h] += dsc
                dqn_ref[pl.ds(t0, PAIR), lanes] = dqn
                dkn_ref[pl.ds(t0, PAIR_W), lanes] += dkn
                dvp_ref[pl.ds(t0, PAIR_W), lanes] += dvp
            return 0

        lax.fori_loop(0, s // (PAIR * unroll), chunks, 0)
        o_ref[2] = dvp_ref[pl.ds(PAD_K, s), :].astype(BF16)

        @pl.when(pl.program_id(0) == 0)
        def _():
            dqg_ref[...] = jnp.zeros_like(dqg_ref)
            dkg_ref[...] = jnp.zeros_like(dkg_ref)

        for lanes in heads:
            qh, rq, _ = _qk_norm(p_ref[0, :, lanes], qg_ref[...])
            dq, dqg_rows = _norm_bwd(dqn_ref[:, lanes], qh, rq, qg_ref[...])
            o_ref[0, :, lanes] = dq.astype(BF16)
            dqg_ref[...] += jnp.sum(dqg_rows, axis=0, keepdims=True)
            kh, rk, _ = _qk_norm(p_ref[1, :, lanes], kg_ref[...])
            dk, dkg_rows = _norm_bwd(dkn_ref[pl.ds(PAD_K, s), lanes], kh, rk, kg_ref[...])
            o_ref[1, :, lanes] = dk.astype(BF16)
            dkg_ref[...] += jnp.sum(dkg_rows, axis=0, keepdims=True)

    return pl.pallas_call(
        body, name=f"chunk_backward_l{layer}", grid=(steps,),
        in_specs=[pl.BlockSpec((4, s, width), lambda h: (1, 0, h)),
                  pl.BlockSpec((hp, PAIR, PAIR_W), lambda h: (layer * steps + h, 0, 0)),
                  pl.BlockSpec((1, HEAD_DIM), lambda h: (0, 0)),
                  pl.BlockSpec((1, HEAD_DIM), lambda h: (0, 0)),
                  pl.BlockSpec((s, width), lambda h: (0, steps + h)),
                  pl.BlockSpec((s, width), lambda h: (0, steps + h)),
                  _any()],
        out_specs=(pl.BlockSpec((4, s, width), lambda h: (1, 0, h)),
                   pl.BlockSpec((hp, PAIR, PAIR_W), lambda h: (h, 0, 0)),
                   pl.BlockSpec((1, HEAD_DIM), lambda h: (0, 0)),
                   pl.BlockSpec((1, HEAD_DIM), lambda h: (0, 0))),
        out_shape=(jax.ShapeDtypeStruct(dproj.shape, BF16),
                   jax.ShapeDtypeStruct((e // HEAD_DIM, PAIR, PAIR_W), F32),
                   jax.ShapeDtypeStruct((1, HEAD_DIM), F32), jax.ShapeDtypeStruct((1, HEAD_DIM), F32)),
        input_output_aliases={6: 0},
        scratch_shapes=[pltpu.VMEM((s, width), BF16), pltpu.VMEM((s + PAD_K, width), BF16),
                        pltpu.VMEM((s + PAD_K, width), BF16), pltpu.VMEM((s, width), BF16),
                        pltpu.VMEM((s, width), F32), pltpu.VMEM((s + PAD_K, width), F32),
                        pltpu.VMEM((s + PAD_K, width), F32)],
        compiler_params=_params("arbitrary"),
    )(proj, bias, q_gain, k_gain, y, dmixed, dproj)


def _proj_bwd_input(dproj, w_all, x, g, dx, tie, layer):
    s, d = x.shape
    e = w_all.shape[2]
    tm = _tile(s, 512)

    def body(dp_ref, w_ref, x_ref, g_ref, dx_ref, tie_ref, o_ref, dg_ref, acc_ref):
        del tie_ref
        j = pl.program_id(1)

        @pl.when(j == 0)
        def _():
            acc_ref[...] = jnp.zeros_like(acc_ref)

        acc_ref[...] += _dot(dp_ref[...], w_ref[...], NT)

        @pl.when(jnp.logical_and(j == N_DEV - 1, pl.program_id(0) == 0))
        def _():
            dg_ref[...] = jnp.zeros_like(dg_ref)

        @pl.when(j == N_DEV - 1)
        def _():
            xv = x_ref[...]
            r = lax.rsqrt(jnp.mean(xv * xv, axis=-1, keepdims=True) + NORM_EPS)
            dxn, dg_rows = _norm_bwd(acc_ref[...], xv * r, r, g_ref[...])
            o_ref[...] = dx_ref[...] + dxn
            dg_ref[...] += jnp.sum(dg_rows, axis=0, keepdims=True)

    return pl.pallas_call(
        body, name=f"proj_dx_l{layer}", grid=(s // tm, N_DEV),
        in_specs=[pl.BlockSpec((None, tm, e), lambda m, j: (j, m, 0)),
                  pl.BlockSpec((None, d, e), lambda m, j: (j, 0, 0)),
                  pl.BlockSpec((tm, d), lambda m, j: (m, 0)),
                  pl.BlockSpec((1, d), lambda m, j: (0, 0)),
                  pl.BlockSpec((tm, d), lambda m, j: (m, 0)), _any()],
        out_specs=(pl.BlockSpec((tm, d), lambda m, j: (m, 0)), pl.BlockSpec((1, d), lambda m, j: (0, 0))),
        out_shape=(jax.ShapeDtypeStruct((s, d), F32), jax.ShapeDtypeStruct((1, d), F32)),
        scratch_shapes=[pltpu.VMEM((tm, d), F32)],
        compiler_params=_params("arbitrary", "arbitrary"),
    )(dproj, w_all, x, g, dx, tie)


def _proj_bwd_weight(h, dproj, layer):
    s, d = h.shape
    e = dproj.shape[2]
    td, tn = _tile(d, 1024), _tile(e, 1024)
    nb = e // tn

    def body(h_ref, dp_ref, o_ref):
        o_ref[...] = _dot(h_ref[...], dp_ref[...], TN).astype(BF16)

    return pl.pallas_call(
        body, name=f"proj_dw_l{layer}", grid=(d // td, N_DEV * nb),
        in_specs=[pl.BlockSpec((s, td), lambda i, n: (0, i)),
                  pl.BlockSpec((None, s, tn), lambda i, n: (n // nb, 0, n % nb))],
        out_specs=pl.BlockSpec((None, td, tn), lambda i, n: (n // nb, i, n % nb)),
        out_shape=jax.ShapeDtypeStruct((N_DEV, d, e), BF16),
        compiler_params=_params("parallel", "parallel"),
    )(h, dproj)


def _adamw_math(w, g, m, v):
    m = ADAM_B1 * m + (1.0 - ADAM_B1) * g
    v = ADAM_B2 * v + (1.0 - ADAM_B2) * (g * g)
    m_hat = m / (1.0 - ADAM_B1 ** ADAM_STEP)
    v_hat = v / (1.0 - ADAM_B2 ** ADAM_STEP)
    return -ADAM_LR * (m_hat / (jnp.sqrt(v_hat) + ADAM_EPS) + ADAM_WD * w), m, v


def _adamw_layer(parts, own, me, w, m, v, prev, layer, name):
    n_layers, rows, cols = w.shape
    n_parts = parts.shape[0]
    tr = _tile(rows, max(8, (256 * 1024) // cols))

    def body(me_ref, p_ref, own_ref, w_ref, m_ref, v_ref, *rest):
        g_ref, d_ref, nm_ref, nv_ref = rest[-4:]
        mine = own_ref[...].astype(F32)
        if n_parts == N_DEV:
            g = None
            for j in range(N_DEV):
                term = jnp.where(me_ref[0] == j, mine, p_ref[j].astype(F32))
                g = term if g is None else g + term
        else:
            g = mine
            for j in range(n_parts):
                g = g + p_ref[j].astype(F32)
        g_ref[...] = g
        d_ref[...], nm_ref[...], nv_ref[...] = _adamw_math(w_ref[...], g, m_ref[...], v_ref[...])

    blk = pl.BlockSpec((None, tr, cols), lambda i, me_ref: (layer, i, 0))
    out_shape = tuple(jax.ShapeDtypeStruct(w.shape, F32) for _ in range(4))
    in_specs = [pl.BlockSpec((n_parts, tr, cols), lambda i, me_ref: (0, i, 0)),
                pl.BlockSpec((None, tr, cols), lambda i, me_ref: (me_ref[0], i, 0)), blk, blk, blk]
    args = [me, parts, own, w, m, v]
    aliases = {}
    if prev is not None:
        in_specs += [_any()] * 4
        args += list(prev)
        aliases = {6 + k: k for k in range(4)}
    return pl.pallas_call(
        body, name=f"{name}_l{layer}",
        grid_spec=pltpu.PrefetchScalarGridSpec(
            num_scalar_prefetch=1, grid=(rows // tr,), in_specs=in_specs, out_specs=(blk, blk, blk, blk)),
        out_shape=out_shape, input_output_aliases=aliases,
        compiler_params=_params("parallel"),
    )(*args)


def _sum_slots(parts):
    def body(p_ref, o_ref):
        g = p_ref[0]
        for j in range(1, N_DEV):
            g = g + p_ref[j]
        o_ref[...] = g

    return pl.pallas_call(
        body, name="sum_small_grads",
        in_specs=[_vmem()], out_specs=_vmem(),
        out_shape=jax.ShapeDtypeStruct(parts.shape[1:], F32),
        compiler_params=_params(),
    )(parts)


def _adamw_small(w, g, m, v):
    def body(w_ref, g_ref, m_ref, v_ref, d_ref, nm_ref, nv_ref):
        d_ref[...], nm_ref[...], nv_ref[...] = _adamw_math(w_ref[...], g_ref[...], m_ref[...], v_ref[...])

    return pl.pallas_call(
        body, name="adamw_small",
        in_specs=[_vmem()] * 4, out_specs=(_vmem(),) * 3,
        out_shape=tuple(jax.ShapeDtypeStruct(w.shape, F32) for _ in range(3)),
        compiler_params=_params(),
    )(w, g, m, v)


def _pack_rows(arrays):
    rows = []
    for a in arrays:
        flat = a.reshape(-1)
        pad = (-flat.shape[0]) % (8 * LANES)
        rows.append(jnp.pad(flat, (0, pad)).reshape(-1, LANES))
    return jnp.concatenate(rows, axis=0)


def _unpack_rows(packed, like):
    out, r0 = [], 0
    for a in like:
        n = a.size
        nr = -(-n // (8 * LANES)) * 8
        out.append(packed[r0:r0 + nr].reshape(-1)[:n].reshape(a.shape))
        r0 += nr
    return out


def kernel(x, norm_g, w_in, q_norm_g, k_norm_g, rel_bias, w_out, loss_target, m_norm_g, m_w_in, m_q_norm_g, m_k_norm_g, m_rel_bias, m_w_out, v_norm_g, v_w_in, v_q_norm_g, v_k_norm_g, v_rel_bias, v_w_out):
    depth, d, e = w_in.shape
    r_out = w_out.shape[1]
    heads = e // HEAD_DIM
    rel_w = rel_bias.shape[2]
    x0 = x[0]
    target = loss_target[0]
    s = x0.shape[0]

    me = jnp.reshape(_flat(_my_place()), (1,)).astype(jnp.int32)

    casts = [(_cast_layer(w_in, me, l, "cast_w_in"), _cast_layer(w_out, me, l, "cast_w_out"))
             for l in range(depth)]

    def begin_gather(l, after):
        (win_b, win_land), (wout_b, wout_land) = casts[l]
        return _gather_send((win_b, wout_b), (win_land, wout_land), after, f"gather_send_l{l}")

    rel_all = _gather_small(rel_bias, [], "gather_rel_bias")
    sent = begin_gather(0, [rel_all])
    rel_full = jnp.transpose(rel_all, (1, 2, 0, 3)).reshape(depth * heads, N_DEV * rel_w)
    bias = jnp.transpose(_bias_expand(rel_full), (1, 0, 2))
    head_work = [bias] + [shard for cast in casts[1:] for shard, _ in cast]
    forwarded = _gather_forward(sent, head_work, "gather_forward_l0")

    xs, hs, projs, ys, mixes, tots, weights = [], [], [], [], [], [], []
    xl = x0
    for l in range(depth):
        win_all, wout_all = _gather_finish(forwarded, [xl, forwarded[-1]], f"gather_finish_l{l}")
        more = l + 1 < depth
        if more:
            sent = begin_gather(l + 1, [win_all])
        wout_full = wout_all.reshape(d, d)
        proj, h = _norm_proj(xl, norm_g[l:l + 1], win_all, sent[-1] if more else None, l)
        y, mixed, tot = _sb_forward(proj, l)
        if more:
            forwarded = _gather_forward(sent, [tot], f"gather_forward_l{l + 1}")
        y, mixed = _chunk_forward(proj, bias, q_norm_g[l:l + 1], k_norm_g[l:l + 1], y, mixed,
                                  forwarded[-1] if more else None, l)
        xs.append(xl), hs.append(h), projs.append(proj), ys.append(y), mixes.append(mixed), tots.append(tot)
        weights.append((win_all, wout_full))
        xl = _out_proj(mixed, wout_full, xl, l)

    dx, loss_parts = _loss_head(xl, target)
    loss = lax.psum(jnp.sum(loss_parts), AXES)

    dbias, dng, dqg, dkg = [None] * depth, [None] * depth, [None] * depth, [None] * depth
    res_in, res_out = None, None

    peer_slots = jnp.stack([_flat(_flip(_my_place(), k)) for k in SAME_CORE]).astype(jnp.int32)

    def finish_exchange(exchanging, after, l):
        sems, bufs, own = exchanging
        copies = N_DEV - 1 if own is None else 3
        bufs = _wait_copies(f"exchange_finish_l{l}", bufs, sems, 2, lambda refs, t: refs[t].at[0], copies, after)
        own = bufs[:2] if own is None else own
        rin, rout = bufs[2:]
        return (_adamw_layer(rin, own[0], me, w_in, m_w_in, v_w_in, res_in, l, "adamw_w_in"),
                _adamw_layer(rout, own[1], me, w_out, m_w_out, v_w_out, res_out, l, "adamw_w_out"))

    pending = []
    for l in reversed(range(depth)):
        win_all, wout_full = weights[l]
        dmixed = _out_proj_bwd_input(dx, wout_full, None, l)
        gwout = _out_proj_bwd_weight(mixes[l], dx, l).reshape(N_DEV, r_out, d)
        dproj = _sb_backward(projs[l], ys[l], dmixed, tots[l], l)
        dproj, dbias[l], dqg[l], dkg[l] = _chunk_backward(
            projs[l], bias, q_norm_g[l:l + 1], k_norm_g[l:l + 1], ys[l], dmixed, dproj, l)
        grads_l = (_proj_bwd_weight(hs[l], dproj, l), gwout)
        if l > 0:
            sems, bufs, token = _exchange_direct(grads_l, [], f"exchange_direct_l{l}")
            pending.append(((sems, bufs, None), l))
        else:
            sems, bufs, token = _exchange_to_sibling(grads_l, [], "exchange_sibling_l0")
            bufs = _wait_copies("exchange_sibling_wait_l0", bufs, sems, 2, lambda refs, t: refs[t].at[0], 4, [token])
            own, parts, lands = bufs[0:2], bufs[2:4], bufs[4:6]
            csums = [_chip_sums(own[t], parts[t], peer_slots, f"chip_sums_{t}_l0") for t in range(2)]
            sems, csums_lands, token = _exchange_to_chips(csums, lands, [], "exchange_chips_l0")
            pending.append(((sems, csums_lands, own), l))
        dx, dng[l] = _proj_bwd_input(dproj, win_all, xs[l], norm_g[l:l + 1], dx, token, l)
    tie = token
    for exchanging, l in pending[:-1]:
        res_in, res_out = finish_exchange(exchanging, [dx, tie], l)
    drel = _bias_grad(jnp.transpose(jnp.concatenate(dbias, axis=0), (1, 0, 2)), [tie])
    small_like = [norm_g, q_norm_g, k_norm_g, drel]
    mine = _pack_rows([jnp.concatenate(dng, axis=0), jnp.concatenate(dqg, axis=0),
                       jnp.concatenate(dkg, axis=0), drel])
    gathered = _gather_small(mine, [res_in[0], res_out[0]], "gather_small_grads")
    g_norm, g_qn, g_kn, g_rel_full = _unpack_rows(_sum_slots(gathered), small_like)
    my_block = _flat(_my_place())
    g_rel = lax.dynamic_slice_in_dim(g_rel_full.reshape(depth, heads, N_REL), my_block * rel_w, rel_w, axis=2)
    small_w = [norm_g, q_norm_g, k_norm_g, rel_bias]
    small = _adamw_small(_pack_rows(small_w), _pack_rows([g_norm, g_qn, g_kn, g_rel]),
                         _pack_rows([m_norm_g, m_q_norm_g, m_k_norm_g, m_rel_bias]),
                         _pack_rows([v_norm_g, v_q_norm_g, v_k_norm_g, v_rel_bias]))
    d_small, nm_small, nv_small = (_unpack_rows(p, small_w) for p in small)

    res_in, res_out = finish_exchange(pending[-1][0], [small[0], res_in[0], res_out[0]], 0)
    g_win, d_win, nm_win, nv_win = res_in
    g_wout, d_wout, nm_wout, nv_wout = res_out
    grads = (g_norm, g_win, g_qn, g_kn, g_rel, g_wout)

    def order(sm, big_in, big_out):
        return (sm[0], big_in, sm[1], sm[2], sm[3], big_out)

    return (loss, dx[None], *grads, *order(d_small, d_win, d_wout),
            *order(nm_small, nm_win, nm_wout), *order(nv_small, nv_win, nv_wout))
```

```python
import functools

import jax
import jax.numpy as jnp
from jax import lax
from jax.experimental import pallas as pl
from jax.experimental.pallas import tpu as pltpu

F32 = jnp.float32
BF16 = jnp.bfloat16
MESH_ID = pl.DeviceIdType.MESH
AXES = ("x", "y", "c")

N_DEV = 8
HEAD_DIM = 128
CHUNK = 64
LEFT_CHUNKS = 8
BAND_W = (LEFT_CHUNKS + 1) * CHUNK
PAD_K = LEFT_CHUNKS * CHUNK
REL_CLIP = 256
N_REL = REL_CLIP + CHUNK
NORM_EPS = 1e-6
NEG_BIG = -1e30
PAIR = 2 * CHUNK
PAIR_W = BAND_W + CHUNK
CHUNK_HEADS = 2
PAIR_UNROLL = 2
SB_Q = 512
SB_K = 128
SB_HEADS = 2
SB_UNROLL = 4
LANES = 128

ADAM_LR = 0.001
ADAM_B1 = 0.9
ADAM_B2 = 0.999
ADAM_EPS = 1e-08
ADAM_WD = 0.01
ADAM_STEP = 10

VMEM_LIMIT_BYTES = 56 * 1024 * 1024

NT = (((1,), (1,)), ((), ()))
TN = (((0,), (0,)), ((), ()))


def _params(*sem, **kw):
    return pltpu.CompilerParams(dimension_semantics=sem or None, vmem_limit_bytes=VMEM_LIMIT_BYTES, **kw)


def _any():
    return pl.BlockSpec(memory_space=pl.ANY)


def _vmem():
    return pl.BlockSpec(memory_space=pltpu.VMEM)


def _tile(n, want):
    return want if n % want == 0 else n


def _dot(a, b, dims=None):
    if dims is None:
        return jnp.dot(a, b, preferred_element_type=F32)
    return lax.dot_general(a, b, dims, preferred_element_type=F32)


def _split_dot(a, b, parts, dims=None):
    acc = None
    rest = a
    for _ in range(parts):
        piece = rest.astype(BF16)
        rest = rest - piece.astype(F32)
        term = _dot(piece, b, dims)
        acc = term if acc is None else acc + term
    return acc


def _log_sigmoid(z):
    return jnp.minimum(z, 0.0) - jnp.log(1.0 + jnp.exp(-jnp.abs(z)))


def _silu_and_grad(g):
    sig = jax.nn.sigmoid(g)
    return g * sig, sig * (1.0 + g * (1.0 - sig))


def _my_place():
    return lax.axis_index("x"), lax.axis_index("y"), lax.axis_index("c")


def _flat(place):
    return 4 * place[0] + 2 * place[1] + place[2]


def _flip(place, k):
    return tuple(1 - p if (k >> s) & 1 else p for p, s in zip(place, (2, 1, 0)))


def _cast_layer(w, me, layer, name):
    _, rows, cols = w.shape
    tr = _tile(rows, 1024)

    def body(me_ref, a_ref, shard_ref, land_ref):
        del me_ref
        shard_ref[...] = a_ref[...].astype(BF16)
        land_ref[...] = shard_ref[...]

    return pl.pallas_call(
        body, name=f"{name}_l{layer}",
        grid_spec=pltpu.PrefetchScalarGridSpec(
            num_scalar_prefetch=1, grid=(rows // tr,),
            in_specs=[pl.BlockSpec((None, tr, cols), lambda i, me_ref: (layer, i, 0))],
            out_specs=(pl.BlockSpec((tr, cols), lambda i, me_ref: (i, 0)),
                       pl.BlockSpec((None, tr, cols), lambda i, me_ref: (me_ref[0], i, 0)))),
        out_shape=(jax.ShapeDtypeStruct((rows, cols), BF16), jax.ShapeDtypeStruct((N_DEV, rows, cols), BF16)),
        compiler_params=_params("parallel"),
    )(me, w)


HBM_SPEC = pl.BlockSpec(memory_space=pltpu.HBM)
SEM_SPEC = pl.BlockSpec(memory_space=pltpu.SEMAPHORE)
SAME_CORE = (2, 4, 6)
SIBLING = 1
SPLIT_EFFECT = pltpu.SideEffectType.DATAFLOW_SIDE_EFFECTING


def _split_call(body, name, bufs, sems_in, sem_counts_out, after, token):
    bufs, sems_in, after = list(bufs), list(sems_in), list(after)
    nb, ni, no = len(bufs), len(sems_in), len(sem_counts_out)

    def wrapped(*refs):
        outs = nb + ni + len(after)
        body(refs[:nb], refs[nb:nb + ni], refs[outs:outs + no])
        if token:
            refs[-1][...] = jnp.zeros_like(refs[-1])

    out = pl.pallas_call(
        wrapped, name=name,
        in_specs=[HBM_SPEC] * nb + [SEM_SPEC] * ni + [_any()] * len(after),
        out_specs=tuple([SEM_SPEC] * no + [HBM_SPEC] * nb + ([_vmem()] if token else [])),
        out_shape=tuple([pltpu.SemaphoreType.DMA((c,)) for c in sem_counts_out]
                        + [pltpu.HBM(a.shape, a.dtype) for a in bufs]
                        + ([jax.ShapeDtypeStruct((8, LANES), F32)] if token else [])),
        input_output_aliases={i: no + i for i in range(nb)},
        compiler_params=pltpu.CompilerParams(has_side_effects=SPLIT_EFFECT),
    )(*[pltpu.with_memory_space_constraint(a, pltpu.HBM) for a in bufs], *sems_in, *after)
    return list(out[:no]), list(out[no:no + nb]), (out[-1] if token else None)


def _remote(src, dst, send_sems, recv_sems, i, to):
    return pltpu.make_async_remote_copy(src_ref=src, dst_ref=dst, send_sem=send_sems.at[i], recv_sem=recv_sems.at[i],
                                        device_id=to, device_id_type=MESH_ID)


def _wait_copies(name, bufs, sems, n, slot_of, count, after):
    def body(refs, sems_in, _):
        me = _my_place()
        for t in range(n):
            slot = slot_of(refs, t)
            for a in range(count):
                cp = _remote(slot, slot, sems_in[0], sems_in[1], t * count + a, me)
                cp.wait_send()
                cp.wait_recv()

    return _split_call(body, name, bufs, sems, (), after, False)[1]


def _exchange_direct(grads, after, name):
    n = len(grads)
    lands = [lax.empty(g.shape, g.dtype) for g in grads]

    def body(refs, _, sems_out):
        me = _my_place()
        for k in range(1, N_DEV):
            peer = _flip(me, k)
            for t in range(n):
                _remote(refs[t].at[_flat(peer)], refs[n + t].at[_flat(me)], *sems_out, t * (N_DEV - 1) + k - 1,
                        peer).start()

    return _split_call(body, name, list(grads) + lands, (), (n * (N_DEV - 1), n * (N_DEV - 1)), after, True)


def _exchange_to_sibling(grads, after, name):
    n = len(grads)
    parts = [lax.empty((3,) + g.shape[1:], g.dtype) for g in grads]
    lands = [lax.empty((4,) + g.shape[1:], g.dtype) for g in grads]

    def body(refs, _, sems_out):
        me = _my_place()
        sib = _flip(me, SIBLING)
        for t in range(n):
            g, part, land = refs[t], refs[n + t], refs[2 * n + t]
            _remote(g.at[_flat(sib)], land.at[0], *sems_out, 4 * t, sib).start()
            for a, k in enumerate(SAME_CORE):
                _remote(g.at[_flat(_flip(sib, k))], part.at[a], *sems_out, 4 * t + 1 + a, sib).start()

    return _split_call(body, name, list(grads) + parts + lands, (), (4 * n, 4 * n), after, True)


def _chip_sums(grads, parts, slots, name):
    _, rows, cols = grads.shape
    tr = _tile(rows, max(8, (512 * 1024) // cols))

    def body(slots_ref, g_ref, p_ref, o_ref):
        del slots_ref
        o_ref[...] = (g_ref[...].astype(F32) + p_ref[...].astype(F32)).astype(BF16)

    return pl.pallas_call(
        body, name=name,
        grid_spec=pltpu.PrefetchScalarGridSpec(
            num_scalar_prefetch=1, grid=(3, rows // tr),
            in_specs=[pl.BlockSpec((None, tr, cols), lambda a, i, slots_ref: (slots_ref[a], i, 0)),
                      pl.BlockSpec((None, tr, cols), lambda a, i, slots_ref: (a, i, 0))],
            out_specs=pl.BlockSpec((None, tr, cols), lambda a, i, slots_ref: (a, i, 0))),
        out_shape=jax.ShapeDtypeStruct((3, rows, cols), BF16),
        compiler_params=_params("parallel", "parallel"),
    )(slots, grads, parts)


def _exchange_to_chips(csums, lands, after, name):
    n = len(csums)

    def body(refs, _, sems_out):
        me = _my_place()
        for t in range(n):
            for a, k in enumerate(SAME_CORE):
                _remote(refs[t].at[a], refs[n + t].at[1 + a], *sems_out, 3 * t + a, _flip(me, k)).start()

    return _split_call(body, name, list(csums) + list(lands), (), (3 * n, 3 * n), after, True)


def _hbm_call(body, name, n_hbm, sems_in, sems_out, after, token, like):
    after = list(after)
    in_specs = [HBM_SPEC] * n_hbm + [SEM_SPEC] * len(sems_in) + [_any()] * len(after)
    out_specs = [SEM_SPEC] * len(sems_out) + [HBM_SPEC] * n_hbm + ([_vmem()] if token else [])
    out_shape = ([pltpu.SemaphoreType.DMA((c,)) for c in sems_out] + [pltpu.HBM(a.shape, a.dtype) for a in like]
                 + ([jax.ShapeDtypeStruct((8, LANES), F32)] if token else []))
    return in_specs, tuple(out_specs), tuple(out_shape), {i: len(sems_out) + i for i in range(n_hbm)}, after


def _gather_send(shards, lands, after, name):
    n = len(shards)
    peers = (SIBLING,) + SAME_CORE
    after = list(after)

    def body(*refs):
        me = _my_place()
        send_sems, recv_sems = refs[2 * n + len(after)], refs[2 * n + len(after) + 1]
        for a, k in enumerate(peers):
            for t in range(n):
                pltpu.make_async_remote_copy(
                    src_ref=refs[t], dst_ref=refs[n + t].at[_flat(me)],
                    send_sem=send_sems.at[t * 4 + a], recv_sem=recv_sems.at[t * 4 + a],
                    device_id=_flip(me, k), device_id_type=MESH_ID).start()
        refs[-1][...] = jnp.zeros_like(refs[-1])

    bufs = list(shards) + list(lands)
    in_specs, out_specs, out_shape, aliases, after = _hbm_call(body, name, 2 * n, (), (4 * n, 4 * n), after, True, bufs)
    out = pl.pallas_call(
        body, name=name, in_specs=in_specs, out_specs=out_specs, out_shape=out_shape,
        input_output_aliases=aliases, compiler_params=pltpu.CompilerParams(has_side_effects=SPLIT_EFFECT),
    )(*[pltpu.with_memory_space_constraint(a, pltpu.HBM) for a in bufs], *after)
    return out[0], out[1], out[2:2 + n], out[2 + n:2 + 2 * n], out[-1]


def _gather_forward(sent, after, name):
    send1, recv1, shards, lands, _ = sent
    n = len(shards)
    after = list(after)

    def body(*refs):
        me = _my_place()
        recv1_ref = refs[2 * n + 1]
        out0 = 2 * n + 2 + len(after)
        send2_ref, recv2_ref = refs[out0], refs[out0 + 1]
        for a, k in enumerate(SAME_CORE):
            owner = _flat(_flip(me, k))
            for t in range(n):
                slot = refs[n + t].at[owner]
                pltpu.make_async_remote_copy(
                    src_ref=refs[t], dst_ref=slot, send_sem=refs[2 * n].at[t * 4 + 1 + a],
                    recv_sem=recv1_ref.at[t * 4 + 1 + a], device_id=_flip(me, k), device_id_type=MESH_ID).wait_recv()
                pltpu.make_async_remote_copy(
                    src_ref=slot, dst_ref=slot, send_sem=send2_ref.at[t * 3 + a], recv_sem=recv2_ref.at[t * 3 + a],
                    device_id=_flip(me, SIBLING), device_id_type=MESH_ID).start()
        refs[-1][...] = jnp.zeros_like(refs[-1])

    bufs = list(shards) + list(lands)
    in_specs, out_specs, out_shape, aliases, after = _hbm_call(body, name, 2 * n, (4 * n, 4 * n), (3 * n, 3 * n), after,
                                                               True, bufs)
    out = pl.pallas_call(
        body, name=name, in_specs=in_specs, out_specs=out_specs, out_shape=out_shape,
        input_output_aliases=aliases, compiler_params=pltpu.CompilerParams(has_side_effects=SPLIT_EFFECT),
    )(*bufs, send1, recv1, *after)
    return (send1, recv1), (out[0], out[1]), out[2:2 + n], out[2 + n:2 + 2 * n], out[-1]


def _gather_finish(forwarded, after, name):
    (send1, recv1), (send2, recv2), shards, lands, _ = forwarded
    n = len(shards)
    after = list(after)

    def body(*refs):
        me = _my_place()
        send1_ref, recv1_ref, send2_ref, recv2_ref = refs[2 * n:2 * n + 4]
        sib = _flip(me, SIBLING)
        for t in range(n):
            for a in range(4):
                cp = pltpu.make_async_remote_copy(
                    src_ref=refs[t], dst_ref=refs[n + t].at[_flat(sib)], send_sem=send1_ref.at[t * 4 + a],
                    recv_sem=recv1_ref.at[t * 4 + a], device_id=sib, device_id_type=MESH_ID)
                cp.wait_send()
                if a == 0:
                    cp.wait_recv()
            for a in range(3):
                cp = pltpu.make_async_remote_copy(
                    src_ref=refs[t], dst_ref=refs[n + t].at[_flat(sib)], send_sem=send2_ref.at[t * 3 + a],
                    recv_sem=recv2_ref.at[t * 3 + a], device_id=sib, device_id_type=MESH_ID)
                cp.wait_send()
                cp.wait_recv()

    bufs = list(shards) + list(lands)
    in_specs, out_specs, out_shape, aliases, after = _hbm_call(body, name, 2 * n, (4 * n, 4 * n, 3 * n, 3 * n), (), after,
                                                               False, bufs)
    out = pl.pallas_call(
        body, name=name, in_specs=in_specs, out_specs=out_specs, out_shape=out_shape,
        input_output_aliases=aliases, compiler_params=pltpu.CompilerParams(has_side_effects=SPLIT_EFFECT),
    )(*bufs, send1, recv1, send2, recv2, *after)
    return out[n:]


def _gather_small(v, after, name):
    after = list(after)

    def body(v_ref, *rest):
        o_ref, send_sems, recv_sems = rest[-3:]
        me = _my_place()
        o_ref[_flat(me)] = v_ref[...]
        copies = []
        for k in range(1, N_DEV):
            copies.append(pltpu.make_async_remote_copy(
                src_ref=v_ref, dst_ref=o_ref.at[_flat(me)],
                send_sem=send_sems.at[k - 1], recv_sem=recv_sems.at[k - 1],
                device_id=_flip(me, k), device_id_type=MESH_ID))
        for cp in copies:
            cp.start()
        for cp in copies:
            cp.wait()

    return pl.pallas_call(
        body, name=name,
        in_specs=[_vmem()] + [_any()] * len(after), out_specs=_vmem(),
        out_shape=jax.ShapeDtypeStruct((N_DEV,) + v.shape, v.dtype),
        scratch_shapes=[pltpu.SemaphoreType.DMA((7,)), pltpu.SemaphoreType.DMA((7,))],
        compiler_params=_params(has_side_effects=True),
    )(v, *after)


def _rel_onehot(row):
    r_io = lax.broadcasted_iota(jnp.int32, (N_REL, PAIR_W), 0)
    p_io = lax.broadcasted_iota(jnp.int32, (N_REL, PAIR_W), 1)
    band_col = p_io - (row // CHUNK) * CHUNK
    in_band = jnp.logical_and(band_col >= 0, band_col < BAND_W)
    idx = jnp.clip(PAD_K + row % CHUNK - band_col, -(CHUNK - 1), REL_CLIP) + (CHUNK - 1)
    return jnp.logical_and(r_io == idx, in_band).astype(BF16), in_band[0:1]


def _bias_expand(rel):
    lh = rel.shape[0]

    def body(rel_ref, o_ref):
        onehot, in_band = _rel_onehot(pl.program_id(0))
        o_ref[...] = jnp.where(in_band, _split_dot(rel_ref[...], onehot, 3), NEG_BIG)

    return pl.pallas_call(
        body, name="bias_expand", grid=(PAIR,),
        in_specs=[pl.BlockSpec((lh, N_REL), lambda i: (0, 0))],
        out_specs=pl.BlockSpec((None, lh, PAIR_W), lambda i: (i, 0, 0)),
        out_shape=jax.ShapeDtypeStruct((PAIR, lh, PAIR_W), F32),
        compiler_params=_params("parallel"),
    )(rel)


SHEAR_W = PAIR_W + LANES
BIAS_HEADS = 8


def _bias_grad(dbias, after):
    lh = dbias.shape[0]
    hb = _tile(lh, BIAS_HEADS)
    after = list(after)

    def body(db_ref, *rest):
        o_ref = rest[-1]
        a_io = lax.broadcasted_iota(jnp.int32, (PAIR, PAIR), 0)
        b_io = lax.broadcasted_iota(jnp.int32, (PAIR, PAIR), 1)
        flip_rows = (a_io + b_io == PAIR - 1).astype(BF16)
        diags = []
        for j in range(hb):
            rest_part = jnp.concatenate([db_ref[j], jnp.zeros((PAIR, SHEAR_W - PAIR_W), F32)], axis=1)
            flipped = None
            for _ in range(3):
                piece = rest_part.astype(BF16)
                rest_part = rest_part - piece.astype(F32)
                term = _dot(flip_rows, piece)
                flipped = term if flipped is None else flipped + term
            sheared = pltpu.roll(flipped, 0, 1, stride=1, stride_axis=0)
            diags.append(jnp.sum(sheared, axis=0, keepdims=True))
        c_io = lax.broadcasted_iota(jnp.int32, (SHEAR_W, N_REL), 0)
        r_io = lax.broadcasted_iota(jnp.int32, (SHEAR_W, N_REL), 1)
        entry = jnp.clip(PAD_K + (PAIR - 1) - c_io, -(CHUNK - 1), REL_CLIP) + (CHUNK - 1)
        o_ref[...] = _split_dot(jnp.concatenate(diags, axis=0), (r_io == entry).astype(BF16), 3)

    return pl.pallas_call(
        body, name="bias_grad", grid=(lh // hb,),
        in_specs=[pl.BlockSpec((hb, PAIR, PAIR_W), lambda i: (i, 0, 0))] + [_any()] * len(after),
        out_specs=pl.BlockSpec((hb, N_REL), lambda i: (i, 0)),
        out_shape=jax.ShapeDtypeStruct((lh, N_REL), F32),
        compiler_params=_params("parallel"),
    )(dbias, *after)


def _norm_proj(x, g, w_all, tie, layer):
    s, d = x.shape
    e = w_all.shape[2]
    tm, tn = _tile(s, 1024), _tile(e, 1024)
    nb = e // tn
    ties = [] if tie is None else [tie]

    def body(x_ref, g_ref, w_ref, *rest):
        proj_ref, h_ref = rest[-2:]

        @pl.when(pl.program_id(1) == 0)
        def _():
            xv = x_ref[...]
            r = lax.rsqrt(jnp.mean(xv * xv, axis=-1, keepdims=True) + NORM_EPS)
            h_ref[...] = ((xv * r) * g_ref[...]).astype(BF16)

        proj_ref[...] = _dot(h_ref[...], w_ref[...])

    return pl.pallas_call(
        body, name=f"norm_proj_l{layer}", grid=(s // tm, N_DEV * nb),
        in_specs=[pl.BlockSpec((tm, d), lambda m, n: (m, 0)),
                  pl.BlockSpec((1, d), lambda m, n: (0, 0)),
                  pl.BlockSpec((None, d, tn), lambda m, n: (n // nb, 0, n % nb))] + [_any()] * len(ties),
        out_specs=(pl.BlockSpec((None, tm, tn), lambda m, n: (n // nb, m, n % nb)),
                   pl.BlockSpec((tm, d), lambda m, n: (m, 0))),
        out_shape=(jax.ShapeDtypeStruct((N_DEV, s, e), F32), jax.ShapeDtypeStruct((s, d), BF16)),
        compiler_params=_params("parallel", "arbitrary"),
    )(x, g, w_all, *ties)


def _sb_forward(proj, layer):
    _, s, e = proj.shape
    hp = _tile(e // HEAD_DIM, SB_HEADS)
    width = hp * HEAD_DIM
    tq = _tile(s, SB_Q)
    diag_tiles = tq // SB_K
    scale = HEAD_DIM ** -0.5

    def body(p_ref, y_ref, mix_ref, tot_ref, kb_ref, vb_ref):
        kb_ref[...] = p_ref[1].astype(BF16)
        vb_ref[...] = p_ref[2].astype(BF16)
        row = lax.broadcasted_iota(jnp.int32, (tq, SB_K), 0)
        col = lax.broadcasted_iota(jnp.int32, (tq, SB_K), 1)
        kj = lax.broadcasted_iota(jnp.int32, (SB_K, SB_K), 0)
        ks = lax.broadcasted_iota(jnp.int32, (SB_K, SB_K), 1)
        later = (kj > ks).astype(BF16)

        def q_block(qi, _):
            t0 = pl.multiple_of(qi * tq, tq)
            qb = [p_ref[0, pl.ds(t0, tq), h * HEAD_DIM:(h + 1) * HEAD_DIM].astype(BF16) for h in range(hp)]

            def tile(s0, state, causal):
                out = []
                for h in range(hp):
                    carry, acc = state[h]
                    lanes = slice(h * HEAD_DIM, (h + 1) * HEAD_DIM)
                    z = _dot(qb[h], kb_ref[pl.ds(s0, SB_K), lanes], NT) * scale
                    ls = _log_sigmoid(z)
                    stay = ls - z
                    if causal is not None:
                        stay = jnp.where(causal, stay, 0.0)
                    w = jnp.exp(ls + carry + _split_dot(stay, later, 2))
                    if causal is not None:
                        w = jnp.where(causal, w, 0.0)
                    acc = acc + _dot(w.astype(BF16), vb_ref[pl.ds(s0, SB_K), lanes])
                    out.append((carry + jnp.sum(stay, axis=1, keepdims=True), acc))
                return tuple(out)

            state = tuple((jnp.zeros((tq, 1), F32), jnp.zeros((tq, HEAD_DIM), F32)) for _ in range(hp))
            for dt in reversed(range(diag_tiles)):
                state = tile(t0 + dt * SB_K, state, col + dt * SB_K < row)

            def k_blocks(j, st):
                for u in range(SB_UNROLL):
                    st = tile(pl.multiple_of((diag_tiles * qi - 1 - SB_UNROLL * j - u) * SB_K, SB_K), st, None)
                return st

            state = lax.fori_loop(0, diag_tiles * qi // SB_UNROLL, k_blocks, state)
            silu, _ = _silu_and_grad(p_ref[3, pl.ds(t0, tq), :])
            for h in range(hp):
                lanes = slice(h * HEAD_DIM, (h + 1) * HEAD_DIM)
                y_ref[pl.ds(t0, tq), lanes] = state[h][1]
                mix_ref[pl.ds(t0, tq), lanes] = (state[h][1] * silu[:, lanes]).astype(BF16)
                tot_ref[h, pl.ds(t0, tq), :] = state[h][0]
            return 0

        lax.fori_loop(0, s // tq, q_block, 0)

    return pl.pallas_call(
        body, name=f"sb_forward_l{layer}", grid=(e // width,),
        in_specs=[pl.BlockSpec((4, s, width), lambda h: (0, 0, h))],
        out_specs=(pl.BlockSpec((s, width), lambda h: (0, h)),
                   pl.BlockSpec((s, width), lambda h: (0, h)),
                   pl.BlockSpec((hp, s, 1), lambda h: (h, 0, 0))),
        out_shape=(jax.ShapeDtypeStruct((s, 2 * e), F32), jax.ShapeDtypeStruct((s, 2 * e), BF16),
                   jax.ShapeDtypeStruct((e // HEAD_DIM, s, 1), F32)),
        scratch_shapes=[pltpu.VMEM((s, width), BF16), pltpu.VMEM((s, width), BF16)],
        compiler_params=_params("parallel"),
    )(proj)


def _qk_norm(t, gain):
    r = lax.rsqrt(jnp.mean(t * t, axis=-1, keepdims=True) + NORM_EPS)
    return t * r, r, (t * r) * gain


def _chunk_scores(qc, kw, bias, t0, scale):
    sc = _dot(qc, kw, NT) * scale + bias
    col = lax.broadcasted_iota(jnp.int32, (PAIR, PAIR_W), 1)
    sc = jnp.where(col + t0 >= PAD_K, sc, NEG_BIG)
    ex = jnp.exp(sc - jnp.max(sc, axis=-1, keepdims=True))
    return ex / jnp.sum(ex, axis=-1, keepdims=True)


def _chunk_forward(proj, bias, q_gain, k_gain, y, mixed, tie, layer):
    ties = [] if tie is None else [tie]
    _, s, e = proj.shape
    hp = _tile(e // HEAD_DIM, CHUNK_HEADS)
    width = hp * HEAD_DIM
    steps = e // width
    unroll = _tile(s // PAIR, PAIR_UNROLL)
    scale = HEAD_DIM ** -0.5
    heads = [slice(h * HEAD_DIM, (h + 1) * HEAD_DIM) for h in range(hp)]

    def body(p_ref, b_ref, qg_ref, kg_ref, *rest):
        y_ref, mix_ref, qn_ref, kp_ref, vp_ref = rest[-5:]
        kp_ref[pl.ds(0, PAD_K), :] = jnp.zeros((PAD_K, width), BF16)
        vp_ref[pl.ds(0, PAD_K), :] = jnp.zeros((PAD_K, width), BF16)
        vp_ref[pl.ds(PAD_K, s), :] = p_ref[2].astype(BF16)
        for lanes in heads:
            qn_ref[:, lanes] = _qk_norm(p_ref[0, :, lanes], qg_ref[...])[2].astype(BF16)
            kp_ref[pl.ds(PAD_K, s), lanes] = _qk_norm(p_ref[1, :, lanes], kg_ref[...])[2].astype(BF16)

        def chunks(ci, _):
            done = []
            for u in range(unroll):
                t0 = pl.multiple_of((ci * unroll + u) * PAIR, PAIR)
                silu, _ = _silu_and_grad(p_ref[3, pl.ds(t0, PAIR), :])
                for h, lanes in enumerate(heads):
                    probs = _chunk_scores(qn_ref[pl.ds(t0, PAIR), lanes], kp_ref[pl.ds(t0, PAIR_W), lanes],
                                          b_ref[h], t0, scale)
                    out = _dot(probs.astype(BF16), vp_ref[pl.ds(t0, PAIR_W), lanes])
                    done.append((t0, lanes, out, (out * silu[:, lanes]).astype(BF16)))
            for t0, lanes, out, gated in done:
                y_ref[pl.ds(t0, PAIR), lanes] = out
                mix_ref[pl.ds(t0, PAIR), lanes] = gated
            return 0

        lax.fori_loop(0, s // (PAIR * unroll), chunks, 0)

    return pl.pallas_call(
        body, name=f"chunk_forward_l{layer}", grid=(steps,),
        in_specs=[pl.BlockSpec((4, s, width), lambda h: (1, 0, h)),
                  pl.BlockSpec((hp, PAIR, PAIR_W), lambda h: (layer * steps + h, 0, 0)),
                  pl.BlockSpec((1, HEAD_DIM), lambda h: (0, 0)),
                  pl.BlockSpec((1, HEAD_DIM), lambda h: (0, 0)),
                  _any(), _any()] + [_any()] * len(ties),
        out_specs=(pl.BlockSpec((s, width), lambda h: (0, steps + h)),
                   pl.BlockSpec((s, width), lambda h: (0, steps + h))),
        out_shape=(jax.ShapeDtypeStruct(y.shape, F32), jax.ShapeDtypeStruct(mixed.shape, BF16)),
        input_output_aliases={4: 0, 5: 1},
        scratch_shapes=[pltpu.VMEM((s, width), BF16), pltpu.VMEM((s + PAD_K, width), BF16),
                        pltpu.VMEM((s + PAD_K, width), BF16)],
        compiler_params=_params("parallel"),
    )(proj, bias, q_gain, k_gain, y, mixed, *ties)


def _out_proj(mixed, w, x, layer):
    s, d = x.shape
    tm, tn = _tile(s, 1024), _tile(d, 1024)

    def body(a_ref, w_ref, x_ref, o_ref):
        o_ref[...] = x_ref[...] + _dot(a_ref[...], w_ref[...])

    return pl.pallas_call(
        body, name=f"out_proj_l{layer}", grid=(s // tm, d // tn),
        in_specs=[pl.BlockSpec((tm, d), lambda m, n: (m, 0)),
                  pl.BlockSpec((d, tn), lambda m, n: (0, n)),
                  pl.BlockSpec((tm, tn), lambda m, n: (m, n))],
        out_specs=pl.BlockSpec((tm, tn), lambda m, n: (m, n)),
        out_shape=jax.ShapeDtypeStruct((s, d), F32),
        compiler_params=_params("parallel", "parallel"),
    )(mixed, w, x)


def _loss_head(y, target):
    s, d = y.shape
    tm = _tile(s, 256)

    def body(y_ref, t_ref, dy_ref, part_ref):
        diff = y_ref[...] - t_ref[...]
        dy_ref[...] = diff * (1.0 / d)
        sq = (diff * diff).reshape(tm // 8, 8, d).sum(axis=0)
        acc = sq[:, 0:LANES]
        for j in range(1, d // LANES):
            acc = acc + sq[:, j * LANES:(j + 1) * LANES]
        part_ref[...] = acc * (0.5 / d)

    return pl.pallas_call(
        body, name="loss_head", grid=(s // tm,),
        in_specs=[pl.BlockSpec((tm, d), lambda i: (i, 0)), pl.BlockSpec((tm, d), lambda i: (i, 0))],
        out_specs=(pl.BlockSpec((tm, d), lambda i: (i, 0)), pl.BlockSpec((None, 8, LANES), lambda i: (i, 0, 0))),
        out_shape=(jax.ShapeDtypeStruct((s, d), F32), jax.ShapeDtypeStruct((s // tm, 8, LANES), F32)),
        compiler_params=_params("parallel"),
    )(y, target)


def _out_proj_bwd_input(dx, w, tie, layer):
    s, d = dx.shape
    tm, tn = _tile(s, 1024), _tile(d, 1024)
    ties = [] if tie is None else [tie]

    def body(dx_ref, w_ref, *rest):
        rest[-1][...] = _dot(dx_ref[...].astype(BF16), w_ref[...], NT)

    return pl.pallas_call(
        body, name=f"out_proj_dx_l{layer}", grid=(s // tm, d // tn),
        in_specs=[pl.BlockSpec((tm, d), lambda m, n: (m, 0)),
                  pl.BlockSpec((tn, d), lambda m, n: (n, 0))] + [_any()] * len(ties),
        out_specs=pl.BlockSpec((tm, tn), lambda m, n: (m, n)),
        out_shape=jax.ShapeDtypeStruct((s, d), F32),
        compiler_params=_params("parallel", "parallel"),
    )(dx, w, *ties)


def _out_proj_bwd_weight(mixed, dx, layer):
    s, d = dx.shape
    te, tn = _tile(d, 1024), _tile(d, 1024)

    def body(a_ref, dx_ref, o_ref):
        o_ref[...] = _dot(a_ref[...], dx_ref[...].astype(BF16), TN).astype(BF16)

    return pl.pallas_call(
        body, name=f"out_proj_dw_l{layer}", grid=(d // te, d // tn),
        in_specs=[pl.BlockSpec((s, te), lambda i, n: (0, i)), pl.BlockSpec((s, tn), lambda i, n: (0, n))],
        out_specs=pl.BlockSpec((te, tn), lambda i, n: (i, n)),
        out_shape=jax.ShapeDtypeStruct((d, d), BF16),
        compiler_params=_params("parallel", "parallel"),
    )(mixed, dx)


def _sb_backward(proj, y, dmixed, tot, layer):
    _, s, e = proj.shape
    hp = _tile(e // HEAD_DIM, SB_HEADS)
    width = hp * HEAD_DIM
    tq = _tile(s, SB_Q)
    diag_tiles = tq // SB_K
    scale = HEAD_DIM ** -0.5

    def body(p_ref, y_ref, dm_ref, tot_ref, o_ref, kb_ref, vb_ref, do_ref, dk_ref, dv_ref):
        kb_ref[...] = p_ref[1].astype(BF16)
        vb_ref[...] = p_ref[2].astype(BF16)
        silu, dsilu = _silu_and_grad(p_ref[3])
        dm = dm_ref[...]
        do_ref[...] = (dm * silu).astype(BF16)
        o_ref[3] = (dm * y_ref[...] * dsilu).astype(BF16)
        dk_ref[...] = jnp.zeros_like(dk_ref)
        dv_ref[...] = jnp.zeros_like(dv_ref)
        row = lax.broadcasted_iota(jnp.int32, (tq, SB_K), 0)
        col = lax.broadcasted_iota(jnp.int32, (tq, SB_K), 1)
        kj = lax.broadcasted_iota(jnp.int32, (SB_K, SB_K), 0)
        ks = lax.broadcasted_iota(jnp.int32, (SB_K, SB_K), 1)
        upto = (kj <= ks).astype(BF16)
        before = (kj < ks).astype(BF16)

        def q_block(qi, _):
            t0 = pl.multiple_of(qi * tq, tq)
            heads = [slice(h * HEAD_DIM, (h + 1) * HEAD_DIM) for h in range(hp)]
            qb = [p_ref[0, pl.ds(t0, tq), lanes].astype(BF16) for lanes in heads]
            dob = [do_ref[pl.ds(t0, tq), lanes] for lanes in heads]
            total = [tot_ref[h, pl.ds(t0, tq), :] for h in range(hp)]

            def tile(s0, state, causal):
                out, adds = [], []
                for h, lanes in enumerate(heads):
                    stay_sum, dlw_sum, dq = state[h]
                    kt = kb_ref[pl.ds(s0, SB_K), lanes]
                    vt = vb_ref[pl.ds(s0, SB_K), lanes]
                    z = _dot(qb[h], kt, NT) * scale
                    ls = _log_sigmoid(z)
                    stay = ls - z
                    if causal is not None:
                        stay = jnp.where(causal, stay, 0.0)
                    after = total[h] - (stay_sum + _split_dot(stay, upto, 2))
                    w = jnp.exp(ls + after)
                    if causal is not None:
                        w = jnp.where(causal, w, 0.0)
                    dlw = _dot(dob[h], vt, NT) * w
                    prior = dlw_sum + _split_dot(dlw, before, 2)
                    sig = jnp.exp(ls)
                    dz = (dlw * (1.0 - sig) - sig * prior) * scale
                    if causal is not None:
                        dz = jnp.where(causal, dz, 0.0)
                    dzb = dz.astype(BF16)
                    dq = dq + _dot(dzb, kt)
                    adds.append((lanes, _dot(dzb, qb[h], TN), _dot(w.astype(BF16), dob[h], TN)))
                    out.append((stay_sum + jnp.sum(stay, axis=1, keepdims=True),
                                dlw_sum + jnp.sum(dlw, axis=1, keepdims=True), dq))
                for lanes, dk, dv in adds:
                    dk_ref[pl.ds(s0, SB_K), lanes] += dk
                    dv_ref[pl.ds(s0, SB_K), lanes] += dv
                return tuple(out)

            def k_blocks(j, st):
                for u in range(SB_UNROLL):
                    st = tile(pl.multiple_of((SB_UNROLL * j + u) * SB_K, SB_K), st, None)
                return st

            zero = jnp.zeros((tq, 1), F32)
            state = tuple((zero, zero, jnp.zeros((tq, HEAD_DIM), F32)) for _ in range(hp))
            state = lax.fori_loop(0, diag_tiles * qi // SB_UNROLL, k_blocks, state)
            for dt in range(diag_tiles):
                state = tile(t0 + dt * SB_K, state, col + dt * SB_K < row)
            for h, lanes in enumerate(heads):
                o_ref[0, pl.ds(t0, tq), lanes] = state[h][2].astype(BF16)
            return 0

        lax.fori_loop(0, s // tq, q_block, 0)
        o_ref[1] = dk_ref[...].astype(BF16)
        o_ref[2] = dv_ref[...].astype(BF16)

    return pl.pallas_call(
        body, name=f"sb_backward_l{layer}", grid=(e // width,),
        in_specs=[pl.BlockSpec((4, s, width), lambda h: (0, 0, h)),
                  pl.BlockSpec((s, width), lambda h: (0, h)),
                  pl.BlockSpec((s, width), lambda h: (0, h)),
                  pl.BlockSpec((hp, s, 1), lambda h: (h, 0, 0))],
        out_specs=pl.BlockSpec((4, s, width), lambda h: (0, 0, h)),
        out_shape=jax.ShapeDtypeStruct((N_DEV, s, e), BF16),
        scratch_shapes=[pltpu.VMEM((s, width), BF16), pltpu.VMEM((s, width), BF16),
                        pltpu.VMEM((s, width), BF16), pltpu.VMEM((s, width), F32),
                        pltpu.VMEM((s, width), F32)],
        compiler_params=_params("parallel"),
    )(proj, y, dmixed, tot)


def _norm_bwd(dn, xh, r, gain):
    dxh = dn * gain
    return r * (dxh - xh * jnp.mean(dxh * xh, axis=-1, keepdims=True)), dn * xh


def _chunk_backward(proj, bias, q_gain, k_gain, y, dmixed, dproj, layer):
    _, s, e = proj.shape
    hp = _tile(e // HEAD_DIM, CHUNK_HEADS)
    width = hp * HEAD_DIM
    steps = e // width
    unroll = _tile(s // PAIR, PAIR_UNROLL)
    scale = HEAD_DIM ** -0.5
    heads = [slice(h * HEAD_DIM, (h + 1) * HEAD_DIM) for h in range(hp)]

    def body(p_ref, b_ref, qg_ref, kg_ref, y_ref, dm_ref, dp_in, o_ref, db_ref, dqg_ref, dkg_ref,
             qn_ref, kp_ref, vp_ref, do_ref, dqn_ref, dkn_ref, dvp_ref):
        del dp_in
        kp_ref[pl.ds(0, PAD_K), :] = jnp.zeros((PAD_K, width), BF16)
        vp_ref[pl.ds(0, PAD_K), :] = jnp.zeros((PAD_K, width), BF16)
        vp_ref[pl.ds(PAD_K, s), :] = p_ref[2].astype(BF16)
        for lanes in heads:
            qn_ref[:, lanes] = _qk_norm(p_ref[0, :, lanes], qg_ref[...])[2].astype(BF16)
            kp_ref[pl.ds(PAD_K, s), lanes] = _qk_norm(p_ref[1, :, lanes], kg_ref[...])[2].astype(BF16)
        silu, dsilu = _silu_and_grad(p_ref[3])
        dm = dm_ref[...]
        do_ref[...] = (dm * silu).astype(BF16)
        o_ref[3] = (dm * y_ref[...] * dsilu).astype(BF16)
        dkn_ref[...] = jnp.zeros_like(dkn_ref)
        dvp_ref[...] = jnp.zeros_like(dvp_ref)
        db_ref[...] = jnp.zeros_like(db_ref)

        def chunks(ci, _):
            done = []
            for u in range(unroll):
                t0 = pl.multiple_of((ci * unroll + u) * PAIR, PAIR)
                for h, lanes in enumerate(heads):
                    qc = qn_ref[pl.ds(t0, PAIR), lanes]
                    kw = kp_ref[pl.ds(t0, PAIR_W), lanes]
                    vw = vp_ref[pl.ds(t0, PAIR_W), lanes]
                    dob = do_ref[pl.ds(t0, PAIR), lanes]
                    probs = _chunk_scores(qc, kw, b_ref[h], t0, scale)
                    dprobs = _dot(dob, vw, NT)
                    dsc = probs * (dprobs - jnp.sum(probs * dprobs, axis=-1, keepdims=True))
                    dsb = (dsc * scale).astype(BF16)
                    done.append((t0, h, lanes, dsc, _dot(dsb, kw), _dot(dsb, qc, TN),
                                 _dot(probs.astype(BF16), dob, TN)))
            for t0, h, lanes, dsc, dqn, dkn, dvp in done:
                db_ref[h] += dsc
                dqn_ref[pl.ds(t0, PAIR), lanes] = dqn
                dkn_ref[pl.ds(t0, PAIR_W), lanes] += dkn
                dvp_ref[pl.ds(t0, PAIR_W), lanes] += dvp
            return 0

        lax.fori_loop(0, s // (PAIR * unroll), chunks, 0)
        o_ref[2] = dvp_ref[pl.ds(PAD_K, s), :].astype(BF16)

        @pl.when(pl.program_id(0) == 0)
        def _():
            dqg_ref[...] = jnp.zeros_like(dqg_ref)
            dkg_ref[...] = jnp.zeros_like(dkg_ref)

        for lanes in heads:
            qh, rq, _ = _qk_norm(p_ref[0, :, lanes], qg_ref[...])
            dq, dqg_rows = _norm_bwd(dqn_ref[:, lanes], qh, rq, qg_ref[...])
            o_ref[0, :, lanes] = dq.astype(BF16)
            dqg_ref[...] += jnp.sum(dqg_rows, axis=0, keepdims=True)
            kh, rk, _ = _qk_norm(p_ref[1, :, lanes], kg_ref[...])
            dk, dkg_rows = _norm_bwd(dkn_ref[pl.ds(PAD_K, s), lanes], kh, rk, kg_ref[...])
            o_ref[1, :, lanes] = dk.astype(BF16)
            dkg_ref[...] += jnp.sum(dkg_rows, axis=0, keepdims=True)

    return pl.pallas_call(
        body, name=f"chunk_backward_l{layer}", grid=(steps,),
        in_specs=[pl.BlockSpec((4, s, width), lambda h: (1, 0, h)),
                  pl.BlockSpec((hp, PAIR, PAIR_W), lambda h: (layer * steps + h, 0, 0)),
                  pl.BlockSpec((1, HEAD_DIM), lambda h: (0, 0)),
                  pl.BlockSpec((1, HEAD_DIM), lambda h: (0, 0)),
                  pl.BlockSpec((s, width), lambda h: (0, steps + h)),
                  pl.BlockSpec((s, width), lambda h: (0, steps + h)),
                  _any()],
        out_specs=(pl.BlockSpec((4, s, width), lambda h: (1, 0, h)),
                   pl.BlockSpec((hp, PAIR, PAIR_W), lambda h: (h, 0, 0)),
                   pl.BlockSpec((1, HEAD_DIM), lambda h: (0, 0)),
                   pl.BlockSpec((1, HEAD_DIM), lambda h: (0, 0))),
        out_shape=(jax.ShapeDtypeStruct(dproj.shape, BF16),
                   jax.ShapeDtypeStruct((e // HEAD_DIM, PAIR, PAIR_W), F32),
                   jax.ShapeDtypeStruct((1, HEAD_DIM), F32), jax.ShapeDtypeStruct((1, HEAD_DIM), F32)),
        input_output_aliases={6: 0},
        scratch_shapes=[pltpu.VMEM((s, width), BF16), pltpu.VMEM((s + PAD_K, width), BF16),
                        pltpu.VMEM((s + PAD_K, width), BF16), pltpu.VMEM((s, width), BF16),
                        pltpu.VMEM((s, width), F32), pltpu.VMEM((s + PAD_K, width), F32),
                        pltpu.VMEM((s + PAD_K, width), F32)],
        compiler_params=_params("arbitrary"),
    )(proj, bias, q_gain, k_gain, y, dmixed, dproj)


def _proj_bwd_input(dproj, w_all, x, g, dx, tie, layer):
    s, d = x.shape
    e = w_all.shape[2]
    tm = _tile(s, 512)
    gs = 1
    steps = N_DEV // gs

    def body(dp_ref, w_ref, x_ref, g_ref, dx_ref, tie_ref, o_ref, dg_ref, acc_ref):
        del tie_ref
        j = pl.program_id(1)

        @pl.when(j == 0)
        def _():
            acc_ref[...] = jnp.zeros_like(acc_ref)

        part = _dot(dp_ref[0], w_ref[0], NT)
        for k in range(1, gs):
            part = part + _dot(dp_ref[k], w_ref[k], NT)
        acc_ref[...] += part

        @pl.when(jnp.logical_and(j == steps - 1, pl.program_id(0) == 0))
        def _():
            dg_ref[...] = jnp.zeros_like(dg_ref)

        @pl.when(j == steps - 1)
        def _():
            xv = x_ref[...]
            r = lax.rsqrt(jnp.mean(xv * xv, axis=-1, keepdims=True) + NORM_EPS)
            dxn, dg_rows = _norm_bwd(acc_ref[...], xv * r, r, g_ref[...])
            o_ref[...] = dx_ref[...] + dxn
            dg_ref[...] += jnp.sum(dg_rows, axis=0, keepdims=True)

    return pl.pallas_call(
        body, name=f"proj_dx_l{layer}", grid=(s // tm, steps),
        in_specs=[pl.BlockSpec((gs, tm, e), lambda m, j: (j, m, 0)),
                  pl.BlockSpec((gs, d, e), lambda m, j: (j, 0, 0)),
                  pl.BlockSpec((tm, d), lambda m, j: (m, 0)),
                  pl.BlockSpec((1, d), lambda m, j: (0, 0)),
                  pl.BlockSpec((tm, d), lambda m, j: (m, 0)), _any()],
        out_specs=(pl.BlockSpec((tm, d), lambda m, j: (m, 0)), pl.BlockSpec((1, d), lambda m, j: (0, 0))),
        out_shape=(jax.ShapeDtypeStruct((s, d), F32), jax.ShapeDtypeStruct((1, d), F32)),
        scratch_shapes=[pltpu.VMEM((tm, d), F32)],
        compiler_params=_params("arbitrary", "arbitrary"),
    )(dproj, w_all, x, g, dx, tie)


def _proj_bwd_weight(h, dproj, layer):
    s, d = h.shape
    e = dproj.shape[2]
    td, tn = _tile(d, 1024), _tile(e, 1024)
    nb = e // tn

    def body(h_ref, dp_ref, o_ref):
        o_ref[...] = _dot(h_ref[...], dp_ref[...], TN).astype(BF16)

    return pl.pallas_call(
        body, name=f"proj_dw_l{layer}", grid=(d // td, N_DEV * nb),
        in_specs=[pl.BlockSpec((s, td), lambda i, n: (0, i)),
                  pl.BlockSpec((None, s, tn), lambda i, n: (n // nb, 0, n % nb))],
        out_specs=pl.BlockSpec((None, td, tn), lambda i, n: (n // nb, i, n % nb)),
        out_shape=jax.ShapeDtypeStruct((N_DEV, d, e), BF16),
        compiler_params=_params("parallel", "parallel"),
    )(h, dproj)


def _adamw_math(w, g, m, v):
    m = ADAM_B1 * m + (1.0 - ADAM_B1) * g
    v = ADAM_B2 * v + (1.0 - ADAM_B2) * (g * g)
    m_hat = m / (1.0 - ADAM_B1 ** ADAM_STEP)
    v_hat = v / (1.0 - ADAM_B2 ** ADAM_STEP)
    return -ADAM_LR * (m_hat / (jnp.sqrt(v_hat) + ADAM_EPS) + ADAM_WD * w), m, v


def _adamw_layer(parts, own, me, w, m, v, prev, layer, name):
    n_layers, rows, cols = w.shape
    n_parts = parts.shape[0]
    tr = _tile(rows, max(8, (256 * 1024) // cols))

    def body(me_ref, p_ref, own_ref, w_ref, m_ref, v_ref, *rest):
        g_ref, d_ref, nm_ref, nv_ref = rest[-4:]
        mine = own_ref[...].astype(F32)
        if n_parts == N_DEV:
            g = None
            for j in range(N_DEV):
                term = jnp.where(me_ref[0] == j, mine, p_ref[j].astype(F32))
                g = term if g is None else g + term
        else:
            g = mine
            for j in range(n_parts):
                g = g + p_ref[j].astype(F32)
        g_ref[...] = g
        d_ref[...], nm_ref[...], nv_ref[...] = _adamw_math(w_ref[...], g, m_ref[...], v_ref[...])

    blk = pl.BlockSpec((None, tr, cols), lambda i, me_ref: (layer, i, 0))
    out_shape = tuple(jax.ShapeDtypeStruct(w.shape, F32) for _ in range(4))
    in_specs = [pl.BlockSpec((n_parts, tr, cols), lambda i, me_ref: (0, i, 0)),
                pl.BlockSpec((None, tr, cols), lambda i, me_ref: (me_ref[0], i, 0)), blk, blk, blk]
    args = [me, parts, own, w, m, v]
    aliases = {}
    if prev is not None:
        in_specs += [_any()] * 4
        args += list(prev)
        aliases = {6 + k: k for k in range(4)}
    return pl.pallas_call(
        body, name=f"{name}_l{layer}",
        grid_spec=pltpu.PrefetchScalarGridSpec(
            num_scalar_prefetch=1, grid=(rows // tr,), in_specs=in_specs, out_specs=(blk, blk, blk, blk)),
        out_shape=out_shape, input_output_aliases=aliases,
        compiler_params=_params("parallel"),
    )(*args)


def _sum_slots(parts):
    def body(p_ref, o_ref):
        g = p_ref[0]
        for j in range(1, N_DEV):
            g = g + p_ref[j]
        o_ref[...] = g

    return pl.pallas_call(
        body, name="sum_small_grads",
        in_specs=[_vmem()], out_specs=_vmem(),
        out_shape=jax.ShapeDtypeStruct(parts.shape[1:], F32),
        compiler_params=_params(),
    )(parts)


def _adamw_small(w, g, m, v):
    def body(w_ref, g_ref, m_ref, v_ref, d_ref, nm_ref, nv_ref):
        d_ref[...], nm_ref[...], nv_ref[...] = _adamw_math(w_ref[...], g_ref[...], m_ref[...], v_ref[...])

    return pl.pallas_call(
        body, name="adamw_small",
        in_specs=[_vmem()] * 4, out_specs=(_vmem(),) * 3,
        out_shape=tuple(jax.ShapeDtypeStruct(w.shape, F32) for _ in range(3)),
        compiler_params=_params(),
    )(w, g, m, v)


def _pack_rows(arrays):
    rows = []
    for a in arrays:
        flat = a.reshape(-1)
        pad = (-flat.shape[0]) % (8 * LANES)
        rows.append(jnp.pad(flat, (0, pad)).reshape(-1, LANES))
    return jnp.concatenate(rows, axis=0)


def _unpack_rows(packed, like):
    out, r0 = [], 0
    for a in like:
        n = a.size
        nr = -(-n // (8 * LANES)) * 8
        out.append(packed[r0:r0 + nr].reshape(-1)[:n].reshape(a.shape))
        r0 += nr
    return out


def kernel(x, norm_g, w_in, q_norm_g, k_norm_g, rel_bias, w_out, loss_target, m_norm_g, m_w_in, m_q_norm_g, m_k_norm_g, m_rel_bias, m_w_out, v_norm_g, v_w_in, v_q_norm_g, v_k_norm_g, v_rel_bias, v_w_out):
    depth, d, e = w_in.shape
    r_out = w_out.shape[1]
    heads = e // HEAD_DIM
    rel_w = rel_bias.shape[2]
    x0 = x[0]
    target = loss_target[0]
    s = x0.shape[0]

    me = jnp.reshape(_flat(_my_place()), (1,)).astype(jnp.int32)

    casts = [(_cast_layer(w_in, me, l, "cast_w_in"), _cast_layer(w_out, me, l, "cast_w_out"))
             for l in range(depth)]

    def begin_gather(l, after):
        (win_b, win_land), (wout_b, wout_land) = casts[l]
        return _gather_send((win_b, wout_b), (win_land, wout_land), after, f"gather_send_l{l}")

    rel_all = _gather_small(rel_bias, [], "gather_rel_bias")
    sent = begin_gather(0, [rel_all])
    rel_full = jnp.transpose(rel_all, (1, 2, 0, 3)).reshape(depth * heads, N_DEV * rel_w)
    bias = jnp.transpose(_bias_expand(rel_full), (1, 0, 2))
    head_work = [bias] + [shard for cast in casts[1:] for shard, _ in cast]
    forwarded = _gather_forward(sent, head_work, "gather_forward_l0")

    xs, hs, projs, ys, mixes, tots, weights = [], [], [], [], [], [], []
    xl = x0
    for l in range(depth):
        win_all, wout_all = _gather_finish(forwarded, [xl, forwarded[-1]], f"gather_finish_l{l}")
        more = l + 1 < depth
        if more:
            sent = begin_gather(l + 1, [win_all])
        wout_full = wout_all.reshape(d, d)
        proj, h = _norm_proj(xl, norm_g[l:l + 1], win_all, sent[-1] if more else None, l)
        y, mixed, tot = _sb_forward(proj, l)
        if more:
            forwarded = _gather_forward(sent, [tot], f"gather_forward_l{l + 1}")
        y, mixed = _chunk_forward(proj, bias, q_norm_g[l:l + 1], k_norm_g[l:l + 1], y, mixed,
                                  forwarded[-1] if more else None, l)
        xs.append(xl), hs.append(h), projs.append(proj), ys.append(y), mixes.append(mixed), tots.append(tot)
        weights.append((win_all, wout_full))
        xl = _out_proj(mixed, wout_full, xl, l)

    dx, loss_parts = _loss_head(xl, target)
    loss = lax.psum(jnp.sum(loss_parts), AXES)

    dbias, dng, dqg, dkg = [None] * depth, [None] * depth, [None] * depth, [None] * depth
    res_in, res_out = None, None

    peer_slots = jnp.stack([_flat(_flip(_my_place(), k)) for k in SAME_CORE]).astype(jnp.int32)

    def finish_exchange(exchanging, after, l):
        sems, bufs, own = exchanging
        copies = N_DEV - 1 if own is None else 3
        bufs = _wait_copies(f"exchange_finish_l{l}", bufs, sems, 2, lambda refs, t: refs[t].at[0], copies, after)
        own = bufs[:2] if own is None else own
        rin, rout = bufs[2:]
        return (_adamw_layer(rin, own[0], me, w_in, m_w_in, v_w_in, res_in, l, "adamw_w_in"),
                _adamw_layer(rout, own[1], me, w_out, m_w_out, v_w_out, res_out, l, "adamw_w_out"))

    pending = []
    for l in reversed(range(depth)):
        win_all, wout_full = weights[l]
        dmixed = _out_proj_bwd_input(dx, wout_full, None, l)
        gwout = _out_proj_bwd_weight(mixes[l], dx, l).reshape(N_DEV, r_out, d)
        dproj = _sb_backward(projs[l], ys[l], dmixed, tots[l], l)
        dproj, dbias[l], dqg[l], dkg[l] = _chunk_backward(
            projs[l], bias, q_norm_g[l:l + 1], k_norm_g[l:l + 1], ys[l], dmixed, dproj, l)
        grads_l = (_proj_bwd_weight(hs[l], dproj, l), gwout)
        if l > 0:
            sems, bufs, token = _exchange_direct(grads_l, [], f"exchange_direct_l{l}")
            pending.append(((sems, bufs, None), l))
        else:
            sems, bufs, token = _exchange_to_sibling(grads_l, [], "exchange_sibling_l0")
            bufs = _wait_copies("exchange_sibling_wait_l0", bufs, sems, 2, lambda refs, t: refs[t].at[0], 4, [token])
            own, parts, lands = bufs[0:2], bufs[2:4], bufs[4:6]
            csums = [_chip_sums(own[t], parts[t], peer_slots, f"chip_sums_{t}_l0") for t in range(2)]
            sems, csums_lands, token = _exchange_to_chips(csums, lands, [], "exchange_chips_l0")
            pending.append(((sems, csums_lands, own), l))
        dx, dng[l] = _proj_bwd_input(dproj, win_all, xs[l], norm_g[l:l + 1], dx, token, l)
    tie = token
    for exchanging, l in pending[:-1]:
        res_in, res_out = finish_exchange(exchanging, [dx, tie], l)
    drel = _bias_grad(jnp.concatenate(dbias, axis=0), [tie])
    small_like = [norm_g, q_norm_g, k_norm_g, drel]
    mine = _pack_rows([jnp.concatenate(dng, axis=0), jnp.concatenate(dqg, axis=0),
                       jnp.concatenate(dkg, axis=0), drel])
    gathered = _gather_small(mine, [res_in[0], res_out[0]], "gather_small_grads")
    g_norm, g_qn, g_kn, g_rel_full = _unpack_rows(_sum_slots(gathered), small_like)
    my_block = _flat(_my_place())
    g_rel = lax.dynamic_slice_in_dim(g_rel_full.reshape(depth, heads, N_REL), my_block * rel_w, rel_w, axis=2)
    small_w = [norm_g, q_norm_g, k_norm_g, rel_bias]
    small = _adamw_small(_pack_rows(small_w), _pack_rows([g_norm, g_qn, g_kn, g_rel]),
                         _pack_rows([m_norm_g, m_q_norm_g, m_k_norm_g, m_rel_bias]),
                         _pack_rows([v_norm_g, v_q_norm_g, v_k_norm_g, v_rel_bias]))
    d_small, nm_small, nv_small = (_unpack_rows(p, small_w) for p in small)

    res_in, res_out = finish_exchange(pending[-1][0], [small[0], res_in[0], res_out[0]], 0)
    g_win, d_win, nm_win, nv_win = res_in
    g_wout, d_wout, nm_wout, nv_wout = res_out
    grads = (g_norm, g_win, g_qn, g_kn, g_rel, g_wout)

    def order(sm, big_in, big_out):
        return (sm[0], big_in, sm[1], sm[2], sm[3], big_out)

    return (loss, dx[None], *grads, *order(d_small, d_win, d_wout),
            *order(nm_small, nm_win, nm_wout), *order(nv_small, nv_win, nv_wout))
```

```python
import functools

import jax
import jax.numpy as jnp
from jax import lax
from jax.experimental import pallas as pl
from jax.experimental.pallas import tpu as pltpu

F32 = jnp.float32
BF16 = jnp.bfloat16
MESH_ID = pl.DeviceIdType.MESH
AXES = ("x", "y", "c")

N_DEV = 8
HEAD_DIM = 128
CHUNK = 64
LEFT_CHUNKS = 8
BAND_W = (LEFT_CHUNKS + 1) * CHUNK
PAD_K = LEFT_CHUNKS * CHUNK
REL_CLIP = 256
N_REL = REL_CLIP + CHUNK
NORM_EPS = 1e-6
NEG_BIG = -1e30
PAIR = 2 * CHUNK
PAIR_W = BAND_W + CHUNK
CHUNK_HEADS = 2
PAIR_UNROLL = 2
SB_Q = 512
SB_K = 128
SB_HEADS = 2
SB_UNROLL = 4
LANES = 128

ADAM_LR = 0.001
ADAM_B1 = 0.9
ADAM_B2 = 0.999
ADAM_EPS = 1e-08
ADAM_WD = 0.01
ADAM_STEP = 10

VMEM_LIMIT_BYTES = 56 * 1024 * 1024

NT = (((1,), (1,)), ((), ()))
TN = (((0,), (0,)), ((), ()))


def _params(*sem, **kw):
    return pltpu.CompilerParams(dimension_semantics=sem or None, vmem_limit_bytes=VMEM_LIMIT_BYTES, **kw)


def _any():
    return pl.BlockSpec(memory_space=pl.ANY)


def _vmem():
    return pl.BlockSpec(memory_space=pltpu.VMEM)


def _tile(n, want):
    return want if n % want == 0 else n


def _dot(a, b, dims=None):
    if dims is None:
        return jnp.dot(a, b, preferred_element_type=F32)
    return lax.dot_general(a, b, dims, preferred_element_type=F32)


def _split_dot(a, b, parts, dims=None):
    acc = None
    rest = a
    for _ in range(parts):
        piece = rest.astype(BF16)
        rest = rest - piece.astype(F32)
        term = _dot(piece, b, dims)
        acc = term if acc is None else acc + term
    return acc


def _split_dot_deep(a, b_twice):
    high = a.astype(BF16)
    low = (a - high.astype(F32)).astype(BF16)
    return _dot(jnp.concatenate([high, low], axis=1), b_twice)


def _log_sigmoid(z):
    return jnp.minimum(z, 0.0) - jnp.log(1.0 + jnp.exp(-jnp.abs(z)))


def _silu_and_grad(g):
    sig = jax.nn.sigmoid(g)
    return g * sig, sig * (1.0 + g * (1.0 - sig))


def _my_place():
    return lax.axis_index("x"), lax.axis_index("y"), lax.axis_index("c")


def _flat(place):
    return 4 * place[0] + 2 * place[1] + place[2]


def _flip(place, k):
    return tuple(1 - p if (k >> s) & 1 else p for p, s in zip(place, (2, 1, 0)))


def _cast_layer(w, me, layer, name):
    _, rows, cols = w.shape
    tr = _tile(rows, 1024)

    def body(me_ref, a_ref, shard_ref, land_ref):
        del me_ref
        shard_ref[...] = a_ref[...].astype(BF16)
        land_ref[...] = shard_ref[...]

    return pl.pallas_call(
        body, name=f"{name}_l{layer}",
        grid_spec=pltpu.PrefetchScalarGridSpec(
            num_scalar_prefetch=1, grid=(rows // tr,),
            in_specs=[pl.BlockSpec((None, tr, cols), lambda i, me_ref: (layer, i, 0))],
            out_specs=(pl.BlockSpec((tr, cols), lambda i, me_ref: (i, 0)),
                       pl.BlockSpec((None, tr, cols), lambda i, me_ref: (me_ref[0], i, 0)))),
        out_shape=(jax.ShapeDtypeStruct((rows, cols), BF16), jax.ShapeDtypeStruct((N_DEV, rows, cols), BF16)),
        compiler_params=_params("parallel"),
    )(me, w)


HBM_SPEC = pl.BlockSpec(memory_space=pltpu.HBM)
SEM_SPEC = pl.BlockSpec(memory_space=pltpu.SEMAPHORE)
SAME_CORE = (2, 4, 6)
SIBLING = 1
SPLIT_EFFECT = pltpu.SideEffectType.DATAFLOW_SIDE_EFFECTING


def _split_call(body, name, bufs, sems_in, sem_counts_out, after, token):
    bufs, sems_in, after = list(bufs), list(sems_in), list(after)
    nb, ni, no = len(bufs), len(sems_in), len(sem_counts_out)

    def wrapped(*refs):
        outs = nb + ni + len(after)
        body(refs[:nb], refs[nb:nb + ni], refs[outs:outs + no])
        if token:
            refs[-1][...] = jnp.zeros_like(refs[-1])

    out = pl.pallas_call(
        wrapped, name=name,
        in_specs=[HBM_SPEC] * nb + [SEM_SPEC] * ni + [_any()] * len(after),
        out_specs=tuple([SEM_SPEC] * no + [HBM_SPEC] * nb + ([_vmem()] if token else [])),
        out_shape=tuple([pltpu.SemaphoreType.DMA((c,)) for c in sem_counts_out]
                        + [pltpu.HBM(a.shape, a.dtype) for a in bufs]
                        + ([jax.ShapeDtypeStruct((8, LANES), F32)] if token else [])),
        input_output_aliases={i: no + i for i in range(nb)},
        compiler_params=pltpu.CompilerParams(has_side_effects=SPLIT_EFFECT),
    )(*[pltpu.with_memory_space_constraint(a, pltpu.HBM) for a in bufs], *sems_in, *after)
    return list(out[:no]), list(out[no:no + nb]), (out[-1] if token else None)


def _remote(src, dst, send_sems, recv_sems, i, to):
    return pltpu.make_async_remote_copy(src_ref=src, dst_ref=dst, send_sem=send_sems.at[i], recv_sem=recv_sems.at[i],
                                        device_id=to, device_id_type=MESH_ID)


def _wait_copies(name, bufs, sems, n, slot_of, count, after):
    def body(refs, sems_in, _):
        me = _my_place()
        for t in range(n):
            slot = slot_of(refs, t)
            for a in range(count):
                cp = _remote(slot, slot, sems_in[0], sems_in[1], t * count + a, me)
                cp.wait_send()
                cp.wait_recv()

    return _split_call(body, name, bufs, sems, (), after, False)[1]


def _exchange_direct(grads, after, name):
    n = len(grads)
    lands = [lax.empty(g.shape, g.dtype) for g in grads]

    def body(refs, _, sems_out):
        me = _my_place()
        for k in range(1, N_DEV):
            peer = _flip(me, k)
            for t in range(n):
                _remote(refs[t].at[_flat(peer)], refs[n + t].at[_flat(me)], *sems_out, t * (N_DEV - 1) + k - 1,
                        peer).start()

    return _split_call(body, name, list(grads) + lands, (), (n * (N_DEV - 1), n * (N_DEV - 1)), after, True)


def _exchange_to_sibling(grads, after, name):
    n = len(grads)
    parts = [lax.empty((3,) + g.shape[1:], g.dtype) for g in grads]
    lands = [lax.empty((4,) + g.shape[1:], g.dtype) for g in grads]

    def body(refs, _, sems_out):
        me = _my_place()
        sib = _flip(me, SIBLING)
        for t in range(n):
            g, part, land = refs[t], refs[n + t], refs[2 * n + t]
            _remote(g.at[_flat(sib)], land.at[0], *sems_out, 4 * t, sib).start()
            for a, k in enumerate(SAME_CORE):
                _remote(g.at[_flat(_flip(sib, k))], part.at[a], *sems_out, 4 * t + 1 + a, sib).start()

    return _split_call(body, name, list(grads) + parts + lands, (), (4 * n, 4 * n), after, True)


def _chip_sums(grads, parts, slots, name):
    _, rows, cols = grads.shape
    tr = _tile(rows, max(8, (512 * 1024) // cols))

    def body(slots_ref, g_ref, p_ref, o_ref):
        del slots_ref
        o_ref[...] = (g_ref[...].astype(F32) + p_ref[...].astype(F32)).astype(BF16)

    return pl.pallas_call(
        body, name=name,
        grid_spec=pltpu.PrefetchScalarGridSpec(
            num_scalar_prefetch=1, grid=(3, rows // tr),
            in_specs=[pl.BlockSpec((None, tr, cols), lambda a, i, slots_ref: (slots_ref[a], i, 0)),
                      pl.BlockSpec((None, tr, cols), lambda a, i, slots_ref: (a, i, 0))],
            out_specs=pl.BlockSpec((None, tr, cols), lambda a, i, slots_ref: (a, i, 0))),
        out_shape=jax.ShapeDtypeStruct((3, rows, cols), BF16),
        compiler_params=_params("parallel", "parallel"),
    )(slots, grads, parts)


def _exchange_to_chips(csums, lands, after, name):
    n = len(csums)

    def body(refs, _, sems_out):
        me = _my_place()
        for t in range(n):
            for a, k in enumerate(SAME_CORE):
                _remote(refs[t].at[a], refs[n + t].at[1 + a], *sems_out, 3 * t + a, _flip(me, k)).start()

    return _split_call(body, name, list(csums) + list(lands), (), (3 * n, 3 * n), after, True)


def _hbm_call(body, name, n_hbm, sems_in, sems_out, after, token, like):
    after = list(after)
    in_specs = [HBM_SPEC] * n_hbm + [SEM_SPEC] * len(sems_in) + [_any()] * len(after)
    out_specs = [SEM_SPEC] * len(sems_out) + [HBM_SPEC] * n_hbm + ([_vmem()] if token else [])
    out_shape = ([pltpu.SemaphoreType.DMA((c,)) for c in sems_out] + [pltpu.HBM(a.shape, a.dtype) for a in like]
                 + ([jax.ShapeDtypeStruct((8, LANES), F32)] if token else []))
    return in_specs, tuple(out_specs), tuple(out_shape), {i: len(sems_out) + i for i in range(n_hbm)}, after


def _gather_send(shards, lands, after, name):
    n = len(shards)
    peers = (SIBLING,) + SAME_CORE
    after = list(after)

    def body(*refs):
        me = _my_place()
        send_sems, recv_sems = refs[2 * n + len(after)], refs[2 * n + len(after) + 1]
        for a, k in enumerate(peers):
            for t in range(n):
                pltpu.make_async_remote_copy(
                    src_ref=refs[t], dst_ref=refs[n + t].at[_flat(me)],
                    send_sem=send_sems.at[t * 4 + a], recv_sem=recv_sems.at[t * 4 + a],
                    device_id=_flip(me, k), device_id_type=MESH_ID).start()
        refs[-1][...] = jnp.zeros_like(refs[-1])

    bufs = list(shards) + list(lands)
    in_specs, out_specs, out_shape, aliases, after = _hbm_call(body, name, 2 * n, (), (4 * n, 4 * n), after, True, bufs)
    out = pl.pallas_call(
        body, name=name, in_specs=in_specs, out_specs=out_specs, out_shape=out_shape,
        input_output_aliases=aliases, compiler_params=pltpu.CompilerParams(has_side_effects=SPLIT_EFFECT),
    )(*[pltpu.with_memory_space_constraint(a, pltpu.HBM) for a in bufs], *after)
    return out[0], out[1], out[2:2 + n], out[2 + n:2 + 2 * n], out[-1]


def _gather_forward(sent, after, name):
    send1, recv1, shards, lands, _ = sent
    n = len(shards)
    after = list(after)

    def body(*refs):
        me = _my_place()
        recv1_ref = refs[2 * n + 1]
        out0 = 2 * n + 2 + len(after)
        send2_ref, recv2_ref = refs[out0], refs[out0 + 1]
        for a, k in enumerate(SAME_CORE):
            owner = _flat(_flip(me, k))
            for t in range(n):
                slot = refs[n + t].at[owner]
                pltpu.make_async_remote_copy(
                    src_ref=refs[t], dst_ref=slot, send_sem=refs[2 * n].at[t * 4 + 1 + a],
                    recv_sem=recv1_ref.at[t * 4 + 1 + a], device_id=_flip(me, k), device_id_type=MESH_ID).wait_recv()
                pltpu.make_async_remote_copy(
                    src_ref=slot, dst_ref=slot, send_sem=send2_ref.at[t * 3 + a], recv_sem=recv2_ref.at[t * 3 + a],
                    device_id=_flip(me, SIBLING), device_id_type=MESH_ID).start()
        refs[-1][...] = jnp.zeros_like(refs[-1])

    bufs = list(shards) + list(lands)
    in_specs, out_specs, out_shape, aliases, after = _hbm_call(body, name, 2 * n, (4 * n, 4 * n), (3 * n, 3 * n), after,
                                                               True, bufs)
    out = pl.pallas_call(
        body, name=name, in_specs=in_specs, out_specs=out_specs, out_shape=out_shape,
        input_output_aliases=aliases, compiler_params=pltpu.CompilerParams(has_side_effects=SPLIT_EFFECT),
    )(*bufs, send1, recv1, *after)
    return (send1, recv1), (out[0], out[1]), out[2:2 + n], out[2 + n:2 + 2 * n], out[-1]


def _gather_finish(forwarded, after, name):
    (send1, recv1), (send2, recv2), shards, lands, _ = forwarded
    n = len(shards)
    after = list(after)

    def body(*refs):
        me = _my_place()
        send1_ref, recv1_ref, send2_ref, recv2_ref = refs[2 * n:2 * n + 4]
        sib = _flip(me, SIBLING)
        for t in range(n):
            for a in range(4):
                cp = pltpu.make_async_remote_copy(
                    src_ref=refs[t], dst_ref=refs[n + t].at[_flat(sib)], send_sem=send1_ref.at[t * 4 + a],
                    recv_sem=recv1_ref.at[t * 4 + a], device_id=sib, device_id_type=MESH_ID)
                cp.wait_send()
                if a == 0:
                    cp.wait_recv()
            for a in range(3):
                cp = pltpu.make_async_remote_copy(
                    src_ref=refs[t], dst_ref=refs[n + t].at[_flat(sib)], send_sem=send2_ref.at[t * 3 + a],
                    recv_sem=recv2_ref.at[t * 3 + a], device_id=sib, device_id_type=MESH_ID)
                cp.wait_send()
                cp.wait_recv()

    bufs = list(shards) + list(lands)
    in_specs, out_specs, out_shape, aliases, after = _hbm_call(body, name, 2 * n, (4 * n, 4 * n, 3 * n, 3 * n), (), after,
                                                               False, bufs)
    out = pl.pallas_call(
        body, name=name, in_specs=in_specs, out_specs=out_specs, out_shape=out_shape,
        input_output_aliases=aliases, compiler_params=pltpu.CompilerParams(has_side_effects=SPLIT_EFFECT),
    )(*bufs, send1, recv1, send2, recv2, *after)
    return out[n:]


def _gather_small(v, after, name):
    after = list(after)

    def body(v_ref, *rest):
        o_ref, send_sems, recv_sems = rest[-3:]
        me = _my_place()
        o_ref[_flat(me)] = v_ref[...]
        copies = []
        for k in range(1, N_DEV):
            copies.append(pltpu.make_async_remote_copy(
                src_ref=v_ref, dst_ref=o_ref.at[_flat(me)],
                send_sem=send_sems.at[k - 1], recv_sem=recv_sems.at[k - 1],
                device_id=_flip(me, k), device_id_type=MESH_ID))
        for cp in copies:
            cp.start()
        for cp in copies:
            cp.wait()

    return pl.pallas_call(
        body, name=name,
        in_specs=[_vmem()] + [_any()] * len(after), out_specs=_vmem(),
        out_shape=jax.ShapeDtypeStruct((N_DEV,) + v.shape, v.dtype),
        scratch_shapes=[pltpu.SemaphoreType.DMA((7,)), pltpu.SemaphoreType.DMA((7,))],
        compiler_params=_params(has_side_effects=True),
    )(v, *after)


def _rel_onehot(row):
    r_io = lax.broadcasted_iota(jnp.int32, (N_REL, PAIR_W), 0)
    p_io = lax.broadcasted_iota(jnp.int32, (N_REL, PAIR_W), 1)
    band_col = p_io - (row // CHUNK) * CHUNK
    in_band = jnp.logical_and(band_col >= 0, band_col < BAND_W)
    idx = jnp.clip(PAD_K + row % CHUNK - band_col, -(CHUNK - 1), REL_CLIP) + (CHUNK - 1)
    return jnp.logical_and(r_io == idx, in_band).astype(BF16), in_band[0:1]


def _bias_expand(rel):
    lh = rel.shape[0]

    def body(rel_ref, o_ref):
        onehot, in_band = _rel_onehot(pl.program_id(0))
        o_ref[...] = jnp.where(in_band, _split_dot(rel_ref[...], onehot, 3), NEG_BIG)

    return pl.pallas_call(
        body, name="bias_expand", grid=(PAIR,),
        in_specs=[pl.BlockSpec((lh, N_REL), lambda i: (0, 0))],
        out_specs=pl.BlockSpec((None, lh, PAIR_W), lambda i: (i, 0, 0)),
        out_shape=jax.ShapeDtypeStruct((PAIR, lh, PAIR_W), F32),
        compiler_params=_params("parallel"),
    )(rel)


SHEAR_W = PAIR_W + LANES
BIAS_HEADS = 8


def _bias_grad(dbias, after):
    lh = dbias.shape[0]
    hb = _tile(lh, BIAS_HEADS)
    after = list(after)

    def body(db_ref, *rest):
        o_ref = rest[-1]
        a_io = lax.broadcasted_iota(jnp.int32, (PAIR, PAIR), 0)
        b_io = lax.broadcasted_iota(jnp.int32, (PAIR, PAIR), 1)
        flip_rows = (a_io + b_io == PAIR - 1).astype(BF16)
        diags = []
        for j in range(hb):
            rest_part = jnp.concatenate([db_ref[j], jnp.zeros((PAIR, SHEAR_W - PAIR_W), F32)], axis=1)
            flipped = None
            for _ in range(3):
                piece = rest_part.astype(BF16)
                rest_part = rest_part - piece.astype(F32)
                term = _dot(flip_rows, piece)
                flipped = term if flipped is None else flipped + term
            sheared = pltpu.roll(flipped, 0, 1, stride=1, stride_axis=0)
            diags.append(jnp.sum(sheared, axis=0, keepdims=True))
        c_io = lax.broadcasted_iota(jnp.int32, (SHEAR_W, N_REL), 0)
        r_io = lax.broadcasted_iota(jnp.int32, (SHEAR_W, N_REL), 1)
        entry = jnp.clip(PAD_K + (PAIR - 1) - c_io, -(CHUNK - 1), REL_CLIP) + (CHUNK - 1)
        o_ref[...] = _split_dot(jnp.concatenate(diags, axis=0), (r_io == entry).astype(BF16), 3)

    return pl.pallas_call(
        body, name="bias_grad", grid=(lh // hb,),
        in_specs=[pl.BlockSpec((hb, PAIR, PAIR_W), lambda i: (i, 0, 0))] + [_any()] * len(after),
        out_specs=pl.BlockSpec((hb, N_REL), lambda i: (i, 0)),
        out_shape=jax.ShapeDtypeStruct((lh, N_REL), F32),
        compiler_params=_params("parallel"),
    )(dbias, *after)


def _norm_proj(x, g, w_all, tie, layer):
    s, d = x.shape
    e = w_all.shape[2]
    tm, tn = _tile(s, 1024), _tile(e, 1024)
    nb = e // tn
    ties = [] if tie is None else [tie]

    def body(x_ref, g_ref, w_ref, *rest):
        proj_ref, h_ref = rest[-2:]

        @pl.when(pl.program_id(1) == 0)
        def _():
            xv = x_ref[...]
            r = lax.rsqrt(jnp.mean(xv * xv, axis=-1, keepdims=True) + NORM_EPS)
            h_ref[...] = ((xv * r) * g_ref[...]).astype(BF16)

        proj_ref[...] = _dot(h_ref[...], w_ref[...])

    return pl.pallas_call(
        body, name=f"norm_proj_l{layer}", grid=(s // tm, N_DEV * nb),
        in_specs=[pl.BlockSpec((tm, d), lambda m, n: (m, 0)),
                  pl.BlockSpec((1, d), lambda m, n: (0, 0)),
                  pl.BlockSpec((None, d, tn), lambda m, n: (n // nb, 0, n % nb))] + [_any()] * len(ties),
        out_specs=(pl.BlockSpec((None, tm, tn), lambda m, n: (n // nb, m, n % nb)),
                   pl.BlockSpec((tm, d), lambda m, n: (m, 0))),
        out_shape=(jax.ShapeDtypeStruct((N_DEV, s, e), F32), jax.ShapeDtypeStruct((s, d), BF16)),
        compiler_params=_params("parallel", "arbitrary"),
    )(x, g, w_all, *ties)


def _sb_forward(proj, layer):
    _, s, e = proj.shape
    hp = _tile(e // HEAD_DIM, SB_HEADS)
    width = hp * HEAD_DIM
    tq = _tile(s, SB_Q)
    diag_tiles = tq // SB_K
    scale = HEAD_DIM ** -0.5

    def body(p_ref, y_ref, mix_ref, tot_ref, kb_ref, vb_ref):
        kb_ref[...] = p_ref[1].astype(BF16)
        vb_ref[...] = p_ref[2].astype(BF16)
        row = lax.broadcasted_iota(jnp.int32, (tq, SB_K), 0)
        col = lax.broadcasted_iota(jnp.int32, (tq, SB_K), 1)
        kj = lax.broadcasted_iota(jnp.int32, (SB_K, SB_K), 0)
        ks = lax.broadcasted_iota(jnp.int32, (SB_K, SB_K), 1)
        later = (kj > ks).astype(BF16)

        def q_block(qi, _):
            t0 = pl.multiple_of(qi * tq, tq)
            qb = [p_ref[0, pl.ds(t0, tq), h * HEAD_DIM:(h + 1) * HEAD_DIM].astype(BF16) for h in range(hp)]

            def tile(s0, state, causal):
                out = []
                for h in range(hp):
                    carry, acc = state[h]
                    lanes = slice(h * HEAD_DIM, (h + 1) * HEAD_DIM)
                    z = _dot(qb[h], kb_ref[pl.ds(s0, SB_K), lanes], NT) * scale
                    ls = _log_sigmoid(z)
                    stay = ls - z
                    if causal is not None:
                        stay = jnp.where(causal, stay, 0.0)
                    w = jnp.exp(ls + carry + _split_dot(stay, later, 2))
                    if causal is not None:
                        w = jnp.where(causal, w, 0.0)
                    acc = acc + _dot(w.astype(BF16), vb_ref[pl.ds(s0, SB_K), lanes])
                    out.append((carry + jnp.sum(stay, axis=1, keepdims=True), acc))
                return tuple(out)

            state = tuple((jnp.zeros((tq, 1), F32), jnp.zeros((tq, HEAD_DIM), F32)) for _ in range(hp))
            for dt in reversed(range(diag_tiles)):
                state = tile(t0 + dt * SB_K, state, col + dt * SB_K < row)

            def k_blocks(j, st):
                for u in range(SB_UNROLL):
                    st = tile(pl.multiple_of((diag_tiles * qi - 1 - SB_UNROLL * j - u) * SB_K, SB_K), st, None)
                return st

            state = lax.fori_loop(0, diag_tiles * qi // SB_UNROLL, k_blocks, state)
            silu, _ = _silu_and_grad(p_ref[3, pl.ds(t0, tq), :])
            for h in range(hp):
                lanes = slice(h * HEAD_DIM, (h + 1) * HEAD_DIM)
                y_ref[pl.ds(t0, tq), lanes] = state[h][1]
                mix_ref[pl.ds(t0, tq), lanes] = (state[h][1] * silu[:, lanes]).astype(BF16)
                tot_ref[h, pl.ds(t0, tq), :] = state[h][0]
            return 0

        lax.fori_loop(0, s // tq, q_block, 0)

    return pl.pallas_call(
        body, name=f"sb_forward_l{layer}", grid=(e // width,),
        in_specs=[pl.BlockSpec((4, s, width), lambda h: (0, 0, h))],
        out_specs=(pl.BlockSpec((s, width), lambda h: (0, h)),
                   pl.BlockSpec((s, width), lambda h: (0, h)),
                   pl.BlockSpec((hp, s, 1), lambda h: (h, 0, 0))),
        out_shape=(jax.ShapeDtypeStruct((s, 2 * e), F32), jax.ShapeDtypeStruct((s, 2 * e), BF16),
                   jax.ShapeDtypeStruct((e // HEAD_DIM, s, 1), F32)),
        scratch_shapes=[pltpu.VMEM((s, width), BF16), pltpu.VMEM((s, width), BF16)],
        compiler_params=_params("parallel"),
    )(proj)


def _qk_norm(t, gain):
    r = lax.rsqrt(jnp.mean(t * t, axis=-1, keepdims=True) + NORM_EPS)
    return t * r, r, (t * r) * gain


def _chunk_scores(qc, kw, bias, t0, scale):
    sc = _dot(qc, kw, NT) * scale + bias
    col = lax.broadcasted_iota(jnp.int32, (PAIR, PAIR_W), 1)
    sc = jnp.where(col + t0 >= PAD_K, sc, NEG_BIG)
    ex = jnp.exp(sc - jnp.max(sc, axis=-1, keepdims=True))
    return ex / jnp.sum(ex, axis=-1, keepdims=True)


def _chunk_forward(proj, bias, q_gain, k_gain, y, mixed, tie, layer):
    ties = [] if tie is None else [tie]
    _, s, e = proj.shape
    hp = _tile(e // HEAD_DIM, CHUNK_HEADS)
    width = hp * HEAD_DIM
    steps = e // width
    unroll = _tile(s // PAIR, PAIR_UNROLL)
    scale = HEAD_DIM ** -0.5
    heads = [slice(h * HEAD_DIM, (h + 1) * HEAD_DIM) for h in range(hp)]

    def body(p_ref, b_ref, qg_ref, kg_ref, *rest):
        y_ref, mix_ref, qn_ref, kp_ref, vp_ref = rest[-5:]
        kp_ref[pl.ds(0, PAD_K), :] = jnp.zeros((PAD_K, width), BF16)
        vp_ref[pl.ds(0, PAD_K), :] = jnp.zeros((PAD_K, width), BF16)
        vp_ref[pl.ds(PAD_K, s), :] = p_ref[2].astype(BF16)
        for lanes in heads:
            qn_ref[:, lanes] = _qk_norm(p_ref[0, :, lanes], qg_ref[...])[2].astype(BF16)
            kp_ref[pl.ds(PAD_K, s), lanes] = _qk_norm(p_ref[1, :, lanes], kg_ref[...])[2].astype(BF16)

        def chunks(ci, _):
            done = []
            for u in range(unroll):
                t0 = pl.multiple_of((ci * unroll + u) * PAIR, PAIR)
                silu, _ = _silu_and_grad(p_ref[3, pl.ds(t0, PAIR), :])
                for h, lanes in enumerate(heads):
                    probs = _chunk_scores(qn_ref[pl.ds(t0, PAIR), lanes], kp_ref[pl.ds(t0, PAIR_W), lanes],
                                          b_ref[h], t0, scale)
                    out = _dot(probs.astype(BF16), vp_ref[pl.ds(t0, PAIR_W), lanes])
                    done.append((t0, lanes, out, (out * silu[:, lanes]).astype(BF16)))
            for t0, lanes, out, gated in done:
                y_ref[pl.ds(t0, PAIR), lanes] = out
                mix_ref[pl.ds(t0, PAIR), lanes] = gated
            return 0

        lax.fori_loop(0, s // (PAIR * unroll), chunks, 0)

    return pl.pallas_call(
        body, name=f"chunk_forward_l{layer}", grid=(steps,),
        in_specs=[pl.BlockSpec((4, s, width), lambda h: (1, 0, h)),
                  pl.BlockSpec((hp, PAIR, PAIR_W), lambda h: (layer * steps + h, 0, 0)),
                  pl.BlockSpec((1, HEAD_DIM), lambda h: (0, 0)),
                  pl.BlockSpec((1, HEAD_DIM), lambda h: (0, 0)),
                  _any(), _any()] + [_any()] * len(ties),
        out_specs=(pl.BlockSpec((s, width), lambda h: (0, steps + h)),
                   pl.BlockSpec((s, width), lambda h: (0, steps + h))),
        out_shape=(jax.ShapeDtypeStruct(y.shape, F32), jax.ShapeDtypeStruct(mixed.shape, BF16)),
        input_output_aliases={4: 0, 5: 1},
        scratch_shapes=[pltpu.VMEM((s, width), BF16), pltpu.VMEM((s + PAD_K, width), BF16),
                        pltpu.VMEM((s + PAD_K, width), BF16)],
        compiler_params=_params("parallel"),
    )(proj, bias, q_gain, k_gain, y, mixed, *ties)


def _out_proj(mixed, w, x, layer):
    s, d = x.shape
    tm, tn = _tile(s, 1024), _tile(d, 1024)

    def body(a_ref, w_ref, x_ref, o_ref):
        o_ref[...] = x_ref[...] + _dot(a_ref[...], w_ref[...])

    return pl.pallas_call(
        body, name=f"out_proj_l{layer}", grid=(s // tm, d // tn),
        in_specs=[pl.BlockSpec((tm, d), lambda m, n: (m, 0)),
                  pl.BlockSpec((d, tn), lambda m, n: (0, n)),
                  pl.BlockSpec((tm, tn), lambda m, n: (m, n))],
        out_specs=pl.BlockSpec((tm, tn), lambda m, n: (m, n)),
        out_shape=jax.ShapeDtypeStruct((s, d), F32),
        compiler_params=_params("parallel", "parallel"),
    )(mixed, w, x)


def _loss_head(y, target):
    s, d = y.shape
    tm = _tile(s, 256)

    def body(y_ref, t_ref, dy_ref, part_ref):
        diff = y_ref[...] - t_ref[...]
        dy_ref[...] = diff * (1.0 / d)
        sq = (diff * diff).reshape(tm // 8, 8, d).sum(axis=0)
        acc = sq[:, 0:LANES]
        for j in range(1, d // LANES):
            acc = acc + sq[:, j * LANES:(j + 1) * LANES]
        part_ref[...] = acc * (0.5 / d)

    return pl.pallas_call(
        body, name="loss_head", grid=(s // tm,),
        in_specs=[pl.BlockSpec((tm, d), lambda i: (i, 0)), pl.BlockSpec((tm, d), lambda i: (i, 0))],
        out_specs=(pl.BlockSpec((tm, d), lambda i: (i, 0)), pl.BlockSpec((None, 8, LANES), lambda i: (i, 0, 0))),
        out_shape=(jax.ShapeDtypeStruct((s, d), F32), jax.ShapeDtypeStruct((s // tm, 8, LANES), F32)),
        compiler_params=_params("parallel"),
    )(y, target)


def _out_proj_bwd_input(dx, w, tie, layer):
    s, d = dx.shape
    tm, tn = _tile(s, 1024), _tile(d, 1024)
    ties = [] if tie is None else [tie]

    def body(dx_ref, w_ref, *rest):
        rest[-1][...] = _dot(dx_ref[...].astype(BF16), w_ref[...], NT)

    return pl.pallas_call(
        body, name=f"out_proj_dx_l{layer}", grid=(s // tm, d // tn),
        in_specs=[pl.BlockSpec((tm, d), lambda m, n: (m, 0)),
                  pl.BlockSpec((tn, d), lambda m, n: (n, 0))] + [_any()] * len(ties),
        out_specs=pl.BlockSpec((tm, tn), lambda m, n: (m, n)),
        out_shape=jax.ShapeDtypeStruct((s, d), F32),
        compiler_params=_params("parallel", "parallel"),
    )(dx, w, *ties)


def _out_proj_bwd_weight(mixed, dx, layer):
    s, d = dx.shape
    te, tn = _tile(d, 1024), _tile(d, 1024)

    def body(a_ref, dx_ref, o_ref):
        o_ref[...] = _dot(a_ref[...], dx_ref[...].astype(BF16), TN).astype(BF16)

    return pl.pallas_call(
        body, name=f"out_proj_dw_l{layer}", grid=(d // te, d // tn),
        in_specs=[pl.BlockSpec((s, te), lambda i, n: (0, i)), pl.BlockSpec((s, tn), lambda i, n: (0, n))],
        out_specs=pl.BlockSpec((te, tn), lambda i, n: (i, n)),
        out_shape=jax.ShapeDtypeStruct((d, d), BF16),
        compiler_params=_params("parallel", "parallel"),
    )(mixed, dx)


def _sb_backward(proj, y, dmixed, tot, layer):
    _, s, e = proj.shape
    hp = _tile(e // HEAD_DIM, SB_HEADS)
    width = hp * HEAD_DIM
    tq = _tile(s, SB_Q)
    diag_tiles = tq // SB_K
    scale = HEAD_DIM ** -0.5

    def body(p_ref, y_ref, dm_ref, tot_ref, o_ref, kb_ref, vb_ref, do_ref, dk_ref, dv_ref):
        kb_ref[...] = p_ref[1].astype(BF16)
        vb_ref[...] = p_ref[2].astype(BF16)
        silu, dsilu = _silu_and_grad(p_ref[3])
        dm = dm_ref[...]
        do_ref[...] = (dm * silu).astype(BF16)
        o_ref[3] = (dm * y_ref[...] * dsilu).astype(BF16)
        dk_ref[...] = jnp.zeros_like(dk_ref)
        dv_ref[...] = jnp.zeros_like(dv_ref)
        row = lax.broadcasted_iota(jnp.int32, (tq, SB_K), 0)
        col = lax.broadcasted_iota(jnp.int32, (tq, SB_K), 1)
        kj = lax.broadcasted_iota(jnp.int32, (SB_K, SB_K), 0)
        ks = lax.broadcasted_iota(jnp.int32, (SB_K, SB_K), 1)
        upto = (kj <= ks).astype(BF16)
        before = (kj < ks).astype(BF16)
        upto = jnp.concatenate([upto, upto], axis=0)
        before = jnp.concatenate([before, before], axis=0)

        def q_block(qi, _):
            t0 = pl.multiple_of(qi * tq, tq)
            heads = [slice(h * HEAD_DIM, (h + 1) * HEAD_DIM) for h in range(hp)]
            qb = [p_ref[0, pl.ds(t0, tq), lanes].astype(BF16) for lanes in heads]
            dob = [do_ref[pl.ds(t0, tq), lanes] for lanes in heads]
            total = [tot_ref[h, pl.ds(t0, tq), :] for h in range(hp)]

            def tile(s0, state, causal):
                out, adds = [], []
                for h, lanes in enumerate(heads):
                    stay_sum, dlw_sum, dq = state[h]
                    kt = kb_ref[pl.ds(s0, SB_K), lanes]
                    vt = vb_ref[pl.ds(s0, SB_K), lanes]
                    z = _dot(qb[h], kt, NT) * scale
                    ls = _log_sigmoid(z)
                    stay = ls - z
                    if causal is not None:
                        stay = jnp.where(causal, stay, 0.0)
                    after = total[h] - (stay_sum + _split_dot_deep(stay, upto))
                    w = jnp.exp(ls + after)
                    if causal is not None:
                        w = jnp.where(causal, w, 0.0)
                    dlw = _dot(dob[h], vt, NT) * w
                    prior = dlw_sum + _split_dot_deep(dlw, before)
                    sig = jnp.exp(ls)
                    dz = (dlw * (1.0 - sig) - sig * prior) * scale
                    if causal is not None:
                        dz = jnp.where(causal, dz, 0.0)
                    dzb = dz.astype(BF16)
                    dq = dq + _dot(dzb, kt)
                    adds.append((lanes, _dot(dzb, qb[h], TN), _dot(w.astype(BF16), dob[h], TN)))
                    out.append((stay_sum + jnp.sum(stay, axis=1, keepdims=True),
                                dlw_sum + jnp.sum(dlw, axis=1, keepdims=True), dq))
                for lanes, dk, dv in adds:
                    dk_ref[pl.ds(s0, SB_K), lanes] += dk
                    dv_ref[pl.ds(s0, SB_K), lanes] += dv
                return tuple(out)

            def k_blocks(j, st):
                for u in range(SB_UNROLL):
                    st = tile(pl.multiple_of((SB_UNROLL * j + u) * SB_K, SB_K), st, None)
                return st

            zero = jnp.zeros((tq, 1), F32)
            state = tuple((zero, zero, jnp.zeros((tq, HEAD_DIM), F32)) for _ in range(hp))
            state = lax.fori_loop(0, diag_tiles * qi // SB_UNROLL, k_blocks, state)
            for dt in range(diag_tiles):
                state = tile(t0 + dt * SB_K, state, col + dt * SB_K < row)
            for h, lanes in enumerate(heads):
                o_ref[0, pl.ds(t0, tq), lanes] = state[h][2].astype(BF16)
            return 0

        lax.fori_loop(0, s // tq, q_block, 0)
        o_ref[1] = dk_ref[...].astype(BF16)
        o_ref[2] = dv_ref[...].astype(BF16)

    return pl.pallas_call(
        body, name=f"sb_backward_l{layer}", grid=(e // width,),
        in_specs=[pl.BlockSpec((4, s, width), lambda h: (0, 0, h)),
                  pl.BlockSpec((s, width), lambda h: (0, h)),
                  pl.BlockSpec((s, width), lambda h: (0, h)),
                  pl.BlockSpec((hp, s, 1), lambda h: (h, 0, 0))],
        out_specs=pl.BlockSpec((4, s, width), lambda h: (0, 0, h)),
        out_shape=jax.ShapeDtypeStruct((N_DEV, s, e), BF16),
        scratch_shapes=[pltpu.VMEM((s, width), BF16), pltpu.VMEM((s, width), BF16),
                        pltpu.VMEM((s, width), BF16), pltpu.VMEM((s, width), F32),
                        pltpu.VMEM((s, width), F32)],
        compiler_params=_params("parallel"),
    )(proj, y, dmixed, tot)


def _norm_bwd(dn, xh, r, gain):
    dxh = dn * gain
    return r * (dxh - xh * jnp.mean(dxh * xh, axis=-1, keepdims=True)), dn * xh


def _chunk_backward(proj, bias, q_gain, k_gain, y, dmixed, dproj, layer):
    _, s, e = proj.shape
    hp = _tile(e // HEAD_DIM, CHUNK_HEADS)
    width = hp * HEAD_DIM
    steps = e // width
    unroll = _tile(s // PAIR, PAIR_UNROLL)
    scale = HEAD_DIM ** -0.5
    heads = [slice(h * HEAD_DIM, (h + 1) * HEAD_DIM) for h in range(hp)]

    def body(p_ref, b_ref, qg_ref, kg_ref, y_ref, dm_ref, dp_in, o_ref, db_ref, dqg_ref, dkg_ref,
             qn_ref, kp_ref, vp_ref, do_ref, dqn_ref, dkn_ref, dvp_ref):
        del dp_in
        kp_ref[pl.ds(0, PAD_K), :] = jnp.zeros((PAD_K, width), BF16)
        vp_ref[pl.ds(0, PAD_K), :] = jnp.zeros((PAD_K, width), BF16)
        vp_ref[pl.ds(PAD_K, s), :] = p_ref[2].astype(BF16)
        for lanes in heads:
            qn_ref[:, lanes] = _qk_norm(p_ref[0, :, lanes], qg_ref[...])[2].astype(BF16)
            kp_ref[pl.ds(PAD_K, s), lanes] = _qk_norm(p_ref[1, :, lanes], kg_ref[...])[2].astype(BF16)
        silu, dsilu = _silu_and_grad(p_ref[3])
        dm = dm_ref[...]
        do_ref[...] = (dm * silu).astype(BF16)
        o_ref[3] = (dm * y_ref[...] * dsilu).astype(BF16)
        dkn_ref[...] = jnp.zeros_like(dkn_ref)
        dvp_ref[...] = jnp.zeros_like(dvp_ref)
        db_ref[...] = jnp.zeros_like(db_ref)

        def chunks(ci, _):
            done = []
            for u in range(unroll):
                t0 = pl.multiple_of((ci * unroll + u) * PAIR, PAIR)
                for h, lanes in enumerate(heads):
                    qc = qn_ref[pl.ds(t0, PAIR), lanes]
                    kw = kp_ref[pl.ds(t0, PAIR_W), lanes]
                    vw = vp_ref[pl.ds(t0, PAIR_W), lanes]
                    dob = do_ref[pl.ds(t0, PAIR), lanes]
                    probs = _chunk_scores(qc, kw, b_ref[h], t0, scale)
                    dprobs = _dot(dob, vw, NT)
                    dsc = probs * (dprobs - jnp.sum(probs * dprobs, axis=-1, keepdims=True))
                    dsb = (dsc * scale).astype(BF16)
                    done.append((t0, h, lanes, dsc, _dot(dsb, kw), _dot(dsb, qc, TN),
                                 _dot(probs.astype(BF16), dob, TN)))
            for t0, h, lanes, dsc, dqn, dkn, dvp in done:
                db_ref[h] += dsc
                dqn_ref[pl.ds(t0, PAIR), lanes] = dqn
                dkn_ref[pl.ds(t0, PAIR_W), lanes] += dkn
                dvp_ref[pl.ds(t0, PAIR_W), lanes] += dvp
            return 0

        lax.fori_loop(0, s // (PAIR * unroll), chunks, 0)
        o_ref[2] = dvp_ref[pl.ds(PAD_K, s), :].astype(BF16)

        @pl.when(pl.program_id(0) == 0)
        def _():
            dqg_ref[...] = jnp.zeros_like(dqg_ref)
            dkg_ref[...] = jnp.zeros_like(dkg_ref)

        for lanes in heads:
            qh, rq, _ = _qk_norm(p_ref[0, :, lanes], qg_ref[...])
            dq, dqg_rows = _norm_bwd(dqn_ref[:, lanes], qh, rq, qg_ref[...])
            o_ref[0, :, lanes] = dq.astype(BF16)
            dqg_ref[...] += jnp.sum(dqg_rows, axis=0, keepdims=True)
            kh, rk, _ = _qk_norm(p_ref[1, :, lanes], kg_ref[...])
            dk, dkg_rows = _norm_bwd(dkn_ref[pl.ds(PAD_K, s), lanes], kh, rk, kg_ref[...])
            o_ref[1, :, lanes] = dk.astype(BF16)
            dkg_ref[...] += jnp.sum(dkg_rows, axis=0, keepdims=True)

    return pl.pallas_call(
        body, name=f"chunk_backward_l{layer}", grid=(steps,),
        in_specs=[pl.BlockSpec((4, s, width), lambda h: (1, 0, h)),
                  pl.BlockSpec((hp, PAIR, PAIR_W), lambda h: (layer * steps + h, 0, 0)),
                  pl.BlockSpec((1, HEAD_DIM), lambda h: (0, 0)),
                  pl.BlockSpec((1, HEAD_DIM), lambda h: (0, 0)),
                  pl.BlockSpec((s, width), lambda h: (0, steps + h)),
                  pl.BlockSpec((s, width), lambda h: (0, steps + h)),
                  _any()],
        out_specs=(pl.BlockSpec((4, s, width), lambda h: (1, 0, h)),
                   pl.BlockSpec((hp, PAIR, PAIR_W), lambda h: (h, 0, 0)),
                   pl.BlockSpec((1, HEAD_DIM), lambda h: (0, 0)),
                   pl.BlockSpec((1, HEAD_DIM), lambda h: (0, 0))),
        out_shape=(jax.ShapeDtypeStruct(dproj.shape, BF16),
                   jax.ShapeDtypeStruct((e // HEAD_DIM, PAIR, PAIR_W), F32),
                   jax.ShapeDtypeStruct((1, HEAD_DIM), F32), jax.ShapeDtypeStruct((1, HEAD_DIM), F32)),
        input_output_aliases={6: 0},
        scratch_shapes=[pltpu.VMEM((s, width), BF16), pltpu.VMEM((s + PAD_K, width), BF16),
                        pltpu.VMEM((s + PAD_K, width), BF16), pltpu.VMEM((s, width), BF16),
                        pltpu.VMEM((s, width), F32), pltpu.VMEM((s + PAD_K, width), F32),
                        pltpu.VMEM((s + PAD_K, width), F32)],
        compiler_params=_params("arbitrary"),
    )(proj, bias, q_gain, k_gain, y, dmixed, dproj)


def _proj_bwd_input(dproj, w_all, x, g, dx, tie, layer):
    s, d = x.shape
    e = w_all.shape[2]
    tm = _tile(s, 512)
    gs = 1
    steps = N_DEV // gs

    def body(dp_ref, w_ref, x_ref, g_ref, dx_ref, tie_ref, o_ref, dg_ref, acc_ref):
        del tie_ref
        j = pl.program_id(1)

        @pl.when(j == 0)
        def _():
            acc_ref[...] = jnp.zeros_like(acc_ref)

        part = _dot(dp_ref[0], w_ref[0], NT)
        for k in range(1, gs):
            part = part + _dot(dp_ref[k], w_ref[k], NT)
        acc_ref[...] += part

        @pl.when(jnp.logical_and(j == steps - 1, pl.program_id(0) == 0))
        def _():
            dg_ref[...] = jnp.zeros_like(dg_ref)

        @pl.when(j == steps - 1)
        def _():
            xv = x_ref[...]
            r = lax.rsqrt(jnp.mean(xv * xv, axis=-1, keepdims=True) + NORM_EPS)
            dxn, dg_rows = _norm_bwd(acc_ref[...], xv * r, r, g_ref[...])
            o_ref[...] = dx_ref[...] + dxn
            dg_ref[...] += jnp.sum(dg_rows, axis=0, keepdims=True)

    return pl.pallas_call(
        body, name=f"proj_dx_l{layer}", grid=(s // tm, steps),
        in_specs=[pl.BlockSpec((gs, tm, e), lambda m, j: (j, m, 0)),
                  pl.BlockSpec((gs, d, e), lambda m, j: (j, 0, 0)),
                  pl.BlockSpec((tm, d), lambda m, j: (m, 0)),
                  pl.BlockSpec((1, d), lambda m, j: (0, 0)),
                  pl.BlockSpec((tm, d), lambda m, j: (m, 0)), _any()],
        out_specs=(pl.BlockSpec((tm, d), lambda m, j: (m, 0)), pl.BlockSpec((1, d), lambda m, j: (0, 0))),
        out_shape=(jax.ShapeDtypeStruct((s, d), F32), jax.ShapeDtypeStruct((1, d), F32)),
        scratch_shapes=[pltpu.VMEM((tm, d), F32)],
        compiler_params=_params("arbitrary", "arbitrary"),
    )(dproj, w_all, x, g, dx, tie)


def _proj_bwd_weight(h, dproj, layer):
    s, d = h.shape
    e = dproj.shape[2]
    td, tn = _tile(d, 1024), _tile(e, 1024)
    nb = e // tn

    def body(h_ref, dp_ref, o_ref):
        o_ref[...] = _dot(h_ref[...], dp_ref[...], TN).astype(BF16)

    return pl.pallas_call(
        body, name=f"proj_dw_l{layer}", grid=(d // td, N_DEV * nb),
        in_specs=[pl.BlockSpec((s, td), lambda i, n: (0, i)),
                  pl.BlockSpec((None, s, tn), lambda i, n: (n // nb, 0, n % nb))],
        out_specs=pl.BlockSpec((None, td, tn), lambda i, n: (n // nb, i, n % nb)),
        out_shape=jax.ShapeDtypeStruct((N_DEV, d, e), BF16),
        compiler_params=_params("parallel", "parallel"),
    )(h, dproj)


def _adamw_math(w, g, m, v):
    m = ADAM_B1 * m + (1.0 - ADAM_B1) * g
    v = ADAM_B2 * v + (1.0 - ADAM_B2) * (g * g)
    m_hat = m / (1.0 - ADAM_B1 ** ADAM_STEP)
    v_hat = v / (1.0 - ADAM_B2 ** ADAM_STEP)
    return -ADAM_LR * (m_hat / (jnp.sqrt(v_hat) + ADAM_EPS) + ADAM_WD * w), m, v


def _adamw_layer(parts, own, me, w, m, v, prev, layer, name):
    n_layers, rows, cols = w.shape
    n_parts = parts.shape[0]
    tr = _tile(rows, max(8, (256 * 1024) // cols))

    def body(me_ref, p_ref, own_ref, w_ref, m_ref, v_ref, *rest):
        g_ref, d_ref, nm_ref, nv_ref = rest[-4:]
        mine = own_ref[...].astype(F32)
        if n_parts == N_DEV:
            g = None
            for j in range(N_DEV):
                term = jnp.where(me_ref[0] == j, mine, p_ref[j].astype(F32))
                g = term if g is None else g + term
        else:
            g = mine
            for j in range(n_parts):
                g = g + p_ref[j].astype(F32)
        g_ref[...] = g
        d_ref[...], nm_ref[...], nv_ref[...] = _adamw_math(w_ref[...], g, m_ref[...], v_ref[...])

    blk = pl.BlockSpec((None, tr, cols), lambda i, me_ref: (layer, i, 0))
    out_shape = tuple(jax.ShapeDtypeStruct(w.shape, F32) for _ in range(4))
    in_specs = [pl.BlockSpec((n_parts, tr, cols), lambda i, me_ref: (0, i, 0)),
                pl.BlockSpec((None, tr, cols), lambda i, me_ref: (me_ref[0], i, 0)), blk, blk, blk]
    args = [me, parts, own, w, m, v]
    aliases = {}
    if prev is not None:
        in_specs += [_any()] * 4
        args += list(prev)
        aliases = {6 + k: k for k in range(4)}
    return pl.pallas_call(
        body, name=f"{name}_l{layer}",
        grid_spec=pltpu.PrefetchScalarGridSpec(
            num_scalar_prefetch=1, grid=(rows // tr,), in_specs=in_specs, out_specs=(blk, blk, blk, blk)),
        out_shape=out_shape, input_output_aliases=aliases,
        compiler_params=_params("parallel"),
    )(*args)


def _sum_slots(parts):
    def body(p_ref, o_ref):
        g = p_ref[0]
        for j in range(1, N_DEV):
            g = g + p_ref[j]
        o_ref[...] = g

    return pl.pallas_call(
        body, name="sum_small_grads",
        in_specs=[_vmem()], out_specs=_vmem(),
        out_shape=jax.ShapeDtypeStruct(parts.shape[1:], F32),
        compiler_params=_params(),
    )(parts)


def _adamw_small(w, g, m, v):
    def body(w_ref, g_ref, m_ref, v_ref, d_ref, nm_ref, nv_ref):
        d_ref[...], nm_ref[...], nv_ref[...] = _adamw_math(w_ref[...], g_ref[...], m_ref[...], v_ref[...])

    return pl.pallas_call(
        body, name="adamw_small",
        in_specs=[_vmem()] * 4, out_specs=(_vmem(),) * 3,
        out_shape=tuple(jax.ShapeDtypeStruct(w.shape, F32) for _ in range(3)),
        compiler_params=_params(),
    )(w, g, m, v)


def _pack_rows(arrays):
    rows = []
    for a in arrays:
        flat = a.reshape(-1)
        pad = (-flat.shape[0]) % (8 * LANES)
        rows.append(jnp.pad(flat, (0, pad)).reshape(-1, LANES))
    return jnp.concatenate(rows, axis=0)


def _unpack_rows(packed, like):
    out, r0 = [], 0
    for a in like:
        n = a.size
        nr = -(-n // (8 * LANES)) * 8
        out.append(packed[r0:r0 + nr].reshape(-1)[:n].reshape(a.shape))
        r0 += nr
    return out


def kernel(x, norm_g, w_in, q_norm_g, k_norm_g, rel_bias, w_out, loss_target, m_norm_g, m_w_in, m_q_norm_g, m_k_norm_g, m_rel_bias, m_w_out, v_norm_g, v_w_in, v_q_norm_g, v_k_norm_g, v_rel_bias, v_w_out):
    depth, d, e = w_in.shape
    r_out = w_out.shape[1]
    heads = e // HEAD_DIM
    rel_w = rel_bias.shape[2]
    x0 = x[0]
    target = loss_target[0]
    s = x0.shape[0]

    me = jnp.reshape(_flat(_my_place()), (1,)).astype(jnp.int32)

    casts = [(_cast_layer(w_in, me, l, "cast_w_in"), _cast_layer(w_out, me, l, "cast_w_out"))
             for l in range(depth)]

    def begin_gather(l, after):
        (win_b, win_land), (wout_b, wout_land) = casts[l]
        return _gather_send((win_b, wout_b), (win_land, wout_land), after, f"gather_send_l{l}")

    rel_all = _gather_small(rel_bias, [], "gather_rel_bias")
    sent = begin_gather(0, [rel_all])
    rel_full = jnp.transpose(rel_all, (1, 2, 0, 3)).reshape(depth * heads, N_DEV * rel_w)
    bias = jnp.transpose(_bias_expand(rel_full), (1, 0, 2))
    head_work = [bias] + [shard for cast in casts[1:] for shard, _ in cast]
    forwarded = _gather_forward(sent, head_work, "gather_forward_l0")

    xs, hs, projs, ys, mixes, tots, weights = [], [], [], [], [], [], []
    xl = x0
    for l in range(depth):
        win_all, wout_all = _gather_finish(forwarded, [xl, forwarded[-1]], f"gather_finish_l{l}")
        more = l + 1 < depth
        if more:
            sent = begin_gather(l + 1, [win_all])
        wout_full = wout_all.reshape(d, d)
        proj, h = _norm_proj(xl, norm_g[l:l + 1], win_all, sent[-1] if more else None, l)
        y, mixed, tot = _sb_forward(proj, l)
        if more:
            forwarded = _gather_forward(sent, [tot], f"gather_forward_l{l + 1}")
        y, mixed = _chunk_forward(proj, bias, q_norm_g[l:l + 1], k_norm_g[l:l + 1], y, mixed,
                                  forwarded[-1] if more else None, l)
        xs.append(xl), hs.append(h), projs.append(proj), ys.append(y), mixes.append(mixed), tots.append(tot)
        weights.append((win_all, wout_full))
        xl = _out_proj(mixed, wout_full, xl, l)

    dx, loss_parts = _loss_head(xl, target)
    loss = lax.psum(jnp.sum(loss_parts), AXES)

    dbias, dng, dqg, dkg = [None] * depth, [None] * depth, [None] * depth, [None] * depth
    res_in, res_out = None, None

    peer_slots = jnp.stack([_flat(_flip(_my_place(), k)) for k in SAME_CORE]).astype(jnp.int32)

    def finish_exchange(exchanging, after, l):
        sems, bufs, own = exchanging
        copies = N_DEV - 1 if own is None else 3
        bufs = _wait_copies(f"exchange_finish_l{l}", bufs, sems, 2, lambda refs, t: refs[t].at[0], copies, after)
        own = bufs[:2] if own is None else own
        rin, rout = bufs[2:]
        return (_adamw_layer(rin, own[0], me, w_in, m_w_in, v_w_in, res_in, l, "adamw_w_in"),
                _adamw_layer(rout, own[1], me, w_out, m_w_out, v_w_out, res_out, l, "adamw_w_out"))

    pending = []
    for l in reversed(range(depth)):
        win_all, wout_full = weights[l]
        dmixed = _out_proj_bwd_input(dx, wout_full, None, l)
        gwout = _out_proj_bwd_weight(mixes[l], dx, l).reshape(N_DEV, r_out, d)
        dproj = _sb_backward(projs[l], ys[l], dmixed, tots[l], l)
        dproj, dbias[l], dqg[l], dkg[l] = _chunk_backward(
            projs[l], bias, q_norm_g[l:l + 1], k_norm_g[l:l + 1], ys[l], dmixed, dproj, l)
        grads_l = (_proj_bwd_weight(hs[l], dproj, l), gwout)
        if l > 0:
            sems, bufs, token = _exchange_direct(grads_l, [], f"exchange_direct_l{l}")
            pending.append(((sems, bufs, None), l))
        else:
            sems, bufs, token = _exchange_to_sibling(grads_l, [], "exchange_sibling_l0")
            bufs = _wait_copies("exchange_sibling_wait_l0", bufs, sems, 2, lambda refs, t: refs[t].at[0], 4, [token])
            own, parts, lands = bufs[0:2], bufs[2:4], bufs[4:6]
            csums = [_chip_sums(own[t], parts[t], peer_slots, f"chip_sums_{t}_l0") for t in range(2)]
            sems, csums_lands, token = _exchange_to_chips(csums, lands, [], "exchange_chips_l0")
            pending.append(((sems, csums_lands, own), l))
        dx, dng[l] = _proj_bwd_input(dproj, win_all, xs[l], norm_g[l:l + 1], dx, token, l)
    tie = token
    for exchanging, l in pending[:-1]:
        res_in, res_out = finish_exchange(exchanging, [dx, tie], l)
    drel = _bias_grad(jnp.concatenate(dbias, axis=0), [tie])
    small_like = [norm_g, q_norm_g, k_norm_g, drel]
    mine = _pack_rows([jnp.concatenate(dng, axis=0), jnp.concatenate(dqg, axis=0),
                       jnp.concatenate(dkg, axis=0), drel])
    gathered = _gather_small(mine, [res_in[0], res_out[0]], "gather_small_grads")
    g_norm, g_qn, g_kn, g_rel_full = _unpack_rows(_sum_slots(gathered), small_like)
    my_block = _flat(_my_place())
    g_rel = lax.dynamic_slice_in_dim(g_rel_full.reshape(depth, heads, N_REL), my_block * rel_w, rel_w, axis=2)
    small_w = [norm_g, q_norm_g, k_norm_g, rel_bias]
    small = _adamw_small(_pack_rows(small_w), _pack_rows([g_norm, g_qn, g_kn, g_rel]),
                         _pack_rows([m_norm_g, m_q_norm_g, m_k_norm_g, m_rel_bias]),
                         _pack_rows([v_norm_g, v_q_norm_g, v_k_norm_g, v_rel_bias]))
    d_small, nm_small, nv_small = (_unpack_rows(p, small_w) for p in small)

    res_in, res_out = finish_exchange(pending[-1][0], [small[0], res_in[0], res_out[0]], 0)
    g_win, d_win, nm_win, nv_win = res_in
    g_wout, d_wout, nm_wout, nv_wout = res_out
    grads = (g_norm, g_win, g_qn, g_kn, g_rel, g_wout)

    def order(sm, big_in, big_out):
        return (sm[0], big_in, sm[1], sm[2], sm[3], big_out)

    return (loss, dx[None], *grads, *order(d_small, d_win, d_wout),
            *order(nm_small, nm_win, nm_wout), *order(nv_small, nv_win, nv_wout))
```

```python
import functools

import jax
import jax.numpy as jnp
from jax import lax
from jax.experimental import pallas as pl
from jax.experimental.pallas import tpu as pltpu

F32 = jnp.float32
BF16 = jnp.bfloat16
MESH_ID = pl.DeviceIdType.MESH
AXES = ("x", "y", "c")

N_DEV = 8
HEAD_DIM = 128
CHUNK = 64
LEFT_CHUNKS = 8
BAND_W = (LEFT_CHUNKS + 1) * CHUNK
PAD_K = LEFT_CHUNKS * CHUNK
REL_CLIP = 256
N_REL = REL_CLIP + CHUNK
NORM_EPS = 1e-6
NEG_BIG = -1e30
GROUP_CHUNKS = 4
PAIR = GROUP_CHUNKS * CHUNK
PAIR_W = BAND_W + PAIR - CHUNK
CHUNK_HEADS = 2
CHUNK_HEADS_BWD = 1
PAIR_UNROLL_BWD = 2
PAIR_UNROLL = 1
SB_Q = 512
SB_K = 128
SB_HEADS = 2
SB_UNROLL = 4
LANES = 128

ADAM_LR = 0.001
ADAM_B1 = 0.9
ADAM_B2 = 0.999
ADAM_EPS = 1e-08
ADAM_WD = 0.01
ADAM_STEP = 10

VMEM_LIMIT_BYTES = 56 * 1024 * 1024

NT = (((1,), (1,)), ((), ()))
TN = (((0,), (0,)), ((), ()))


def _params(*sem, **kw):
    return pltpu.CompilerParams(dimension_semantics=sem or None, vmem_limit_bytes=VMEM_LIMIT_BYTES, **kw)


def _any():
    return pl.BlockSpec(memory_space=pl.ANY)


def _vmem():
    return pl.BlockSpec(memory_space=pltpu.VMEM)


def _tile(n, want):
    return want if n % want == 0 else n


def _dot(a, b, dims=None):
    if dims is None:
        return jnp.dot(a, b, preferred_element_type=F32)
    return lax.dot_general(a, b, dims, preferred_element_type=F32)


def _split_dot(a, b, parts, dims=None):
    acc = None
    rest = a
    for _ in range(parts):
        piece = rest.astype(BF16)
        rest = rest - piece.astype(F32)
        term = _dot(piece, b, dims)
        acc = term if acc is None else acc + term
    return acc


def _split_dot_deep(a, b_twice):
    high = a.astype(BF16)
    low = (a - high.astype(F32)).astype(BF16)
    return _dot(jnp.concatenate([high, low], axis=1), b_twice)


def _log_sigmoid(z):
    return jnp.minimum(z, 0.0) - jnp.log(1.0 + jnp.exp(-jnp.abs(z)))


def _silu_and_grad(g):
    sig = jax.nn.sigmoid(g)
    return g * sig, sig * (1.0 + g * (1.0 - sig))


def _my_place():
    return lax.axis_index("x"), lax.axis_index("y"), lax.axis_index("c")


def _flat(place):
    return 4 * place[0] + 2 * place[1] + place[2]


def _flip(place, k):
    return tuple(1 - p if (k >> s) & 1 else p for p, s in zip(place, (2, 1, 0)))


def _cast_layer(w, me, layer, name):
    _, rows, cols = w.shape
    tr = _tile(rows, 1024)

    def body(me_ref, a_ref, shard_ref, land_ref):
        del me_ref
        shard_ref[...] = a_ref[...].astype(BF16)
        land_ref[...] = shard_ref[...]

    return pl.pallas_call(
        body, name=f"{name}_l{layer}",
        grid_spec=pltpu.PrefetchScalarGridSpec(
            num_scalar_prefetch=1, grid=(rows // tr,),
            in_specs=[pl.BlockSpec((None, tr, cols), lambda i, me_ref: (layer, i, 0))],
            out_specs=(pl.BlockSpec((tr, cols), lambda i, me_ref: (i, 0)),
                       pl.BlockSpec((None, tr, cols), lambda i, me_ref: (me_ref[0], i, 0)))),
        out_shape=(jax.ShapeDtypeStruct((rows, cols), BF16), jax.ShapeDtypeStruct((N_DEV, rows, cols), BF16)),
        compiler_params=_params("parallel"),
    )(me, w)


HBM_SPEC = pl.BlockSpec(memory_space=pltpu.HBM)
SEM_SPEC = pl.BlockSpec(memory_space=pltpu.SEMAPHORE)
SAME_CORE = (2, 4, 6)
SIBLING = 1
SPLIT_EFFECT = pltpu.SideEffectType.DATAFLOW_SIDE_EFFECTING


def _split_call(body, name, bufs, sems_in, sem_counts_out, after, token):
    bufs, sems_in, after = list(bufs), list(sems_in), list(after)
    nb, ni, no = len(bufs), len(sems_in), len(sem_counts_out)

    def wrapped(*refs):
        outs = nb + ni + len(after)
        body(refs[:nb], refs[nb:nb + ni], refs[outs:outs + no])
        if token:
            refs[-1][...] = jnp.zeros_like(refs[-1])

    out = pl.pallas_call(
        wrapped, name=name,
        in_specs=[HBM_SPEC] * nb + [SEM_SPEC] * ni + [_any()] * len(after),
        out_specs=tuple([SEM_SPEC] * no + [HBM_SPEC] * nb + ([_vmem()] if token else [])),
        out_shape=tuple([pltpu.SemaphoreType.DMA((c,)) for c in sem_counts_out]
                        + [pltpu.HBM(a.shape, a.dtype) for a in bufs]
                        + ([jax.ShapeDtypeStruct((8, LANES), F32)] if token else [])),
        input_output_aliases={i: no + i for i in range(nb)},
        compiler_params=pltpu.CompilerParams(has_side_effects=SPLIT_EFFECT),
    )(*[pltpu.with_memory_space_constraint(a, pltpu.HBM) for a in bufs], *sems_in, *after)
    return list(out[:no]), list(out[no:no + nb]), (out[-1] if token else None)


def _remote(src, dst, send_sems, recv_sems, i, to):
    return pltpu.make_async_remote_copy(src_ref=src, dst_ref=dst, send_sem=send_sems.at[i], recv_sem=recv_sems.at[i],
                                        device_id=to, device_id_type=MESH_ID)


def _wait_copies(name, bufs, sems, n, slot_of, count, after):
    def body(refs, sems_in, _):
        me = _my_place()
        for t in range(n):
            slot = slot_of(refs, t)
            for a in range(count):
                cp = _remote(slot, slot, sems_in[0], sems_in[1], t * count + a, me)
                cp.wait_send()
                cp.wait_recv()

    return _split_call(body, name, bufs, sems, (), after, False)[1]


def _exchange_direct(grads, after, name):
    n = len(grads)
    lands = [lax.empty(g.shape, g.dtype) for g in grads]

    def body(refs, _, sems_out):
        me = _my_place()
        for k in range(1, N_DEV):
            peer = _flip(me, k)
            for t in range(n):
                _remote(refs[t].at[_flat(peer)], refs[n + t].at[_flat(me)], *sems_out, t * (N_DEV - 1) + k - 1,
                        peer).start()

    return _split_call(body, name, list(grads) + lands, (), (n * (N_DEV - 1), n * (N_DEV - 1)), after, True)


def _exchange_to_sibling(grads, after, name):
    n = len(grads)
    parts = [lax.empty((3,) + g.shape[1:], g.dtype) for g in grads]
    lands = [lax.empty((4,) + g.shape[1:], g.dtype) for g in grads]

    def body(refs, _, sems_out):
        me = _my_place()
        sib = _flip(me, SIBLING)
        for t in range(n):
            g, part, land = refs[t], refs[n + t], refs[2 * n + t]
            _remote(g.at[_flat(sib)], land.at[0], *sems_out, 4 * t, sib).start()
            for a, k in enumerate(SAME_CORE):
                _remote(g.at[_flat(_flip(sib, k))], part.at[a], *sems_out, 4 * t + 1 + a, sib).start()

    return _split_call(body, name, list(grads) + parts + lands, (), (4 * n, 4 * n), after, True)


def _chip_sums(grads, parts, slots, name):
    _, rows, cols = grads.shape
    tr = _tile(rows, max(8, (512 * 1024) // cols))

    def body(slots_ref, g_ref, p_ref, o_ref):
        del slots_ref
        o_ref[...] = (g_ref[...].astype(F32) + p_ref[...].astype(F32)).astype(BF16)

    return pl.pallas_call(
        body, name=name,
        grid_spec=pltpu.PrefetchScalarGridSpec(
            num_scalar_prefetch=1, grid=(3, rows // tr),
            in_specs=[pl.BlockSpec((None, tr, cols), lambda a, i, slots_ref: (slots_ref[a], i, 0)),
                      pl.BlockSpec((None, tr, cols), lambda a, i, slots_ref: (a, i, 0))],
            out_specs=pl.BlockSpec((None, tr, cols), lambda a, i, slots_ref: (a, i, 0))),
        out_shape=jax.ShapeDtypeStruct((3, rows, cols), BF16),
        compiler_params=_params("parallel", "parallel"),
    )(slots, grads, parts)


def _exchange_to_chips(csums, lands, after, name):
    n = len(csums)

    def body(refs, _, sems_out):
        me = _my_place()
        for t in range(n):
            for a, k in enumerate(SAME_CORE):
                _remote(refs[t].at[a], refs[n + t].at[1 + a], *sems_out, 3 * t + a, _flip(me, k)).start()

    return _split_call(body, name, list(csums) + list(lands), (), (3 * n, 3 * n), after, True)


def _hbm_call(body, name, n_hbm, sems_in, sems_out, after, token, like):
    after = list(after)
    in_specs = [HBM_SPEC] * n_hbm + [SEM_SPEC] * len(sems_in) + [_any()] * len(after)
    out_specs = [SEM_SPEC] * len(sems_out) + [HBM_SPEC] * n_hbm + ([_vmem()] if token else [])
    out_shape = ([pltpu.SemaphoreType.DMA((c,)) for c in sems_out] + [pltpu.HBM(a.shape, a.dtype) for a in like]
                 + ([jax.ShapeDtypeStruct((8, LANES), F32)] if token else []))
    return in_specs, tuple(out_specs), tuple(out_shape), {i: len(sems_out) + i for i in range(n_hbm)}, after


def _gather_send(shards, lands, after, name):
    n = len(shards)
    peers = (SIBLING,) + SAME_CORE
    after = list(after)

    def body(*refs):
        me = _my_place()
        send_sems, recv_sems = refs[2 * n + len(after)], refs[2 * n + len(after) + 1]
        for a, k in enumerate(peers):
            for t in range(n):
                pltpu.make_async_remote_copy(
                    src_ref=refs[t], dst_ref=refs[n + t].at[_flat(me)],
                    send_sem=send_sems.at[t * 4 + a], recv_sem=recv_sems.at[t * 4 + a],
                    device_id=_flip(me, k), device_id_type=MESH_ID).start()
        refs[-1][...] = jnp.zeros_like(refs[-1])

    bufs = list(shards) + list(lands)
    in_specs, out_specs, out_shape, aliases, after = _hbm_call(body, name, 2 * n, (), (4 * n, 4 * n), after, True, bufs)
    out = pl.pallas_call(
        body, name=name, in_specs=in_specs, out_specs=out_specs, out_shape=out_shape,
        input_output_aliases=aliases, compiler_params=pltpu.CompilerParams(has_side_effects=SPLIT_EFFECT),
    )(*[pltpu.with_memory_space_constraint(a, pltpu.HBM) for a in bufs], *after)
    return out[0], out[1], out[2:2 + n], out[2 + n:2 + 2 * n], out[-1]


def _gather_forward(sent, after, name):
    send1, recv1, shards, lands, _ = sent
    n = len(shards)
    after = list(after)

    def body(*refs):
        me = _my_place()
        recv1_ref = refs[2 * n + 1]
        out0 = 2 * n + 2 + len(after)
        send2_ref, recv2_ref = refs[out0], refs[out0 + 1]
        for a, k in enumerate(SAME_CORE):
            owner = _flat(_flip(me, k))
            for t in range(n):
                slot = refs[n + t].at[owner]
                pltpu.make_async_remote_copy(
                    src_ref=refs[t], dst_ref=slot, send_sem=refs[2 * n].at[t * 4 + 1 + a],
                    recv_sem=recv1_ref.at[t * 4 + 1 + a], device_id=_flip(me, k), device_id_type=MESH_ID).wait_recv()
                pltpu.make_async_remote_copy(
                    src_ref=slot, dst_ref=slot, send_sem=send2_ref.at[t * 3 + a], recv_sem=recv2_ref.at[t * 3 + a],
                    device_id=_flip(me, SIBLING), device_id_type=MESH_ID).start()
        refs[-1][...] = jnp.zeros_like(refs[-1])

    bufs = list(shards) + list(lands)
    in_specs, out_specs, out_shape, aliases, after = _hbm_call(body, name, 2 * n, (4 * n, 4 * n), (3 * n, 3 * n), after,
                                                               True, bufs)
    out = pl.pallas_call(
        body, name=name, in_specs=in_specs, out_specs=out_specs, out_shape=out_shape,
        input_output_aliases=aliases, compiler_params=pltpu.CompilerParams(has_side_effects=SPLIT_EFFECT),
    )(*bufs, send1, recv1, *after)
    return (send1, recv1), (out[0], out[1]), out[2:2 + n], out[2 + n:2 + 2 * n], out[-1]


def _gather_finish(forwarded, after, name):
    (send1, recv1), (send2, recv2), shards, lands, _ = forwarded
    n = len(shards)
    after = list(after)

    def body(*refs):
        me = _my_place()
        send1_ref, recv1_ref, send2_ref, recv2_ref = refs[2 * n:2 * n + 4]
        sib = _flip(me, SIBLING)
        for t in range(n):
            for a in range(4):
                cp = pltpu.make_async_remote_copy(
                    src_ref=refs[t], dst_ref=refs[n + t].at[_flat(sib)], send_sem=send1_ref.at[t * 4 + a],
                    recv_sem=recv1_ref.at[t * 4 + a], device_id=sib, device_id_type=MESH_ID)
                cp.wait_send()
                if a == 0:
                    cp.wait_recv()
            for a in range(3):
                cp = pltpu.make_async_remote_copy(
                    src_ref=refs[t], dst_ref=refs[n + t].at[_flat(sib)], send_sem=send2_ref.at[t * 3 + a],
                    recv_sem=recv2_ref.at[t * 3 + a], device_id=sib, device_id_type=MESH_ID)
                cp.wait_send()
                cp.wait_recv()

    bufs = list(shards) + list(lands)
    in_specs, out_specs, out_shape, aliases, after = _hbm_call(body, name, 2 * n, (4 * n, 4 * n, 3 * n, 3 * n), (), after,
                                                               False, bufs)
    out = pl.pallas_call(
        body, name=name, in_specs=in_specs, out_specs=out_specs, out_shape=out_shape,
        input_output_aliases=aliases, compiler_params=pltpu.CompilerParams(has_side_effects=SPLIT_EFFECT),
    )(*bufs, send1, recv1, send2, recv2, *after)
    return out[n:]


def _gather_small(v, after, name):
    after = list(after)

    def body(v_ref, *rest):
        o_ref, send_sems, recv_sems = rest[-3:]
        me = _my_place()
        o_ref[_flat(me)] = v_ref[...]
        copies = []
        for k in range(1, N_DEV):
            copies.append(pltpu.make_async_remote_copy(
                src_ref=v_ref, dst_ref=o_ref.at[_flat(me)],
                send_sem=send_sems.at[k - 1], recv_sem=recv_sems.at[k - 1],
                device_id=_flip(me, k), device_id_type=MESH_ID))
        for cp in copies:
            cp.start()
        for cp in copies:
            cp.wait()

    return pl.pallas_call(
        body, name=name,
        in_specs=[_vmem()] + [_any()] * len(after), out_specs=_vmem(),
        out_shape=jax.ShapeDtypeStruct((N_DEV,) + v.shape, v.dtype),
        scratch_shapes=[pltpu.SemaphoreType.DMA((7,)), pltpu.SemaphoreType.DMA((7,))],
        compiler_params=_params(has_side_effects=True),
    )(v, *after)


def _rel_onehot(row):
    r_io = lax.broadcasted_iota(jnp.int32, (N_REL, PAIR_W), 0)
    p_io = lax.broadcasted_iota(jnp.int32, (N_REL, PAIR_W), 1)
    band_col = p_io - (row // CHUNK) * CHUNK
    in_band = jnp.logical_and(band_col >= 0, band_col < BAND_W)
    idx = jnp.clip(PAD_K + row % CHUNK - band_col, -(CHUNK - 1), REL_CLIP) + (CHUNK - 1)
    return jnp.logical_and(r_io == idx, in_band).astype(BF16), in_band[0:1]


def _bias_expand(rel):
    lh = rel.shape[0]

    def body(rel_ref, o_ref):
        onehot, in_band = _rel_onehot(pl.program_id(0))
        o_ref[...] = jnp.where(in_band, _split_dot(rel_ref[...], onehot, 3), NEG_BIG)

    return pl.pallas_call(
        body, name="bias_expand", grid=(PAIR,),
        in_specs=[pl.BlockSpec((lh, N_REL), lambda i: (0, 0))],
        out_specs=pl.BlockSpec((None, lh, PAIR_W), lambda i: (i, 0, 0)),
        out_shape=jax.ShapeDtypeStruct((PAIR, lh, PAIR_W), F32),
        compiler_params=_params("parallel"),
    )(rel)


SHEAR_W = PAIR_W + PAIR
BIAS_HEADS = 8


def _bias_grad(dbias, after):
    lh = dbias.shape[0]
    hb = _tile(lh, BIAS_HEADS)
    after = list(after)

    def body(db_ref, *rest):
        o_ref = rest[-1]
        a_io = lax.broadcasted_iota(jnp.int32, (PAIR, PAIR), 0)
        b_io = lax.broadcasted_iota(jnp.int32, (PAIR, PAIR), 1)
        flip_rows = (a_io + b_io == PAIR - 1).astype(BF16)
        diags = []
        for j in range(hb):
            rest_part = jnp.concatenate([db_ref[j], jnp.zeros((PAIR, SHEAR_W - PAIR_W), F32)], axis=1)
            flipped = None
            for _ in range(3):
                piece = rest_part.astype(BF16)
                rest_part = rest_part - piece.astype(F32)
                term = _dot(flip_rows, piece)
                flipped = term if flipped is None else flipped + term
            sheared = pltpu.roll(flipped, 0, 1, stride=1, stride_axis=0)
            diags.append(jnp.sum(sheared, axis=0, keepdims=True))
        c_io = lax.broadcasted_iota(jnp.int32, (SHEAR_W, N_REL), 0)
        r_io = lax.broadcasted_iota(jnp.int32, (SHEAR_W, N_REL), 1)
        entry = jnp.clip(PAD_K + (PAIR - 1) - c_io, -(CHUNK - 1), REL_CLIP) + (CHUNK - 1)
        o_ref[...] = _split_dot(jnp.concatenate(diags, axis=0), (r_io == entry).astype(BF16), 3)

    return pl.pallas_call(
        body, name="bias_grad", grid=(lh // hb,),
        in_specs=[pl.BlockSpec((hb, PAIR, PAIR_W), lambda i: (i, 0, 0))] + [_any()] * len(after),
        out_specs=pl.BlockSpec((hb, N_REL), lambda i: (i, 0)),
        out_shape=jax.ShapeDtypeStruct((lh, N_REL), F32),
        compiler_params=_params("parallel"),
    )(dbias, *after)


def _norm_proj(x, g, w_all, tie, layer):
    s, d = x.shape
    e = w_all.shape[2]
    tm, tn = _tile(s, 1024), _tile(e, 1024)
    nb = e // tn
    ties = [] if tie is None else [tie]

    def body(x_ref, g_ref, w_ref, *rest):
        proj_ref, h_ref = rest[-2:]

        @pl.when(pl.program_id(1) == 0)
        def _():
            xv = x_ref[...]
            r = lax.rsqrt(jnp.mean(xv * xv, axis=-1, keepdims=True) + NORM_EPS)
            h_ref[...] = ((xv * r) * g_ref[...]).astype(BF16)

        proj_ref[...] = _dot(h_ref[...], w_ref[...])

    return pl.pallas_call(
        body, name=f"norm_proj_l{layer}", grid=(s // tm, N_DEV * nb),
        in_specs=[pl.BlockSpec((tm, d), lambda m, n: (m, 0)),
                  pl.BlockSpec((1, d), lambda m, n: (0, 0)),
                  pl.BlockSpec((None, d, tn), lambda m, n: (n // nb, 0, n % nb))] + [_any()] * len(ties),
        out_specs=(pl.BlockSpec((None, tm, tn), lambda m, n: (n // nb, m, n % nb)),
                   pl.BlockSpec((tm, d), lambda m, n: (m, 0))),
        out_shape=(jax.ShapeDtypeStruct((N_DEV, s, e), F32), jax.ShapeDtypeStruct((s, d), BF16)),
        compiler_params=_params("parallel", "arbitrary"),
    )(x, g, w_all, *ties)


def _sb_forward(proj, layer):
    _, s, e = proj.shape
    hp = _tile(e // HEAD_DIM, SB_HEADS)
    width = hp * HEAD_DIM
    tq = _tile(s, SB_Q)
    diag_tiles = tq // SB_K
    scale = HEAD_DIM ** -0.5

    def body(p_ref, y_ref, mix_ref, tot_ref, kb_ref, vb_ref):
        kb_ref[...] = p_ref[1].astype(BF16)
        vb_ref[...] = p_ref[2].astype(BF16)
        row = lax.broadcasted_iota(jnp.int32, (tq, SB_K), 0)
        col = lax.broadcasted_iota(jnp.int32, (tq, SB_K), 1)
        kj = lax.broadcasted_iota(jnp.int32, (SB_K, SB_K), 0)
        ks = lax.broadcasted_iota(jnp.int32, (SB_K, SB_K), 1)
        later = (kj > ks).astype(BF16)

        def q_block(qi, _):
            t0 = pl.multiple_of(qi * tq, tq)
            qb = [p_ref[0, pl.ds(t0, tq), h * HEAD_DIM:(h + 1) * HEAD_DIM].astype(BF16) for h in range(hp)]

            def tile(s0, state, causal):
                out = []
                for h in range(hp):
                    carry, acc = state[h]
                    lanes = slice(h * HEAD_DIM, (h + 1) * HEAD_DIM)
                    z = _dot(qb[h], kb_ref[pl.ds(s0, SB_K), lanes], NT) * scale
                    ls = _log_sigmoid(z)
                    stay = ls - z
                    if causal is not None:
                        stay = jnp.where(causal, stay, 0.0)
                    w = jnp.exp(ls + carry + _split_dot(stay, later, 2))
                    if causal is not None:
                        w = jnp.where(causal, w, 0.0)
                    acc = acc + _dot(w.astype(BF16), vb_ref[pl.ds(s0, SB_K), lanes])
                    out.append((carry + jnp.sum(stay, axis=1, keepdims=True), acc))
                return tuple(out)

            state = tuple((jnp.zeros((tq, 1), F32), jnp.zeros((tq, HEAD_DIM), F32)) for _ in range(hp))
            for dt in reversed(range(diag_tiles)):
                state = tile(t0 + dt * SB_K, state, col + dt * SB_K < row)

            def k_blocks(j, st):
                for u in range(SB_UNROLL):
                    st = tile(pl.multiple_of((diag_tiles * qi - 1 - SB_UNROLL * j - u) * SB_K, SB_K), st, None)
                return st

            state = lax.fori_loop(0, diag_tiles * qi // SB_UNROLL, k_blocks, state)
            silu, _ = _silu_and_grad(p_ref[3, pl.ds(t0, tq), :])
            for h in range(hp):
                lanes = slice(h * HEAD_DIM, (h + 1) * HEAD_DIM)
                y_ref[pl.ds(t0, tq), lanes] = state[h][1]
                mix_ref[pl.ds(t0, tq), lanes] = (state[h][1] * silu[:, lanes]).astype(BF16)
                tot_ref[h, pl.ds(t0, tq), :] = state[h][0]
            return 0

        lax.fori_loop(0, s // tq, q_block, 0)

    return pl.pallas_call(
        body, name=f"sb_forward_l{layer}", grid=(e // width,),
        in_specs=[pl.BlockSpec((4, s, width), lambda h: (0, 0, h))],
        out_specs=(pl.BlockSpec((s, width), lambda h: (0, h)),
                   pl.BlockSpec((s, width), lambda h: (0, h)),
                   pl.BlockSpec((hp, s, 1), lambda h: (h, 0, 0))),
        out_shape=(jax.ShapeDtypeStruct((s, 2 * e), F32), jax.ShapeDtypeStruct((s, 2 * e), BF16),
                   jax.ShapeDtypeStruct((e // HEAD_DIM, s, 1), F32)),
        scratch_shapes=[pltpu.VMEM((s, width), BF16), pltpu.VMEM((s, width), BF16)],
        compiler_params=_params("parallel"),
    )(proj)


def _qk_norm(t, gain):
    r = lax.rsqrt(jnp.mean(t * t, axis=-1, keepdims=True) + NORM_EPS)
    return t * r, r, (t * r) * gain


def _chunk_scores(qc, kw, bias, t0, scale):
    sc = _dot(qc, kw, NT) * scale + bias
    col = lax.broadcasted_iota(jnp.int32, (PAIR, PAIR_W), 1)
    sc = jnp.where(col + t0 >= PAD_K, sc, NEG_BIG)
    ex = jnp.exp(sc - jnp.max(sc, axis=-1, keepdims=True))
    return ex / jnp.sum(ex, axis=-1, keepdims=True)


def _chunk_forward(proj, bias, q_gain, k_gain, y, mixed, tie, layer):
    ties = [] if tie is None else [tie]
    _, s, e = proj.shape
    hp = _tile(e // HEAD_DIM, CHUNK_HEADS)
    width = hp * HEAD_DIM
    steps = e // width
    unroll = _tile(s // PAIR, PAIR_UNROLL)
    scale = HEAD_DIM ** -0.5
    heads = [slice(h * HEAD_DIM, (h + 1) * HEAD_DIM) for h in range(hp)]

    def body(p_ref, b_ref, qg_ref, kg_ref, *rest):
        y_ref, mix_ref, qn_ref, kp_ref, vp_ref = rest[-5:]
        kp_ref[pl.ds(0, PAD_K), :] = jnp.zeros((PAD_K, width), BF16)
        vp_ref[pl.ds(0, PAD_K), :] = jnp.zeros((PAD_K, width), BF16)
        vp_ref[pl.ds(PAD_K, s), :] = p_ref[2].astype(BF16)
        for lanes in heads:
            qn_ref[:, lanes] = _qk_norm(p_ref[0, :, lanes], qg_ref[...])[2].astype(BF16)
            kp_ref[pl.ds(PAD_K, s), lanes] = _qk_norm(p_ref[1, :, lanes], kg_ref[...])[2].astype(BF16)

        def chunks(ci, _):
            done = []
            for u in range(unroll):
                t0 = pl.multiple_of((ci * unroll + u) * PAIR, PAIR)
                silu, _ = _silu_and_grad(p_ref[3, pl.ds(t0, PAIR), :])
                for h, lanes in enumerate(heads):
                    probs = _chunk_scores(qn_ref[pl.ds(t0, PAIR), lanes], kp_ref[pl.ds(t0, PAIR_W), lanes],
                                          b_ref[h], t0, scale)
                    out = _dot(probs.astype(BF16), vp_ref[pl.ds(t0, PAIR_W), lanes])
                    done.append((t0, lanes, out, (out * silu[:, lanes]).astype(BF16)))
            for t0, lanes, out, gated in done:
                y_ref[pl.ds(t0, PAIR), lanes] = out
                mix_ref[pl.ds(t0, PAIR), lanes] = gated
            return 0

        lax.fori_loop(0, s // (PAIR * unroll), chunks, 0)

    return pl.pallas_call(
        body, name=f"chunk_forward_l{layer}", grid=(steps,),
        in_specs=[pl.BlockSpec((4, s, width), lambda h: (1, 0, h)),
                  pl.BlockSpec((hp, PAIR, PAIR_W), lambda h: (layer * steps + h, 0, 0)),
                  pl.BlockSpec((1, HEAD_DIM), lambda h: (0, 0)),
                  pl.BlockSpec((1, HEAD_DIM), lambda h: (0, 0)),
                  _any(), _any()] + [_any()] * len(ties),
        out_specs=(pl.BlockSpec((s, width), lambda h: (0, steps + h)),
                   pl.BlockSpec((s, width), lambda h: (0, steps + h))),
        out_shape=(jax.ShapeDtypeStruct(y.shape, F32), jax.ShapeDtypeStruct(mixed.shape, BF16)),
        input_output_aliases={4: 0, 5: 1},
        scratch_shapes=[pltpu.VMEM((s, width), BF16), pltpu.VMEM((s + PAD_K, width), BF16),
                        pltpu.VMEM((s + PAD_K, width), BF16)],
        compiler_params=_params("parallel"),
    )(proj, bias, q_gain, k_gain, y, mixed, *ties)


def _out_proj(mixed, w, x, layer):
    s, d = x.shape
    tm, tn = _tile(s, 1024), _tile(d, 1024)

    def body(a_ref, w_ref, x_ref, o_ref):
        o_ref[...] = x_ref[...] + _dot(a_ref[...], w_ref[...])

    return pl.pallas_call(
        body, name=f"out_proj_l{layer}", grid=(s // tm, d // tn),
        in_specs=[pl.BlockSpec((tm, d), lambda m, n: (m, 0)),
                  pl.BlockSpec((d, tn), lambda m, n: (0, n)),
                  pl.BlockSpec((tm, tn), lambda m, n: (m, n))],
        out_specs=pl.BlockSpec((tm, tn), lambda m, n: (m, n)),
        out_shape=jax.ShapeDtypeStruct((s, d), F32),
        compiler_params=_params("parallel", "parallel"),
    )(mixed, w, x)


def _loss_head(y, target):
    s, d = y.shape
    tm = _tile(s, 256)

    def body(y_ref, t_ref, dy_ref, part_ref):
        diff = y_ref[...] - t_ref[...]
        dy_ref[...] = diff * (1.0 / d)
        sq = (diff * diff).reshape(tm // 8, 8, d).sum(axis=0)
        acc = sq[:, 0:LANES]
        for j in range(1, d // LANES):
            acc = acc + sq[:, j * LANES:(j + 1) * LANES]
        part_ref[...] = acc * (0.5 / d)

    return pl.pallas_call(
        body, name="loss_head", grid=(s // tm,),
        in_specs=[pl.BlockSpec((tm, d), lambda i: (i, 0)), pl.BlockSpec((tm, d), lambda i: (i, 0))],
        out_specs=(pl.BlockSpec((tm, d), lambda i: (i, 0)), pl.BlockSpec((None, 8, LANES), lambda i: (i, 0, 0))),
        out_shape=(jax.ShapeDtypeStruct((s, d), F32), jax.ShapeDtypeStruct((s // tm, 8, LANES), F32)),
        compiler_params=_params("parallel"),
    )(y, target)


def _out_proj_bwd_input(dx, w, tie, layer):
    s, d = dx.shape
    tm, tn = _tile(s, 1024), _tile(d, 1024)
    ties = [] if tie is None else [tie]

    def body(dx_ref, w_ref, *rest):
        rest[-1][...] = _dot(dx_ref[...].astype(BF16), w_ref[...], NT)

    return pl.pallas_call(
        body, name=f"out_proj_dx_l{layer}", grid=(s // tm, d // tn),
        in_specs=[pl.BlockSpec((tm, d), lambda m, n: (m, 0)),
                  pl.BlockSpec((tn, d), lambda m, n: (n, 0))] + [_any()] * len(ties),
        out_specs=pl.BlockSpec((tm, tn), lambda m, n: (m, n)),
        out_shape=jax.ShapeDtypeStruct((s, d), F32),
        compiler_params=_params("parallel", "parallel"),
    )(dx, w, *ties)


def _out_proj_bwd_weight(mixed, dx, layer):
    s, d = dx.shape
    te, tn = _tile(d, 1024), _tile(d, 1024)

    def body(a_ref, dx_ref, o_ref):
        o_ref[...] = _dot(a_ref[...], dx_ref[...].astype(BF16), TN).astype(BF16)

    return pl.pallas_call(
        body, name=f"out_proj_dw_l{layer}", grid=(d // te, d // tn),
        in_specs=[pl.BlockSpec((s, te), lambda i, n: (0, i)), pl.BlockSpec((s, tn), lambda i, n: (0, n))],
        out_specs=pl.BlockSpec((te, tn), lambda i, n: (i, n)),
        out_shape=jax.ShapeDtypeStruct((d, d), BF16),
        compiler_params=_params("parallel", "parallel"),
    )(mixed, dx)


def _sb_backward(proj, y, dmixed, tot, layer):
    _, s, e = proj.shape
    hp = _tile(e // HEAD_DIM, SB_HEADS)
    width = hp * HEAD_DIM
    tq = _tile(s, SB_Q)
    diag_tiles = tq // SB_K
    scale = HEAD_DIM ** -0.5

    def body(p_ref, y_ref, dm_ref, tot_ref, o_ref, kb_ref, vb_ref, do_ref, dk_ref, dv_ref):
        kb_ref[...] = p_ref[1].astype(BF16)
        vb_ref[...] = p_ref[2].astype(BF16)
        silu, dsilu = _silu_and_grad(p_ref[3])
        dm = dm_ref[...]
        do_ref[...] = (dm * silu).astype(BF16)
        o_ref[3] = (dm * y_ref[...] * dsilu).astype(BF16)
        dk_ref[...] = jnp.zeros_like(dk_ref)
        dv_ref[...] = jnp.zeros_like(dv_ref)
        row = lax.broadcasted_iota(jnp.int32, (tq, SB_K), 0)
        col = lax.broadcasted_iota(jnp.int32, (tq, SB_K), 1)
        kj = lax.broadcasted_iota(jnp.int32, (SB_K, SB_K), 0)
        ks = lax.broadcasted_iota(jnp.int32, (SB_K, SB_K), 1)
        upto = (kj <= ks).astype(BF16)
        before = (kj < ks).astype(BF16)
        upto = jnp.concatenate([upto, upto], axis=0)
        before = jnp.concatenate([before, before], axis=0)

        def q_block(qi, _):
            t0 = pl.multiple_of(qi * tq, tq)
            heads = [slice(h * HEAD_DIM, (h + 1) * HEAD_DIM) for h in range(hp)]
            qb = [p_ref[0, pl.ds(t0, tq), lanes].astype(BF16) for lanes in heads]
            dob = [do_ref[pl.ds(t0, tq), lanes] for lanes in heads]
            total = [tot_ref[h, pl.ds(t0, tq), :] for h in range(hp)]

            def tile(s0, state, causal):
                out, adds = [], []
                for h, lanes in enumerate(heads):
                    stay_sum, dlw_sum, dq = state[h]
                    kt = kb_ref[pl.ds(s0, SB_K), lanes]
                    vt = vb_ref[pl.ds(s0, SB_K), lanes]
                    z = _dot(qb[h], kt, NT) * scale
                    ls = _log_sigmoid(z)
                    stay = ls - z
                    if causal is not None:
                        stay = jnp.where(causal, stay, 0.0)
                    after = total[h] - (stay_sum + _split_dot_deep(stay, upto))
                    w = jnp.exp(ls + after)
                    if causal is not None:
                        w = jnp.where(causal, w, 0.0)
                    dlw = _dot(dob[h], vt, NT) * w
                    prior = dlw_sum + _split_dot_deep(dlw, before)
                    sig = jnp.exp(ls)
                    dz = (dlw * (1.0 - sig) - sig * prior) * scale
                    if causal is not None:
                        dz = jnp.where(causal, dz, 0.0)
                    dzb = dz.astype(BF16)
                    dq = dq + _dot(dzb, kt)
                    adds.append((lanes, _dot(dzb, qb[h], TN), _dot(w.astype(BF16), dob[h], TN)))
                    out.append((stay_sum + jnp.sum(stay, axis=1, keepdims=True),
                                dlw_sum + jnp.sum(dlw, axis=1, keepdims=True), dq))
                for lanes, dk, dv in adds:
                    dk_ref[pl.ds(s0, SB_K), lanes] += dk
                    dv_ref[pl.ds(s0, SB_K), lanes] += dv
                return tuple(out)

            def k_blocks(j, st):
                for u in range(SB_UNROLL):
                    st = tile(pl.multiple_of((SB_UNROLL * j + u) * SB_K, SB_K), st, None)
                return st

            zero = jnp.zeros((tq, 1), F32)
            state = tuple((zero, zero, jnp.zeros((tq, HEAD_DIM), F32)) for _ in range(hp))
            state = lax.fori_loop(0, diag_tiles * qi // SB_UNROLL, k_blocks, state)
            for dt in range(diag_tiles):
                state = tile(t0 + dt * SB_K, state, col + dt * SB_K < row)
            for h, lanes in enumerate(heads):
                o_ref[0, pl.ds(t0, tq), lanes] = state[h][2].astype(BF16)
            return 0

        lax.fori_loop(0, s // tq, q_block, 0)
        o_ref[1] = dk_ref[...].astype(BF16)
        o_ref[2] = dv_ref[...].astype(BF16)

    return pl.pallas_call(
        body, name=f"sb_backward_l{layer}", grid=(e // width,),
        in_specs=[pl.BlockSpec((4, s, width), lambda h: (0, 0, h)),
                  pl.BlockSpec((s, width), lambda h: (0, h)),
                  pl.BlockSpec((s, width), lambda h: (0, h)),
                  pl.BlockSpec((hp, s, 1), lambda h: (h, 0, 0))],
        out_specs=pl.BlockSpec((4, s, width), lambda h: (0, 0, h)),
        out_shape=jax.ShapeDtypeStruct((N_DEV, s, e), BF16),
        scratch_shapes=[pltpu.VMEM((s, width), BF16), pltpu.VMEM((s, width), BF16),
                        pltpu.VMEM((s, width), BF16), pltpu.VMEM((s, width), F32),
                        pltpu.VMEM((s, width), F32)],
        compiler_params=_params("parallel"),
    )(proj, y, dmixed, tot)


def _norm_bwd(dn, xh, r, gain):
    dxh = dn * gain
    return r * (dxh - xh * jnp.mean(dxh * xh, axis=-1, keepdims=True)), dn * xh


def _chunk_backward(proj, bias, q_gain, k_gain, y, dmixed, dproj, layer):
    _, s, e = proj.shape
    hp = CHUNK_HEADS_BWD
    width = hp * HEAD_DIM
    steps = e // width
    unroll = _tile(s // PAIR, PAIR_UNROLL_BWD)
    scale = HEAD_DIM ** -0.5
    heads = [slice(h * HEAD_DIM, (h + 1) * HEAD_DIM) for h in range(hp)]

    def body(p_ref, b_ref, qg_ref, kg_ref, y_ref, dm_ref, dp_in, o_ref, db_ref, dqg_ref, dkg_ref,
             qn_ref, kp_ref, vp_ref, do_ref, dqn_ref, dkn_ref, dvp_ref):
        del dp_in
        kp_ref[pl.ds(0, PAD_K), :] = jnp.zeros((PAD_K, width), BF16)
        vp_ref[pl.ds(0, PAD_K), :] = jnp.zeros((PAD_K, width), BF16)
        vp_ref[pl.ds(PAD_K, s), :] = p_ref[2].astype(BF16)
        for lanes in heads:
            qn_ref[:, lanes] = _qk_norm(p_ref[0, :, lanes], qg_ref[...])[2].astype(BF16)
            kp_ref[pl.ds(PAD_K, s), lanes] = _qk_norm(p_ref[1, :, lanes], kg_ref[...])[2].astype(BF16)
        silu, dsilu = _silu_and_grad(p_ref[3])
        dm = dm_ref[...]
        do_ref[...] = (dm * silu).astype(BF16)
        o_ref[3] = (dm * y_ref[...] * dsilu).astype(BF16)
        dkn_ref[...] = jnp.zeros_like(dkn_ref)
        dvp_ref[...] = jnp.zeros_like(dvp_ref)
        db_ref[...] = jnp.zeros_like(db_ref)

        def chunks(ci, _):
            done = []
            for u in range(unroll):
                t0 = pl.multiple_of((ci * unroll + u) * PAIR, PAIR)
                for h, lanes in enumerate(heads):
                    qc = qn_ref[pl.ds(t0, PAIR), lanes]
                    kw = kp_ref[pl.ds(t0, PAIR_W), lanes]
                    vw = vp_ref[pl.ds(t0, PAIR_W), lanes]
                    dob = do_ref[pl.ds(t0, PAIR), lanes]
                    probs = _chunk_scores(qc, kw, b_ref[h], t0, scale)
                    dprobs = _dot(dob, vw, NT)
                    dsc = probs * (dprobs - jnp.sum(probs * dprobs, axis=-1, keepdims=True))
                    dsb = (dsc * scale).astype(BF16)
                    done.append((t0, h, lanes, dsc, _dot(dsb, kw), _dot(dsb, qc, TN),
                                 _dot(probs.astype(BF16), dob, TN)))
            for t0, h, lanes, dsc, dqn, dkn, dvp in done:
                db_ref[h] += dsc
                dqn_ref[pl.ds(t0, PAIR), lanes] = dqn
                dkn_ref[pl.ds(t0, PAIR_W), lanes] += dkn
                dvp_ref[pl.ds(t0, PAIR_W), lanes] += dvp
            return 0

        lax.fori_loop(0, s // (PAIR * unroll), chunks, 0)
        o_ref[2] = dvp_ref[pl.ds(PAD_K, s), :].astype(BF16)

        @pl.when(pl.program_id(0) == 0)
        def _():
            dqg_ref[...] = jnp.zeros_like(dqg_ref)
            dkg_ref[...] = jnp.zeros_like(dkg_ref)

        for lanes in heads:
            qh, rq, _ = _qk_norm(p_ref[0, :, lanes], qg_ref[...])
            dq, dqg_rows = _norm_bwd(dqn_ref[:, lanes], qh, rq, qg_ref[...])
            o_ref[0, :, lanes] = dq.astype(BF16)
            dqg_ref[...] += jnp.sum(dqg_rows, axis=0, keepdims=True)
            kh, rk, _ = _qk_norm(p_ref[1, :, lanes], kg_ref[...])
            dk, dkg_rows = _norm_bwd(dkn_ref[pl.ds(PAD_K, s), lanes], kh, rk, kg_ref[...])
            o_ref[1, :, lanes] = dk.astype(BF16)
            dkg_ref[...] += jnp.sum(dkg_rows, axis=0, keepdims=True)

    return pl.pallas_call(
        body, name=f"chunk_backward_l{layer}", grid=(steps,),
        in_specs=[pl.BlockSpec((4, s, width), lambda h: (1, 0, h)),
                  pl.BlockSpec((hp, PAIR, PAIR_W), lambda h: (layer * steps + h, 0, 0)),
                  pl.BlockSpec((1, HEAD_DIM), lambda h: (0, 0)),
                  pl.BlockSpec((1, HEAD_DIM), lambda h: (0, 0)),
                  pl.BlockSpec((s, width), lambda h: (0, steps + h)),
                  pl.BlockSpec((s, width), lambda h: (0, steps + h)),
                  _any()],
        out_specs=(pl.BlockSpec((4, s, width), lambda h: (1, 0, h)),
                   pl.BlockSpec((hp, PAIR, PAIR_W), lambda h: (h, 0, 0)),
                   pl.BlockSpec((1, HEAD_DIM), lambda h: (0, 0)),
                   pl.BlockSpec((1, HEAD_DIM), lambda h: (0, 0))),
        out_shape=(jax.ShapeDtypeStruct(dproj.shape, BF16),
                   jax.ShapeDtypeStruct((e // HEAD_DIM, PAIR, PAIR_W), F32),
                   jax.ShapeDtypeStruct((1, HEAD_DIM), F32), jax.ShapeDtypeStruct((1, HEAD_DIM), F32)),
        input_output_aliases={6: 0},
        scratch_shapes=[pltpu.VMEM((s, width), BF16), pltpu.VMEM((s + PAD_K, width), BF16),
                        pltpu.VMEM((s + PAD_K, width), BF16), pltpu.VMEM((s, width), BF16),
                        pltpu.VMEM((s, width), F32), pltpu.VMEM((s + PAD_K, width), F32),
                        pltpu.VMEM((s + PAD_K, width), F32)],
        compiler_params=_params("arbitrary"),
    )(proj, bias, q_gain, k_gain, y, dmixed, dproj)


def _proj_bwd_input(dproj, w_all, x, g, dx, tie, layer):
    s, d = x.shape
    e = w_all.shape[2]
    tm = _tile(s, 512)
    gs = 1
    steps = N_DEV // gs

    def body(dp_ref, w_ref, x_ref, g_ref, dx_ref, tie_ref, o_ref, dg_ref, acc_ref):
        del tie_ref
        j = pl.program_id(1)

        @pl.when(j == 0)
        def _():
            acc_ref[...] = jnp.zeros_like(acc_ref)

        part = _dot(dp_ref[0], w_ref[0], NT)
        for k in range(1, gs):
            part = part + _dot(dp_ref[k], w_ref[k], NT)
        acc_ref[...] += part

        @pl.when(jnp.logical_and(j == steps - 1, pl.program_id(0) == 0))
        def _():
            dg_ref[...] = jnp.zeros_like(dg_ref)

        @pl.when(j == steps - 1)
        def _():
            xv = x_ref[...]
            r = lax.rsqrt(jnp.mean(xv * xv, axis=-1, keepdims=True) + NORM_EPS)
            dxn, dg_rows = _norm_bwd(acc_ref[...], xv * r, r, g_ref[...])
            o_ref[...] = dx_ref[...] + dxn
            dg_ref[...] += jnp.sum(dg_rows, axis=0, keepdims=True)

    return pl.pallas_call(
        body, name=f"proj_dx_l{layer}", grid=(s // tm, steps),
        in_specs=[pl.BlockSpec((gs, tm, e), lambda m, j: (j, m, 0)),
                  pl.BlockSpec((gs, d, e), lambda m, j: (j, 0, 0)),
                  pl.BlockSpec((tm, d), lambda m, j: (m, 0)),
                  pl.BlockSpec((1, d), lambda m, j: (0, 0)),
                  pl.BlockSpec((tm, d), lambda m, j: (m, 0)), _any()],
        out_specs=(pl.BlockSpec((tm, d), lambda m, j: (m, 0)), pl.BlockSpec((1, d), lambda m, j: (0, 0))),
        out_shape=(jax.ShapeDtypeStruct((s, d), F32), jax.ShapeDtypeStruct((1, d), F32)),
        scratch_shapes=[pltpu.VMEM((tm, d), F32)],
        compiler_params=_params("arbitrary", "arbitrary"),
    )(dproj, w_all, x, g, dx, tie)


def _proj_bwd_weight(h, dproj, layer):
    s, d = h.shape
    e = dproj.shape[2]
    td, tn = _tile(d, 1024), _tile(e, 1024)
    nb = e // tn

    def body(h_ref, dp_ref, o_ref):
        o_ref[...] = _dot(h_ref[...], dp_ref[...], TN).astype(BF16)

    return pl.pallas_call(
        body, name=f"proj_dw_l{layer}", grid=(d // td, N_DEV * nb),
        in_specs=[pl.BlockSpec((s, td), lambda i, n: (0, i)),
                  pl.BlockSpec((None, s, tn), lambda i, n: (n // nb, 0, n % nb))],
        out_specs=pl.BlockSpec((None, td, tn), lambda i, n: (n // nb, i, n % nb)),
        out_shape=jax.ShapeDtypeStruct((N_DEV, d, e), BF16),
        compiler_params=_params("parallel", "parallel"),
    )(h, dproj)


def _adamw_math(w, g, m, v):
    m = ADAM_B1 * m + (1.0 - ADAM_B1) * g
    v = ADAM_B2 * v + (1.0 - ADAM_B2) * (g * g)
    m_hat = m / (1.0 - ADAM_B1 ** ADAM_STEP)
    v_hat = v / (1.0 - ADAM_B2 ** ADAM_STEP)
    return -ADAM_LR * (m_hat / (jnp.sqrt(v_hat) + ADAM_EPS) + ADAM_WD * w), m, v


def _adamw_layer(parts, own, me, w, m, v, prev, layer, name):
    n_layers, rows, cols = w.shape
    n_parts = parts.shape[0]
    tr = _tile(rows, max(8, (256 * 1024) // cols))

    def body(me_ref, p_ref, own_ref, w_ref, m_ref, v_ref, *rest):
        g_ref, d_ref, nm_ref, nv_ref = rest[-4:]
        mine = own_ref[...].astype(F32)
        if n_parts == N_DEV:
            g = None
            for j in range(N_DEV):
                term = jnp.where(me_ref[0] == j, mine, p_ref[j].astype(F32))
                g = term if g is None else g + term
        else:
            g = mine
            for j in range(n_parts):
                g = g + p_ref[j].astype(F32)
        g_ref[...] = g
        d_ref[...], nm_ref[...], nv_ref[...] = _adamw_math(w_ref[...], g, m_ref[...], v_ref[...])

    blk = pl.BlockSpec((None, tr, cols), lambda i, me_ref: (layer, i, 0))
    out_shape = tuple(jax.ShapeDtypeStruct(w.shape, F32) for _ in range(4))
    in_specs = [pl.BlockSpec((n_parts, tr, cols), lambda i, me_ref: (0, i, 0)),
                pl.BlockSpec((None, tr, cols), lambda i, me_ref: (me_ref[0], i, 0)), blk, blk, blk]
    args = [me, parts, own, w, m, v]
    aliases = {}
    if prev is not None:
        in_specs += [_any()] * 4
        args += list(prev)
        aliases = {6 + k: k for k in range(4)}
    return pl.pallas_call(
        body, name=f"{name}_l{layer}",
        grid_spec=pltpu.PrefetchScalarGridSpec(
            num_scalar_prefetch=1, grid=(rows // tr,), in_specs=in_specs, out_specs=(blk, blk, blk, blk)),
        out_shape=out_shape, input_output_aliases=aliases,
        compiler_params=_params("parallel"),
    )(*args)


def _sum_slots(parts):
    def body(p_ref, o_ref):
        g = p_ref[0]
        for j in range(1, N_DEV):
            g = g + p_ref[j]
        o_ref[...] = g

    return pl.pallas_call(
        body, name="sum_small_grads",
        in_specs=[_vmem()], out_specs=_vmem(),
        out_shape=jax.ShapeDtypeStruct(parts.shape[1:], F32),
        compiler_params=_params(),
    )(parts)


def _adamw_small(w, g, m, v):
    def body(w_ref, g_ref, m_ref, v_ref, d_ref, nm_ref, nv_ref):
        d_ref[...], nm_ref[...], nv_ref[...] = _adamw_math(w_ref[...], g_ref[...], m_ref[...], v_ref[...])

    return pl.pallas_call(
        body, name="adamw_small",
        in_specs=[_vmem()] * 4, out_specs=(_vmem(),) * 3,
        out_shape=tuple(jax.ShapeDtypeStruct(w.shape, F32) for _ in range(3)),
        compiler_params=_params(),
    )(w, g, m, v)


def _pack_rows(arrays):
    rows = []
    for a in arrays:
        flat = a.reshape(-1)
        pad = (-flat.shape[0]) % (8 * LANES)
        rows.append(jnp.pad(flat, (0, pad)).reshape(-1, LANES))
    return jnp.concatenate(rows, axis=0)


def _unpack_rows(packed, like):
    out, r0 = [], 0
    for a in like:
        n = a.size
        nr = -(-n // (8 * LANES)) * 8
        out.append(packed[r0:r0 + nr].reshape(-1)[:n].reshape(a.shape))
        r0 += nr
    return out


def kernel(x, norm_g, w_in, q_norm_g, k_norm_g, rel_bias, w_out, loss_target, m_norm_g, m_w_in, m_q_norm_g, m_k_norm_g, m_rel_bias, m_w_out, v_norm_g, v_w_in, v_q_norm_g, v_k_norm_g, v_rel_bias, v_w_out):
    depth, d, e = w_in.shape
    r_out = w_out.shape[1]
    heads = e // HEAD_DIM
    rel_w = rel_bias.shape[2]
    x0 = x[0]
    target = loss_target[0]
    s = x0.shape[0]

    me = jnp.reshape(_flat(_my_place()), (1,)).astype(jnp.int32)

    casts = [(_cast_layer(w_in, me, l, "cast_w_in"), _cast_layer(w_out, me, l, "cast_w_out"))
             for l in range(depth)]

    def begin_gather(l, after):
        (win_b, win_land), (wout_b, wout_land) = casts[l]
        return _gather_send((win_b, wout_b), (win_land, wout_land), after, f"gather_send_l{l}")

    rel_all = _gather_small(rel_bias, [], "gather_rel_bias")
    sent = begin_gather(0, [rel_all])
    rel_full = jnp.transpose(rel_all, (1, 2, 0, 3)).reshape(depth * heads, N_DEV * rel_w)
    bias = jnp.transpose(_bias_expand(rel_full), (1, 0, 2))
    head_work = [bias] + [shard for cast in casts[1:] for shard, _ in cast]
    forwarded = _gather_forward(sent, head_work, "gather_forward_l0")

    xs, hs, projs, ys, mixes, tots, weights = [], [], [], [], [], [], []
    xl = x0
    for l in range(depth):
        win_all, wout_all = _gather_finish(forwarded, [xl, forwarded[-1]], f"gather_finish_l{l}")
        more = l + 1 < depth
        if more:
            sent = begin_gather(l + 1, [win_all])
        wout_full = wout_all.reshape(d, d)
        proj, h = _norm_proj(xl, norm_g[l:l + 1], win_all, sent[-1] if more else None, l)
        y, mixed, tot = _sb_forward(proj, l)
        if more:
            forwarded = _gather_forward(sent, [tot], f"gather_forward_l{l + 1}")
        y, mixed = _chunk_forward(proj, bias, q_norm_g[l:l + 1], k_norm_g[l:l + 1], y, mixed,
                                  forwarded[-1] if more else None, l)
        xs.append(xl), hs.append(h), projs.append(proj), ys.append(y), mixes.append(mixed), tots.append(tot)
        weights.append((win_all, wout_full))
        xl = _out_proj(mixed, wout_full, xl, l)

    dx, loss_parts = _loss_head(xl, target)
    loss = lax.psum(jnp.sum(loss_parts), AXES)

    dbias, dng, dqg, dkg = [None] * depth, [None] * depth, [None] * depth, [None] * depth
    res_in, res_out = None, None

    peer_slots = jnp.stack([_flat(_flip(_my_place(), k)) for k in SAME_CORE]).astype(jnp.int32)

    def finish_exchange(exchanging, after, l):
        sems, bufs, own = exchanging
        copies = N_DEV - 1 if own is None else 3
        bufs = _wait_copies(f"exchange_finish_l{l}", bufs, sems, 2, lambda refs, t: refs[t].at[0], copies, after)
        own = bufs[:2] if own is None else own
        rin, rout = bufs[2:]
        return (_adamw_layer(rin, own[0], me, w_in, m_w_in, v_w_in, res_in, l, "adamw_w_in"),
                _adamw_layer(rout, own[1], me, w_out, m_w_out, v_w_out, res_out, l, "adamw_w_out"))

    pending = []
    for l in reversed(range(depth)):
        win_all, wout_full = weights[l]
        dmixed = _out_proj_bwd_input(dx, wout_full, None, l)
        gwout = _out_proj_bwd_weight(mixes[l], dx, l).reshape(N_DEV, r_out, d)
        dproj = _sb_backward(projs[l], ys[l], dmixed, tots[l], l)
        dproj, dbias[l], dqg[l], dkg[l] = _chunk_backward(
            projs[l], bias, q_norm_g[l:l + 1], k_norm_g[l:l + 1], ys[l], dmixed, dproj, l)
        grads_l = (_proj_bwd_weight(hs[l], dproj, l), gwout)
        if l > 0:
            sems, bufs, token = _exchange_direct(grads_l, [], f"exchange_direct_l{l}")
            pending.append(((sems, bufs, None), l))
        else:
            sems, bufs, token = _exchange_to_sibling(grads_l, [], "exchange_sibling_l0")
            bufs = _wait_copies("exchange_sibling_wait_l0", bufs, sems, 2, lambda refs, t: refs[t].at[0], 4, [token])
            own, parts, lands = bufs[0:2], bufs[2:4], bufs[4:6]
            csums = [_chip_sums(own[t], parts[t], peer_slots, f"chip_sums_{t}_l0") for t in range(2)]
            sems, csums_lands, token = _exchange_to_chips(csums, lands, [], "exchange_chips_l0")
            pending.append(((sems, csums_lands, own), l))
        dx, dng[l] = _proj_bwd_input(dproj, win_all, xs[l], norm_g[l:l + 1], dx, token, l)
    tie = token
    for exchanging, l in pending[:-1]:
        res_in, res_out = finish_exchange(exchanging, [dx, tie], l)
    drel = _bias_grad(jnp.concatenate(dbias, axis=0), [tie])
    small_like = [norm_g, q_norm_g, k_norm_g, drel]
    mine = _pack_rows([jnp.concatenate(dng, axis=0), jnp.concatenate(dqg, axis=0),
                       jnp.concatenate(dkg, axis=0), drel])
    gathered = _gather_small(mine, [res_in[0], res_out[0]], "gather_small_grads")
    g_norm, g_qn, g_kn, g_rel_full = _unpack_rows(_sum_slots(gathered), small_like)
    my_block = _flat(_my_place())
    g_rel = lax.dynamic_slice_in_dim(g_rel_full.reshape(depth, heads, N_REL), my_block * rel_w, rel_w, axis=2)
    small_w = [norm_g, q_norm_g, k_norm_g, rel_bias]
    small = _adamw_small(_pack_rows(small_w), _pack_rows([g_norm, g_qn, g_kn, g_rel]),
                         _pack_rows([m_norm_g, m_q_norm_g, m_k_norm_g, m_rel_bias]),
                         _pack_rows([v_norm_g, v_q_norm_g, v_k_norm_g, v_rel_bias]))
    d_small, nm_small, nv_small = (_unpack_rows(p, small_w) for p in small)

    res_in, res_out = finish_exchange(pending[-1][0], [small[0], res_in[0], res_out[0]], 0)
    g_win, d_win, nm_win, nv_win = res_in
    g_wout, d_wout, nm_wout, nv_wout = res_out
    grads = (g_norm, g_win, g_qn, g_kn, g_rel, g_wout)

    def order(sm, big_in, big_out):
        return (sm[0], big_in, sm[1], sm[2], sm[3], big_out)

    return (loss, dx[None], *grads, *order(d_small, d_win, d_wout),
            *order(nm_small, nm_win, nm_wout), *order(nv_small, nv_win, nv_wout))
```

```python
import functools

import jax
import jax.numpy as jnp
from jax import lax
from jax.experimental import pallas as pl
from jax.experimental.pallas import tpu as pltpu

F32 = jnp.float32
BF16 = jnp.bfloat16
MESH_ID = pl.DeviceIdType.MESH
AXES = ("x", "y", "c")

N_DEV = 8
HEAD_DIM = 128
CHUNK = 64
LEFT_CHUNKS = 8
BAND_W = (LEFT_CHUNKS + 1) * CHUNK
PAD_K = LEFT_CHUNKS * CHUNK
REL_CLIP = 256
N_REL = REL_CLIP + CHUNK
NORM_EPS = 1e-6
NEG_BIG = -1e30
GROUP_CHUNKS = 4
PAIR = GROUP_CHUNKS * CHUNK
PAIR_W = BAND_W + PAIR - CHUNK
CHUNK_HEADS = 2
CHUNK_HEADS_BWD = 1
PAIR_UNROLL_BWD = 4
PAIR_UNROLL = 2
SB_Q = 512
SB_K = 128
SB_HEADS = 2
SB_UNROLL = 4
SB_UNROLL_BWD = 4
LANES = 128

ADAM_LR = 0.001
ADAM_B1 = 0.9
ADAM_B2 = 0.999
ADAM_EPS = 1e-08
ADAM_WD = 0.01
ADAM_STEP = 10

VMEM_LIMIT_BYTES = 56 * 1024 * 1024

NT = (((1,), (1,)), ((), ()))
TN = (((0,), (0,)), ((), ()))


def _params(*sem, **kw):
    return pltpu.CompilerParams(dimension_semantics=sem or None, vmem_limit_bytes=VMEM_LIMIT_BYTES, **kw)


def _any():
    return pl.BlockSpec(memory_space=pl.ANY)


def _vmem():
    return pl.BlockSpec(memory_space=pltpu.VMEM)


def _tile(n, want):
    return want if n % want == 0 else n


def _dot(a, b, dims=None):
    if dims is None:
        return jnp.dot(a, b, preferred_element_type=F32)
    return lax.dot_general(a, b, dims, preferred_element_type=F32)


def _split_dot(a, b, parts, dims=None):
    acc = None
    rest = a
    for _ in range(parts):
        piece = rest.astype(BF16)
        rest = rest - piece.astype(F32)
        term = _dot(piece, b, dims)
        acc = term if acc is None else acc + term
    return acc


def _split_dot_deep(a, b_twice):
    high = a.astype(BF16)
    low = (a - high.astype(F32)).astype(BF16)
    return _dot(jnp.concatenate([high, low], axis=1), b_twice)


def _log_sigmoid(z):
    return jnp.minimum(z, 0.0) - jnp.log(1.0 + jnp.exp(-jnp.abs(z)))


def _silu_and_grad(g):
    sig = jax.nn.sigmoid(g)
    return g * sig, sig * (1.0 + g * (1.0 - sig))


def _my_place():
    return lax.axis_index("x"), lax.axis_index("y"), lax.axis_index("c")


def _flat(place):
    return 4 * place[0] + 2 * place[1] + place[2]


def _flip(place, k):
    return tuple(1 - p if (k >> s) & 1 else p for p, s in zip(place, (2, 1, 0)))


def _cast_layer(w, me, layer, name):
    _, rows, cols = w.shape
    tr = _tile(rows, 1024)

    def body(me_ref, a_ref, shard_ref, land_ref):
        del me_ref
        shard_ref[...] = a_ref[...].astype(BF16)
        land_ref[...] = shard_ref[...]

    return pl.pallas_call(
        body, name=f"{name}_l{layer}",
        grid_spec=pltpu.PrefetchScalarGridSpec(
            num_scalar_prefetch=1, grid=(rows // tr,),
            in_specs=[pl.BlockSpec((None, tr, cols), lambda i, me_ref: (layer, i, 0))],
            out_specs=(pl.BlockSpec((tr, cols), lambda i, me_ref: (i, 0)),
                       pl.BlockSpec((None, tr, cols), lambda i, me_ref: (me_ref[0], i, 0)))),
        out_shape=(jax.ShapeDtypeStruct((rows, cols), BF16), jax.ShapeDtypeStruct((N_DEV, rows, cols), BF16)),
        compiler_params=_params("parallel"),
    )(me, w)


HBM_SPEC = pl.BlockSpec(memory_space=pltpu.HBM)
SEM_SPEC = pl.BlockSpec(memory_space=pltpu.SEMAPHORE)
SAME_CORE = (2, 4, 6)
SIBLING = 1
SPLIT_EFFECT = pltpu.SideEffectType.DATAFLOW_SIDE_EFFECTING


def _split_call(body, name, bufs, sems_in, sem_counts_out, after, token):
    bufs, sems_in, after = list(bufs), list(sems_in), list(after)
    nb, ni, no = len(bufs), len(sems_in), len(sem_counts_out)

    def wrapped(*refs):
        outs = nb + ni + len(after)
        body(refs[:nb], refs[nb:nb + ni], refs[outs:outs + no])
        if token:
            refs[-1][...] = jnp.zeros_like(refs[-1])

    out = pl.pallas_call(
        wrapped, name=name,
        in_specs=[HBM_SPEC] * nb + [SEM_SPEC] * ni + [_any()] * len(after),
        out_specs=tuple([SEM_SPEC] * no + [HBM_SPEC] * nb + ([_vmem()] if token else [])),
        out_shape=tuple([pltpu.SemaphoreType.DMA((c,)) for c in sem_counts_out]
                        + [pltpu.HBM(a.shape, a.dtype) for a in bufs]
                        + ([jax.ShapeDtypeStruct((8, LANES), F32)] if token else [])),
        input_output_aliases={i: no + i for i in range(nb)},
        compiler_params=pltpu.CompilerParams(has_side_effects=SPLIT_EFFECT),
    )(*[pltpu.with_memory_space_constraint(a, pltpu.HBM) for a in bufs], *sems_in, *after)
    return list(out[:no]), list(out[no:no + nb]), (out[-1] if token else None)


def _remote(src, dst, send_sems, recv_sems, i, to):
    return pltpu.make_async_remote_copy(src_ref=src, dst_ref=dst, send_sem=send_sems.at[i], recv_sem=recv_sems.at[i],
                                        device_id=to, device_id_type=MESH_ID)


def _wait_copies(name, bufs, sems, n, slot_of, count, after):
    def body(refs, sems_in, _):
        me = _my_place()
        for t in range(n):
            slot = slot_of(refs, t)
            for a in range(count):
                cp = _remote(slot, slot, sems_in[0], sems_in[1], t * count + a, me)
                cp.wait_send()
                cp.wait_recv()

    return _split_call(body, name, bufs, sems, (), after, False)[1]


def _exchange_direct(grads, after, name):
    n = len(grads)
    lands = [lax.empty(g.shape, g.dtype) for g in grads]

    def body(refs, _, sems_out):
        me = _my_place()
        for k in range(1, N_DEV):
            peer = _flip(me, k)
            for t in range(n):
                _remote(refs[t].at[_flat(peer)], refs[n + t].at[_flat(me)], *sems_out, t * (N_DEV - 1) + k - 1,
                        peer).start()

    return _split_call(body, name, list(grads) + lands, (), (n * (N_DEV - 1), n * (N_DEV - 1)), after, True)


def _exchange_to_sibling(grads, after, name):
    n = len(grads)
    parts = [lax.empty((3,) + g.shape[1:], g.dtype) for g in grads]
    lands = [lax.empty((4,) + g.shape[1:], g.dtype) for g in grads]

    def body(refs, _, sems_out):
        me = _my_place()
        sib = _flip(me, SIBLING)
        for t in range(n):
            g, part, land = refs[t], refs[n + t], refs[2 * n + t]
            _remote(g.at[_flat(sib)], land.at[0], *sems_out, 4 * t, sib).start()
            for a, k in enumerate(SAME_CORE):
                _remote(g.at[_flat(_flip(sib, k))], part.at[a], *sems_out, 4 * t + 1 + a, sib).start()

    return _split_call(body, name, list(grads) + parts + lands, (), (4 * n, 4 * n), after, True)


def _chip_sums(grads, parts, slots, name):
    _, rows, cols = grads.shape
    tr = _tile(rows, max(8, (512 * 1024) // cols))

    def body(slots_ref, g_ref, p_ref, o_ref):
        del slots_ref
        o_ref[...] = (g_ref[...].astype(F32) + p_ref[...].astype(F32)).astype(BF16)

    return pl.pallas_call(
        body, name=name,
        grid_spec=pltpu.PrefetchScalarGridSpec(
            num_scalar_prefetch=1, grid=(3, rows // tr),
            in_specs=[pl.BlockSpec((None, tr, cols), lambda a, i, slots_ref: (slots_ref[a], i, 0)),
                      pl.BlockSpec((None, tr, cols), lambda a, i, slots_ref: (a, i, 0))],
            out_specs=pl.BlockSpec((None, tr, cols), lambda a, i, slots_ref: (a, i, 0))),
        out_shape=jax.ShapeDtypeStruct((3, rows, cols), BF16),
        compiler_params=_params("parallel", "parallel"),
    )(slots, grads, parts)


def _exchange_to_chips(csums, lands, after, name):
    n = len(csums)

    def body(refs, _, sems_out):
        me = _my_place()
        for t in range(n):
            for a, k in enumerate(SAME_CORE):
                _remote(refs[t].at[a], refs[n + t].at[1 + a], *sems_out, 3 * t + a, _flip(me, k)).start()

    return _split_call(body, name, list(csums) + list(lands), (), (3 * n, 3 * n), after, True)


def _hbm_call(body, name, n_hbm, sems_in, sems_out, after, token, like):
    after = list(after)
    in_specs = [HBM_SPEC] * n_hbm + [SEM_SPEC] * len(sems_in) + [_any()] * len(after)
    out_specs = [SEM_SPEC] * len(sems_out) + [HBM_SPEC] * n_hbm + ([_vmem()] if token else [])
    out_shape = ([pltpu.SemaphoreType.DMA((c,)) for c in sems_out] + [pltpu.HBM(a.shape, a.dtype) for a in like]
                 + ([jax.ShapeDtypeStruct((8, LANES), F32)] if token else []))
    return in_specs, tuple(out_specs), tuple(out_shape), {i: len(sems_out) + i for i in range(n_hbm)}, after


def _gather_send(shards, lands, after, name):
    n = len(shards)
    peers = (SIBLING,) + SAME_CORE
    after = list(after)

    def body(*refs):
        me = _my_place()
        send_sems, recv_sems = refs[2 * n + len(after)], refs[2 * n + len(after) + 1]
        for a, k in enumerate(peers):
            for t in range(n):
                pltpu.make_async_remote_copy(
                    src_ref=refs[t], dst_ref=refs[n + t].at[_flat(me)],
                    send_sem=send_sems.at[t * 4 + a], recv_sem=recv_sems.at[t * 4 + a],
                    device_id=_flip(me, k), device_id_type=MESH_ID).start()
        refs[-1][...] = jnp.zeros_like(refs[-1])

    bufs = list(shards) + list(lands)
    in_specs, out_specs, out_shape, aliases, after = _hbm_call(body, name, 2 * n, (), (4 * n, 4 * n), after, True, bufs)
    out = pl.pallas_call(
        body, name=name, in_specs=in_specs, out_specs=out_specs, out_shape=out_shape,
        input_output_aliases=aliases, compiler_params=pltpu.CompilerParams(has_side_effects=SPLIT_EFFECT),
    )(*[pltpu.with_memory_space_constraint(a, pltpu.HBM) for a in bufs], *after)
    return out[0], out[1], out[2:2 + n], out[2 + n:2 + 2 * n], out[-1]


def _gather_forward(sent, after, name):
    send1, recv1, shards, lands, _ = sent
    n = len(shards)
    after = list(after)

    def body(*refs):
        me = _my_place()
        recv1_ref = refs[2 * n + 1]
        out0 = 2 * n + 2 + len(after)
        send2_ref, recv2_ref = refs[out0], refs[out0 + 1]
        for a, k in enumerate(SAME_CORE):
            owner = _flat(_flip(me, k))
            for t in range(n):
                slot = refs[n + t].at[owner]
                pltpu.make_async_remote_copy(
                    src_ref=refs[t], dst_ref=slot, send_sem=refs[2 * n].at[t * 4 + 1 + a],
                    recv_sem=recv1_ref.at[t * 4 + 1 + a], device_id=_flip(me, k), device_id_type=MESH_ID).wait_recv()
                pltpu.make_async_remote_copy(
                    src_ref=slot, dst_ref=slot, send_sem=send2_ref.at[t * 3 + a], recv_sem=recv2_ref.at[t * 3 + a],
                    device_id=_flip(me, SIBLING), device_id_type=MESH_ID).start()
        refs[-1][...] = jnp.zeros_like(refs[-1])

    bufs = list(shards) + list(lands)
    in_specs, out_specs, out_shape, aliases, after = _hbm_call(body, name, 2 * n, (4 * n, 4 * n), (3 * n, 3 * n), after,
                                                               True, bufs)
    out = pl.pallas_call(
        body, name=name, in_specs=in_specs, out_specs=out_specs, out_shape=out_shape,
        input_output_aliases=aliases, compiler_params=pltpu.CompilerParams(has_side_effects=SPLIT_EFFECT),
    )(*bufs, send1, recv1, *after)
    return (send1, recv1), (out[0], out[1]), out[2:2 + n], out[2 + n:2 + 2 * n], out[-1]


def _gather_finish(forwarded, after, name):
    (send1, recv1), (send2, recv2), shards, lands, _ = forwarded
    n = len(shards)
    after = list(after)

    def body(*refs):
        me = _my_place()
        send1_ref, recv1_ref, send2_ref, recv2_ref = refs[2 * n:2 * n + 4]
        sib = _flip(me, SIBLING)
        for t in range(n):
            for a in range(4):
                cp = pltpu.make_async_remote_copy(
                    src_ref=refs[t], dst_ref=refs[n + t].at[_flat(sib)], send_sem=send1_ref.at[t * 4 + a],
                    recv_sem=recv1_ref.at[t * 4 + a], device_id=sib, device_id_type=MESH_ID)
                cp.wait_send()
                if a == 0:
                    cp.wait_recv()
            for a in range(3):
                cp = pltpu.make_async_remote_copy(
                    src_ref=refs[t], dst_ref=refs[n + t].at[_flat(sib)], send_sem=send2_ref.at[t * 3 + a],
                    recv_sem=recv2_ref.at[t * 3 + a], device_id=sib, device_id_type=MESH_ID)
                cp.wait_send()
                cp.wait_recv()

    bufs = list(shards) + list(lands)
    in_specs, out_specs, out_shape, aliases, after = _hbm_call(body, name, 2 * n, (4 * n, 4 * n, 3 * n, 3 * n), (), after,
                                                               False, bufs)
    out = pl.pallas_call(
        body, name=name, in_specs=in_specs, out_specs=out_specs, out_shape=out_shape,
        input_output_aliases=aliases, compiler_params=pltpu.CompilerParams(has_side_effects=SPLIT_EFFECT),
    )(*bufs, send1, recv1, send2, recv2, *after)
    return out[n:]


def _gather_small(v, after, name):
    after = list(after)

    def body(v_ref, *rest):
        o_ref, send_sems, recv_sems = rest[-3:]
        me = _my_place()
        o_ref[_flat(me)] = v_ref[...]
        copies = []
        for k in range(1, N_DEV):
            copies.append(pltpu.make_async_remote_copy(
                src_ref=v_ref, dst_ref=o_ref.at[_flat(me)],
                send_sem=send_sems.at[k - 1], recv_sem=recv_sems.at[k - 1],
                device_id=_flip(me, k), device_id_type=MESH_ID))
        for cp in copies:
            cp.start()
        for cp in copies:
            cp.wait()

    return pl.pallas_call(
        body, name=name,
        in_specs=[_vmem()] + [_any()] * len(after), out_specs=_vmem(),
        out_shape=jax.ShapeDtypeStruct((N_DEV,) + v.shape, v.dtype),
        scratch_shapes=[pltpu.SemaphoreType.DMA((7,)), pltpu.SemaphoreType.DMA((7,))],
        compiler_params=_params(has_side_effects=True),
    )(v, *after)


def _rel_onehot(row):
    r_io = lax.broadcasted_iota(jnp.int32, (N_REL, PAIR_W), 0)
    p_io = lax.broadcasted_iota(jnp.int32, (N_REL, PAIR_W), 1)
    band_col = p_io - (row // CHUNK) * CHUNK
    in_band = jnp.logical_and(band_col >= 0, band_col < BAND_W)
    idx = jnp.clip(PAD_K + row % CHUNK - band_col, -(CHUNK - 1), REL_CLIP) + (CHUNK - 1)
    return jnp.logical_and(r_io == idx, in_band).astype(BF16), in_band[0:1]


def _bias_expand(rel):
    lh = rel.shape[0]

    def body(rel_ref, o_ref):
        onehot, in_band = _rel_onehot(pl.program_id(0))
        o_ref[...] = jnp.where(in_band, _split_dot(rel_ref[...], onehot, 3), NEG_BIG)

    return pl.pallas_call(
        body, name="bias_expand", grid=(PAIR,),
        in_specs=[pl.BlockSpec((lh, N_REL), lambda i: (0, 0))],
        out_specs=pl.BlockSpec((None, lh, PAIR_W), lambda i: (i, 0, 0)),
        out_shape=jax.ShapeDtypeStruct((PAIR, lh, PAIR_W), F32),
        compiler_params=_params("parallel"),
    )(rel)


SHEAR_W = PAIR_W + PAIR
BIAS_HEADS = 8


def _bias_grad(dbias, after):
    lh = dbias.shape[0]
    hb = _tile(lh, BIAS_HEADS)
    after = list(after)

    def body(db_ref, *rest):
        o_ref = rest[-1]
        a_io = lax.broadcasted_iota(jnp.int32, (PAIR, PAIR), 0)
        b_io = lax.broadcasted_iota(jnp.int32, (PAIR, PAIR), 1)
        flip_rows = (a_io + b_io == PAIR - 1).astype(BF16)
        diags = []
        for j in range(hb):
            rest_part = jnp.concatenate([db_ref[j], jnp.zeros((PAIR, SHEAR_W - PAIR_W), F32)], axis=1)
            flipped = None
            for _ in range(3):
                piece = rest_part.astype(BF16)
                rest_part = rest_part - piece.astype(F32)
                term = _dot(flip_rows, piece)
                flipped = term if flipped is None else flipped + term
            sheared = pltpu.roll(flipped, 0, 1, stride=1, stride_axis=0)
            diags.append(jnp.sum(sheared, axis=0, keepdims=True))
        c_io = lax.broadcasted_iota(jnp.int32, (SHEAR_W, N_REL), 0)
        r_io = lax.broadcasted_iota(jnp.int32, (SHEAR_W, N_REL), 1)
        entry = jnp.clip(PAD_K + (PAIR - 1) - c_io, -(CHUNK - 1), REL_CLIP) + (CHUNK - 1)
        o_ref[...] = _split_dot(jnp.concatenate(diags, axis=0), (r_io == entry).astype(BF16), 3)

    return pl.pallas_call(
        body, name="bias_grad", grid=(lh // hb,),
        in_specs=[pl.BlockSpec((hb, PAIR, PAIR_W), lambda i: (i, 0, 0))] + [_any()] * len(after),
        out_specs=pl.BlockSpec((hb, N_REL), lambda i: (i, 0)),
        out_shape=jax.ShapeDtypeStruct((lh, N_REL), F32),
        compiler_params=_params("parallel"),
    )(dbias, *after)


def _norm_proj(x, g, w_all, tie, layer):
    s, d = x.shape
    e = w_all.shape[2]
    tm, tn = _tile(s, 1024), _tile(e, 1024)
    nb = e // tn
    ties = [] if tie is None else [tie]

    def body(x_ref, g_ref, w_ref, *rest):
        proj_ref, h_ref = rest[-2:]

        @pl.when(pl.program_id(1) == 0)
        def _():
            xv = x_ref[...]
            r = lax.rsqrt(jnp.mean(xv * xv, axis=-1, keepdims=True) + NORM_EPS)
            h_ref[...] = ((xv * r) * g_ref[...]).astype(BF16)

        proj_ref[...] = _dot(h_ref[...], w_ref[...])

    return pl.pallas_call(
        body, name=f"norm_proj_l{layer}", grid=(s // tm, N_DEV * nb),
        in_specs=[pl.BlockSpec((tm, d), lambda m, n: (m, 0)),
                  pl.BlockSpec((1, d), lambda m, n: (0, 0)),
                  pl.BlockSpec((None, d, tn), lambda m, n: (n // nb, 0, n % nb))] + [_any()] * len(ties),
        out_specs=(pl.BlockSpec((None, tm, tn), lambda m, n: (n // nb, m, n % nb)),
                   pl.BlockSpec((tm, d), lambda m, n: (m, 0))),
        out_shape=(jax.ShapeDtypeStruct((N_DEV, s, e), F32), jax.ShapeDtypeStruct((s, d), BF16)),
        compiler_params=_params("parallel", "arbitrary"),
    )(x, g, w_all, *ties)


def _sb_forward(proj, layer):
    _, s, e = proj.shape
    hp = _tile(e // HEAD_DIM, SB_HEADS)
    width = hp * HEAD_DIM
    tq = _tile(s, SB_Q)
    diag_tiles = tq // SB_K
    scale = HEAD_DIM ** -0.5

    def body(p_ref, y_ref, mix_ref, tot_ref, kb_ref, vb_ref):
        kb_ref[...] = p_ref[1].astype(BF16)
        vb_ref[...] = p_ref[2].astype(BF16)
        row = lax.broadcasted_iota(jnp.int32, (tq, SB_K), 0)
        col = lax.broadcasted_iota(jnp.int32, (tq, SB_K), 1)
        kj = lax.broadcasted_iota(jnp.int32, (SB_K, SB_K), 0)
        ks = lax.broadcasted_iota(jnp.int32, (SB_K, SB_K), 1)
        later = (kj > ks).astype(BF16)

        def q_block(qi, _):
            t0 = pl.multiple_of(qi * tq, tq)
            qb = [p_ref[0, pl.ds(t0, tq), h * HEAD_DIM:(h + 1) * HEAD_DIM].astype(BF16) for h in range(hp)]

            def tile(s0, state, causal):
                out = []
                for h in range(hp):
                    carry, acc = state[h]
                    lanes = slice(h * HEAD_DIM, (h + 1) * HEAD_DIM)
                    z = _dot(qb[h], kb_ref[pl.ds(s0, SB_K), lanes], NT) * scale
                    ls = _log_sigmoid(z)
                    stay = ls - z
                    if causal is not None:
                        stay = jnp.where(causal, stay, 0.0)
                    w = jnp.exp(ls + carry + _split_dot(stay, later, 2))
                    if causal is not None:
                        w = jnp.where(causal, w, 0.0)
                    acc = acc + _dot(w.astype(BF16), vb_ref[pl.ds(s0, SB_K), lanes])
                    out.append((carry + jnp.sum(stay, axis=1, keepdims=True), acc))
                return tuple(out)

            state = tuple((jnp.zeros((tq, 1), F32), jnp.zeros((tq, HEAD_DIM), F32)) for _ in range(hp))
            for dt in reversed(range(diag_tiles)):
                state = tile(t0 + dt * SB_K, state, col + dt * SB_K < row)

            def k_blocks(j, st):
                for u in range(SB_UNROLL):
                    st = tile(pl.multiple_of((diag_tiles * qi - 1 - SB_UNROLL * j - u) * SB_K, SB_K), st, None)
                return st

            state = lax.fori_loop(0, diag_tiles * qi // SB_UNROLL, k_blocks, state)
            silu, _ = _silu_and_grad(p_ref[3, pl.ds(t0, tq), :])
            for h in range(hp):
                lanes = slice(h * HEAD_DIM, (h + 1) * HEAD_DIM)
                y_ref[pl.ds(t0, tq), lanes] = state[h][1]
                mix_ref[pl.ds(t0, tq), lanes] = (state[h][1] * silu[:, lanes]).astype(BF16)
                tot_ref[h, pl.ds(t0, tq), :] = state[h][0]
            return 0

        lax.fori_loop(0, s // tq, q_block, 0)

    return pl.pallas_call(
        body, name=f"sb_forward_l{layer}", grid=(e // width,),
        in_specs=[pl.BlockSpec((4, s, width), lambda h: (0, 0, h))],
        out_specs=(pl.BlockSpec((s, width), lambda h: (0, h)),
                   pl.BlockSpec((s, width), lambda h: (0, h)),
                   pl.BlockSpec((hp, s, 1), lambda h: (h, 0, 0))),
        out_shape=(jax.ShapeDtypeStruct((s, 2 * e), F32), jax.ShapeDtypeStruct((s, 2 * e), BF16),
                   jax.ShapeDtypeStruct((e // HEAD_DIM, s, 1), F32)),
        scratch_shapes=[pltpu.VMEM((s, width), BF16), pltpu.VMEM((s, width), BF16)],
        compiler_params=_params("parallel"),
    )(proj)


def _qk_norm(t, gain):
    r = lax.rsqrt(jnp.mean(t * t, axis=-1, keepdims=True) + NORM_EPS)
    return t * r, r, (t * r) * gain


def _chunk_scores(qc, kw, bias, t0, scale):
    sc = _dot(qc, kw, NT) * scale + bias
    col = lax.broadcasted_iota(jnp.int32, (PAIR, PAIR_W), 1)
    sc = jnp.where(col + t0 >= PAD_K, sc, NEG_BIG)
    ex = jnp.exp(sc - jnp.max(sc, axis=-1, keepdims=True))
    return ex / jnp.sum(ex, axis=-1, keepdims=True)


def _chunk_forward(proj, bias, q_gain, k_gain, y, mixed, tie, layer):
    ties = [] if tie is None else [tie]
    _, s, e = proj.shape
    hp = _tile(e // HEAD_DIM, CHUNK_HEADS)
    width = hp * HEAD_DIM
    steps = e // width
    unroll = _tile(s // PAIR, PAIR_UNROLL)
    scale = HEAD_DIM ** -0.5
    heads = [slice(h * HEAD_DIM, (h + 1) * HEAD_DIM) for h in range(hp)]

    def body(p_ref, b_ref, qg_ref, kg_ref, *rest):
        y_ref, mix_ref, qn_ref, kp_ref, vp_ref = rest[-5:]
        kp_ref[pl.ds(0, PAD_K), :] = jnp.zeros((PAD_K, width), BF16)
        vp_ref[pl.ds(0, PAD_K), :] = jnp.zeros((PAD_K, width), BF16)
        vp_ref[pl.ds(PAD_K, s), :] = p_ref[2].astype(BF16)
        for lanes in heads:
            qn_ref[:, lanes] = _qk_norm(p_ref[0, :, lanes], qg_ref[...])[2].astype(BF16)
            kp_ref[pl.ds(PAD_K, s), lanes] = _qk_norm(p_ref[1, :, lanes], kg_ref[...])[2].astype(BF16)

        def chunks(ci, _):
            done = []
            for u in range(unroll):
                t0 = pl.multiple_of((ci * unroll + u) * PAIR, PAIR)
                silu, _ = _silu_and_grad(p_ref[3, pl.ds(t0, PAIR), :])
                for h, lanes in enumerate(heads):
                    probs = _chunk_scores(qn_ref[pl.ds(t0, PAIR), lanes], kp_ref[pl.ds(t0, PAIR_W), lanes],
                                          b_ref[h], t0, scale)
                    out = _dot(probs.astype(BF16), vp_ref[pl.ds(t0, PAIR_W), lanes])
                    done.append((t0, lanes, out, (out * silu[:, lanes]).astype(BF16)))
            for t0, lanes, out, gated in done:
                y_ref[pl.ds(t0, PAIR), lanes] = out
                mix_ref[pl.ds(t0, PAIR), lanes] = gated
            return 0

        lax.fori_loop(0, s // (PAIR * unroll), chunks, 0)

    return pl.pallas_call(
        body, name=f"chunk_forward_l{layer}", grid=(steps,),
        in_specs=[pl.BlockSpec((4, s, width), lambda h: (1, 0, h)),
                  pl.BlockSpec((hp, PAIR, PAIR_W), lambda h: (layer * steps + h, 0, 0)),
                  pl.BlockSpec((1, HEAD_DIM), lambda h: (0, 0)),
                  pl.BlockSpec((1, HEAD_DIM), lambda h: (0, 0)),
                  _any(), _any()] + [_any()] * len(ties),
        out_specs=(pl.BlockSpec((s, width), lambda h: (0, steps + h)),
                   pl.BlockSpec((s, width), lambda h: (0, steps + h))),
        out_shape=(jax.ShapeDtypeStruct(y.shape, F32), jax.ShapeDtypeStruct(mixed.shape, BF16)),
        input_output_aliases={4: 0, 5: 1},
        scratch_shapes=[pltpu.VMEM((s, width), BF16), pltpu.VMEM((s + PAD_K, width), BF16),
                        pltpu.VMEM((s + PAD_K, width), BF16)],
        compiler_params=_params("parallel"),
    )(proj, bias, q_gain, k_gain, y, mixed, *ties)


def _out_proj(mixed, w, x, layer):
    s, d = x.shape
    tm, tn = _tile(s, 1024), _tile(d, 1024)

    def body(a_ref, w_ref, x_ref, o_ref):
        o_ref[...] = x_ref[...] + _dot(a_ref[...], w_ref[...])

    return pl.pallas_call(
        body, name=f"out_proj_l{layer}", grid=(s // tm, d // tn),
        in_specs=[pl.BlockSpec((tm, d), lambda m, n: (m, 0)),
                  pl.BlockSpec((d, tn), lambda m, n: (0, n)),
                  pl.BlockSpec((tm, tn), lambda m, n: (m, n))],
        out_specs=pl.BlockSpec((tm, tn), lambda m, n: (m, n)),
        out_shape=jax.ShapeDtypeStruct((s, d), F32),
        compiler_params=_params("parallel", "parallel"),
    )(mixed, w, x)


def _loss_head(y, target):
    s, d = y.shape
    tm = _tile(s, 256)

    def body(y_ref, t_ref, dy_ref, part_ref):
        diff = y_ref[...] - t_ref[...]
        dy_ref[...] = diff * (1.0 / d)
        sq = (diff * diff).reshape(tm // 8, 8, d).sum(axis=0)
        acc = sq[:, 0:LANES]
        for j in range(1, d // LANES):
            acc = acc + sq[:, j * LANES:(j + 1) * LANES]
        part_ref[...] = acc * (0.5 / d)

    return pl.pallas_call(
        body, name="loss_head", grid=(s // tm,),
        in_specs=[pl.BlockSpec((tm, d), lambda i: (i, 0)), pl.BlockSpec((tm, d), lambda i: (i, 0))],
        out_specs=(pl.BlockSpec((tm, d), lambda i: (i, 0)), pl.BlockSpec((None, 8, LANES), lambda i: (i, 0, 0))),
        out_shape=(jax.ShapeDtypeStruct((s, d), F32), jax.ShapeDtypeStruct((s // tm, 8, LANES), F32)),
        compiler_params=_params("parallel"),
    )(y, target)


def _out_proj_bwd_input(dx, w, tie, layer):
    s, d = dx.shape
    tm, tn = _tile(s, 1024), _tile(d, 1024)
    ties = [] if tie is None else [tie]

    def body(dx_ref, w_ref, *rest):
        rest[-1][...] = _dot(dx_ref[...].astype(BF16), w_ref[...], NT)

    return pl.pallas_call(
        body, name=f"out_proj_dx_l{layer}", grid=(s // tm, d // tn),
        in_specs=[pl.BlockSpec((tm, d), lambda m, n: (m, 0)),
                  pl.BlockSpec((tn, d), lambda m, n: (n, 0))] + [_any()] * len(ties),
        out_specs=pl.BlockSpec((tm, tn), lambda m, n: (m, n)),
        out_shape=jax.ShapeDtypeStruct((s, d), F32),
        compiler_params=_params("parallel", "parallel"),
    )(dx, w, *ties)


def _out_proj_bwd_weight(mixed, dx, layer):
    s, d = dx.shape
    te, tn = _tile(d, 1024), _tile(d, 1024)

    def body(a_ref, dx_ref, o_ref):
        o_ref[...] = _dot(a_ref[...], dx_ref[...].astype(BF16), TN).astype(BF16)

    return pl.pallas_call(
        body, name=f"out_proj_dw_l{layer}", grid=(d // te, d // tn),
        in_specs=[pl.BlockSpec((s, te), lambda i, n: (0, i)), pl.BlockSpec((s, tn), lambda i, n: (0, n))],
        out_specs=pl.BlockSpec((te, tn), lambda i, n: (i, n)),
        out_shape=jax.ShapeDtypeStruct((d, d), BF16),
        compiler_params=_params("parallel", "parallel"),
    )(mixed, dx)


def _sb_backward(proj, y, dmixed, tot, layer):
    _, s, e = proj.shape
    hp = _tile(e // HEAD_DIM, SB_HEADS)
    width = hp * HEAD_DIM
    tq = _tile(s, SB_Q)
    diag_tiles = tq // SB_K
    scale = HEAD_DIM ** -0.5

    def body(p_ref, y_ref, dm_ref, tot_ref, o_ref, kb_ref, vb_ref, do_ref, dk_ref, dv_ref):
        kb_ref[...] = p_ref[1].astype(BF16)
        vb_ref[...] = p_ref[2].astype(BF16)
        silu, dsilu = _silu_and_grad(p_ref[3])
        dm = dm_ref[...]
        do_ref[...] = (dm * silu).astype(BF16)
        o_ref[3] = (dm * y_ref[...] * dsilu).astype(BF16)
        dk_ref[...] = jnp.zeros_like(dk_ref)
        dv_ref[...] = jnp.zeros_like(dv_ref)
        row = lax.broadcasted_iota(jnp.int32, (tq, SB_K), 0)
        col = lax.broadcasted_iota(jnp.int32, (tq, SB_K), 1)
        kj = lax.broadcasted_iota(jnp.int32, (SB_K, SB_K), 0)
        ks = lax.broadcasted_iota(jnp.int32, (SB_K, SB_K), 1)
        upto = (kj <= ks).astype(BF16)
        before = (kj < ks).astype(BF16)
        upto = jnp.concatenate([upto, upto], axis=0)
        before = jnp.concatenate([before, before], axis=0)

        def q_block(qi, _):
            t0 = pl.multiple_of(qi * tq, tq)
            heads = [slice(h * HEAD_DIM, (h + 1) * HEAD_DIM) for h in range(hp)]
            qb = [p_ref[0, pl.ds(t0, tq), lanes].astype(BF16) for lanes in heads]
            dob = [do_ref[pl.ds(t0, tq), lanes] for lanes in heads]
            total = [tot_ref[h, pl.ds(t0, tq), :] for h in range(hp)]

            def tile(s0, state, causal):
                out, adds = [], []
                for h, lanes in enumerate(heads):
                    stay_sum, dlw_sum, dq = state[h]
                    kt = kb_ref[pl.ds(s0, SB_K), lanes]
                    vt = vb_ref[pl.ds(s0, SB_K), lanes]
                    z = _dot(qb[h], kt, NT) * scale
                    ls = _log_sigmoid(z)
                    stay = ls - z
                    if causal is not None:
                        stay = jnp.where(causal, stay, 0.0)
                    after = total[h] - (stay_sum + _split_dot_deep(stay, upto))
                    w = jnp.exp(ls + after)
                    if causal is not None:
                        w = jnp.where(causal, w, 0.0)
                    dlw = _dot(dob[h], vt, NT) * w
                    prior = dlw_sum + _split_dot_deep(dlw, before)
                    sig = jnp.exp(ls)
                    dz = (dlw * (1.0 - sig) - sig * prior) * scale
                    if causal is not None:
                        dz = jnp.where(causal, dz, 0.0)
                    dzb = dz.astype(BF16)
                    dq = dq + _dot(dzb, kt)
                    adds.append((lanes, _dot(dzb, qb[h], TN), _dot(w.astype(BF16), dob[h], TN)))
                    out.append((stay_sum + jnp.sum(stay, axis=1, keepdims=True),
                                dlw_sum + jnp.sum(dlw, axis=1, keepdims=True), dq))
                for lanes, dk, dv in adds:
                    dk_ref[pl.ds(s0, SB_K), lanes] += dk
                    dv_ref[pl.ds(s0, SB_K), lanes] += dv
                return tuple(out)

            def k_blocks(j, st):
                for u in range(SB_UNROLL_BWD):
                    st = tile(pl.multiple_of((SB_UNROLL_BWD * j + u) * SB_K, SB_K), st, None)
                return st

            zero = jnp.zeros((tq, 1), F32)
            state = tuple((zero, zero, jnp.zeros((tq, HEAD_DIM), F32)) for _ in range(hp))
            state = lax.fori_loop(0, diag_tiles * qi // SB_UNROLL_BWD, k_blocks, state)
            for dt in range(diag_tiles):
                state = tile(t0 + dt * SB_K, state, col + dt * SB_K < row)
            for h, lanes in enumerate(heads):
                o_ref[0, pl.ds(t0, tq), lanes] = state[h][2].astype(BF16)
            return 0

        lax.fori_loop(0, s // tq, q_block, 0)
        o_ref[1] = dk_ref[...].astype(BF16)
        o_ref[2] = dv_ref[...].astype(BF16)

    return pl.pallas_call(
        body, name=f"sb_backward_l{layer}", grid=(e // width,),
        in_specs=[pl.BlockSpec((4, s, width), lambda h: (0, 0, h)),
                  pl.BlockSpec((s, width), lambda h: (0, h)),
                  pl.BlockSpec((s, width), lambda h: (0, h)),
                  pl.BlockSpec((hp, s, 1), lambda h: (h, 0, 0))],
        out_specs=pl.BlockSpec((4, s, width), lambda h: (0, 0, h)),
        out_shape=jax.ShapeDtypeStruct((N_DEV, s, e), BF16),
        scratch_shapes=[pltpu.VMEM((s, width), BF16), pltpu.VMEM((s, width), BF16),
                        pltpu.VMEM((s, width), BF16), pltpu.VMEM((s, width), F32),
                        pltpu.VMEM((s, width), F32)],
        compiler_params=_params("parallel"),
    )(proj, y, dmixed, tot)


def _norm_bwd(dn, xh, r, gain):
    dxh = dn * gain
    return r * (dxh - xh * jnp.mean(dxh * xh, axis=-1, keepdims=True)), dn * xh


def _chunk_backward(proj, bias, q_gain, k_gain, y, dmixed, dproj, layer):
    _, s, e = proj.shape
    hp = CHUNK_HEADS_BWD
    width = hp * HEAD_DIM
    steps = e // width
    unroll = _tile(s // PAIR, PAIR_UNROLL_BWD)
    scale = HEAD_DIM ** -0.5
    heads = [slice(h * HEAD_DIM, (h + 1) * HEAD_DIM) for h in range(hp)]

    def body(p_ref, b_ref, qg_ref, kg_ref, y_ref, dm_ref, dp_in, o_ref, db_ref, dqg_ref, dkg_ref,
             qn_ref, kp_ref, vp_ref, do_ref, dqn_ref, dkn_ref, dvp_ref):
        del dp_in
        kp_ref[pl.ds(0, PAD_K), :] = jnp.zeros((PAD_K, width), BF16)
        vp_ref[pl.ds(0, PAD_K), :] = jnp.zeros((PAD_K, width), BF16)
        vp_ref[pl.ds(PAD_K, s), :] = p_ref[2].astype(BF16)
        for lanes in heads:
            qn_ref[:, lanes] = _qk_norm(p_ref[0, :, lanes], qg_ref[...])[2].astype(BF16)
            kp_ref[pl.ds(PAD_K, s), lanes] = _qk_norm(p_ref[1, :, lanes], kg_ref[...])[2].astype(BF16)
        silu, dsilu = _silu_and_grad(p_ref[3])
        dm = dm_ref[...]
        do_ref[...] = (dm * silu).astype(BF16)
        o_ref[3] = (dm * y_ref[...] * dsilu).astype(BF16)
        dkn_ref[...] = jnp.zeros_like(dkn_ref)
        dvp_ref[...] = jnp.zeros_like(dvp_ref)
        db_ref[...] = jnp.zeros_like(db_ref)

        def chunks(ci, _):
            done = []
            for u in range(unroll):
                t0 = pl.multiple_of((ci * unroll + u) * PAIR, PAIR)
                for h, lanes in enumerate(heads):
                    qc = qn_ref[pl.ds(t0, PAIR), lanes]
                    kw = kp_ref[pl.ds(t0, PAIR_W), lanes]
                    vw = vp_ref[pl.ds(t0, PAIR_W), lanes]
                    dob = do_ref[pl.ds(t0, PAIR), lanes]
                    probs = _chunk_scores(qc, kw, b_ref[h], t0, scale)
                    dprobs = _dot(dob, vw, NT)
                    dsc = probs * (dprobs - jnp.sum(probs * dprobs, axis=-1, keepdims=True))
                    dsb = (dsc * scale).astype(BF16)
                    done.append((t0, h, lanes, dsc, _dot(dsb, kw), _dot(dsb, qc, TN),
                                 _dot(probs.astype(BF16), dob, TN)))
            for t0, h, lanes, dsc, dqn, dkn, dvp in done:
                db_ref[h] += dsc
                dqn_ref[pl.ds(t0, PAIR), lanes] = dqn
                dkn_ref[pl.ds(t0, PAIR_W), lanes] += dkn
                dvp_ref[pl.ds(t0, PAIR_W), lanes] += dvp
            return 0

        lax.fori_loop(0, s // (PAIR * unroll), chunks, 0)
        o_ref[2] = dvp_ref[pl.ds(PAD_K, s), :].astype(BF16)

        @pl.when(pl.program_id(0) == 0)
        def _():
            dqg_ref[...] = jnp.zeros_like(dqg_ref)
            dkg_ref[...] = jnp.zeros_like(dkg_ref)

        for lanes in heads:
            qh, rq, _ = _qk_norm(p_ref[0, :, lanes], qg_ref[...])
            dq, dqg_rows = _norm_bwd(dqn_ref[:, lanes], qh, rq, qg_ref[...])
            o_ref[0, :, lanes] = dq.astype(BF16)
            dqg_ref[...] += jnp.sum(dqg_rows, axis=0, keepdims=True)
            kh, rk, _ = _qk_norm(p_ref[1, :, lanes], kg_ref[...])
            dk, dkg_rows = _norm_bwd(dkn_ref[pl.ds(PAD_K, s), lanes], kh, rk, kg_ref[...])
            o_ref[1, :, lanes] = dk.astype(BF16)
            dkg_ref[...] += jnp.sum(dkg_rows, axis=0, keepdims=True)

    return pl.pallas_call(
        body, name=f"chunk_backward_l{layer}", grid=(steps,),
        in_specs=[pl.BlockSpec((4, s, width), lambda h: (1, 0, h)),
                  pl.BlockSpec((hp, PAIR, PAIR_W), lambda h: (layer * steps + h, 0, 0)),
                  pl.BlockSpec((1, HEAD_DIM), lambda h: (0, 0)),
                  pl.BlockSpec((1, HEAD_DIM), lambda h: (0, 0)),
                  pl.BlockSpec((s, width), lambda h: (0, steps + h)),
                  pl.BlockSpec((s, width), lambda h: (0, steps + h)),
                  _any()],
        out_specs=(pl.BlockSpec((4, s, width), lambda h: (1, 0, h)),
                   pl.BlockSpec((hp, PAIR, PAIR_W), lambda h: (h, 0, 0)),
                   pl.BlockSpec((1, HEAD_DIM), lambda h: (0, 0)),
                   pl.BlockSpec((1, HEAD_DIM), lambda h: (0, 0))),
        out_shape=(jax.ShapeDtypeStruct(dproj.shape, BF16),
                   jax.ShapeDtypeStruct((e // HEAD_DIM, PAIR, PAIR_W), F32),
                   jax.ShapeDtypeStruct((1, HEAD_DIM), F32), jax.ShapeDtypeStruct((1, HEAD_DIM), F32)),
        input_output_aliases={6: 0},
        scratch_shapes=[pltpu.VMEM((s, width), BF16), pltpu.VMEM((s + PAD_K, width), BF16),
                        pltpu.VMEM((s + PAD_K, width), BF16), pltpu.VMEM((s, width), BF16),
                        pltpu.VMEM((s, width), F32), pltpu.VMEM((s + PAD_K, width), F32),
                        pltpu.VMEM((s + PAD_K, width), F32)],
        compiler_params=_params("arbitrary"),
    )(proj, bias, q_gain, k_gain, y, dmixed, dproj)


def _proj_bwd_input(dproj, w_all, x, g, dx, tie, layer):
    s, d = x.shape
    e = w_all.shape[2]
    tm = _tile(s, 512)
    gs = 1
    steps = N_DEV // gs

    def body(dp_ref, w_ref, x_ref, g_ref, dx_ref, tie_ref, o_ref, dg_ref, acc_ref):
        del tie_ref
        j = pl.program_id(1)

        @pl.when(j == 0)
        def _():
            acc_ref[...] = jnp.zeros_like(acc_ref)

        part = _dot(dp_ref[0], w_ref[0], NT)
        for k in range(1, gs):
            part = part + _dot(dp_ref[k], w_ref[k], NT)
        acc_ref[...] += part

        @pl.when(jnp.logical_and(j == steps - 1, pl.program_id(0) == 0))
        def _():
            dg_ref[...] = jnp.zeros_like(dg_ref)

        @pl.when(j == steps - 1)
        def _():
            xv = x_ref[...]
            r = lax.rsqrt(jnp.mean(xv * xv, axis=-1, keepdims=True) + NORM_EPS)
            dxn, dg_rows = _norm_bwd(acc_ref[...], xv * r, r, g_ref[...])
            o_ref[...] = dx_ref[...] + dxn
            dg_ref[...] += jnp.sum(dg_rows, axis=0, keepdims=True)

    return pl.pallas_call(
        body, name=f"proj_dx_l{layer}", grid=(s // tm, steps),
        in_specs=[pl.BlockSpec((gs, tm, e), lambda m, j: (j, m, 0)),
                  pl.BlockSpec((gs, d, e), lambda m, j: (j, 0, 0)),
                  pl.BlockSpec((tm, d), lambda m, j: (m, 0)),
                  pl.BlockSpec((1, d), lambda m, j: (0, 0)),
                  pl.BlockSpec((tm, d), lambda m, j: (m, 0)), _any()],
        out_specs=(pl.BlockSpec((tm, d), lambda m, j: (m, 0)), pl.BlockSpec((1, d), lambda m, j: (0, 0))),
        out_shape=(jax.ShapeDtypeStruct((s, d), F32), jax.ShapeDtypeStruct((1, d), F32)),
        scratch_shapes=[pltpu.VMEM((tm, d), F32)],
        compiler_params=_params("arbitrary", "arbitrary"),
    )(dproj, w_all, x, g, dx, tie)


def _proj_bwd_weight(h, dproj, layer):
    s, d = h.shape
    e = dproj.shape[2]
    td, tn = _tile(d, 1024), _tile(e, 1024)
    nb = e // tn

    def body(h_ref, dp_ref, o_ref):
        o_ref[...] = _dot(h_ref[...], dp_ref[...], TN).astype(BF16)

    return pl.pallas_call(
        body, name=f"proj_dw_l{layer}", grid=(d // td, N_DEV * nb),
        in_specs=[pl.BlockSpec((s, td), lambda i, n: (0, i)),
                  pl.BlockSpec((None, s, tn), lambda i, n: (n // nb, 0, n % nb))],
        out_specs=pl.BlockSpec((None, td, tn), lambda i, n: (n // nb, i, n % nb)),
        out_shape=jax.ShapeDtypeStruct((N_DEV, d, e), BF16),
        compiler_params=_params("parallel", "parallel"),
    )(h, dproj)


def _adamw_math(w, g, m, v):
    m = ADAM_B1 * m + (1.0 - ADAM_B1) * g
    v = ADAM_B2 * v + (1.0 - ADAM_B2) * (g * g)
    m_hat = m / (1.0 - ADAM_B1 ** ADAM_STEP)
    v_hat = v / (1.0 - ADAM_B2 ** ADAM_STEP)
    return -ADAM_LR * (m_hat / (jnp.sqrt(v_hat) + ADAM_EPS) + ADAM_WD * w), m, v


def _adamw_layer(parts, own, me, w, m, v, prev, layer, name):
    n_layers, rows, cols = w.shape
    n_parts = parts.shape[0]
    tr = _tile(rows, max(8, (256 * 1024) // cols))

    def body(me_ref, p_ref, own_ref, w_ref, m_ref, v_ref, *rest):
        g_ref, d_ref, nm_ref, nv_ref = rest[-4:]
        mine = own_ref[...].astype(F32)
        if n_parts == N_DEV:
            g = None
            for j in range(N_DEV):
                term = jnp.where(me_ref[0] == j, mine, p_ref[j].astype(F32))
                g = term if g is None else g + term
        else:
            g = mine
            for j in range(n_parts):
                g = g + p_ref[j].astype(F32)
        g_ref[...] = g
        d_ref[...], nm_ref[...], nv_ref[...] = _adamw_math(w_ref[...], g, m_ref[...], v_ref[...])

    blk = pl.BlockSpec((None, tr, cols), lambda i, me_ref: (layer, i, 0))
    out_shape = tuple(jax.ShapeDtypeStruct(w.shape, F32) for _ in range(4))
    in_specs = [pl.BlockSpec((n_parts, tr, cols), lambda i, me_ref: (0, i, 0)),
                pl.BlockSpec((None, tr, cols), lambda i, me_ref: (me_ref[0], i, 0)), blk, blk, blk]
    args = [me, parts, own, w, m, v]
    aliases = {}
    if prev is not None:
        in_specs += [_any()] * 4
        args += list(prev)
        aliases = {6 + k: k for k in range(4)}
    return pl.pallas_call(
        body, name=f"{name}_l{layer}",
        grid_spec=pltpu.PrefetchScalarGridSpec(
            num_scalar_prefetch=1, grid=(rows // tr,), in_specs=in_specs, out_specs=(blk, blk, blk, blk)),
        out_shape=out_shape, input_output_aliases=aliases,
        compiler_params=_params("parallel"),
    )(*args)


def _sum_slots(parts):
    def body(p_ref, o_ref):
        g = p_ref[0]
        for j in range(1, N_DEV):
            g = g + p_ref[j]
        o_ref[...] = g

    return pl.pallas_call(
        body, name="sum_small_grads",
        in_specs=[_vmem()], out_specs=_vmem(),
        out_shape=jax.ShapeDtypeStruct(parts.shape[1:], F32),
        compiler_params=_params(),
    )(parts)


def _adamw_small(w, g, m, v):
    def body(w_ref, g_ref, m_ref, v_ref, d_ref, nm_ref, nv_ref):
        d_ref[...], nm_ref[...], nv_ref[...] = _adamw_math(w_ref[...], g_ref[...], m_ref[...], v_ref[...])

    return pl.pallas_call(
        body, name="adamw_small",
        in_specs=[_vmem()] * 4, out_specs=(_vmem(),) * 3,
        out_shape=tuple(jax.ShapeDtypeStruct(w.shape, F32) for _ in range(3)),
        compiler_params=_params(),
    )(w, g, m, v)


def _pack_rows(arrays):
    rows = []
    for a in arrays:
        flat = a.reshape(-1)
        pad = (-flat.shape[0]) % (8 * LANES)
        rows.append(jnp.pad(flat, (0, pad)).reshape(-1, LANES))
    return jnp.concatenate(rows, axis=0)


def _unpack_rows(packed, like):
    out, r0 = [], 0
    for a in like:
        n = a.size
        nr = -(-n // (8 * LANES)) * 8
        out.append(packed[r0:r0 + nr].reshape(-1)[:n].reshape(a.shape))
        r0 += nr
    return out


def kernel(x, norm_g, w_in, q_norm_g, k_norm_g, rel_bias, w_out, loss_target, m_norm_g, m_w_in, m_q_norm_g, m_k_norm_g, m_rel_bias, m_w_out, v_norm_g, v_w_in, v_q_norm_g, v_k_norm_g, v_rel_bias, v_w_out):
    depth, d, e = w_in.shape
    r_out = w_out.shape[1]
    heads = e // HEAD_DIM
    rel_w = rel_bias.shape[2]
    x0 = x[0]
    target = loss_target[0]
    s = x0.shape[0]

    me = jnp.reshape(_flat(_my_place()), (1,)).astype(jnp.int32)

    casts = [(_cast_layer(w_in, me, l, "cast_w_in"), _cast_layer(w_out, me, l, "cast_w_out"))
             for l in range(depth)]

    def begin_gather(l, after):
        (win_b, win_land), (wout_b, wout_land) = casts[l]
        return _gather_send((win_b, wout_b), (win_land, wout_land), after, f"gather_send_l{l}")

    rel_all = _gather_small(rel_bias, [], "gather_rel_bias")
    sent = begin_gather(0, [rel_all])
    rel_full = jnp.transpose(rel_all, (1, 2, 0, 3)).reshape(depth * heads, N_DEV * rel_w)
    bias = jnp.transpose(_bias_expand(rel_full), (1, 0, 2))
    head_work = [bias] + [shard for cast in casts[1:] for shard, _ in cast]
    forwarded = _gather_forward(sent, head_work, "gather_forward_l0")

    xs, hs, projs, ys, mixes, tots, weights = [], [], [], [], [], [], []
    xl = x0
    for l in range(depth):
        win_all, wout_all = _gather_finish(forwarded, [xl, forwarded[-1]], f"gather_finish_l{l}")
        more = l + 1 < depth
        if more:
            sent = begin_gather(l + 1, [win_all])
        wout_full = wout_all.reshape(d, d)
        proj, h = _norm_proj(xl, norm_g[l:l + 1], win_all, sent[-1] if more else None, l)
        y, mixed, tot = _sb_forward(proj, l)
        if more:
            forwarded = _gather_forward(sent, [tot], f"gather_forward_l{l + 1}")
        y, mixed = _chunk_forward(proj, bias, q_norm_g[l:l + 1], k_norm_g[l:l + 1], y, mixed,
                                  forwarded[-1] if more else None, l)
        xs.append(xl), hs.append(h), projs.append(proj), ys.append(y), mixes.append(mixed), tots.append(tot)
        weights.append((win_all, wout_full))
        xl = _out_proj(mixed, wout_full, xl, l)

    dx, loss_parts = _loss_head(xl, target)
    loss = lax.psum(jnp.sum(loss_parts), AXES)

    dbias, dng, dqg, dkg = [None] * depth, [None] * depth, [None] * depth, [None] * depth
    res_in, res_out = None, None

    peer_slots = jnp.stack([_flat(_flip(_my_place(), k)) for k in SAME_CORE]).astype(jnp.int32)

    def finish_exchange(exchanging, after, l):
        sems, bufs, own = exchanging
        copies = N_DEV - 1 if own is None else 3
        bufs = _wait_copies(f"exchange_finish_l{l}", bufs, sems, 2, lambda refs, t: refs[t].at[0], copies, after)
        own = bufs[:2] if own is None else own
        rin, rout = bufs[2:]
        return (_adamw_layer(rin, own[0], me, w_in, m_w_in, v_w_in, res_in, l, "adamw_w_in"),
                _adamw_layer(rout, own[1], me, w_out, m_w_out, v_w_out, res_out, l, "adamw_w_out"))

    pending = []
    for l in reversed(range(depth)):
        win_all, wout_full = weights[l]
        dmixed = _out_proj_bwd_input(dx, wout_full, None, l)
        gwout = _out_proj_bwd_weight(mixes[l], dx, l).reshape(N_DEV, r_out, d)
        dproj = _sb_backward(projs[l], ys[l], dmixed, tots[l], l)
        dproj, dbias[l], dqg[l], dkg[l] = _chunk_backward(
            projs[l], bias, q_norm_g[l:l + 1], k_norm_g[l:l + 1], ys[l], dmixed, dproj, l)
        grads_l = (_proj_bwd_weight(hs[l], dproj, l), gwout)
        if l > 0:
            sems, bufs, token = _exchange_direct(grads_l, [], f"exchange_direct_l{l}")
            pending.append(((sems, bufs, None), l))
        else:
            sems, bufs, token = _exchange_to_sibling(grads_l, [], "exchange_sibling_l0")
            bufs = _wait_copies("exchange_sibling_wait_l0", bufs, sems, 2, lambda refs, t: refs[t].at[0], 4, [token])
            own, parts, lands = bufs[0:2], bufs[2:4], bufs[4:6]
            csums = [_chip_sums(own[t], parts[t], peer_slots, f"chip_sums_{t}_l0") for t in range(2)]
            sems, csums_lands, token = _exchange_to_chips(csums, lands, [], "exchange_chips_l0")
            pending.append(((sems, csums_lands, own), l))
        dx, dng[l] = _proj_bwd_input(dproj, win_all, xs[l], norm_g[l:l + 1], dx, token, l)
    tie = token
    for exchanging, l in pending[:-1]:
        res_in, res_out = finish_exchange(exchanging, [dx, tie], l)
    drel = _bias_grad(jnp.concatenate(dbias, axis=0), [tie])
    small_like = [norm_g, q_norm_g, k_norm_g, drel]
    mine = _pack_rows([jnp.concatenate(dng, axis=0), jnp.concatenate(dqg, axis=0),
                       jnp.concatenate(dkg, axis=0), drel])
    gathered = _gather_small(mine, [res_in[0], res_out[0]], "gather_small_grads")
    g_norm, g_qn, g_kn, g_rel_full = _unpack_rows(_sum_slots(gathered), small_like)
    my_block = _flat(_my_place())
    g_rel = lax.dynamic_slice_in_dim(g_rel_full.reshape(depth, heads, N_REL), my_block * rel_w, rel_w, axis=2)
    small_w = [norm_g, q_norm_g, k_norm_g, rel_bias]
    small = _adamw_small(_pack_rows(small_w), _pack_rows([g_norm, g_qn, g_kn, g_rel]),
                         _pack_rows([m_norm_g, m_q_norm_g, m_k_norm_g, m_rel_bias]),
                         _pack_rows([v_norm_g, v_q_norm_g, v_k_norm_g, v_rel_bias]))
    d_small, nm_small, nv_small = (_unpack_rows(p, small_w) for p in small)

    res_in, res_out = finish_exchange(pending[-1][0], [small[0], res_in[0], res_out[0]], 0)
    g_win, d_win, nm_win, nv_win = res_in
    g_wout, d_wout, nm_wout, nv_wout = res_out
    grads = (g_norm, g_win, g_qn, g_kn, g_rel, g_wout)

    def order(sm, big_in, big_out):
        return (sm[0], big_in, sm[1], sm[2], sm[3], big_out)

    return (loss, dx[None], *grads, *order(d_small, d_win, d_wout),
            *order(nm_small, nm_win, nm_wout), *order(nv_small, nv_win, nv_wout))
```

```python
import functools

import jax
import jax.numpy as jnp
from jax import lax
from jax.experimental import pallas as pl
from jax.experimental.pallas import tpu as pltpu

F32 = jnp.float32
BF16 = jnp.bfloat16
MESH_ID = pl.DeviceIdType.MESH
AXES = ("x", "y", "c")

N_DEV = 8
HEAD_DIM = 128
CHUNK = 64
LEFT_CHUNKS = 8
BAND_W = (LEFT_CHUNKS + 1) * CHUNK
PAD_K = LEFT_CHUNKS * CHUNK
REL_CLIP = 256
N_REL = REL_CLIP + CHUNK
NORM_EPS = 1e-6
NEG_BIG = -1e30
GROUP_CHUNKS = 4
PAIR = GROUP_CHUNKS * CHUNK
PAIR_W = BAND_W + PAIR - CHUNK
CHUNK_HEADS = 2
CHUNK_HEADS_BWD = 1
PAIR_UNROLL_BWD = 4
PAIR_UNROLL = 2
SB_Q = 512
SB_K = 128
SB_HEADS = 2
SB_UNROLL = 4
SB_UNROLL_BWD = 4
LANES = 128

ADAM_LR = 0.001
ADAM_B1 = 0.9
ADAM_B2 = 0.999
ADAM_EPS = 1e-08
ADAM_WD = 0.01
ADAM_STEP = 10

VMEM_LIMIT_BYTES = 56 * 1024 * 1024

NT = (((1,), (1,)), ((), ()))
TN = (((0,), (0,)), ((), ()))


def _params(*sem, **kw):
    return pltpu.CompilerParams(dimension_semantics=sem or None, vmem_limit_bytes=VMEM_LIMIT_BYTES, **kw)


def _any():
    return pl.BlockSpec(memory_space=pl.ANY)


def _vmem():
    return pl.BlockSpec(memory_space=pltpu.VMEM)


def _tile(n, want):
    return want if n % want == 0 else n


def _dot(a, b, dims=None):
    if dims is None:
        return jnp.dot(a, b, preferred_element_type=F32)
    return lax.dot_general(a, b, dims, preferred_element_type=F32)


def _split_dot(a, b, parts, dims=None):
    acc = None
    rest = a
    for _ in range(parts):
        piece = rest.astype(BF16)
        rest = rest - piece.astype(F32)
        term = _dot(piece, b, dims)
        acc = term if acc is None else acc + term
    return acc


def _split_dot_deep(a, b_twice):
    high = a.astype(BF16)
    low = (a - high.astype(F32)).astype(BF16)
    return _dot(jnp.concatenate([high, low], axis=1), b_twice)


def _log_sigmoid(z):
    return jnp.minimum(z, 0.0) - jnp.log(1.0 + jnp.exp(-jnp.abs(z)))


def _silu_and_grad(g):
    sig = jax.nn.sigmoid(g)
    return g * sig, sig * (1.0 + g * (1.0 - sig))


def _my_place():
    return lax.axis_index("x"), lax.axis_index("y"), lax.axis_index("c")


def _flat(place):
    return 4 * place[0] + 2 * place[1] + place[2]


def _flip(place, k):
    return tuple(1 - p if (k >> s) & 1 else p for p, s in zip(place, (2, 1, 0)))


def _cast_layer(w, me, layer, name):
    _, rows, cols = w.shape
    tr = _tile(rows, 1024)

    def body(me_ref, a_ref, shard_ref, land_ref):
        del me_ref
        shard_ref[...] = a_ref[...].astype(BF16)
        land_ref[...] = shard_ref[...]

    return pl.pallas_call(
        body, name=f"{name}_l{layer}",
        grid_spec=pltpu.PrefetchScalarGridSpec(
            num_scalar_prefetch=1, grid=(rows // tr,),
            in_specs=[pl.BlockSpec((None, tr, cols), lambda i, me_ref: (layer, i, 0))],
            out_specs=(pl.BlockSpec((tr, cols), lambda i, me_ref: (i, 0)),
                       pl.BlockSpec((None, tr, cols), lambda i, me_ref: (me_ref[0], i, 0)))),
        out_shape=(jax.ShapeDtypeStruct((rows, cols), BF16), jax.ShapeDtypeStruct((N_DEV, rows, cols), BF16)),
        compiler_params=_params("parallel"),
    )(me, w)


HBM_SPEC = pl.BlockSpec(memory_space=pltpu.HBM)
SEM_SPEC = pl.BlockSpec(memory_space=pltpu.SEMAPHORE)
SAME_CORE = (2, 4, 6)
SIBLING = 1
SPLIT_EFFECT = pltpu.SideEffectType.DATAFLOW_SIDE_EFFECTING


def _split_call(body, name, bufs, sems_in, sem_counts_out, after, token):
    bufs, sems_in, after = list(bufs), list(sems_in), list(after)
    nb, ni, no = len(bufs), len(sems_in), len(sem_counts_out)

    def wrapped(*refs):
        outs = nb + ni + len(after)
        body(refs[:nb], refs[nb:nb + ni], refs[outs:outs + no])
        if token:
            refs[-1][...] = jnp.zeros_like(refs[-1])

    out = pl.pallas_call(
        wrapped, name=name,
        in_specs=[HBM_SPEC] * nb + [SEM_SPEC] * ni + [_any()] * len(after),
        out_specs=tuple([SEM_SPEC] * no + [HBM_SPEC] * nb + ([_vmem()] if token else [])),
        out_shape=tuple([pltpu.SemaphoreType.DMA((c,)) for c in sem_counts_out]
                        + [pltpu.HBM(a.shape, a.dtype) for a in bufs]
                        + ([jax.ShapeDtypeStruct((8, LANES), F32)] if token else [])),
        input_output_aliases={i: no + i for i in range(nb)},
        compiler_params=pltpu.CompilerParams(has_side_effects=SPLIT_EFFECT),
    )(*[pltpu.with_memory_space_constraint(a, pltpu.HBM) for a in bufs], *sems_in, *after)
    return list(out[:no]), list(out[no:no + nb]), (out[-1] if token else None)


def _remote(src, dst, send_sems, recv_sems, i, to):
    return pltpu.make_async_remote_copy(src_ref=src, dst_ref=dst, send_sem=send_sems.at[i], recv_sem=recv_sems.at[i],
                                        device_id=to, device_id_type=MESH_ID)


def _wait_copies(name, bufs, sems, n, slot_of, count, after):
    def body(refs, sems_in, _):
        me = _my_place()
        for t in range(n):
            slot = slot_of(refs, t)
            for a in range(count):
                cp = _remote(slot, slot, sems_in[0], sems_in[1], t * count + a, me)
                cp.wait_send()
                cp.wait_recv()

    return _split_call(body, name, bufs, sems, (), after, False)[1]


def _exchange_direct(grads, after, name):
    n = len(grads)
    lands = [lax.empty(g.shape, g.dtype) for g in grads]

    def body(refs, _, sems_out):
        me = _my_place()
        for k in range(1, N_DEV):
            peer = _flip(me, k)
            for t in range(n):
                _remote(refs[t].at[_flat(peer)], refs[n + t].at[_flat(me)], *sems_out, t * (N_DEV - 1) + k - 1,
                        peer).start()

    return _split_call(body, name, list(grads) + lands, (), (n * (N_DEV - 1), n * (N_DEV - 1)), after, True)


def _exchange_to_sibling(grads, after, name):
    n = len(grads)
    parts = [lax.empty((3,) + g.shape[1:], g.dtype) for g in grads]
    lands = [lax.empty((4,) + g.shape[1:], g.dtype) for g in grads]

    def body(refs, _, sems_out):
        me = _my_place()
        sib = _flip(me, SIBLING)
        for t in range(n):
            g, part, land = refs[t], refs[n + t], refs[2 * n + t]
            _remote(g.at[_flat(sib)], land.at[0], *sems_out, 4 * t, sib).start()
            for a, k in enumerate(SAME_CORE):
                _remote(g.at[_flat(_flip(sib, k))], part.at[a], *sems_out, 4 * t + 1 + a, sib).start()

    return _split_call(body, name, list(grads) + parts + lands, (), (4 * n, 4 * n), after, True)


def _chip_sums(grads, parts, slots, name):
    _, rows, cols = grads.shape
    tr = _tile(rows, max(8, (512 * 1024) // cols))

    def body(slots_ref, g_ref, p_ref, o_ref):
        del slots_ref
        o_ref[...] = (g_ref[...].astype(F32) + p_ref[...].astype(F32)).astype(BF16)

    return pl.pallas_call(
        body, name=name,
        grid_spec=pltpu.PrefetchScalarGridSpec(
            num_scalar_prefetch=1, grid=(3, rows // tr),
            in_specs=[pl.BlockSpec((None, tr, cols), lambda a, i, slots_ref: (slots_ref[a], i, 0)),
                      pl.BlockSpec((None, tr, cols), lambda a, i, slots_ref: (a, i, 0))],
            out_specs=pl.BlockSpec((None, tr, cols), lambda a, i, slots_ref: (a, i, 0))),
        out_shape=jax.ShapeDtypeStruct((3, rows, cols), BF16),
        compiler_params=_params("parallel", "parallel"),
    )(slots, grads, parts)


def _exchange_to_chips(csums, lands, after, name):
    n = len(csums)

    def body(refs, _, sems_out):
        me = _my_place()
        for t in range(n):
            for a, k in enumerate(SAME_CORE):
                _remote(refs[t].at[a], refs[n + t].at[1 + a], *sems_out, 3 * t + a, _flip(me, k)).start()

    return _split_call(body, name, list(csums) + list(lands), (), (3 * n, 3 * n), after, True)


def _hbm_call(body, name, n_hbm, sems_in, sems_out, after, token, like):
    after = list(after)
    in_specs = [HBM_SPEC] * n_hbm + [SEM_SPEC] * len(sems_in) + [_any()] * len(after)
    out_specs = [SEM_SPEC] * len(sems_out) + [HBM_SPEC] * n_hbm + ([_vmem()] if token else [])
    out_shape = ([pltpu.SemaphoreType.DMA((c,)) for c in sems_out] + [pltpu.HBM(a.shape, a.dtype) for a in like]
                 + ([jax.ShapeDtypeStruct((8, LANES), F32)] if token else []))
    return in_specs, tuple(out_specs), tuple(out_shape), {i: len(sems_out) + i for i in range(n_hbm)}, after


def _gather_send(shards, lands, after, name):
    n = len(shards)
    peers = (SIBLING,) + SAME_CORE
    after = list(after)

    def body(*refs):
        me = _my_place()
        send_sems, recv_sems = refs[2 * n + len(after)], refs[2 * n + len(after) + 1]
        for a, k in enumerate(peers):
            for t in range(n):
                pltpu.make_async_remote_copy(
                    src_ref=refs[t], dst_ref=refs[n + t].at[_flat(me)],
                    send_sem=send_sems.at[t * 4 + a], recv_sem=recv_sems.at[t * 4 + a],
                    device_id=_flip(me, k), device_id_type=MESH_ID).start()
        refs[-1][...] = jnp.zeros_like(refs[-1])

    bufs = list(shards) + list(lands)
    in_specs, out_specs, out_shape, aliases, after = _hbm_call(body, name, 2 * n, (), (4 * n, 4 * n), after, True, bufs)
    out = pl.pallas_call(
        body, name=name, in_specs=in_specs, out_specs=out_specs, out_shape=out_shape,
        input_output_aliases=aliases, compiler_params=pltpu.CompilerParams(has_side_effects=SPLIT_EFFECT),
    )(*[pltpu.with_memory_space_constraint(a, pltpu.HBM) for a in bufs], *after)
    return out[0], out[1], out[2:2 + n], out[2 + n:2 + 2 * n], out[-1]


def _gather_forward(sent, after, name):
    send1, recv1, shards, lands, _ = sent
    n = len(shards)
    after = list(after)

    def body(*refs):
        me = _my_place()
        recv1_ref = refs[2 * n + 1]
        out0 = 2 * n + 2 + len(after)
        send2_ref, recv2_ref = refs[out0], refs[out0 + 1]
        for a, k in enumerate(SAME_CORE):
            owner = _flat(_flip(me, k))
            for t in range(n):
                slot = refs[n + t].at[owner]
                pltpu.make_async_remote_copy(
                    src_ref=refs[t], dst_ref=slot, send_sem=refs[2 * n].at[t * 4 + 1 + a],
                    recv_sem=recv1_ref.at[t * 4 + 1 + a], device_id=_flip(me, k), device_id_type=MESH_ID).wait_recv()
                pltpu.make_async_remote_copy(
                    src_ref=slot, dst_ref=slot, send_sem=send2_ref.at[t * 3 + a], recv_sem=recv2_ref.at[t * 3 + a],
                    device_id=_flip(me, SIBLING), device_id_type=MESH_ID).start()
        refs[-1][...] = jnp.zeros_like(refs[-1])

    bufs = list(shards) + list(lands)
    in_specs, out_specs, out_shape, aliases, after = _hbm_call(body, name, 2 * n, (4 * n, 4 * n), (3 * n, 3 * n), after,
                                                               True, bufs)
    out = pl.pallas_call(
        body, name=name, in_specs=in_specs, out_specs=out_specs, out_shape=out_shape,
        input_output_aliases=aliases, compiler_params=pltpu.CompilerParams(has_side_effects=SPLIT_EFFECT),
    )(*bufs, send1, recv1, *after)
    return (send1, recv1), (out[0], out[1]), out[2:2 + n], out[2 + n:2 + 2 * n], out[-1]


def _gather_finish(forwarded, after, name):
    (send1, recv1), (send2, recv2), shards, lands, _ = forwarded
    n = len(shards)
    after = list(after)

    def body(*refs):
        me = _my_place()
        send1_ref, recv1_ref, send2_ref, recv2_ref = refs[2 * n:2 * n + 4]
        sib = _flip(me, SIBLING)
        for t in range(n):
            for a in range(4):
                cp = pltpu.make_async_remote_copy(
                    src_ref=refs[t], dst_ref=refs[n + t].at[_flat(sib)], send_sem=send1_ref.at[t * 4 + a],
                    recv_sem=recv1_ref.at[t * 4 + a], device_id=sib, device_id_type=MESH_ID)
                cp.wait_send()
                if a == 0:
                    cp.wait_recv()
            for a in range(3):
                cp = pltpu.make_async_remote_copy(
                    src_ref=refs[t], dst_ref=refs[n + t].at[_flat(sib)], send_sem=send2_ref.at[t * 3 + a],
                    recv_sem=recv2_ref.at[t * 3 + a], device_id=sib, device_id_type=MESH_ID)
                cp.wait_send()
                cp.wait_recv()

    bufs = list(shards) + list(lands)
    in_specs, out_specs, out_shape, aliases, after = _hbm_call(body, name, 2 * n, (4 * n, 4 * n, 3 * n, 3 * n), (), after,
                                                               False, bufs)
    out = pl.pallas_call(
        body, name=name, in_specs=in_specs, out_specs=out_specs, out_shape=out_shape,
        input_output_aliases=aliases, compiler_params=pltpu.CompilerParams(has_side_effects=SPLIT_EFFECT),
    )(*bufs, send1, recv1, send2, recv2, *after)
    return out[n:]


def _gather_small(v, after, name):
    after = list(after)

    def body(v_ref, *rest):
        o_ref, send_sems, recv_sems = rest[-3:]
        me = _my_place()
        o_ref[_flat(me)] = v_ref[...]
        copies = []
        for k in range(1, N_DEV):
            copies.append(pltpu.make_async_remote_copy(
                src_ref=v_ref, dst_ref=o_ref.at[_flat(me)],
                send_sem=send_sems.at[k - 1], recv_sem=recv_sems.at[k - 1],
                device_id=_flip(me, k), device_id_type=MESH_ID))
        for cp in copies:
            cp.start()
        for cp in copies:
            cp.wait()

    return pl.pallas_call(
        body, name=name,
        in_specs=[_vmem()] + [_any()] * len(after), out_specs=_vmem(),
        out_shape=jax.ShapeDtypeStruct((N_DEV,) + v.shape, v.dtype),
        scratch_shapes=[pltpu.SemaphoreType.DMA((7,)), pltpu.SemaphoreType.DMA((7,))],
        compiler_params=_params(has_side_effects=True),
    )(v, *after)


def _rel_onehot(row):
    r_io = lax.broadcasted_iota(jnp.int32, (N_REL, PAIR_W), 0)
    p_io = lax.broadcasted_iota(jnp.int32, (N_REL, PAIR_W), 1)
    band_col = p_io - (row // CHUNK) * CHUNK
    in_band = jnp.logical_and(band_col >= 0, band_col < BAND_W)
    idx = jnp.clip(PAD_K + row % CHUNK - band_col, -(CHUNK - 1), REL_CLIP) + (CHUNK - 1)
    return jnp.logical_and(r_io == idx, in_band).astype(BF16), in_band[0:1]


def _bias_expand(rel):
    lh = rel.shape[0]

    def body(rel_ref, o_ref):
        onehot, in_band = _rel_onehot(pl.program_id(0))
        o_ref[...] = jnp.where(in_band, _split_dot(rel_ref[...], onehot, 3), NEG_BIG)

    return pl.pallas_call(
        body, name="bias_expand", grid=(PAIR,),
        in_specs=[pl.BlockSpec((lh, N_REL), lambda i: (0, 0))],
        out_specs=pl.BlockSpec((None, lh, PAIR_W), lambda i: (i, 0, 0)),
        out_shape=jax.ShapeDtypeStruct((PAIR, lh, PAIR_W), F32),
        compiler_params=_params("parallel"),
    )(rel)


SHEAR_W = PAIR_W + PAIR
BIAS_HEADS = 8


def _bias_grad(dbias, after):
    lh = dbias.shape[0]
    hb = _tile(lh, BIAS_HEADS)
    after = list(after)

    def body(db_ref, *rest):
        o_ref = rest[-1]
        a_io = lax.broadcasted_iota(jnp.int32, (PAIR, PAIR), 0)
        b_io = lax.broadcasted_iota(jnp.int32, (PAIR, PAIR), 1)
        flip_rows = (a_io + b_io == PAIR - 1).astype(BF16)
        diags = []
        for j in range(hb):
            rest_part = jnp.concatenate([db_ref[j], jnp.zeros((PAIR, SHEAR_W - PAIR_W), F32)], axis=1)
            flipped = None
            for _ in range(3):
                piece = rest_part.astype(BF16)
                rest_part = rest_part - piece.astype(F32)
                term = _dot(flip_rows, piece)
                flipped = term if flipped is None else flipped + term
            sheared = pltpu.roll(flipped, 0, 1, stride=1, stride_axis=0)
            diags.append(jnp.sum(sheared, axis=0, keepdims=True))
        c_io = lax.broadcasted_iota(jnp.int32, (SHEAR_W, N_REL), 0)
        r_io = lax.broadcasted_iota(jnp.int32, (SHEAR_W, N_REL), 1)
        entry = jnp.clip(PAD_K + (PAIR - 1) - c_io, -(CHUNK - 1), REL_CLIP) + (CHUNK - 1)
        o_ref[...] = _split_dot(jnp.concatenate(diags, axis=0), (r_io == entry).astype(BF16), 3)

    return pl.pallas_call(
        body, name="bias_grad", grid=(lh // hb,),
        in_specs=[pl.BlockSpec((hb, PAIR, PAIR_W), lambda i: (i, 0, 0))] + [_any()] * len(after),
        out_specs=pl.BlockSpec((hb, N_REL), lambda i: (i, 0)),
        out_shape=jax.ShapeDtypeStruct((lh, N_REL), F32),
        compiler_params=_params("parallel"),
    )(dbias, *after)


def _norm_proj(x, g, w_all, tie, layer):
    s, d = x.shape
    e = w_all.shape[2]
    tm, tn = _tile(s, 1024), _tile(e, 1024)
    nb = e // tn
    ties = [] if tie is None else [tie]

    def body(x_ref, g_ref, w_ref, *rest):
        proj_ref, h_ref = rest[-2:]

        @pl.when(pl.program_id(1) == 0)
        def _():
            xv = x_ref[...]
            r = lax.rsqrt(jnp.mean(xv * xv, axis=-1, keepdims=True) + NORM_EPS)
            h_ref[...] = ((xv * r) * g_ref[...]).astype(BF16)

        proj_ref[...] = _dot(h_ref[...], w_ref[...])

    return pl.pallas_call(
        body, name=f"norm_proj_l{layer}", grid=(s // tm, N_DEV * nb),
        in_specs=[pl.BlockSpec((tm, d), lambda m, n: (m, 0)),
                  pl.BlockSpec((1, d), lambda m, n: (0, 0)),
                  pl.BlockSpec((None, d, tn), lambda m, n: (n // nb, 0, n % nb))] + [_any()] * len(ties),
        out_specs=(pl.BlockSpec((None, tm, tn), lambda m, n: (n // nb, m, n % nb)),
                   pl.BlockSpec((tm, d), lambda m, n: (m, 0))),
        out_shape=(jax.ShapeDtypeStruct((N_DEV, s, e), F32), jax.ShapeDtypeStruct((s, d), BF16)),
        compiler_params=_params("parallel", "arbitrary"),
    )(x, g, w_all, *ties)


def _sb_forward(proj, layer):
    _, s, e = proj.shape
    hp = _tile(e // HEAD_DIM, SB_HEADS)
    width = hp * HEAD_DIM
    tq = _tile(s, SB_Q)
    diag_tiles = tq // SB_K
    scale = HEAD_DIM ** -0.5

    def body(p_ref, y_ref, mix_ref, tot_ref, kb_ref, vb_ref):
        kb_ref[...] = p_ref[1].astype(BF16)
        vb_ref[...] = p_ref[2].astype(BF16)
        row = lax.broadcasted_iota(jnp.int32, (tq, SB_K), 0)
        col = lax.broadcasted_iota(jnp.int32, (tq, SB_K), 1)
        kj = lax.broadcasted_iota(jnp.int32, (SB_K, SB_K), 0)
        ks = lax.broadcasted_iota(jnp.int32, (SB_K, SB_K), 1)
        later = (kj > ks).astype(BF16)

        def q_block(qi, _):
            t0 = pl.multiple_of(qi * tq, tq)
            qb = [p_ref[0, pl.ds(t0, tq), h * HEAD_DIM:(h + 1) * HEAD_DIM].astype(BF16) for h in range(hp)]

            def tile(s0, state, causal):
                out = []
                for h in range(hp):
                    carry, acc = state[h]
                    lanes = slice(h * HEAD_DIM, (h + 1) * HEAD_DIM)
                    z = _dot(qb[h], kb_ref[pl.ds(s0, SB_K), lanes], NT) * scale
                    ls = _log_sigmoid(z)
                    stay = ls - z
                    if causal is not None:
                        stay = jnp.where(causal, stay, 0.0)
                    w = jnp.exp(ls + carry + _split_dot(stay, later, 2))
                    if causal is not None:
                        w = jnp.where(causal, w, 0.0)
                    acc = acc + _dot(w.astype(BF16), vb_ref[pl.ds(s0, SB_K), lanes])
                    out.append((carry + jnp.sum(stay, axis=1, keepdims=True), acc))
                return tuple(out)

            state = tuple((jnp.zeros((tq, 1), F32), jnp.zeros((tq, HEAD_DIM), F32)) for _ in range(hp))
            for dt in reversed(range(diag_tiles)):
                state = tile(t0 + dt * SB_K, state, col + dt * SB_K < row)

            def k_blocks(j, st):
                for u in range(SB_UNROLL):
                    st = tile(pl.multiple_of((diag_tiles * qi - 1 - SB_UNROLL * j - u) * SB_K, SB_K), st, None)
                return st

            state = lax.fori_loop(0, diag_tiles * qi // SB_UNROLL, k_blocks, state)
            silu, _ = _silu_and_grad(p_ref[3, pl.ds(t0, tq), :])
            for h in range(hp):
                lanes = slice(h * HEAD_DIM, (h + 1) * HEAD_DIM)
                y_ref[pl.ds(t0, tq), lanes] = state[h][1]
                mix_ref[pl.ds(t0, tq), lanes] = (state[h][1] * silu[:, lanes]).astype(BF16)
                tot_ref[h, pl.ds(t0, tq), :] = state[h][0]
            return 0

        lax.fori_loop(0, s // tq, q_block, 0)

    return pl.pallas_call(
        body, name=f"sb_forward_l{layer}", grid=(e // width,),
        in_specs=[pl.BlockSpec((4, s, width), lambda h: (0, 0, h))],
        out_specs=(pl.BlockSpec((s, width), lambda h: (0, h)),
                   pl.BlockSpec((s, width), lambda h: (0, h)),
                   pl.BlockSpec((hp, s, 1), lambda h: (h, 0, 0))),
        out_shape=(jax.ShapeDtypeStruct((s, 2 * e), F32), jax.ShapeDtypeStruct((s, 2 * e), BF16),
                   jax.ShapeDtypeStruct((e // HEAD_DIM, s, 1), F32)),
        scratch_shapes=[pltpu.VMEM((s, width), BF16), pltpu.VMEM((s, width), BF16)],
        compiler_params=_params("parallel"),
    )(proj)


def _qk_norm(t, gain):
    r = lax.rsqrt(jnp.mean(t * t, axis=-1, keepdims=True) + NORM_EPS)
    return t * r, r, (t * r) * gain


def _chunk_scores(qc, kw, bias, t0, scale):
    sc = _dot(qc, kw, NT) * scale + bias
    col = lax.broadcasted_iota(jnp.int32, (PAIR, PAIR_W), 1)
    sc = jnp.where(col + t0 >= PAD_K, sc, NEG_BIG)
    ex = jnp.exp(sc - jnp.max(sc, axis=-1, keepdims=True))
    return ex / jnp.sum(ex, axis=-1, keepdims=True)


def _chunk_forward(proj, bias, q_gain, k_gain, y, mixed, tie, layer):
    ties = [] if tie is None else [tie]
    _, s, e = proj.shape
    hp = _tile(e // HEAD_DIM, CHUNK_HEADS)
    width = hp * HEAD_DIM
    steps = e // width
    unroll = _tile(s // PAIR, PAIR_UNROLL)
    scale = HEAD_DIM ** -0.5
    heads = [slice(h * HEAD_DIM, (h + 1) * HEAD_DIM) for h in range(hp)]

    def body(p_ref, b_ref, qg_ref, kg_ref, *rest):
        y_ref, mix_ref, qn_ref, kp_ref, vp_ref = rest[-5:]
        kp_ref[pl.ds(0, PAD_K), :] = jnp.zeros((PAD_K, width), BF16)
        vp_ref[pl.ds(0, PAD_K), :] = jnp.zeros((PAD_K, width), BF16)
        vp_ref[pl.ds(PAD_K, s), :] = p_ref[2].astype(BF16)
        for lanes in heads:
            qn_ref[:, lanes] = _qk_norm(p_ref[0, :, lanes], qg_ref[...])[2].astype(BF16)
            kp_ref[pl.ds(PAD_K, s), lanes] = _qk_norm(p_ref[1, :, lanes], kg_ref[...])[2].astype(BF16)

        def chunks(ci, _):
            done = []
            for u in range(unroll):
                t0 = pl.multiple_of((ci * unroll + u) * PAIR, PAIR)
                silu, _ = _silu_and_grad(p_ref[3, pl.ds(t0, PAIR), :])
                for h, lanes in enumerate(heads):
                    probs = _chunk_scores(qn_ref[pl.ds(t0, PAIR), lanes], kp_ref[pl.ds(t0, PAIR_W), lanes],
                                          b_ref[h], t0, scale)
                    out = _dot(probs.astype(BF16), vp_ref[pl.ds(t0, PAIR_W), lanes])
                    done.append((t0, lanes, out, (out * silu[:, lanes]).astype(BF16)))
            for t0, lanes, out, gated in done:
                y_ref[pl.ds(t0, PAIR), lanes] = out
                mix_ref[pl.ds(t0, PAIR), lanes] = gated
            return 0

        lax.fori_loop(0, s // (PAIR * unroll), chunks, 0)

    return pl.pallas_call(
        body, name=f"chunk_forward_l{layer}", grid=(steps,),
        in_specs=[pl.BlockSpec((4, s, width), lambda h: (1, 0, h)),
                  pl.BlockSpec((hp, PAIR, PAIR_W), lambda h: (layer * steps + h, 0, 0)),
                  pl.BlockSpec((1, HEAD_DIM), lambda h: (0, 0)),
                  pl.BlockSpec((1, HEAD_DIM), lambda h: (0, 0)),
                  _any(), _any()] + [_any()] * len(ties),
        out_specs=(pl.BlockSpec((s, width), lambda h: (0, steps + h)),
                   pl.BlockSpec((s, width), lambda h: (0, steps + h))),
        out_shape=(jax.ShapeDtypeStruct(y.shape, F32), jax.ShapeDtypeStruct(mixed.shape, BF16)),
        input_output_aliases={4: 0, 5: 1},
        scratch_shapes=[pltpu.VMEM((s, width), BF16), pltpu.VMEM((s + PAD_K, width), BF16),
                        pltpu.VMEM((s + PAD_K, width), BF16)],
        compiler_params=_params("parallel"),
    )(proj, bias, q_gain, k_gain, y, mixed, *ties)


def _out_proj(mixed, w, x, layer):
    s, d = x.shape
    tm, tn = _tile(s, 1024), _tile(d, 1024)

    def body(a_ref, w_ref, x_ref, o_ref):
        o_ref[...] = x_ref[...] + _dot(a_ref[...], w_ref[...])

    return pl.pallas_call(
        body, name=f"out_proj_l{layer}", grid=(s // tm, d // tn),
        in_specs=[pl.BlockSpec((tm, d), lambda m, n: (m, 0)),
                  pl.BlockSpec((d, tn), lambda m, n: (0, n)),
                  pl.BlockSpec((tm, tn), lambda m, n: (m, n))],
        out_specs=pl.BlockSpec((tm, tn), lambda m, n: (m, n)),
        out_shape=jax.ShapeDtypeStruct((s, d), F32),
        compiler_params=_params("parallel", "parallel"),
    )(mixed, w, x)


def _loss_head(y, target):
    s, d = y.shape
    tm = _tile(s, 256)

    def body(y_ref, t_ref, dy_ref, part_ref):
        diff = y_ref[...] - t_ref[...]
        dy_ref[...] = diff * (1.0 / d)
        sq = (diff * diff).reshape(tm // 8, 8, d).sum(axis=0)
        acc = sq[:, 0:LANES]
        for j in range(1, d // LANES):
            acc = acc + sq[:, j * LANES:(j + 1) * LANES]
        part_ref[...] = acc * (0.5 / d)

    return pl.pallas_call(
        body, name="loss_head", grid=(s // tm,),
        in_specs=[pl.BlockSpec((tm, d), lambda i: (i, 0)), pl.BlockSpec((tm, d), lambda i: (i, 0))],
        out_specs=(pl.BlockSpec((tm, d), lambda i: (i, 0)), pl.BlockSpec((None, 8, LANES), lambda i: (i, 0, 0))),
        out_shape=(jax.ShapeDtypeStruct((s, d), F32), jax.ShapeDtypeStruct((s // tm, 8, LANES), F32)),
        compiler_params=_params("parallel"),
    )(y, target)


def _out_proj_bwd_input(dx, w, tie, layer):
    s, d = dx.shape
    tm, tn = _tile(s, 1024), _tile(d, 1024)
    ties = [] if tie is None else [tie]

    def body(dx_ref, w_ref, *rest):
        rest[-1][...] = _dot(dx_ref[...].astype(BF16), w_ref[...], NT)

    return pl.pallas_call(
        body, name=f"out_proj_dx_l{layer}", grid=(s // tm, d // tn),
        in_specs=[pl.BlockSpec((tm, d), lambda m, n: (m, 0)),
                  pl.BlockSpec((tn, d), lambda m, n: (n, 0))] + [_any()] * len(ties),
        out_specs=pl.BlockSpec((tm, tn), lambda m, n: (m, n)),
        out_shape=jax.ShapeDtypeStruct((s, d), F32),
        compiler_params=_params("parallel", "parallel"),
    )(dx, w, *ties)


def _out_proj_bwd_weight(mixed, dx, layer):
    s, d = dx.shape
    te, tn = _tile(d, 1024), _tile(d, 1024)

    def body(a_ref, dx_ref, o_ref):
        o_ref[...] = _dot(a_ref[...], dx_ref[...].astype(BF16), TN).astype(BF16)

    return pl.pallas_call(
        body, name=f"out_proj_dw_l{layer}", grid=(d // te, d // tn),
        in_specs=[pl.BlockSpec((s, te), lambda i, n: (0, i)), pl.BlockSpec((s, tn), lambda i, n: (0, n))],
        out_specs=pl.BlockSpec((te, tn), lambda i, n: (i, n)),
        out_shape=jax.ShapeDtypeStruct((d, d), BF16),
        compiler_params=_params("parallel", "parallel"),
    )(mixed, dx)


def _sb_backward(proj, y, dmixed, tot, layer):
    _, s, e = proj.shape
    hp = _tile(e // HEAD_DIM, SB_HEADS)
    width = hp * HEAD_DIM
    tq = _tile(s, SB_Q)
    diag_tiles = tq // SB_K
    scale = HEAD_DIM ** -0.5

    def body(p_ref, y_ref, dm_ref, tot_ref, o_ref, kb_ref, vb_ref, do_ref, dk_ref, dv_ref):
        kb_ref[...] = p_ref[1].astype(BF16)
        vb_ref[...] = p_ref[2].astype(BF16)
        silu, dsilu = _silu_and_grad(p_ref[3])
        dm = dm_ref[...]
        do_ref[...] = (dm * silu).astype(BF16)
        o_ref[3] = (dm * y_ref[...] * dsilu).astype(BF16)
        dk_ref[...] = jnp.zeros_like(dk_ref)
        dv_ref[...] = jnp.zeros_like(dv_ref)
        row = lax.broadcasted_iota(jnp.int32, (tq, SB_K), 0)
        col = lax.broadcasted_iota(jnp.int32, (tq, SB_K), 1)
        kj = lax.broadcasted_iota(jnp.int32, (SB_K, SB_K), 0)
        ks = lax.broadcasted_iota(jnp.int32, (SB_K, SB_K), 1)
        upto = (kj <= ks).astype(BF16)
        before = (kj < ks).astype(BF16)
        upto = jnp.concatenate([upto, upto], axis=0)
        before = jnp.concatenate([before, before], axis=0)

        def q_block(qi, _):
            t0 = pl.multiple_of(qi * tq, tq)
            heads = [slice(h * HEAD_DIM, (h + 1) * HEAD_DIM) for h in range(hp)]
            qb = [p_ref[0, pl.ds(t0, tq), lanes].astype(BF16) for lanes in heads]
            dob = [do_ref[pl.ds(t0, tq), lanes] for lanes in heads]
            total = [tot_ref[h, pl.ds(t0, tq), :] for h in range(hp)]
            q_t = [p_ref[0, pl.ds(t0, tq), lanes].T.astype(BF16) for lanes in heads]
            do_t = [do_ref[pl.ds(t0, tq), lanes].astype(F32).T.astype(BF16) for lanes in heads]

            def tile(s0, state, causal):
                out, adds = [], []
                for h, lanes in enumerate(heads):
                    stay_sum, dlw_sum, dq = state[h]
                    kt = kb_ref[pl.ds(s0, SB_K), lanes]
                    vt = vb_ref[pl.ds(s0, SB_K), lanes]
                    z = _dot(qb[h], kt, NT) * scale
                    ls = _log_sigmoid(z)
                    stay = ls - z
                    if causal is not None:
                        stay = jnp.where(causal, stay, 0.0)
                    after = total[h] - (stay_sum + _split_dot_deep(stay, upto))
                    w = jnp.exp(ls + after)
                    if causal is not None:
                        w = jnp.where(causal, w, 0.0)
                    dlw = _dot(dob[h], vt, NT) * w
                    prior = dlw_sum + _split_dot_deep(dlw, before)
                    sig = jnp.exp(ls)
                    dz = (dlw * (1.0 - sig) - sig * prior) * scale
                    if causal is not None:
                        dz = jnp.where(causal, dz, 0.0)
                    dzb = dz.astype(BF16)
                    dq = dq + _dot(dzb, kt)
                    adds.append((lanes, _dot(q_t[h], dzb), _dot(do_t[h], w.astype(BF16))))
                    out.append((stay_sum + jnp.sum(stay, axis=1, keepdims=True),
                                dlw_sum + jnp.sum(dlw, axis=1, keepdims=True), dq))
                for lanes, dk_t, dv_t in adds:
                    dk_ref[s0 // SB_K, lanes, :] += dk_t
                    dv_ref[s0 // SB_K, lanes, :] += dv_t
                return tuple(out)

            def k_blocks(j, st):
                for u in range(SB_UNROLL_BWD):
                    st = tile(pl.multiple_of((SB_UNROLL_BWD * j + u) * SB_K, SB_K), st, None)
                return st

            zero = jnp.zeros((tq, 1), F32)
            state = tuple((zero, zero, jnp.zeros((tq, HEAD_DIM), F32)) for _ in range(hp))
            state = lax.fori_loop(0, diag_tiles * qi // SB_UNROLL_BWD, k_blocks, state)
            for dt in range(diag_tiles):
                state = tile(t0 + dt * SB_K, state, col + dt * SB_K < row)
            for h, lanes in enumerate(heads):
                o_ref[0, pl.ds(t0, tq), lanes] = state[h][2].astype(BF16)
            return 0

        lax.fori_loop(0, s // tq, q_block, 0)
        for j in range(s // SB_K):
            rows = slice(j * SB_K, (j + 1) * SB_K)
            for h in range(hp):
                lanes = slice(h * HEAD_DIM, (h + 1) * HEAD_DIM)
                o_ref[1, rows, lanes] = dk_ref[j, lanes, :].T.astype(BF16)
                o_ref[2, rows, lanes] = dv_ref[j, lanes, :].T.astype(BF16)

    return pl.pallas_call(
        body, name=f"sb_backward_l{layer}", grid=(e // width,),
        in_specs=[pl.BlockSpec((4, s, width), lambda h: (0, 0, h)),
                  pl.BlockSpec((s, width), lambda h: (0, h)),
                  pl.BlockSpec((s, width), lambda h: (0, h)),
                  pl.BlockSpec((hp, s, 1), lambda h: (h, 0, 0))],
        out_specs=pl.BlockSpec((4, s, width), lambda h: (0, 0, h)),
        out_shape=jax.ShapeDtypeStruct((N_DEV, s, e), BF16),
        scratch_shapes=[pltpu.VMEM((s, width), BF16), pltpu.VMEM((s, width), BF16),
                        pltpu.VMEM((s, width), BF16), pltpu.VMEM((s // SB_K, width, SB_K), F32),
                        pltpu.VMEM((s // SB_K, width, SB_K), F32)],
        compiler_params=_params("parallel"),
    )(proj, y, dmixed, tot)


def _norm_bwd(dn, xh, r, gain):
    dxh = dn * gain
    return r * (dxh - xh * jnp.mean(dxh * xh, axis=-1, keepdims=True)), dn * xh


def _chunk_backward(proj, bias, q_gain, k_gain, y, dmixed, dproj, layer):
    _, s, e = proj.shape
    hp = CHUNK_HEADS_BWD
    width = hp * HEAD_DIM
    steps = e // width
    unroll = _tile(s // PAIR, PAIR_UNROLL_BWD)
    scale = HEAD_DIM ** -0.5
    heads = [slice(h * HEAD_DIM, (h + 1) * HEAD_DIM) for h in range(hp)]

    def body(p_ref, b_ref, qg_ref, kg_ref, y_ref, dm_ref, dp_in, o_ref, db_ref, dqg_ref, dkg_ref,
             qn_ref, kp_ref, vp_ref, do_ref, dqn_ref, dkn_ref, dvp_ref):
        del dp_in
        kp_ref[pl.ds(0, PAD_K), :] = jnp.zeros((PAD_K, width), BF16)
        vp_ref[pl.ds(0, PAD_K), :] = jnp.zeros((PAD_K, width), BF16)
        vp_ref[pl.ds(PAD_K, s), :] = p_ref[2].astype(BF16)
        for lanes in heads:
            qn_ref[:, lanes] = _qk_norm(p_ref[0, :, lanes], qg_ref[...])[2].astype(BF16)
            kp_ref[pl.ds(PAD_K, s), lanes] = _qk_norm(p_ref[1, :, lanes], kg_ref[...])[2].astype(BF16)
        silu, dsilu = _silu_and_grad(p_ref[3])
        dm = dm_ref[...]
        do_ref[...] = (dm * silu).astype(BF16)
        o_ref[3] = (dm * y_ref[...] * dsilu).astype(BF16)
        dkn_ref[...] = jnp.zeros_like(dkn_ref)
        dvp_ref[...] = jnp.zeros_like(dvp_ref)
        db_ref[...] = jnp.zeros_like(db_ref)

        def chunks(ci, _):
            done = []
            for u in range(unroll):
                t0 = pl.multiple_of((ci * unroll + u) * PAIR, PAIR)
                for h, lanes in enumerate(heads):
                    qc = qn_ref[pl.ds(t0, PAIR), lanes]
                    kw = kp_ref[pl.ds(t0, PAIR_W), lanes]
                    vw = vp_ref[pl.ds(t0, PAIR_W), lanes]
                    dob = do_ref[pl.ds(t0, PAIR), lanes]
                    probs = _chunk_scores(qc, kw, b_ref[h], t0, scale)
                    dprobs = _dot(dob, vw, NT)
                    dsc = probs * (dprobs - jnp.sum(probs * dprobs, axis=-1, keepdims=True))
                    dsb = (dsc * scale).astype(BF16)
                    done.append((t0, h, lanes, dsc, _dot(dsb, kw), _dot(dsb, qc, TN),
                                 _dot(probs.astype(BF16), dob, TN)))
            for t0, h, lanes, dsc, dqn, dkn, dvp in done:
                db_ref[h] += dsc
                dqn_ref[pl.ds(t0, PAIR), lanes] = dqn
                dkn_ref[pl.ds(t0, PAIR_W), lanes] += dkn
                dvp_ref[pl.ds(t0, PAIR_W), lanes] += dvp
            return 0

        lax.fori_loop(0, s // (PAIR * unroll), chunks, 0)
        o_ref[2] = dvp_ref[pl.ds(PAD_K, s), :].astype(BF16)

        @pl.when(pl.program_id(0) == 0)
        def _():
            dqg_ref[...] = jnp.zeros_like(dqg_ref)
            dkg_ref[...] = jnp.zeros_like(dkg_ref)

        for lanes in heads:
            qh, rq, _ = _qk_norm(p_ref[0, :, lanes], qg_ref[...])
            dq, dqg_rows = _norm_bwd(dqn_ref[:, lanes], qh, rq, qg_ref[...])
            o_ref[0, :, lanes] = dq.astype(BF16)
            dqg_ref[...] += jnp.sum(dqg_rows, axis=0, keepdims=True)
            kh, rk, _ = _qk_norm(p_ref[1, :, lanes], kg_ref[...])
            dk, dkg_rows = _norm_bwd(dkn_ref[pl.ds(PAD_K, s), lanes], kh, rk, kg_ref[...])
            o_ref[1, :, lanes] = dk.astype(BF16)
            dkg_ref[...] += jnp.sum(dkg_rows, axis=0, keepdims=True)

    return pl.pallas_call(
        body, name=f"chunk_backward_l{layer}", grid=(steps,),
        in_specs=[pl.BlockSpec((4, s, width), lambda h: (1, 0, h)),
                  pl.BlockSpec((hp, PAIR, PAIR_W), lambda h: (layer * steps + h, 0, 0)),
                  pl.BlockSpec((1, HEAD_DIM), lambda h: (0, 0)),
                  pl.BlockSpec((1, HEAD_DIM), lambda h: (0, 0)),
                  pl.BlockSpec((s, width), lambda h: (0, steps + h)),
                  pl.BlockSpec((s, width), lambda h: (0, steps + h)),
                  _any()],
        out_specs=(pl.BlockSpec((4, s, width), lambda h: (1, 0, h)),
                   pl.BlockSpec((hp, PAIR, PAIR_W), lambda h: (h, 0, 0)),
                   pl.BlockSpec((1, HEAD_DIM), lambda h: (0, 0)),
                   pl.BlockSpec((1, HEAD_DIM), lambda h: (0, 0))),
        out_shape=(jax.ShapeDtypeStruct(dproj.shape, BF16),
                   jax.ShapeDtypeStruct((e // HEAD_DIM, PAIR, PAIR_W), F32),
                   jax.ShapeDtypeStruct((1, HEAD_DIM), F32), jax.ShapeDtypeStruct((1, HEAD_DIM), F32)),
        input_output_aliases={6: 0},
        scratch_shapes=[pltpu.VMEM((s, width), BF16), pltpu.VMEM((s + PAD_K, width), BF16),
                        pltpu.VMEM((s + PAD_K, width), BF16), pltpu.VMEM((s, width), BF16),
                        pltpu.VMEM((s, width), F32), pltpu.VMEM((s + PAD_K, width), F32),
                        pltpu.VMEM((s + PAD_K, width), F32)],
        compiler_params=_params("arbitrary"),
    )(proj, bias, q_gain, k_gain, y, dmixed, dproj)


def _proj_bwd_input(dproj, w_all, x, g, dx, tie, layer):
    s, d = x.shape
    e = w_all.shape[2]
    tm = _tile(s, 512)
    gs = 1
    steps = N_DEV // gs

    def body(dp_ref, w_ref, x_ref, g_ref, dx_ref, tie_ref, o_ref, dg_ref, acc_ref):
        del tie_ref
        j = pl.program_id(1)

        @pl.when(j == 0)
        def _():
            acc_ref[...] = jnp.zeros_like(acc_ref)

        part = _dot(dp_ref[0], w_ref[0], NT)
        for k in range(1, gs):
            part = part + _dot(dp_ref[k], w_ref[k], NT)
        acc_ref[...] += part

        @pl.when(jnp.logical_and(j == steps - 1, pl.program_id(0) == 0))
        def _():
            dg_ref[...] = jnp.zeros_like(dg_ref)

        @pl.when(j == steps - 1)
        def _():
            xv = x_ref[...]
            r = lax.rsqrt(jnp.mean(xv * xv, axis=-1, keepdims=True) + NORM_EPS)
            dxn, dg_rows = _norm_bwd(acc_ref[...], xv * r, r, g_ref[...])
            o_ref[...] = dx_ref[...] + dxn
            dg_ref[...] += jnp.sum(dg_rows, axis=0, keepdims=True)

    return pl.pallas_call(
        body, name=f"proj_dx_l{layer}", grid=(s // tm, steps),
        in_specs=[pl.BlockSpec((gs, tm, e), lambda m, j: (j, m, 0)),
                  pl.BlockSpec((gs, d, e), lambda m, j: (j, 0, 0)),
                  pl.BlockSpec((tm, d), lambda m, j: (m, 0)),
                  pl.BlockSpec((1, d), lambda m, j: (0, 0)),
                  pl.BlockSpec((tm, d), lambda m, j: (m, 0)), _any()],
        out_specs=(pl.BlockSpec((tm, d), lambda m, j: (m, 0)), pl.BlockSpec((1, d), lambda m, j: (0, 0))),
        out_shape=(jax.ShapeDtypeStruct((s, d), F32), jax.ShapeDtypeStruct((1, d), F32)),
        scratch_shapes=[pltpu.VMEM((tm, d), F32)],
        compiler_params=_params("arbitrary", "arbitrary"),
    )(dproj, w_all, x, g, dx, tie)


def _proj_bwd_weight(h, dproj, layer):
    s, d = h.shape
    e = dproj.shape[2]
    td, tn = _tile(d, 1024), _tile(e, 1024)
    nb = e // tn

    def body(h_ref, dp_ref, o_ref):
        o_ref[...] = _dot(h_ref[...], dp_ref[...], TN).astype(BF16)

    return pl.pallas_call(
        body, name=f"proj_dw_l{layer}", grid=(d // td, N_DEV * nb),
        in_specs=[pl.BlockSpec((s, td), lambda i, n: (0, i)),
                  pl.BlockSpec((None, s, tn), lambda i, n: (n // nb, 0, n % nb))],
        out_specs=pl.BlockSpec((None, td, tn), lambda i, n: (n // nb, i, n % nb)),
        out_shape=jax.ShapeDtypeStruct((N_DEV, d, e), BF16),
        compiler_params=_params("parallel", "parallel"),
    )(h, dproj)


def _adamw_math(w, g, m, v):
    m = ADAM_B1 * m + (1.0 - ADAM_B1) * g
    v = ADAM_B2 * v + (1.0 - ADAM_B2) * (g * g)
    m_hat = m / (1.0 - ADAM_B1 ** ADAM_STEP)
    v_hat = v / (1.0 - ADAM_B2 ** ADAM_STEP)
    return -ADAM_LR * (m_hat / (jnp.sqrt(v_hat) + ADAM_EPS) + ADAM_WD * w), m, v


def _adamw_layer(parts, own, me, w, m, v, prev, layer, name):
    n_layers, rows, cols = w.shape
    n_parts = parts.shape[0]
    tr = _tile(rows, max(8, (256 * 1024) // cols))

    def body(me_ref, p_ref, own_ref, w_ref, m_ref, v_ref, *rest):
        g_ref, d_ref, nm_ref, nv_ref = rest[-4:]
        mine = own_ref[...].astype(F32)
        if n_parts == N_DEV:
            g = None
            for j in range(N_DEV):
                term = jnp.where(me_ref[0] == j, mine, p_ref[j].astype(F32))
                g = term if g is None else g + term
        else:
            g = mine
            for j in range(n_parts):
                g = g + p_ref[j].astype(F32)
        g_ref[...] = g
        d_ref[...], nm_ref[...], nv_ref[...] = _adamw_math(w_ref[...], g, m_ref[...], v_ref[...])

    blk = pl.BlockSpec((None, tr, cols), lambda i, me_ref: (layer, i, 0))
    out_shape = tuple(jax.ShapeDtypeStruct(w.shape, F32) for _ in range(4))
    in_specs = [pl.BlockSpec((n_parts, tr, cols), lambda i, me_ref: (0, i, 0)),
                pl.BlockSpec((None, tr, cols), lambda i, me_ref: (me_ref[0], i, 0)), blk, blk, blk]
    args = [me, parts, own, w, m, v]
    aliases = {}
    if prev is not None:
        in_specs += [_any()] * 4
        args += list(prev)
        aliases = {6 + k: k for k in range(4)}
    return pl.pallas_call(
        body, name=f"{name}_l{layer}",
        grid_spec=pltpu.PrefetchScalarGridSpec(
            num_scalar_prefetch=1, grid=(rows // tr,), in_specs=in_specs, out_specs=(blk, blk, blk, blk)),
        out_shape=out_shape, input_output_aliases=aliases,
        compiler_params=_params("parallel"),
    )(*args)


def _sum_slots(parts):
    def body(p_ref, o_ref):
        g = p_ref[0]
        for j in range(1, N_DEV):
            g = g + p_ref[j]
        o_ref[...] = g

    return pl.pallas_call(
        body, name="sum_small_grads",
        in_specs=[_vmem()], out_specs=_vmem(),
        out_shape=jax.ShapeDtypeStruct(parts.shape[1:], F32),
        compiler_params=_params(),
    )(parts)


def _adamw_small(w, g, m, v):
    def body(w_ref, g_ref, m_ref, v_ref, d_ref, nm_ref, nv_ref):
        d_ref[...], nm_ref[...], nv_ref[...] = _adamw_math(w_ref[...], g_ref[...], m_ref[...], v_ref[...])

    return pl.pallas_call(
        body, name="adamw_small",
        in_specs=[_vmem()] * 4, out_specs=(_vmem(),) * 3,
        out_shape=tuple(jax.ShapeDtypeStruct(w.shape, F32) for _ in range(3)),
        compiler_params=_params(),
    )(w, g, m, v)


def _pack_rows(arrays):
    rows = []
    for a in arrays:
        flat = a.reshape(-1)
        pad = (-flat.shape[0]) % (8 * LANES)
        rows.append(jnp.pad(flat, (0, pad)).reshape(-1, LANES))
    return jnp.concatenate(rows, axis=0)


def _unpack_rows(packed, like):
    out, r0 = [], 0
    for a in like:
        n = a.size
        nr = -(-n // (8 * LANES)) * 8
        out.append(packed[r0:r0 + nr].reshape(-1)[:n].reshape(a.shape))
        r0 += nr
    return out


def kernel(x, norm_g, w_in, q_norm_g, k_norm_g, rel_bias, w_out, loss_target, m_norm_g, m_w_in, m_q_norm_g, m_k_norm_g, m_rel_bias, m_w_out, v_norm_g, v_w_in, v_q_norm_g, v_k_norm_g, v_rel_bias, v_w_out):
    depth, d, e = w_in.shape
    r_out = w_out.shape[1]
    heads = e // HEAD_DIM
    rel_w = rel_bias.shape[2]
    x0 = x[0]
    target = loss_target[0]
    s = x0.shape[0]

    me = jnp.reshape(_flat(_my_place()), (1,)).astype(jnp.int32)

    casts = [(_cast_layer(w_in, me, l, "cast_w_in"), _cast_layer(w_out, me, l, "cast_w_out"))
             for l in range(depth)]

    def begin_gather(l, after):
        (win_b, win_land), (wout_b, wout_land) = casts[l]
        return _gather_send((win_b, wout_b), (win_land, wout_land), after, f"gather_send_l{l}")

    rel_all = _gather_small(rel_bias, [], "gather_rel_bias")
    sent = begin_gather(0, [rel_all])
    rel_full = jnp.transpose(rel_all, (1, 2, 0, 3)).reshape(depth * heads, N_DEV * rel_w)
    bias = jnp.transpose(_bias_expand(rel_full), (1, 0, 2))
    head_work = [bias] + [shard for cast in casts[1:] for shard, _ in cast]
    forwarded = _gather_forward(sent, head_work, "gather_forward_l0")

    xs, hs, projs, ys, mixes, tots, weights = [], [], [], [], [], [], []
    xl = x0
    for l in range(depth):
        win_all, wout_all = _gather_finish(forwarded, [xl, forwarded[-1]], f"gather_finish_l{l}")
        more = l + 1 < depth
        if more:
            sent = begin_gather(l + 1, [win_all])
        wout_full = wout_all.reshape(d, d)
        proj, h = _norm_proj(xl, norm_g[l:l + 1], win_all, sent[-1] if more else None, l)
        y, mixed, tot = _sb_forward(proj, l)
        if more:
            forwarded = _gather_forward(sent, [tot], f"gather_forward_l{l + 1}")
        y, mixed = _chunk_forward(proj, bias, q_norm_g[l:l + 1], k_norm_g[l:l + 1], y, mixed,
                                  forwarded[-1] if more else None, l)
        xs.append(xl), hs.append(h), projs.append(proj), ys.append(y), mixes.append(mixed), tots.append(tot)
        weights.append((win_all, wout_full))
        xl = _out_proj(mixed, wout_full, xl, l)

    dx, loss_parts = _loss_head(xl, target)
    loss = lax.psum(jnp.sum(loss_parts), AXES)

    dbias, dng, dqg, dkg = [None] * depth, [None] * depth, [None] * depth, [None] * depth
    res_in, res_out = None, None

    peer_slots = jnp.stack([_flat(_flip(_my_place(), k)) for k in SAME_CORE]).astype(jnp.int32)

    def finish_exchange(exchanging, after, l):
        sems, bufs, own = exchanging
        copies = N_DEV - 1 if own is None else 3
        bufs = _wait_copies(f"exchange_finish_l{l}", bufs, sems, 2, lambda refs, t: refs[t].at[0], copies, after)
        own = bufs[:2] if own is None else own
        rin, rout = bufs[2:]
        return (_adamw_layer(rin, own[0], me, w_in, m_w_in, v_w_in, res_in, l, "adamw_w_in"),
                _adamw_layer(rout, own[1], me, w_out, m_w_out, v_w_out, res_out, l, "adamw_w_out"))

    pending = []
    for l in reversed(range(depth)):
        win_all, wout_full = weights[l]
        dmixed = _out_proj_bwd_input(dx, wout_full, None, l)
        gwout = _out_proj_bwd_weight(mixes[l], dx, l).reshape(N_DEV, r_out, d)
        dproj = _sb_backward(projs[l], ys[l], dmixed, tots[l], l)
        dproj, dbias[l], dqg[l], dkg[l] = _chunk_backward(
            projs[l], bias, q_norm_g[l:l + 1], k_norm_g[l:l + 1], ys[l], dmixed, dproj, l)
        grads_l = (_proj_bwd_weight(hs[l], dproj, l), gwout)
        if l > 0:
            sems, bufs, token = _exchange_direct(grads_l, [], f"exchange_direct_l{l}")
            pending.append(((sems, bufs, None), l))
        else:
            sems, bufs, token = _exchange_to_sibling(grads_l, [], "exchange_sibling_l0")
            bufs = _wait_copies("exchange_sibling_wait_l0", bufs, sems, 2, lambda refs, t: refs[t].at[0], 4, [token])
            own, parts, lands = bufs[0:2], bufs[2:4], bufs[4:6]
            csums = [_chip_sums(own[t], parts[t], peer_slots, f"chip_sums_{t}_l0") for t in range(2)]
            sems, csums_lands, token = _exchange_to_chips(csums, lands, [], "exchange_chips_l0")
            pending.append(((sems, csums_lands, own), l))
        dx, dng[l] = _proj_bwd_input(dproj, win_all, xs[l], norm_g[l:l + 1], dx, token, l)
    tie = token
    for exchanging, l in pending[:-1]:
        res_in, res_out = finish_exchange(exchanging, [dx, tie], l)
    drel = _bias_grad(jnp.concatenate(dbias, axis=0), [tie])
    small_like = [norm_g, q_norm_g, k_norm_g, drel]
    mine = _pack_rows([jnp.concatenate(dng, axis=0), jnp.concatenate(dqg, axis=0),
                       jnp.concatenate(dkg, axis=0), drel])
    gathered = _gather_small(mine, [res_in[0], res_out[0]], "gather_small_grads")
    g_norm, g_qn, g_kn, g_rel_full = _unpack_rows(_sum_slots(gathered), small_like)
    my_block = _flat(_my_place())
    g_rel = lax.dynamic_slice_in_dim(g_rel_full.reshape(depth, heads, N_REL), my_block * rel_w, rel_w, axis=2)
    small_w = [norm_g, q_norm_g, k_norm_g, rel_bias]
    small = _adamw_small(_pack_rows(small_w), _pack_rows([g_norm, g_qn, g_kn, g_rel]),
                         _pack_rows([m_norm_g, m_q_norm_g, m_k_norm_g, m_rel_bias]),
                         _pack_rows([v_norm_g, v_q_norm_g, v_k_norm_g, v_rel_bias]))
    d_small, nm_small, nv_small = (_unpack_rows(p, small_w) for p in small)

    res_in, res_out = finish_exchange(pending[-1][0], [small[0], res_in[0], res_out[0]], 0)
    g_win, d_win, nm_win, nv_win = res_in
    g_wout, d_wout, nm_wout, nv_wout = res_out
    grads = (g_norm, g_win, g_qn, g_kn, g_rel, g_wout)

    def order(sm, big_in, big_out):
        return (sm[0], big_in, sm[1], sm[2], sm[3], big_out)

    return (loss, dx[None], *grads, *order(d_small, d_win, d_wout),
            *order(nm_small, nm_win, nm_wout), *order(nv_small, nv_win, nv_wout))
```
